```python
import math
import jax
import jax.numpy as jnp
from jax import lax
import numpy as np

D_MODEL = 2048
BATCH = 4
SEQ = 8192
DEPTH = 1

A_HEADS = 16
A_QK_DIM = 128
A_V_DIM = 64
A_Q_RANK = 512
A_KV_RANK = 128
IDX_HEADS = 16
IDX_DIM = 64
IDX_TOPK = 256
Q_BLOCK = 128
RPB_BUCKETS = 32
RPB_MAX_DIST = 128
B_HEADS = 8
B_KEY_DIM = 128
B_VAL_DIM = 128
B_CHUNK = 64
N_EXPERTS = 64
EXPERT_DIM = 512
MOE_TOPK = 8
N_GROUPS = 8
TOPK_GROUPS = 4
ROUTED_SCALE = 2.5
EXPERT_BLOCK = 256
DN_ALPHA = (2 * DEPTH) ** 0.25
DN_BETA = (8 * DEPTH) ** -0.25
EPS = 1e-6

IN_SPLITS = (A_Q_RANK, A_KV_RANK, IDX_DIM, IDX_HEADS,
             B_HEADS * B_KEY_DIM, B_HEADS * B_KEY_DIM, B_HEADS * B_VAL_DIM, B_HEADS * B_VAL_DIM,
             D_MODEL, D_MODEL)
IN_WIDTH = sum(IN_SPLITS)

kernel_name = 'hybrid_dsa_hgrn2_moe_deepnorm_block'


def _layer_norm(x, g=None, b=None):
    xf = x.astype(jnp.float32)
    mu = jnp.mean(xf, -1, keepdims=True)
    var = jnp.mean(jnp.square(xf - mu), -1, keepdims=True)
    y = (xf - mu) * lax.rsqrt(var + EPS)
    if g is not None:
        y = y * g.astype(jnp.float32) + b.astype(jnp.float32)
    return y.astype(x.dtype)


def _rms_norm(x, g):
    xf = x.astype(jnp.float32)
    y = xf * lax.rsqrt(jnp.mean(jnp.square(xf), -1, keepdims=True) + EPS)
    return (y * g.astype(jnp.float32)).astype(x.dtype)


def _swiglu(h, w_gate, w_up, w_down):
    return (jax.nn.silu(h @ w_gate) * (h @ w_up)) @ w_down


def _t5_bucket(dist):
    max_exact = RPB_BUCKETS // 2
    d = jnp.maximum(dist, 1).astype(jnp.float32)
    large = max_exact + (jnp.log(d / max_exact) / math.log(RPB_MAX_DIST / max_exact)
                         * (RPB_BUCKETS - max_exact)).astype(jnp.int32)
    large = jnp.minimum(large, RPB_BUCKETS - 1)
    return jnp.where(dist < max_exact, dist, large)


def _dsa_attention(q_lat, q_idx, w_idx, k_idx, kv_lat, rpb_table):
    B, S, H, C = q_lat.shape
    topk = min(IDX_TOPK, S // 4)
    nb = S // Q_BLOCK
    key_pos = jnp.arange(S)

    def blocks(a):
        return jnp.moveaxis(a.reshape((B, nb, Q_BLOCK) + a.shape[2:]), 1, 0)

    def one_block(inp):
        ql, qi, wi, start = inp
        t = start + jnp.arange(Q_BLOCK)
        rel = jax.nn.relu(jnp.einsum('bqhd,bsd->bqhs', qi, k_idx).astype(jnp.float32))
        score = jnp.einsum('bqh,bqhs->bqs', wi.astype(jnp.float32), rel)
        score = jnp.where(key_pos[None, None, :] <= t[None, :, None], score, -jnp.inf)
        _, sel = lax.top_k(score, topk)
        valid = sel <= t[None, :, None]
        kv_sel = jax.vmap(lambda kv, ix: kv[ix])(kv_lat, sel)
        bias = rpb_table[_t5_bucket(jnp.maximum(t[None, :, None] - sel, 0))]
        logits = (jnp.einsum('bqhc,bqkc->bqhk', ql, kv_sel).astype(jnp.float32) * A_QK_DIM ** -0.5
                  + jnp.moveaxis(bias, -1, 2).astype(jnp.float32))
        logits = jnp.where(valid[:, :, None, :], logits, -jnp.inf)
        p = jax.nn.softmax(logits, axis=-1).astype(kv_sel.dtype)
        return jnp.einsum('bqhk,bqkc->bqhc', p, kv_sel)

    out = lax.map(one_block, (blocks(q_lat), blocks(q_idx), blocks(w_idx),
                              jnp.arange(nb, dtype=jnp.int32) * Q_BLOCK))
    return jnp.moveaxis(out, 0, 1).reshape(B, S, H, C)


def _hgrn2_chunked(q, k, v, log_f):
    B, S, H, dk = q.shape
    dv = v.shape[-1]
    nc = S // B_CHUNK

    def to_chunks(a):
        return a.astype(jnp.float32).reshape(B, nc, B_CHUNK, H, a.shape[-1]).transpose(1, 0, 3, 2, 4)

    causal = jnp.tril(jnp.ones((B_CHUNK, B_CHUNK), bool))

    def step(state, inp):
        qc, kc, vc, lfc = inp
        b = jnp.cumsum(lfc, axis=2)
        o_inter = jnp.einsum('bhtk,bhkv->bhtv', qc * jnp.exp(b), state)
        diff = b[:, :, :, None, :] - b[:, :, None, :, :]
        decay = jnp.exp(jnp.where(causal[:, :, None], diff, -jnp.inf))
        att = jnp.einsum('bhtk,bhsk,bhtsk->bhts', qc, kc, decay)
        o = o_inter + jnp.einsum('bhts,bhsv->bhtv', att, vc)
        b_last = b[:, :, -1:, :]
        k_dec = kc * jnp.exp(b_last - b)
        state = (jnp.exp(b_last[:, :, 0, :])[..., None] * state
                 + jnp.einsum('bhsk,bhsv->bhkv', k_dec, vc))
        return state, o

    state0 = jnp.zeros((B, H, dk, dv), jnp.float32)
    _, o = lax.scan(step, state0, (to_chunks(q), to_chunks(k), to_chunks(v), to_chunks(log_f)))
    return o.transpose(1, 0, 3, 2, 4).reshape(B, S, H, dv)


def _routed_experts(h, w_router, router_bias, w_gate, w_up, w_down):
    n_tok, d = h.shape
    scores = jax.nn.sigmoid((h @ w_router).astype(jnp.float32))
    biased = scores + router_bias.astype(jnp.float32)
    per_group = N_EXPERTS // N_GROUPS
    group_score = lax.top_k(biased.reshape(n_tok, N_GROUPS, per_group), 2)[0].sum(-1)
    _, top_groups = lax.top_k(group_score, TOPK_GROUPS)
    group_mask = jnp.any(top_groups[..., None] == jnp.arange(N_GROUPS), axis=-2)
    expert_mask = jnp.repeat(group_mask, per_group, axis=-1)
    _, sel = lax.top_k(jnp.where(expert_mask, biased, -jnp.inf), MOE_TOPK)
    wts = jnp.take_along_axis(scores, sel, axis=-1)
    wts = wts / jnp.sum(wts, -1, keepdims=True) * ROUTED_SCALE
    n_assign = n_tok * MOE_TOPK
    e_flat = sel.reshape(n_assign)
    order = jnp.argsort(e_flat)
    e_sorted = e_flat[order]
    tok_sorted = (order // MOE_TOPK).astype(jnp.int32)
    w_sorted = wts.reshape(n_assign)[order].astype(h.dtype)
    sizes = jax.ops.segment_sum(jnp.ones((n_assign,), jnp.int32), e_flat, num_segments=N_EXPERTS)
    padded = (sizes + EXPERT_BLOCK - 1) // EXPERT_BLOCK * EXPERT_BLOCK
    start = jnp.cumsum(sizes) - sizes
    pend = jnp.cumsum(padded)
    pstart = pend - padded
    dest = pstart[e_sorted] + jnp.arange(n_assign, dtype=jnp.int32) - start[e_sorted]
    n_rows = -(-(n_assign + N_EXPERTS * (EXPERT_BLOCK - 1)) // EXPERT_BLOCK) * EXPERT_BLOCK
    n_blocks = n_rows // EXPERT_BLOCK
    tok_buf = jnp.zeros((n_rows,), jnp.int32).at[dest].set(tok_sorted)
    w_buf = jnp.zeros((n_rows,), h.dtype).at[dest].set(w_sorted)
    blk_expert = jnp.minimum(
        jnp.searchsorted(pend, jnp.arange(n_blocks, dtype=jnp.int32) * EXPERT_BLOCK, side='right'),
        N_EXPERTS - 1)

    def accumulate(acc, inp):
        tok, wt, e = inp
        y = _swiglu(h[tok], w_gate[e], w_up[e], w_down[e])
        return acc.at[tok].add(y * wt[:, None]), None

    out, _ = lax.scan(accumulate, jnp.zeros_like(h),
                      (tok_buf.reshape(n_blocks, EXPERT_BLOCK), w_buf.reshape(n_blocks, EXPERT_BLOCK), blk_expert))
    return out


def setup_inputs(seed: int = 0) -> dict:
    key = jax.random.key(seed)
    ks = jax.random.split(key, 40)
    D = D_MODEL
    L = DEPTH

    def nrm(i, shape, scale):
        return jax.random.normal(ks[i], shape, jnp.float32) * scale

    return {
        'x': nrm(0, (BATCH, SEQ, D), 1.0),
        'c': nrm(1, (BATCH, D), 1.0),
        'rpb_table': nrm(2, (RPB_BUCKETS, A_HEADS), 0.5),
        'hgrn_lb_logits': nrm(3, (DEPTH + 1, B_HEADS * B_KEY_DIM), 0.5),
        'ada_w': nrm(4, (L, D, 6 * D), D ** -0.5),
        'ada_b': nrm(5, (L, 6 * D), 0.02),
        'w_in': nrm(6, (L, D, IN_WIDTH), D ** -0.5),
        'q_norm_g': 1.0 + nrm(7, (L, A_Q_RANK), 0.02),
        'kv_norm_g': 1.0 + nrm(8, (L, A_KV_RANK), 0.02),
        'w_uq': nrm(9, (L, A_Q_RANK, A_HEADS * A_QK_DIM), A_Q_RANK ** -0.5),
        'w_uk': nrm(10, (L, A_HEADS, A_KV_RANK, A_QK_DIM), A_KV_RANK ** -0.5),
        'w_uv': nrm(11, (L, A_HEADS, A_KV_RANK, A_V_DIM), DN_BETA * A_KV_RANK ** -0.5),
        'w_qidx': nrm(12, (L, A_Q_RANK, IDX_HEADS * IDX_DIM), A_Q_RANK ** -0.5),
        'idx_k_norm_g': 1.0 + nrm(13, (L, IDX_DIM), 0.02),
        'idx_k_norm_b': nrm(14, (L, IDX_DIM), 0.02),
        'hgrn_out_norm_g': 1.0 + nrm(15, (L, B_VAL_DIM), 0.02),
        'w_branch_a': nrm(16, (L, A_HEADS * A_V_DIM, D), (A_HEADS * A_V_DIM) ** -0.5),
        'w_branch_b': nrm(17, (L, B_HEADS * B_VAL_DIM, D), (B_HEADS * B_VAL_DIM) ** -0.5),
        'w_o': nrm(18, (L, D, D), DN_BETA * D ** -0.5),
        'ln1_g': 1.0 + nrm(19, (L, D), 0.02),
        'ln1_b': nrm(20, (L, D), 0.02),
        'w_router': nrm(21, (L, D, N_EXPERTS), D ** -0.5),
        'router_bias': nrm(22, (L, N_EXPERTS), 0.01),
        'w_exp_gate': nrm(23, (L, N_EXPERTS, D, EXPERT_DIM), D ** -0.5),
        'w_exp_up': nrm(24, (L, N_EXPERTS, D, EXPERT_DIM), D ** -0.5),
        'w_exp_down': nrm(25, (L, N_EXPERTS, EXPERT_DIM, D), DN_BETA * EXPERT_DIM ** -0.5),
        'w_sh_gate': nrm(26, (L, D, EXPERT_DIM), D ** -0.5),
        'w_sh_up': nrm(27, (L, D, EXPERT_DIM), D ** -0.5),
        'w_sh_down': nrm(28, (L, EXPERT_DIM, D), DN_BETA * EXPERT_DIM ** -0.5),
        'ln2_g': 1.0 + nrm(29, (L, D), 0.02),
        'ln2_b': nrm(30, (L, D), 0.02),
    }


def reference(x, c, rpb_table, hgrn_lb_logits, ada_w, ada_b, w_in, q_norm_g, kv_norm_g,
              w_uq, w_uk, w_uv, w_qidx, idx_k_norm_g, idx_k_norm_b, hgrn_out_norm_g,
              w_branch_a, w_branch_b, w_o, ln1_g, ln1_b, w_router, router_bias,
              w_exp_gate, w_exp_up, w_exp_down, w_sh_gate, w_sh_up, w_sh_down, ln2_g, ln2_b):
    B, S, D = x.shape
    lower_bounds = jnp.cumsum(jax.nn.softmax(hgrn_lb_logits.astype(jnp.float32), axis=0), axis=0)
    cond = jax.nn.silu(c)
    split_at = np.cumsum(IN_SPLITS)[:-1].tolist()
    for l in range(DEPTH):
        mod = (cond @ ada_w[l] + ada_b[l])[:, None, :]
        shift1, scale1, gate1, shift2, scale2, gate2 = jnp.split(mod, 6, axis=-1)

        h = _layer_norm(x) * (1 + scale1) + shift1
        proj = h @ w_in[l]
        c_q, c_kv, k_i, w_i, hq, hf, hi, hg, ga, gb = jnp.split(proj, split_at, axis=-1)

        c_q = _rms_norm(c_q, q_norm_g[l])
        q = (c_q @ w_uq[l]).reshape(B, S, A_HEADS, A_QK_DIM)
        q_lat = jnp.einsum('bshd,hcd->bshc', q, w_uk[l])
        q_idx = (c_q @ w_qidx[l]).reshape(B, S, IDX_HEADS, IDX_DIM)
        k_idx = _layer_norm(k_i, idx_k_norm_g[l], idx_k_norm_b[l])
        w_idx = w_i * (IDX_HEADS * IDX_DIM) ** -0.5
        kv_lat = _rms_norm(c_kv, kv_norm_g[l])
        o_lat = _dsa_attention(q_lat, q_idx, w_idx, k_idx, kv_lat, rpb_table)
        y_a = jnp.einsum('bshc,hcv->bshv', o_lat, w_uv[l]).reshape(B, S, A_HEADS * A_V_DIM) @ w_branch_a[l]

        lb = lower_bounds[l].reshape(B_HEADS, B_KEY_DIM)
        f = lb + (1 - lb) * jax.nn.sigmoid(hf.reshape(B, S, B_HEADS, B_KEY_DIM).astype(jnp.float32))
        q_b = jax.nn.silu(hq).reshape(B, S, B_HEADS, B_KEY_DIM) * B_KEY_DIM ** -0.5
        o_b = _hgrn2_chunked(q_b, 1 - f, hi.reshape(B, S, B_HEADS, B_VAL_DIM), jnp.log(f)).astype(x.dtype)
        o_b = _rms_norm(o_b, hgrn_out_norm_g[l]) * jax.nn.silu(hg.reshape(B, S, B_HEADS, B_VAL_DIM))
        y_b = o_b.reshape(B, S, B_HEADS * B_VAL_DIM) @ w_branch_b[l]

        mixed = (jax.nn.sigmoid(ga) * y_a + jax.nn.sigmoid(gb) * y_b) @ w_o[l]
        x = _layer_norm(DN_ALPHA * x + gate1 * mixed, ln1_g[l], ln1_b[l])

        h2 = (_layer_norm(x) * (1 + scale2) + shift2).reshape(B * S, D)
        y = (_swiglu(h2, w_sh_gate[l], w_sh_up[l], w_sh_down[l])
             + _routed_experts(h2, w_router[l], router_bias[l], w_exp_gate[l], w_exp_up[l], w_exp_down[l]))
        x = _layer_norm(DN_ALPHA * x + gate2 * y.reshape(B, S, D), ln2_g[l], ln2_b[l])
    return x
```

```python
import functools
import math

import numpy as np
import jax
import jax.numpy as jnp
from jax import lax
from jax.experimental import pallas as pl
from jax.experimental.pallas import tpu as pltpu

F32 = jnp.float32
BF16 = jnp.bfloat16
I32 = jnp.int32

EPS = 1e-6
IDX_TOPK = 256
RPB_MAX_DIST = 128
MOE_TOPK = 8
N_GROUPS = 8
TOPK_GROUPS = 4
ROUTED_SCALE = 2.5
EXPERT_BLOCK = 256

V7X_VMEM_LIMIT_BYTES = 56 * 1024 * 1024
LANES = 128
INT_MIN = -2 ** 31
NEG_BIG = -1e30


def _cparams(*sem):
    return pltpu.CompilerParams(dimension_semantics=tuple(sem), vmem_limit_bytes=V7X_VMEM_LIMIT_BYTES)


def _pick(n, prefs):
    for p in prefs:
        if n % p == 0:
            return p
    return n


def _sigmoid(v):
    return 1.0 / (1.0 + jnp.exp(-v))


def _silu(v):
    return v * _sigmoid(v)


def _ln(v):
    mu = jnp.mean(v, axis=-1, keepdims=True)
    d = v - mu
    var = jnp.mean(d * d, axis=-1, keepdims=True)
    return d * lax.rsqrt(var + EPS)


def _rms(v):
    return v * lax.rsqrt(jnp.mean(v * v, axis=-1, keepdims=True) + EPS)


def _dot(a, b):
    return jnp.dot(a, b, preferred_element_type=F32)


def _dot_nt(a, b):
    return lax.dot_general(a, b, (((1,), (1,)), ((), ())), preferred_element_type=F32)


def _dot_tn(a, b):
    return lax.dot_general(a, b, (((0,), (0,)), ((), ())), preferred_element_type=F32)


def _ada_kernel(c_ref, w_ref, b_ref, o_ref):
    c = c_ref[...]
    o_ref[...] = _dot(_silu(c).astype(BF16), w_ref[...].astype(BF16)) + b_ref[...]


def _ada(c, w, b):
    bsz, d = c.shape
    n = w.shape[1]
    rows = 8
    cp = jnp.zeros((rows, d), F32).at[:bsz].set(c)
    tn = _pick(n, (1024, 512, 256, 128))
    out = pl.pallas_call(
        _ada_kernel,
        grid=(n // tn,),
        in_specs=[pl.BlockSpec((rows, d), lambda j: (0, 0)),
                  pl.BlockSpec((d, tn), lambda j: (0, j)),
                  pl.BlockSpec((1, tn), lambda j: (0, j))],
        out_specs=pl.BlockSpec((rows, tn), lambda j: (0, j)),
        out_shape=jax.ShapeDtypeStruct((rows, n), F32),
        compiler_params=_cparams("arbitrary"),
        name="ada",
    )(cp, w, b.reshape(1, n))
    return out[:bsz]


def _inproj_kernel(x_ref, sc_ref, sh_ref, w_ref, o_ref, h_ref):
    @pl.when(pl.program_id(2) == 0)
    def _():
        h = _ln(x_ref[0]) * (1.0 + sc_ref[0]) + sh_ref[0]
        h_ref[...] = h.astype(BF16)

    o_ref[0] = _dot(h_ref[...], w_ref[...]).astype(o_ref.dtype)


def _inproj(x, scale, shift, w):
    bsz, s, d = x.shape
    n = w.shape[1]
    tm = _pick(s, (1024, 512, 256, 128))
    tn = _pick(n, (1024, 512, 256, 128))
    return pl.pallas_call(
        _inproj_kernel,
        grid=(bsz, s // tm, n // tn),
        in_specs=[pl.BlockSpec((1, tm, d), lambda b, i, j: (b, i, 0)),
                  pl.BlockSpec((1, 1, d), lambda b, i, j: (b, 0, 0)),
                  pl.BlockSpec((1, 1, d), lambda b, i, j: (b, 0, 0)),
                  pl.BlockSpec((d, tn), lambda b, i, j: (0, j))],
        out_specs=pl.BlockSpec((1, tm, tn), lambda b, i, j: (b, i, j)),
        out_shape=jax.ShapeDtypeStruct((bsz, s, n), BF16),
        scratch_shapes=[pltpu.VMEM((tm, d), BF16)],
        compiler_params=_cparams("arbitrary", "arbitrary", "arbitrary"),
        name="inproj",
    )(x, scale, shift, w)


def _prep_kernel(a_ref, qg_ref, kvg_ref, ikg_ref, ikb_ref, wuq_ref, wukT_ref, wqi_ref,
                 qlat_ref, qidx_ref, widx_ref, kidx_ref, kv_ref, *, qr, kvr, idim, ih, ah, qk):
    a = a_ref[0].astype(F32)
    cq = (_rms(a[:, :qr]) * qg_ref[...]).astype(BF16)
    ckv = a[:, qr:qr + kvr]
    ki = a[:, qr + kvr:qr + kvr + idim]
    wi = a[:, qr + kvr + idim:qr + kvr + idim + ih]
    kv_ref[0] = (_rms(ckv) * kvg_ref[...]).astype(BF16)
    kidx_ref[0] = (_ln(ki) * ikg_ref[...] + ikb_ref[...]).astype(BF16)
    widx_ref[0] = wi * float((ih * idim) ** -0.5)
    q = _dot(cq, wuq_ref[...])
    for h in range(ah):
        qh = q[:, h * qk:(h + 1) * qk].astype(BF16)
        qlat_ref[0, h] = (_dot(qh, wukT_ref[h]) * float(qk ** -0.5)).astype(BF16)
    qi = _dot(cq, wqi_ref[...])
    for h in range(ih):
        qidx_ref[0, h] = qi[:, h * idim:(h + 1) * idim].astype(BF16)


def _prep(proj, a_blk, wa, dims, q_norm_g, kv_norm_g, ikg, ikb, w_uq, w_ukT, w_qidx):
    bsz, s, _ = proj.shape
    qr, kvr, idim, ih, ah, qk = dims
    tm = _pick(s, (256, 128))
    kern = functools.partial(_prep_kernel, qr=qr, kvr=kvr, idim=idim, ih=ih, ah=ah, qk=qk)
    full = lambda shape: pl.BlockSpec(shape, lambda b, i: (0,) * len(shape))
    return pl.pallas_call(
        kern,
        grid=(bsz, s // tm),
        in_specs=[pl.BlockSpec((1, tm, wa), lambda b, i: (b, i, a_blk)),
                  full((1, qr)), full((1, kvr)), full((1, idim)), full((1, idim)),
                  full(w_uq.shape), full(w_ukT.shape), full(w_qidx.shape)],
        out_specs=[pl.BlockSpec((1, ah, tm, kvr), lambda b, i: (b, 0, i, 0)),
                   pl.BlockSpec((1, ih, tm, idim), lambda b, i: (b, 0, i, 0)),
                   pl.BlockSpec((1, tm, ih), lambda b, i: (b, i, 0)),
                   pl.BlockSpec((1, tm, idim), lambda b, i: (b, i, 0)),
                   pl.BlockSpec((1, tm, kvr), lambda b, i: (b, i, 0))],
        out_shape=[jax.ShapeDtypeStruct((bsz, ah, s, kvr), BF16),
                   jax.ShapeDtypeStruct((bsz, ih, s, idim), BF16),
                   jax.ShapeDtypeStruct((bsz, s, ih), F32),
                   jax.ShapeDtypeStruct((bsz, s, idim), BF16),
                   jax.ShapeDtypeStruct((bsz, s, kvr), BF16)],
        compiler_params=_cparams("arbitrary", "arbitrary"),
        name="prep",
    )(proj, q_norm_g.reshape(1, qr), kv_norm_g.reshape(1, kvr), ikg.reshape(1, idim), ikb.reshape(1, idim),
      w_uq, w_ukT, w_qidx)


def _dsa_kernel(qidx_ref, widx_ref, qlat_ref, kidxT_ref, kvT_ref, kv_ref, t0_ref, t1_ref, wuv_ref,
                o_ref, keys_ref, m_ref, l_ref, acc_ref, *, qb_sz, topk, ih, ah, vd):
    QB = qb_sz
    qb = pl.program_id(1)
    nchunks = qb + 1
    row_pos = qb * QB + lax.broadcasted_iota(I32, (QB, 1), 0)
    lane_pos = lax.broadcasted_iota(I32, (1, QB), 1)
    w = widx_ref[0]

    wcols = [w[:, h:h + 1] for h in range(ih)]

    def score_chunk(kc, carry):
        off = pl.multiple_of(kc * QB, QB)
        kT = kidxT_ref[0, :, pl.ds(off, QB)]
        acc = jnp.zeros((QB, QB), F32)
        for h in range(ih):
            z = _dot(qidx_ref[0, h], kT)
            acc = acc + wcols[h] * jnp.maximum(z, 0.0)
        bits = pltpu.bitcast(acc, I32)
        skey = bits ^ ((bits >> 31) & 0x7FFFFFFF)
        causal = (off + lane_pos) <= row_pos
        keys_ref[:, pl.ds(off, QB)] = jnp.where(causal, skey, INT_MIN)
        return carry

    lax.fori_loop(0, nchunks, score_chunk, 0)

    def count_ge(cand):
        candb = jnp.broadcast_to(cand, (QB, LANES))

        def body(kc, cnt):
            off = pl.multiple_of(kc * QB, QB)
            for j in range(QB // LANES):
                blk = keys_ref[:, pl.ds(off + j * LANES, LANES)]
                cnt = cnt + jnp.where(blk >= candb, 1, 0)
            return cnt

        cnt = lax.fori_loop(0, nchunks, body, jnp.zeros((QB, LANES), I32))
        return jnp.sum(cnt.astype(F32), axis=1, keepdims=True)

    def bis_cond(st):
        bit, _, done = st
        return jnp.logical_and(bit >= 0, jnp.min(done) < 0.5)

    def bis_body(st):
        bit, c, done = st
        cand = c + jnp.left_shift(jnp.int32(1), bit)
        cnt = count_ge(cand)
        take = jnp.logical_and(cnt >= float(topk), done < 0.5)
        c = jnp.where(take, cand, c)
        done = jnp.where(jnp.logical_and(take, cnt == float(topk)), 1.0, done)
        return bit - 1, c, done

    _, thr, _ = lax.while_loop(
        bis_cond, bis_body,
        (jnp.int32(31), jnp.full((QB, 1), INT_MIN, I32), jnp.zeros((QB, 1), F32)))

    m_ref[...] = jnp.full(m_ref.shape, NEG_BIG, F32)
    l_ref[...] = jnp.zeros(l_ref.shape, F32)
    acc_ref[...] = jnp.zeros(acc_ref.shape, F32)

    def attn_chunk(off, bias_ref, diag):
        sel = keys_ref[:, pl.ds(off, QB)] >= thr
        if diag:
            sel = jnp.logical_and(sel, (off + lane_pos) <= row_pos)
        madd = jnp.where(sel, 0.0, NEG_BIG)
        kT = kvT_ref[0, :, pl.ds(off, QB)]
        kv = kv_ref[0, pl.ds(off, QB), :]
        for h in range(ah):
            s = _dot(qlat_ref[0, h], kT) + madd
            if bias_ref is not None:
                s = s + bias_ref[h]
            m_prev = m_ref[h]
            m_new = jnp.maximum(m_prev, jnp.max(s, axis=1, keepdims=True))
            alpha = jnp.exp(m_prev - m_new)
            p = jnp.exp(s - m_new)
            l_ref[h] = alpha * l_ref[h] + jnp.sum(p, axis=1, keepdims=True)
            acc_ref[h] = alpha * acc_ref[h] + _dot(p.astype(BF16), kv)
            m_ref[h] = m_new

    def far_body(kc, carry):
        attn_chunk(pl.multiple_of(kc * QB, QB), None, False)
        return carry

    lax.fori_loop(0, qb - 1, far_body, 0)

    @pl.when(qb >= 1)
    def _():
        attn_chunk(pl.multiple_of((qb - 1) * QB, QB), t1_ref, False)

    attn_chunk(pl.multiple_of(qb * QB, QB), t0_ref, True)

    for h in range(ah):
        o = acc_ref[h] / l_ref[h]
        o_ref[0, :, h * vd:(h + 1) * vd] = _dot(o.astype(BF16), wuv_ref[h]).astype(o_ref.dtype)


def _t5_bucket_np(d, nbuckets):
    max_exact = nbuckets // 2
    dd = np.maximum(d, 1).astype(np.float32)
    large = max_exact + (np.log(dd / np.float32(max_exact)) / np.float32(math.log(RPB_MAX_DIST / max_exact))
                         * np.float32(nbuckets - max_exact)).astype(np.int32)
    large = np.minimum(large, nbuckets - 1)
    return np.where(d < max_exact, d, large).astype(np.int32)


def _dsa(q_idx, w_idx, q_lat, k_idxT, kv_latT, kv_lat, rpb_table, w_uv, topk):
    bsz, ih, s, idim = q_idx.shape
    ah, c = q_lat.shape[1], q_lat.shape[3]
    vd = w_uv.shape[2]
    nb = rpb_table.shape[0]
    QB = _pick(s, (256, 128))
    assert QB >= RPB_MAX_DIST
    i = np.arange(QB)[:, None]
    j = np.arange(QB)[None, :]
    d0 = np.maximum(i - j, 0)
    d1 = QB + i - j
    rel = rpb_table.astype(F32) - rpb_table[nb - 1].astype(F32)[None, :]
    t0 = jnp.transpose(rel[_t5_bucket_np(d0, nb)], (2, 0, 1))
    t1 = jnp.transpose(rel[_t5_bucket_np(d1, nb)], (2, 0, 1))
    kern = functools.partial(_dsa_kernel, qb_sz=QB, topk=topk, ih=ih, ah=ah, vd=vd)
    const3 = lambda shape: pl.BlockSpec(shape, lambda b, i: (0, 0, 0))
    return pl.pallas_call(
        kern,
        grid=(bsz, s // QB),
        in_specs=[pl.BlockSpec((1, ih, QB, idim), lambda b, i: (b, 0, i, 0)),
                  pl.BlockSpec((1, QB, ih), lambda b, i: (b, i, 0)),
                  pl.BlockSpec((1, ah, QB, c), lambda b, i: (b, 0, i, 0)),
                  pl.BlockSpec((1, idim, s), lambda b, i: (b, 0, 0)),
                  pl.BlockSpec((1, c, s), lambda b, i: (b, 0, 0)),
                  pl.BlockSpec((1, s, c), lambda b, i: (b, 0, 0)),
                  const3((ah, QB, QB)), const3((ah, QB, QB)), const3(w_uv.shape)],
        out_specs=pl.BlockSpec((1, QB, ah * vd), lambda b, i: (b, i, 0)),
        out_shape=jax.ShapeDtypeStruct((bsz, s, ah * vd), BF16),
        scratch_shapes=[pltpu.VMEM((QB, s), I32),
                        pltpu.VMEM((ah, QB, 1), F32),
                        pltpu.VMEM((ah, QB, 1), F32),
                        pltpu.VMEM((ah, QB, c), F32)],
        compiler_params=_cparams("arbitrary", "arbitrary"),
        name="dsa",
    )(q_idx, w_idx, q_lat, k_idxT, kv_latT, kv_lat, t0, t1, w_uv)


HGRN_CHUNK = 64
HGRN_SUB = 16
HGRN_EXP_CLAMP = 80.0


def _hgrn_kernel(hq_ref, hf_ref, hi_ref, hg_ref, lb_ref, g_ref, o_ref, st_ref, *, nh, dk, dv, tt):
    C, SUB = HGRN_CHUNK, HGRN_SUB

    @pl.when(pl.program_id(1) == 0)
    def _():
        st_ref[...] = jnp.zeros(st_ref.shape, F32)

    r = lax.broadcasted_iota(I32, (C, C), 0)
    cc = lax.broadcasted_iota(I32, (C, C), 1)
    tri_mask = r >= cc
    tri = jnp.where(tri_mask, 1.0, 0.0).astype(BF16)
    g = g_ref[...]

    def head(h, carry):
        ko = pl.multiple_of(h * dk, dk)
        vo = pl.multiple_of(h * dv, dv)
        lb = lb_ref[:, pl.ds(ko, dk)]
        stT = st_ref[h]
        for c in range(tt // C):
            rows = pl.ds(c * C, C)
            hf = hf_ref[0, rows, pl.ds(ko, dk)].astype(F32)
            f = lb + (1.0 - lb) * _sigmoid(hf)
            lf = jnp.log(f)
            t1 = lf.astype(BF16)
            r1 = lf - t1.astype(F32)
            t2 = r1.astype(BF16)
            t3 = (r1 - t2.astype(F32)).astype(BF16)
            b = _dot(tri, t1) + _dot(tri, t2) + _dot(tri, t3)
            kk = 1.0 - f
            hq = hq_ref[0, rows, pl.ds(ko, dk)].astype(F32)
            q = _silu(hq) * float(dk ** -0.5)
            v = hi_ref[0, rows, pl.ds(vo, dv)]
            o_inter = _dot_nt((q * jnp.exp(b)).astype(BF16), stT.astype(BF16))
            parts = []
            for i in range(C // SUB):
                lo, n = i * SUB, (i + 1) * SUB
                bi = b[lo - 1:lo] if i > 0 else jnp.zeros((1, dk), F32)
                qs = (q[lo:n] * jnp.exp(b[lo:n] - bi)).astype(BF16)
                ks = (kk * jnp.exp(jnp.minimum(bi - b, HGRN_EXP_CLAMP))).astype(BF16)
                parts.append(_dot_nt(qs, ks))
            att = jnp.where(tri_mask, jnp.concatenate(parts, axis=0), 0.0)
            o = o_inter + _dot(att.astype(BF16), v)
            b_last = b[C - 1:C]
            k_dec = (kk * jnp.exp(b_last - b)).astype(BF16)
            stT = stT * jnp.exp(b_last) + _dot_tn(v, k_dec)
            hg = hg_ref[0, rows, pl.ds(vo, dv)].astype(F32)
            o_ref[0, rows, pl.ds(vo, dv)] = (_rms(o) * g * _silu(hg)).astype(o_ref.dtype)
        st_ref[h] = stT
        return carry

    lax.fori_loop(0, nh, head, 0)


def _hgrn(proj, blks, lb, g, nh, dk, dv):
    bsz, s, _ = proj.shape
    tt = _pick(s, (256, 128, 64))
    kern = functools.partial(_hgrn_kernel, nh=nh, dk=dk, dv=dv, tt=tt)
    col = lambda blk, wdt: pl.BlockSpec((1, tt, wdt), lambda b, i: (b, i, blk))
    return pl.pallas_call(
        kern,
        grid=(bsz, s // tt),
        in_specs=[col(blks[0], nh * dk), col(blks[1], nh * dk), col(blks[2], nh * dv), col(blks[3], nh * dv),
                  pl.BlockSpec((1, nh * dk), lambda b, i: (0, 0)),
                  pl.BlockSpec((1, dv), lambda b, i: (0, 0))],
        out_specs=pl.BlockSpec((1, tt, nh * dv), lambda b, i: (b, i, 0)),
        out_shape=jax.ShapeDtypeStruct((bsz, s, nh * dv), BF16),
        scratch_shapes=[pltpu.VMEM((nh, dv, dk), F32)],
        compiler_params=_cparams("arbitrary", "arbitrary"),
        name="hgrn",
    )(proj, proj, proj, proj, lb.reshape(1, nh * dk), g.reshape(1, dv))


def _postmix_kernel(oa_ref, ob_ref, ga_ref, gb_ref, x_ref, g1_ref, sc2_ref, sh2_ref,
                    wa_ref, wb_ref, wo_ref, wrT_ref, lng_ref, lnb_ref,
                    x1_ref, h2_ref, lgT_ref, *, alpha):
    ya = _dot(oa_ref[0], wa_ref[...])
    yb = _dot(ob_ref[0], wb_ref[...])
    mix = _sigmoid(ga_ref[0].astype(F32)) * ya + _sigmoid(gb_ref[0].astype(F32)) * yb
    mixed = _dot(mix.astype(BF16), wo_ref[...])
    x1 = _ln(alpha * x_ref[0] + g1_ref[0] * mixed) * lng_ref[...] + lnb_ref[...]
    x1_ref[0] = x1
    h2 = _ln(x1) * (1.0 + sc2_ref[0]) + sh2_ref[0]
    h2_ref[0] = h2
    lgT_ref[0] = _dot_nt(wrT_ref[...], h2.astype(BF16))


def _postmix(o_a, o_b, proj, ga_blk, gb_blk, x, gate1, scale2, shift2, wa, wb, wo, wrT, lng, lnb, alpha):
    bsz, s, d = x.shape
    ne = wrT.shape[0]
    tm = _pick(s, (256, 128))
    kern = functools.partial(_postmix_kernel, alpha=alpha)
    row = lambda wdt: pl.BlockSpec((1, tm, wdt), lambda b, i: (b, i, 0))
    vec = pl.BlockSpec((1, 1, d), lambda b, i: (b, 0, 0))
    full = lambda a: pl.BlockSpec(a.shape, lambda b, i: (0,) * a.ndim, pipeline_mode=pl.Buffered(1))
    return pl.pallas_call(
        kern,
        grid=(bsz, s // tm),
        in_specs=[row(o_a.shape[2]), row(o_b.shape[2]),
                  pl.BlockSpec((1, tm, d), lambda b, i: (b, i, ga_blk)),
                  pl.BlockSpec((1, tm, d), lambda b, i: (b, i, gb_blk)),
                  row(d), vec, vec, vec,
                  full(wa), full(wb), full(wo), full(wrT),
                  pl.BlockSpec((1, d), lambda b, i: (0, 0)), pl.BlockSpec((1, d), lambda b, i: (0, 0))],
        out_specs=[row(d), row(d), pl.BlockSpec((1, ne, tm), lambda b, i: (b, 0, i))],
        out_shape=[jax.ShapeDtypeStruct((bsz, s, d), F32),
                   jax.ShapeDtypeStruct((bsz, s, d), F32),
                   jax.ShapeDtypeStruct((bsz, ne, s), F32)],
        compiler_params=_cparams("arbitrary", "arbitrary"),
        name="postmix",
    )(o_a, o_b, proj, proj, x, gate1, scale2, shift2, wa, wb, wo, wrT, lng.reshape(1, d), lnb.reshape(1, d))


def _route_kernel(lg_ref, bias_ref, ids_ref, wts_ref, *, ne):
    per = ne // N_GROUPS
    s = _sigmoid(lg_ref[0])
    bz = s + bias_ref[...]
    tn = s.shape[1]
    ridx = lax.broadcasted_iota(I32, (per, tn), 0)
    neg_inf = jnp.float32(-jnp.inf)
    gs = []
    for g in range(N_GROUPS):
        blk = bz[g * per:(g + 1) * per]
        m1 = jnp.max(blk, axis=0, keepdims=True)
        first = jnp.min(jnp.where(blk == m1, ridx, per), axis=0, keepdims=True)
        m2 = jnp.max(jnp.where(ridx == first, neg_inf, blk), axis=0, keepdims=True)
        gs.append(m1 + m2)
    emask_rows = []
    for g in range(N_GROUPS):
        rank = jnp.zeros((1, tn), I32)
        for g2 in range(N_GROUPS):
            if g2 == g:
                continue
            beats = (gs[g2] > gs[g]) if g2 > g else (gs[g2] >= gs[g])
            rank = rank + jnp.where(beats, 1, 0)
        emask_rows.append(jnp.broadcast_to(rank < TOPK_GROUPS, (per, tn)))
    emask = jnp.concatenate(emask_rows, axis=0)
    masked = jnp.where(emask, bz, neg_inf)
    eidx = lax.broadcasted_iota(I32, (ne, tn), 0)
    rank = jnp.zeros((ne, tn), I32)
    for e2 in range(ne):
        row = masked[e2:e2 + 1]
        beats = jnp.logical_or(row > masked, jnp.logical_and(row == masked, e2 < eidx))
        rank = rank + jnp.where(beats, 1, 0)
    sel = rank < MOE_TOPK
    denom = jnp.sum(jnp.where(sel, s, 0.0), axis=0, keepdims=True)
    wn = s / denom * ROUTED_SCALE
    ids = []
    wts = []
    for k in range(MOE_TOPK):
        hit = rank == k
        ids.append(jnp.sum(jnp.where(hit, eidx, 0), axis=0, keepdims=True))
        wts.append(jnp.sum(jnp.where(hit, wn, 0.0), axis=0, keepdims=True))
    ids_ref[0] = jnp.concatenate(ids, axis=0)
    wts_ref[0] = jnp.concatenate(wts, axis=0)


def _route(lgT, bias):
    bsz, ne, s = lgT.shape
    tn = _pick(s, (1024, 512, 256, 128))
    kern = functools.partial(_route_kernel, ne=ne)
    return pl.pallas_call(
        kern,
        grid=(bsz, s // tn),
        in_specs=[pl.BlockSpec((1, ne, tn), lambda b, j: (b, 0, j)),
                  pl.BlockSpec((ne, 1), lambda b, j: (0, 0))],
        out_specs=[pl.BlockSpec((1, MOE_TOPK, tn), lambda b, j: (b, 0, j)),
                   pl.BlockSpec((1, MOE_TOPK, tn), lambda b, j: (b, 0, j))],
        out_shape=[jax.ShapeDtypeStruct((bsz, MOE_TOPK, s), I32),
                   jax.ShapeDtypeStruct((bsz, MOE_TOPK, s), F32)],
        compiler_params=_cparams("arbitrary", "arbitrary"),
        name="route",
    )(lgT, bias.reshape(ne, 1))


def _experts_kernel(be_ref, nused_ref, tok_ref, tokn_ref, dst_ref, wt_ref, h_hbm, wg_ref, wu_ref, wd_ref,
                    y_hbm, xbuf, ybuf, gsem, ssem, wgb, wub, wdb, *, blk, nblocks):
    i = pl.program_id(0)
    slot = lax.rem(i, 2)
    nused = nused_ref[0]

    def gather_start(ids_ref, sl):
        def body(r, c):
            pltpu.make_async_copy(h_hbm.at[pl.ds(ids_ref[0, 0, r], 1)], xbuf.at[sl, pl.ds(r, 1)], gsem.at[sl]).start()
            return c
        lax.fori_loop(0, blk, body, 0)

    def gather_wait(sl):
        def body(r, c):
            pltpu.make_async_copy(h_hbm.at[pl.ds(0, 1)], xbuf.at[sl, pl.ds(r, 1)], gsem.at[sl]).wait()
            return c
        lax.fori_loop(0, blk, body, 0)

    def scatter_wait(sl):
        def body(r, c):
            pltpu.make_async_copy(ybuf.at[sl, pl.ds(r, 1)], y_hbm.at[pl.ds(0, 1)], ssem.at[sl]).wait()
            return c
        lax.fori_loop(0, blk, body, 0)

    @pl.when(i == 0)
    def _():
        gather_start(tok_ref, 0)

    @pl.when(i + 1 < nused)
    def _():
        gather_start(tokn_ref, 1 - slot)

    @pl.when(i < nused)
    def _():
        prev_e = be_ref[jnp.maximum(i - 1, 0)]

        @pl.when(jnp.logical_or(i == 0, be_ref[i] != prev_e))
        def _():
            wgb[...] = wg_ref[0].astype(BF16)
            wub[...] = wu_ref[0].astype(BF16)
            wdb[...] = wd_ref[0].astype(BF16)

        gather_wait(slot)
        x = xbuf[slot].astype(BF16)
        hgate = _dot(x, wgb[...])
        hup = _dot(x, wub[...])
        act = (_silu(hgate) * hup).astype(BF16)
        y = _dot(act, wdb[...]) * wt_ref[...]

        @pl.when(i >= 2)
        def _():
            scatter_wait(slot)

        ybuf[slot] = y

        def sbody(r, c):
            pltpu.make_async_copy(ybuf.at[slot, pl.ds(r, 1)], y_hbm.at[pl.ds(dst_ref[0, 0, r], 1)], ssem.at[slot]).start()
            return c
        lax.fori_loop(0, blk, sbody, 0)

    @pl.when(i == nused - 1)
    def _():
        scatter_wait(slot)

        @pl.when(i >= 1)
        def _():
            scatter_wait(1 - slot)


def _experts(h2, be, nused, tok_buf, dst_buf, w_buf, wg, wu, wd, n_out_rows):
    n, d = h2.shape
    ne, _, f = wg.shape
    blk = EXPERT_BLOCK
    nblocks = tok_buf.shape[0] // blk
    tok3 = tok_buf.reshape(nblocks, 1, blk)
    dst3 = dst_buf.reshape(nblocks, 1, blk)
    kern = functools.partial(_experts_kernel, blk=blk, nblocks=nblocks)
    smem_blk = lambda fn: pl.BlockSpec((1, 1, blk), fn, memory_space=pltpu.SMEM)
    grid_spec = pltpu.PrefetchScalarGridSpec(
        num_scalar_prefetch=2,
        grid=(nblocks,),
        in_specs=[smem_blk(lambda i, be, nu: (i, 0, 0)),
                  smem_blk(lambda i, be, nu: (jnp.minimum(i + 1, nblocks - 1), 0, 0)),
                  smem_blk(lambda i, be, nu: (i, 0, 0)),
                  pl.BlockSpec((blk, 1), lambda i, be, nu: (i, 0)),
                  pl.BlockSpec(memory_space=pl.ANY),
                  pl.BlockSpec((1, d, f), lambda i, be, nu: (be[i], 0, 0)),
                  pl.BlockSpec((1, d, f), lambda i, be, nu: (be[i], 0, 0)),
                  pl.BlockSpec((1, f, d), lambda i, be, nu: (be[i], 0, 0))],
        out_specs=pl.BlockSpec(memory_space=pl.ANY),
        scratch_shapes=[pltpu.VMEM((2, blk, d), F32),
                        pltpu.VMEM((2, blk, d), F32),
                        pltpu.SemaphoreType.DMA((2,)),
                        pltpu.SemaphoreType.DMA((2,)),
                        pltpu.VMEM((d, f), BF16),
                        pltpu.VMEM((d, f), BF16),
                        pltpu.VMEM((f, d), BF16)],
    )
    return pl.pallas_call(
        kern,
        grid_spec=grid_spec,
        out_shape=jax.ShapeDtypeStruct((n_out_rows, d), F32),
        compiler_params=_cparams("arbitrary"),
        name="experts",
    )(be, nused, tok3, tok3, dst3, w_buf.reshape(-1, 1), h2, wg, wu, wd)


def _final_kernel(*refs, alpha, topk):
    yc_refs = refs[:topk]
    h2_ref, x1_ref, g2_ref, wg_ref, wu_ref, wd_ref, lng_ref, lnb_ref, o_ref = refs[topk:]
    h = h2_ref[0].astype(BF16)
    y = _dot((_silu(_dot(h, wg_ref[...])) * _dot(h, wu_ref[...])).astype(BF16), wd_ref[...])
    for yc_ref in yc_refs:
        y = y + yc_ref[...]
    o_ref[0] = _ln(alpha * x1_ref[0] + g2_ref[0] * y) * lng_ref[...] + lnb_ref[...]


def _final(ycomb, topk, h2, x1, gate2, wg, wu, wd, lng, lnb, alpha):
    bsz, s, d = x1.shape
    tm = _pick(s, (128,))
    nt = s // tm
    kern = functools.partial(_final_kernel, alpha=alpha, topk=topk)
    row = pl.BlockSpec((1, tm, d), lambda b, i: (b, i, 0))
    full = lambda a: pl.BlockSpec(a.shape, lambda b, i: (0,) * a.ndim, pipeline_mode=pl.Buffered(1))
    yc_specs = [pl.BlockSpec((tm, d), lambda b, i, k=k: (k * bsz * nt + b * nt + i, 0)) for k in range(topk)]
    return pl.pallas_call(
        kern,
        grid=(bsz, nt),
        in_specs=yc_specs + [row, row, pl.BlockSpec((1, 1, d), lambda b, i: (b, 0, 0)),
                             full(wg), full(wu), full(wd),
                             pl.BlockSpec((1, d), lambda b, i: (0, 0)), pl.BlockSpec((1, d), lambda b, i: (0, 0))],
        out_specs=row,
        out_shape=jax.ShapeDtypeStruct((bsz, s, d), F32),
        compiler_params=_cparams("arbitrary", "arbitrary"),
        name="final",
    )(*([ycomb] * topk), h2, x1, gate2, wg, wu, wd, lng.reshape(1, d), lnb.reshape(1, d))


def _dispatch(ids, wts, ne):
    n_tok, k = ids.shape
    n_assign = n_tok * k
    blk = EXPERT_BLOCK
    e_flat = ids.reshape(n_assign)
    order = jnp.argsort(e_flat, stable=True)
    e_sorted = e_flat[order]
    sizes = jnp.zeros((ne,), I32).at[e_flat].add(1)
    padded = (sizes + blk - 1) // blk * blk
    start = jnp.cumsum(sizes) - sizes
    pend = jnp.cumsum(padded)
    pstart = pend - padded
    dest = pstart[e_sorted] + jnp.arange(n_assign, dtype=I32) - start[e_sorted]
    n_rows = -(-(n_assign + ne * (blk - 1)) // blk) * blk
    n_blocks = n_rows // blk
    tok_sorted = (order // k).astype(I32)
    slot_sorted = (order % k).astype(I32)
    tok_buf = jnp.zeros((n_rows,), I32).at[dest].set(tok_sorted)
    dst_buf = (n_assign + jnp.arange(n_rows, dtype=I32)).at[dest].set(slot_sorted * n_tok + tok_sorted)
    w_buf = jnp.zeros((n_rows,), F32).at[dest].set(wts.reshape(n_assign)[order])
    blk_expert = jnp.minimum(
        jnp.searchsorted(pend, jnp.arange(n_blocks, dtype=I32) * blk, side='right'), ne - 1).astype(I32)
    nused = (pend[-1] // blk).astype(I32).reshape(1)
    return tok_buf, dst_buf, w_buf, blk_expert, nused


def _proj_layout(d, qr, kvr, idim, ih, hk, hv):
    src = np.cumsum([0, qr, kvr, idim, ih, hk, hk, hv, hv, d, d])
    wa = -(-(qr + kvr + idim + ih) // LANES) * LANES
    wa = max(wa, 1 << (wa - 1).bit_length())
    pieces = [("a", wa, (int(src[0]), int(src[4]))),
              ("hq", hk, (int(src[4]), int(src[5]))), ("hf", hk, (int(src[5]), int(src[6]))),
              ("hi", hv, (int(src[6]), int(src[7]))), ("hg", hv, (int(src[7]), int(src[8]))),
              ("ga", d, (int(src[8]), int(src[9]))), ("gb", d, (int(src[9]), int(src[10])))]
    pieces.sort(key=lambda p: -p[1])
    off = 0
    layout = {}
    for name, wdt, rng in pieces:
        assert off % wdt == 0
        layout[name] = (off, wdt, rng)
        off += wdt
    return layout, off


def kernel(x, c, rpb_table, hgrn_lb_logits, ada_w, ada_b, w_in, q_norm_g, kv_norm_g, w_uq, w_uk, w_uv, w_qidx,
           idx_k_norm_g, idx_k_norm_b, hgrn_out_norm_g, w_branch_a, w_branch_b, w_o, ln1_g, ln1_b, w_router,
           router_bias, w_exp_gate, w_exp_up, w_exp_down, w_sh_gate, w_sh_up, w_sh_down, ln2_g, ln2_b):
    bsz, s, d = x.shape
    depth = ada_w.shape[0]
    qr = w_uq.shape[1]
    ah, kvr, qk = w_uk.shape[1], w_uk.shape[2], w_uk.shape[3]
    idim = idx_k_norm_g.shape[1]
    ih = w_qidx.shape[2] // idim
    dv = hgrn_out_norm_g.shape[1]
    nh = w_branch_b.shape[1] // dv
    dk = hgrn_lb_logits.shape[1] // nh
    ne = w_router.shape[2]
    topk = min(IDX_TOPK, s // 4)
    alpha = float((2 * depth) ** 0.25)
    n_tok = bsz * s

    lower_bounds = jnp.cumsum(jax.nn.softmax(hgrn_lb_logits.astype(F32), axis=0), axis=0)
    layout, wtot = _proj_layout(d, qr, kvr, idim, ih, nh * dk, nh * dv)

    for l in range(depth):
        mod = _ada(c, ada_w[l], ada_b[l])[:, None, :]
        shift1, scale1, gate1, shift2, scale2, gate2 = jnp.split(mod, 6, axis=-1)

        cols = []
        for off, wdt, (lo, hi) in sorted(layout.values()):
            cols.append(w_in[l][:, lo:hi].astype(BF16))
            if wdt > hi - lo:
                cols.append(jnp.zeros((d, wdt - (hi - lo)), BF16))
        proj = _inproj(x, scale1, shift1, jnp.concatenate(cols, axis=1))
        blk = lambda name: layout[name][0] // layout[name][1]

        q_lat, q_idx, w_idx, k_idx, kv_lat = _prep(
            proj, blk("a"), layout["a"][1], (qr, kvr, idim, ih, ah, qk), q_norm_g[l], kv_norm_g[l],
            idx_k_norm_g[l], idx_k_norm_b[l], w_uq[l].astype(BF16),
            jnp.swapaxes(w_uk[l], 1, 2).astype(BF16), w_qidx[l].astype(BF16))
        o_a = _dsa(q_idx, w_idx, q_lat, jnp.swapaxes(k_idx, 1, 2), jnp.swapaxes(kv_lat, 1, 2), kv_lat,
                   rpb_table, w_uv[l].astype(BF16), topk)

        o_b = _hgrn(proj, (blk("hq"), blk("hf"), blk("hi"), blk("hg")), lower_bounds[l],
                    hgrn_out_norm_g[l], nh, dk, dv)

        x1, h2, lgT = _postmix(o_a, o_b, proj, blk("ga"), blk("gb"), x, gate1, scale2, shift2,
                               w_branch_a[l].astype(BF16), w_branch_b[l].astype(BF16), w_o[l].astype(BF16),
                               jnp.swapaxes(w_router[l], 0, 1).astype(BF16), ln1_g[l], ln1_b[l], alpha)

        idsT, wtsT = _route(lgT, router_bias[l])
        ids = jnp.swapaxes(idsT, 1, 2).reshape(n_tok, MOE_TOPK)
        wts = jnp.swapaxes(wtsT, 1, 2).reshape(n_tok, MOE_TOPK)
        tok_buf, dst_buf, w_buf, blk_expert, nused = _dispatch(ids, wts, ne)
        ycomb = _experts(h2.reshape(n_tok, d), blk_expert, nused, tok_buf, dst_buf, w_buf,
                         w_exp_gate[l], w_exp_up[l], w_exp_down[l], n_tok * MOE_TOPK + tok_buf.shape[0])
        x = _final(ycomb, MOE_TOPK, h2, x1, gate2, w_sh_gate[l].astype(BF16), w_sh_up[l].astype(BF16),
                   w_sh_down[l].astype(BF16), ln2_g[l], ln2_b[l], alpha)
    return x
```

```python
import functools
import math

import numpy as np
import jax
import jax.numpy as jnp
from jax import lax
from jax.experimental import pallas as pl
from jax.experimental.pallas import tpu as pltpu

F32 = jnp.float32
BF16 = jnp.bfloat16
I32 = jnp.int32

EPS = 1e-6
IDX_TOPK = 256
RPB_MAX_DIST = 128
MOE_TOPK = 8
N_GROUPS = 8
TOPK_GROUPS = 4
ROUTED_SCALE = 2.5
EXPERT_BLOCK = 256

V7X_VMEM_LIMIT_BYTES = 56 * 1024 * 1024
LANES = 128
INT_MIN = -2 ** 31
NEG_BIG = -1e30


def _cparams(*sem):
    return pltpu.CompilerParams(dimension_semantics=tuple(sem), vmem_limit_bytes=V7X_VMEM_LIMIT_BYTES)


def _pick(n, prefs):
    for p in prefs:
        if n % p == 0:
            return p
    return n


def _sigmoid(v):
    return 1.0 / (1.0 + jnp.exp(-v))


def _silu(v):
    return v * _sigmoid(v)


def _ln(v):
    mu = jnp.mean(v, axis=-1, keepdims=True)
    d = v - mu
    var = jnp.mean(d * d, axis=-1, keepdims=True)
    return d * lax.rsqrt(var + EPS)


def _rms(v):
    return v * lax.rsqrt(jnp.mean(v * v, axis=-1, keepdims=True) + EPS)


def _dot(a, b):
    return jnp.dot(a, b, preferred_element_type=F32)


def _dot_nt(a, b):
    return lax.dot_general(a, b, (((1,), (1,)), ((), ())), preferred_element_type=F32)


def _dot_tn(a, b):
    return lax.dot_general(a, b, (((0,), (0,)), ((), ())), preferred_element_type=F32)


def _ada_kernel(c_ref, w_ref, b_ref, o_ref):
    c = c_ref[...]
    o_ref[...] = _dot(_silu(c).astype(BF16), w_ref[...].astype(BF16)) + b_ref[...]


def _ada(c, w, b):
    bsz, d = c.shape
    n = w.shape[1]
    rows = 8
    cp = jnp.zeros((rows, d), F32).at[:bsz].set(c)
    tn = _pick(n, (1024, 512, 256, 128))
    out = pl.pallas_call(
        _ada_kernel,
        grid=(n // tn,),
        in_specs=[pl.BlockSpec((rows, d), lambda j: (0, 0)),
                  pl.BlockSpec((d, tn), lambda j: (0, j)),
                  pl.BlockSpec((1, tn), lambda j: (0, j))],
        out_specs=pl.BlockSpec((rows, tn), lambda j: (0, j)),
        out_shape=jax.ShapeDtypeStruct((rows, n), F32),
        compiler_params=_cparams("arbitrary"),
        name="ada",
    )(cp, w, b.reshape(1, n))
    return out[:bsz]


def _inproj_kernel(x_ref, sc_ref, sh_ref, w_ref, o_ref, h_ref):
    @pl.when(pl.program_id(2) == 0)
    def _():
        h = _ln(x_ref[0]) * (1.0 + sc_ref[0]) + sh_ref[0]
        h_ref[...] = h.astype(BF16)

    o_ref[0] = _dot(h_ref[...], w_ref[...]).astype(o_ref.dtype)


def _inproj(x, scale, shift, w):
    bsz, s, d = x.shape
    n = w.shape[1]
    tm = _pick(s, (1024, 512, 256, 128))
    tn = _pick(n, (1024, 512, 256, 128))
    return pl.pallas_call(
        _inproj_kernel,
        grid=(bsz, s // tm, n // tn),
        in_specs=[pl.BlockSpec((1, tm, d), lambda b, i, j: (b, i, 0)),
                  pl.BlockSpec((1, 1, d), lambda b, i, j: (b, 0, 0)),
                  pl.BlockSpec((1, 1, d), lambda b, i, j: (b, 0, 0)),
                  pl.BlockSpec((d, tn), lambda b, i, j: (0, j))],
        out_specs=pl.BlockSpec((1, tm, tn), lambda b, i, j: (b, i, j)),
        out_shape=jax.ShapeDtypeStruct((bsz, s, n), BF16),
        scratch_shapes=[pltpu.VMEM((tm, d), BF16)],
        compiler_params=_cparams("arbitrary", "arbitrary", "arbitrary"),
        name="inproj",
    )(x, scale, shift, w)


def _prep_kernel(a_ref, qg_ref, kvg_ref, ikg_ref, ikb_ref, wuqT_ref, wuk_ref, wqiT_ref,
                 qlatT_ref, qidxT_ref, widxT_ref, kidx_ref, kv_ref, *, qr, kvr, idim, ih, ah, qk):
    a = a_ref[0]
    af = a.astype(F32)
    cq = (_rms(af[:, :qr]) * qg_ref[...]).astype(BF16)
    ckv = af[:, qr:qr + kvr]
    ki = af[:, qr + kvr:qr + kvr + idim]
    kv_ref[0] = (_rms(ckv) * kvg_ref[...]).astype(BF16)
    kidx_ref[0] = (_ln(ki) * ikg_ref[...] + ikb_ref[...]).astype(BF16)
    tail = a[:, qr + kvr:qr + kvr + LANES]
    eye = (lax.broadcasted_iota(I32, (LANES, LANES), 0) == lax.broadcasted_iota(I32, (LANES, LANES), 1))
    tailT = _dot_nt(jnp.where(eye, 1.0, 0.0).astype(BF16), tail)
    widxT_ref[0] = tailT[idim:idim + ih] * float((ih * idim) ** -0.5)
    tm = a.shape[0]
    qT = _dot_nt(wuqT_ref[...], cq)
    for h in range(ah):
        qh = qT[h * qk:(h + 1) * qk].astype(BF16)
        qlatT_ref[0, 0, :, h * tm:(h + 1) * tm] = (_dot(wuk_ref[h], qh) * float(qk ** -0.5)).astype(BF16)
    qiT = _dot_nt(wqiT_ref[...], cq)
    for h in range(ih):
        qidxT_ref[0, 0, :, h * tm:(h + 1) * tm] = qiT[h * idim:(h + 1) * idim].astype(BF16)


def _prep(proj, a_blk, wa, dims, q_norm_g, kv_norm_g, ikg, ikb, w_uqT, w_uk, w_qidxT, tm):
    bsz, s, _ = proj.shape
    qr, kvr, idim, ih, ah, qk = dims
    assert idim + ih <= LANES and qr + kvr + LANES <= wa
    kern = functools.partial(_prep_kernel, qr=qr, kvr=kvr, idim=idim, ih=ih, ah=ah, qk=qk)
    full = lambda shape: pl.BlockSpec(shape, lambda b, i: (0,) * len(shape))
    return pl.pallas_call(
        kern,
        grid=(bsz, s // tm),
        in_specs=[pl.BlockSpec((1, tm, wa), lambda b, i: (b, i, a_blk)),
                  full((1, qr)), full((1, kvr)), full((1, idim)), full((1, idim)),
                  full(w_uqT.shape), full(w_uk.shape), full(w_qidxT.shape)],
        out_specs=[pl.BlockSpec((1, 1, kvr, ah * tm), lambda b, i: (b, i, 0, 0)),
                   pl.BlockSpec((1, 1, idim, ih * tm), lambda b, i: (b, i, 0, 0)),
                   pl.BlockSpec((1, ih, tm), lambda b, i: (b, 0, i)),
                   pl.BlockSpec((1, tm, idim), lambda b, i: (b, i, 0)),
                   pl.BlockSpec((1, tm, kvr), lambda b, i: (b, i, 0))],
        out_shape=[jax.ShapeDtypeStruct((bsz, s // tm, kvr, ah * tm), BF16),
                   jax.ShapeDtypeStruct((bsz, s // tm, idim, ih * tm), BF16),
                   jax.ShapeDtypeStruct((bsz, ih, s), F32),
                   jax.ShapeDtypeStruct((bsz, s, idim), BF16),
                   jax.ShapeDtypeStruct((bsz, s, kvr), BF16)],
        compiler_params=_cparams("arbitrary", "arbitrary"),
        name="prep",
    )(proj, q_norm_g.reshape(1, qr), kv_norm_g.reshape(1, kvr), ikg.reshape(1, idim), ikb.reshape(1, idim),
      w_uqT, w_uk, w_qidxT)


def _dsa_kernel(qidxT_ref, widxT_ref, qlatT_ref, kidx_ref, kv_ref, kvT_ref, t0_ref, t1_ref, wuvT_ref,
                o_ref, keys_ref, s_ref, p_ref, m_ref, l_ref, a_ref, acc_ref, *, qb_sz, topk, ih, ah):
    QB = qb_sz
    qb = pl.program_id(1)
    nchunks = qb + 1
    qpos = qb * QB + lax.broadcasted_iota(I32, (1, QB), 1)
    kpos = lax.broadcasted_iota(I32, (QB, 1), 0)
    wT = widxT_ref[0]
    hcols = lambda h: slice(h * QB, (h + 1) * QB)

    def score_chunk(kc, carry):
        off = pl.multiple_of(kc * QB, QB)
        s_ref[:, :ih * QB] = _dot(kidx_ref[0, pl.ds(off, QB), :], qidxT_ref[0, 0])
        acc = jnp.zeros((QB, QB), F32)
        for h in range(ih):
            acc = acc + wT[h:h + 1] * jnp.maximum(s_ref[:, hcols(h)], 0.0)
        bits = pltpu.bitcast(acc, I32)
        skey = bits ^ ((bits >> 31) & 0x7FFFFFFF)
        causal = (off + kpos) <= qpos
        keys_ref[pl.ds(off, QB), :] = jnp.where(causal, skey, INT_MIN)
        return carry

    lax.fori_loop(0, nchunks, score_chunk, 0)

    def count_ge(cand):
        def body(kc, cnt):
            off = pl.multiple_of(kc * QB, QB)
            hit = jnp.where(keys_ref[pl.ds(off, QB), :] >= cand, 1, 0)
            return cnt + jnp.sum(hit.reshape(QB // 8, 8, QB), axis=0)

        cnt = lax.fori_loop(0, nchunks, body, jnp.zeros((8, QB), I32))
        return jnp.sum(cnt.astype(F32), axis=0, keepdims=True)

    def bis_cond(st):
        bit, _, done = st
        return jnp.logical_and(bit >= 0, jnp.min(done) < 0.5)

    def bis_body(st):
        bit, c, done = st
        cand = c + jnp.left_shift(jnp.int32(1), bit)
        cnt = count_ge(cand)
        take = jnp.logical_and(cnt >= float(topk), done < 0.5)
        c = jnp.where(take, cand, c)
        done = jnp.where(jnp.logical_and(take, cnt == float(topk)), 1.0, done)
        return bit - 1, c, done

    _, thr, _ = lax.while_loop(
        bis_cond, bis_body,
        (jnp.int32(31), jnp.full((1, QB), INT_MIN, I32), jnp.zeros((1, QB), F32)))

    m_ref[...] = jnp.full(m_ref.shape, NEG_BIG, F32)
    l_ref[...] = jnp.zeros(l_ref.shape, F32)
    acc_ref[...] = jnp.zeros(acc_ref.shape, F32)

    def attn_chunk(off, bias_ref, diag):
        sel = keys_ref[pl.ds(off, QB), :] >= thr
        if diag:
            sel = jnp.logical_and(sel, (off + kpos) <= qpos)
        madd = jnp.where(sel, 0.0, NEG_BIG)
        s_ref[:, :ah * QB] = _dot(kv_ref[0, pl.ds(off, QB), :], qlatT_ref[0, 0])
        for h in range(ah):
            s = s_ref[:, hcols(h)] + madd
            if bias_ref is not None:
                s = s + bias_ref[:, hcols(h)]
            m_prev = m_ref[:, hcols(h)]
            m_new = jnp.maximum(m_prev, jnp.max(s, axis=0, keepdims=True))
            alpha = jnp.exp(m_prev - m_new)
            p = jnp.exp(s - m_new)
            l_ref[:, hcols(h)] = alpha * l_ref[:, hcols(h)] + jnp.sum(p, axis=0, keepdims=True)
            p_ref[:, hcols(h)] = p.astype(BF16)
            a_ref[:, hcols(h)] = alpha
            m_ref[:, hcols(h)] = m_new
        acc_ref[...] = a_ref[...] * acc_ref[...] + _dot(kvT_ref[0, :, pl.ds(off, QB)], p_ref[...])

    def far_body(kc, carry):
        attn_chunk(pl.multiple_of(kc * QB, QB), None, False)
        return carry

    lax.fori_loop(0, qb - 1, far_body, 0)

    @pl.when(qb >= 1)
    def _():
        attn_chunk(pl.multiple_of((qb - 1) * QB, QB), t1_ref, False)

    attn_chunk(pl.multiple_of(qb * QB, QB), t0_ref, True)

    outs = []
    for h in range(ah):
        o = (acc_ref[:, hcols(h)] / l_ref[:, hcols(h)]).astype(BF16)
        outs.append(_dot(wuvT_ref[h], o))
    o_ref[0] = jnp.concatenate(outs, axis=0).T.astype(o_ref.dtype)


def _t5_bucket_np(d, nbuckets):
    max_exact = nbuckets // 2
    dd = np.maximum(d, 1).astype(np.float32)
    large = max_exact + (np.log(dd / np.float32(max_exact)) / np.float32(math.log(RPB_MAX_DIST / max_exact))
                         * np.float32(nbuckets - max_exact)).astype(np.int32)
    large = np.minimum(large, nbuckets - 1)
    return np.where(d < max_exact, d, large).astype(np.int32)


def _dsa(q_idxT, w_idxT, q_latT, k_idx, kv_lat, kv_latT, rpb_table, w_uvT, topk, QB):
    bsz, nqb, idim, ihq = q_idxT.shape
    c, ahq = q_latT.shape[2], q_latT.shape[3]
    ih, ah = ihq // QB, ahq // QB
    s = nqb * QB
    vd = w_uvT.shape[1]
    nb = rpb_table.shape[0]
    assert QB >= RPB_MAX_DIST
    j = np.arange(QB)[:, None]
    i = np.arange(QB)[None, :]
    d0 = np.maximum(i - j, 0)
    d1 = QB + i - j
    rel = rpb_table.astype(F32) - rpb_table[nb - 1].astype(F32)[None, :]
    t0 = jnp.transpose(rel[_t5_bucket_np(d0, nb)], (0, 2, 1)).reshape(QB, ah * QB)
    t1 = jnp.transpose(rel[_t5_bucket_np(d1, nb)], (0, 2, 1)).reshape(QB, ah * QB)
    kern = functools.partial(_dsa_kernel, qb_sz=QB, topk=topk, ih=ih, ah=ah)
    const = lambda shape: pl.BlockSpec(shape, lambda b, i: (0,) * len(shape), pipeline_mode=pl.Buffered(1))
    hw = max(ih, ah) * QB
    return pl.pallas_call(
        kern,
        grid=(bsz, nqb),
        in_specs=[pl.BlockSpec((1, 1, idim, ih * QB), lambda b, i: (b, i, 0, 0)),
                  pl.BlockSpec((1, ih, QB), lambda b, i: (b, 0, i)),
                  pl.BlockSpec((1, 1, c, ah * QB), lambda b, i: (b, i, 0, 0)),
                  pl.BlockSpec((1, s, idim), lambda b, i: (b, 0, 0)),
                  pl.BlockSpec((1, s, c), lambda b, i: (b, 0, 0)),
                  pl.BlockSpec((1, c, s), lambda b, i: (b, 0, 0)),
                  const((QB, ah * QB)), const((QB, ah * QB)), const(w_uvT.shape)],
        out_specs=pl.BlockSpec((1, QB, ah * vd), lambda b, i: (b, i, 0)),
        out_shape=jax.ShapeDtypeStruct((bsz, s, ah * vd), BF16),
        scratch_shapes=[pltpu.VMEM((s, QB), I32),
                        pltpu.VMEM((QB, hw), F32),
                        pltpu.VMEM((QB, ah * QB), BF16),
                        pltpu.VMEM((1, ah * QB), F32),
                        pltpu.VMEM((1, ah * QB), F32),
                        pltpu.VMEM((1, ah * QB), F32),
                        pltpu.VMEM((c, ah * QB), F32)],
        compiler_params=_cparams("arbitrary", "arbitrary"),
        name="dsa",
    )(q_idxT, w_idxT, q_latT, k_idx, kv_lat, kv_latT, t0, t1, w_uvT)


HGRN_CHUNK = 64
HGRN_SUB = 16
HGRN_EXP_CLAMP = 80.0


def _hgrn_kernel(hq_ref, hf_ref, hi_ref, hg_ref, lb_ref, g_ref, o_ref, st_ref, *, nh, dk, dv, tt):
    C, SUB = HGRN_CHUNK, HGRN_SUB

    @pl.when(pl.program_id(1) == 0)
    def _():
        st_ref[...] = jnp.zeros(st_ref.shape, F32)

    r = lax.broadcasted_iota(I32, (C, C), 0)
    cc = lax.broadcasted_iota(I32, (C, C), 1)
    tri_mask = r >= cc
    tri = jnp.where(tri_mask, 1.0, 0.0).astype(BF16)
    g = g_ref[...]

    def head(h, carry):
        ko = pl.multiple_of(h * dk, dk)
        vo = pl.multiple_of(h * dv, dv)
        lb = lb_ref[:, pl.ds(ko, dk)]
        stT = st_ref[h]
        for c in range(tt // C):
            rows = pl.ds(c * C, C)
            hf = hf_ref[0, rows, pl.ds(ko, dk)].astype(F32)
            f = lb + (1.0 - lb) * _sigmoid(hf)
            lf = jnp.log(f)
            t1 = lf.astype(BF16)
            r1 = lf - t1.astype(F32)
            t2 = r1.astype(BF16)
            t3 = (r1 - t2.astype(F32)).astype(BF16)
            b = _dot(tri, t1) + _dot(tri, t2) + _dot(tri, t3)
            kk = 1.0 - f
            hq = hq_ref[0, rows, pl.ds(ko, dk)].astype(F32)
            q = _silu(hq) * float(dk ** -0.5)
            v = hi_ref[0, rows, pl.ds(vo, dv)]
            o_inter = _dot_nt((q * jnp.exp(b)).astype(BF16), stT.astype(BF16))
            parts = []
            for i in range(C // SUB):
                lo, n = i * SUB, (i + 1) * SUB
                bi = b[lo - 1:lo] if i > 0 else jnp.zeros((1, dk), F32)
                qs = (q[lo:n] * jnp.exp(b[lo:n] - bi)).astype(BF16)
                ks = (kk * jnp.exp(jnp.minimum(bi - b, HGRN_EXP_CLAMP))).astype(BF16)
                parts.append(_dot_nt(qs, ks))
            att = jnp.where(tri_mask, jnp.concatenate(parts, axis=0), 0.0)
            o = o_inter + _dot(att.astype(BF16), v)
            b_last = b[C - 1:C]
            k_dec = (kk * jnp.exp(b_last - b)).astype(BF16)
            stT = stT * jnp.exp(b_last) + _dot_tn(v, k_dec)
            hg = hg_ref[0, rows, pl.ds(vo, dv)].astype(F32)
            o_ref[0, rows, pl.ds(vo, dv)] = (_rms(o) * g * _silu(hg)).astype(o_ref.dtype)
        st_ref[h] = stT
        return carry

    lax.fori_loop(0, nh, head, 0)


def _hgrn(proj, blks, lb, g, nh, dk, dv):
    bsz, s, _ = proj.shape
    tt = _pick(s, (256, 128, 64))
    kern = functools.partial(_hgrn_kernel, nh=nh, dk=dk, dv=dv, tt=tt)
    col = lambda blk, wdt: pl.BlockSpec((1, tt, wdt), lambda b, i: (b, i, blk))
    return pl.pallas_call(
        kern,
        grid=(bsz, s // tt),
        in_specs=[col(blks[0], nh * dk), col(blks[1], nh * dk), col(blks[2], nh * dv), col(blks[3], nh * dv),
                  pl.BlockSpec((1, nh * dk), lambda b, i: (0, 0)),
                  pl.BlockSpec((1, dv), lambda b, i: (0, 0))],
        out_specs=pl.BlockSpec((1, tt, nh * dv), lambda b, i: (b, i, 0)),
        out_shape=jax.ShapeDtypeStruct((bsz, s, nh * dv), BF16),
        scratch_shapes=[pltpu.VMEM((nh, dv, dk), F32)],
        compiler_params=_cparams("arbitrary", "arbitrary"),
        name="hgrn",
    )(proj, proj, proj, proj, lb.reshape(1, nh * dk), g.reshape(1, dv))


def _postmix_kernel(oa_ref, ob_ref, ga_ref, gb_ref, x_ref, g1_ref, sc2_ref, sh2_ref,
                    wa_ref, wb_ref, wo_ref, wrT_ref, lng_ref, lnb_ref,
                    x1_ref, h2_ref, lgT_ref, *, alpha):
    ya = _dot(oa_ref[0], wa_ref[...])
    yb = _dot(ob_ref[0], wb_ref[...])
    mix = _sigmoid(ga_ref[0].astype(F32)) * ya + _sigmoid(gb_ref[0].astype(F32)) * yb
    mixed = _dot(mix.astype(BF16), wo_ref[...])
    x1 = _ln(alpha * x_ref[0] + g1_ref[0] * mixed) * lng_ref[...] + lnb_ref[...]
    x1_ref[0] = x1
    h2 = _ln(x1) * (1.0 + sc2_ref[0]) + sh2_ref[0]
    h2_ref[0] = h2
    lgT_ref[0] = _dot_nt(wrT_ref[...], h2.astype(BF16))


def _postmix(o_a, o_b, proj, ga_blk, gb_blk, x, gate1, scale2, shift2, wa, wb, wo, wrT, lng, lnb, alpha):
    bsz, s, d = x.shape
    ne = wrT.shape[0]
    tm = _pick(s, (256, 128))
    kern = functools.partial(_postmix_kernel, alpha=alpha)
    row = lambda wdt: pl.BlockSpec((1, tm, wdt), lambda b, i: (b, i, 0))
    vec = pl.BlockSpec((1, 1, d), lambda b, i: (b, 0, 0))
    full = lambda a: pl.BlockSpec(a.shape, lambda b, i: (0,) * a.ndim, pipeline_mode=pl.Buffered(1))
    return pl.pallas_call(
        kern,
        grid=(bsz, s // tm),
        in_specs=[row(o_a.shape[2]), row(o_b.shape[2]),
                  pl.BlockSpec((1, tm, d), lambda b, i: (b, i, ga_blk)),
                  pl.BlockSpec((1, tm, d), lambda b, i: (b, i, gb_blk)),
                  row(d), vec, vec, vec,
                  full(wa), full(wb), full(wo), full(wrT),
                  pl.BlockSpec((1, d), lambda b, i: (0, 0)), pl.BlockSpec((1, d), lambda b, i: (0, 0))],
        out_specs=[row(d), row(d), pl.BlockSpec((1, ne, tm), lambda b, i: (b, 0, i))],
        out_shape=[jax.ShapeDtypeStruct((bsz, s, d), F32),
                   jax.ShapeDtypeStruct((bsz, s, d), F32),
                   jax.ShapeDtypeStruct((bsz, ne, s), F32)],
        compiler_params=_cparams("arbitrary", "arbitrary"),
        name="postmix",
    )(o_a, o_b, proj, proj, x, gate1, scale2, shift2, wa, wb, wo, wrT, lng.reshape(1, d), lnb.reshape(1, d))


def _route_kernel(lg_ref, bias_ref, ids_ref, wts_ref, rnk_ref, sizes_ref, upper_ref, carry_ref, *, ne):
    first = jnp.logical_and(pl.program_id(0) == 0, pl.program_id(1) == 0)
    tn = lg_ref.shape[2]

    @pl.when(first)
    def _():
        carry_ref[...] = jnp.zeros(carry_ref.shape, F32)
        r_ = lax.broadcasted_iota(I32, (tn, tn), 0)
        c_ = lax.broadcasted_iota(I32, (tn, tn), 1)
        upper_ref[...] = jnp.where(r_ < c_, 1.0, 0.0).astype(BF16)

    per = ne // N_GROUPS
    s = _sigmoid(lg_ref[0])
    bz = s + bias_ref[...]
    ridx = lax.broadcasted_iota(I32, (per, tn), 0)
    neg_inf = jnp.float32(-jnp.inf)
    gs = []
    for g in range(N_GROUPS):
        blk = bz[g * per:(g + 1) * per]
        m1 = jnp.max(blk, axis=0, keepdims=True)
        first_hit = jnp.min(jnp.where(blk == m1, ridx, per), axis=0, keepdims=True)
        m2 = jnp.max(jnp.where(ridx == first_hit, neg_inf, blk), axis=0, keepdims=True)
        gs.append(m1 + m2)
    emask_rows = []
    for g in range(N_GROUPS):
        rank = jnp.zeros((1, tn), I32)
        for g2 in range(N_GROUPS):
            if g2 == g:
                continue
            beats = (gs[g2] > gs[g]) if g2 > g else (gs[g2] >= gs[g])
            rank = rank + jnp.where(beats, 1, 0)
        emask_rows.append(jnp.broadcast_to(rank < TOPK_GROUPS, (per, tn)))
    emask = jnp.concatenate(emask_rows, axis=0)
    masked = jnp.where(emask, bz, neg_inf)
    eidx = lax.broadcasted_iota(I32, (ne, tn), 0)
    rank = jnp.zeros((ne, tn), I32)
    for e2 in range(ne):
        row = masked[e2:e2 + 1]
        beats = jnp.logical_or(row > masked, jnp.logical_and(row == masked, e2 < eidx))
        rank = rank + jnp.where(beats, 1, 0)
    sel = rank < MOE_TOPK
    sel01 = jnp.where(sel, 1.0, 0.0)
    denom = jnp.sum(jnp.where(sel, s, 0.0), axis=0, keepdims=True)
    wn = s / denom * ROUTED_SCALE
    before = _dot(sel01.astype(BF16), upper_ref[...]) + carry_ref[:, 0:1]
    ids, wts, rnk = [], [], []
    for k in range(MOE_TOPK):
        hit = rank == k
        ids.append(jnp.sum(jnp.where(hit, eidx, 0), axis=0, keepdims=True))
        wts.append(jnp.sum(jnp.where(hit, wn, 0.0), axis=0, keepdims=True))
        rnk.append(jnp.sum(jnp.where(hit, before, 0.0), axis=0, keepdims=True))
    ids_ref[0] = jnp.concatenate(ids, axis=0)
    wts_ref[0] = jnp.concatenate(wts, axis=0)
    rnk_ref[0] = jnp.concatenate(rnk, axis=0).astype(I32)
    carry_ref[...] = carry_ref[...] + jnp.sum(sel01, axis=1, keepdims=True)
    sizes_ref[...] = carry_ref[...]


def _route(lgT, bias):
    bsz, ne, s = lgT.shape
    tn = _pick(s, (1024, 512, 256, 128))
    kern = functools.partial(_route_kernel, ne=ne)
    slot = pl.BlockSpec((1, MOE_TOPK, tn), lambda b, j: (b, 0, j))
    return pl.pallas_call(
        kern,
        grid=(bsz, s // tn),
        in_specs=[pl.BlockSpec((1, ne, tn), lambda b, j: (b, 0, j)),
                  pl.BlockSpec((ne, 1), lambda b, j: (0, 0))],
        out_specs=[slot, slot, slot, pl.BlockSpec((ne, LANES), lambda b, j: (0, 0))],
        out_shape=[jax.ShapeDtypeStruct((bsz, MOE_TOPK, s), I32),
                   jax.ShapeDtypeStruct((bsz, MOE_TOPK, s), F32),
                   jax.ShapeDtypeStruct((bsz, MOE_TOPK, s), I32),
                   jax.ShapeDtypeStruct((ne, LANES), F32)],
        scratch_shapes=[pltpu.VMEM((tn, tn), BF16), pltpu.VMEM((ne, LANES), F32)],
        compiler_params=_cparams("arbitrary", "arbitrary"),
        name="route",
    )(lgT, bias.reshape(ne, 1))


def _dest_kernel(pstart_ref, ids_ref, rnk_ref, o_ref, *, ne):
    ids = ids_ref[0]
    base = jnp.zeros(ids.shape, I32)
    for e in range(ne):
        base = jnp.where(ids == e, pstart_ref[e], base)
    o_ref[0] = base + rnk_ref[0]


def _dest(pstart, ids, rnk, ne):
    bsz, k, s = ids.shape
    tn = _pick(s, (2048, 1024, 512, 256, 128))
    blk = lambda: pl.BlockSpec((1, k, tn), lambda b, j, ps: (b, 0, j))
    return pl.pallas_call(
        functools.partial(_dest_kernel, ne=ne),
        grid_spec=pltpu.PrefetchScalarGridSpec(num_scalar_prefetch=1, grid=(bsz, s // tn),
                                               in_specs=[blk(), blk()], out_specs=blk()),
        out_shape=jax.ShapeDtypeStruct((bsz, k, s), I32),
        compiler_params=_cparams("arbitrary", "arbitrary"),
        name="dest",
    )(pstart, ids, rnk)


def _dispatch_kernel(dst_ref, h_ref, xs_hbm, sem, *, tm, topk):
    def start(t, c):
        for k in range(topk):
            pltpu.make_async_copy(h_ref.at[pl.ds(t, 1)], xs_hbm.at[pl.ds(dst_ref[0, k, t], 1)], sem.at[0]).start()
        return c

    lax.fori_loop(0, tm, start, 0)

    def wait(t, c):
        for k in range(topk):
            pltpu.make_async_copy(h_ref.at[pl.ds(t, 1)], xs_hbm.at[pl.ds(0, 1)], sem.at[0]).wait()
        return c

    lax.fori_loop(0, tm, wait, 0)


def _dispatch(dest, h2, n_rows):
    n, d = h2.shape
    nt, topk, tm = dest.shape
    return pl.pallas_call(
        functools.partial(_dispatch_kernel, tm=tm, topk=topk),
        grid=(nt,),
        in_specs=[pl.BlockSpec((1, topk, tm), lambda i: (i, 0, 0), memory_space=pltpu.SMEM),
                  pl.BlockSpec((tm, d), lambda i: (i, 0))],
        out_specs=pl.BlockSpec(memory_space=pl.ANY),
        out_shape=jax.ShapeDtypeStruct((n_rows, d), h2.dtype),
        scratch_shapes=[pltpu.SemaphoreType.DMA((1,))],
        compiler_params=_cparams("arbitrary"),
        name="dispatch",
    )(dest, h2)


def _experts_kernel(be_ref, nv_ref, x_ref, wg_ref, wu_ref, wd_ref, y_ref, wgb, wub, wdb, *, blk):
    i = pl.program_id(0)
    prev_e = be_ref[jnp.maximum(i - 1, 0)]

    @pl.when(jnp.logical_or(i == 0, be_ref[i] != prev_e))
    def _():
        wgb[...] = wg_ref[0].astype(BF16)
        wub[...] = wu_ref[0].astype(BF16)
        wdb[...] = wd_ref[0].astype(BF16)

    @pl.when(nv_ref[i] > 0)
    def _():
        rows = lax.broadcasted_iota(I32, (blk, 1), 0)
        x = jnp.where(rows < nv_ref[i], x_ref[...], 0.0).astype(BF16)
        act = (_silu(_dot(x, wgb[...])) * _dot(x, wub[...])).astype(BF16)
        y_ref[...] = _dot(act, wdb[...])


def _experts(xs, be, nvalid, wg, wu, wd):
    n_rows, d = xs.shape
    ne, _, f = wg.shape
    blk = EXPERT_BLOCK
    nblocks = n_rows // blk
    grid_spec = pltpu.PrefetchScalarGridSpec(
        num_scalar_prefetch=2,
        grid=(nblocks,),
        in_specs=[pl.BlockSpec((blk, d), lambda i, be, nv: (i, 0)),
                  pl.BlockSpec((1, d, f), lambda i, be, nv: (be[i], 0, 0)),
                  pl.BlockSpec((1, d, f), lambda i, be, nv: (be[i], 0, 0)),
                  pl.BlockSpec((1, f, d), lambda i, be, nv: (be[i], 0, 0))],
        out_specs=pl.BlockSpec((blk, d), lambda i, be, nv: (i, 0)),
        scratch_shapes=[pltpu.VMEM((d, f), BF16), pltpu.VMEM((d, f), BF16), pltpu.VMEM((f, d), BF16)],
    )
    return pl.pallas_call(
        functools.partial(_experts_kernel, blk=blk),
        grid_spec=grid_spec,
        out_shape=jax.ShapeDtypeStruct((n_rows, d), F32),
        compiler_params=_cparams("arbitrary"),
        name="experts",
    )(be, nvalid, xs, wg, wu, wd)


def _final_kernel(dst_ref, dstn_ref, wts_ref, h2_ref, x1_ref, g2_ref, wg_ref, wu_ref, wd_ref, lng_ref, lnb_ref,
                  ys_hbm, o_ref, ybuf, sem, *, alpha, topk, tm, nt):
    i = pl.program_id(0)
    slot = lax.rem(i, 2)

    def gather_start(ids_ref, sl):
        def body(t, c):
            for k in range(topk):
                pltpu.make_async_copy(ys_hbm.at[pl.ds(ids_ref[0, k, t], 1)], ybuf.at[sl, k, pl.ds(t, 1)],
                                      sem.at[sl]).start()
            return c
        lax.fori_loop(0, tm, body, 0)

    @pl.when(i == 0)
    def _():
        gather_start(dst_ref, 0)

    @pl.when(i + 1 < nt)
    def _():
        gather_start(dstn_ref, 1 - slot)

    h = h2_ref[...].astype(BF16)
    y = _dot((_silu(_dot(h, wg_ref[...])) * _dot(h, wu_ref[...])).astype(BF16), wd_ref[...])

    def wait(t, c):
        for k in range(topk):
            pltpu.make_async_copy(ys_hbm.at[pl.ds(0, 1)], ybuf.at[slot, k, pl.ds(t, 1)], sem.at[slot]).wait()
        return c

    lax.fori_loop(0, tm, wait, 0)
    w = wts_ref[...]
    for k in range(topk):
        y = y + w[:, k:k + 1] * ybuf[slot, k]
    o_ref[...] = _ln(alpha * x1_ref[...] + g2_ref[0] * y) * lng_ref[...] + lnb_ref[...]


def _final(ys, dest, wtsT, h2, x1, gate2, wg, wu, wd, lng, lnb, alpha, tiles_per_batch):
    n, d = x1.shape
    nt, topk, tm = dest.shape
    kern = functools.partial(_final_kernel, alpha=alpha, topk=topk, tm=tm, nt=nt)
    row = pl.BlockSpec((tm, d), lambda i: (i, 0))
    full = lambda a: pl.BlockSpec(a.shape, lambda i: (0,) * a.ndim, pipeline_mode=pl.Buffered(1))
    return pl.pallas_call(
        kern,
        grid=(nt,),
        in_specs=[pl.BlockSpec((1, topk, tm), lambda i: (i, 0, 0), memory_space=pltpu.SMEM),
                  pl.BlockSpec((1, topk, tm), lambda i: (jnp.minimum(i + 1, nt - 1), 0, 0), memory_space=pltpu.SMEM),
                  pl.BlockSpec((tm, topk), lambda i: (i, 0)),
                  row, row, pl.BlockSpec((1, 1, d), lambda i: (i // tiles_per_batch, 0, 0)),
                  full(wg), full(wu), full(wd),
                  pl.BlockSpec((1, d), lambda i: (0, 0)), pl.BlockSpec((1, d), lambda i: (0, 0)),
                  pl.BlockSpec(memory_space=pl.ANY)],
        out_specs=row,
        out_shape=jax.ShapeDtypeStruct((n, d), F32),
        scratch_shapes=[pltpu.VMEM((2, topk, tm, d), F32), pltpu.SemaphoreType.DMA((2,))],
        compiler_params=_cparams("arbitrary"),
        name="final",
    )(dest, dest, wtsT, h2, x1, gate2, wg, wu, wd, lng.reshape(1, d), lnb.reshape(1, d), ys)


def _proj_layout(d, qr, kvr, idim, ih, hk, hv):
    src = np.cumsum([0, qr, kvr, idim, ih, hk, hk, hv, hv, d, d])
    wa = -(-(qr + kvr + idim + ih) // LANES) * LANES
    wa = max(wa, 1 << (wa - 1).bit_length())
    pieces = [("a", wa, (int(src[0]), int(src[4]))),
              ("hq", hk, (int(src[4]), int(src[5]))), ("hf", hk, (int(src[5]), int(src[6]))),
              ("hi", hv, (int(src[6]), int(src[7]))), ("hg", hv, (int(src[7]), int(src[8]))),
              ("ga", d, (int(src[8]), int(src[9]))), ("gb", d, (int(src[9]), int(src[10])))]
    pieces.sort(key=lambda p: -p[1])
    off = 0
    layout = {}
    for name, wdt, rng in pieces:
        assert off % wdt == 0
        layout[name] = (off, wdt, rng)
        off += wdt
    return layout, off


def kernel(x, c, rpb_table, hgrn_lb_logits, ada_w, ada_b, w_in, q_norm_g, kv_norm_g, w_uq, w_uk, w_uv, w_qidx,
           idx_k_norm_g, idx_k_norm_b, hgrn_out_norm_g, w_branch_a, w_branch_b, w_o, ln1_g, ln1_b, w_router,
           router_bias, w_exp_gate, w_exp_up, w_exp_down, w_sh_gate, w_sh_up, w_sh_down, ln2_g, ln2_b):
    bsz, s, d = x.shape
    depth = ada_w.shape[0]
    qr = w_uq.shape[1]
    ah, kvr, qk = w_uk.shape[1], w_uk.shape[2], w_uk.shape[3]
    idim = idx_k_norm_g.shape[1]
    ih = w_qidx.shape[2] // idim
    dv = hgrn_out_norm_g.shape[1]
    nh = w_branch_b.shape[1] // dv
    dk = hgrn_lb_logits.shape[1] // nh
    ne = w_router.shape[2]
    topk = min(IDX_TOPK, s // 4)
    alpha = float((2 * depth) ** 0.25)
    n_tok = bsz * s

    lower_bounds = jnp.cumsum(jax.nn.softmax(hgrn_lb_logits.astype(F32), axis=0), axis=0)
    layout, wtot = _proj_layout(d, qr, kvr, idim, ih, nh * dk, nh * dv)

    for l in range(depth):
        mod = _ada(c, ada_w[l], ada_b[l])[:, None, :]
        shift1, scale1, gate1, shift2, scale2, gate2 = jnp.split(mod, 6, axis=-1)

        cols = []
        for off, wdt, (lo, hi) in sorted(layout.values()):
            cols.append(w_in[l][:, lo:hi].astype(BF16))
            if wdt > hi - lo:
                cols.append(jnp.zeros((d, wdt - (hi - lo)), BF16))
        proj = _inproj(x, scale1, shift1, jnp.concatenate(cols, axis=1))
        blk = lambda name: layout[name][0] // layout[name][1]

        qb_sz = _pick(s, (256, 128))
        q_latT, q_idxT, w_idxT, k_idx, kv_lat = _prep(
            proj, blk("a"), layout["a"][1], (qr, kvr, idim, ih, ah, qk), q_norm_g[l], kv_norm_g[l],
            idx_k_norm_g[l], idx_k_norm_b[l], jnp.swapaxes(w_uq[l], 0, 1).astype(BF16),
            w_uk[l].astype(BF16), jnp.swapaxes(w_qidx[l], 0, 1).astype(BF16), qb_sz)
        o_a = _dsa(q_idxT, w_idxT, q_latT, k_idx, kv_lat, jnp.swapaxes(kv_lat, 1, 2),
                   rpb_table, jnp.swapaxes(w_uv[l], 1, 2).astype(BF16), topk, qb_sz)

        o_b = _hgrn(proj, (blk("hq"), blk("hf"), blk("hi"), blk("hg")), lower_bounds[l],
                    hgrn_out_norm_g[l], nh, dk, dv)

        x1, h2, lgT = _postmix(o_a, o_b, proj, blk("ga"), blk("gb"), x, gate1, scale2, shift2,
                               w_branch_a[l].astype(BF16), w_branch_b[l].astype(BF16), w_o[l].astype(BF16),
                               jnp.swapaxes(w_router[l], 0, 1).astype(BF16), ln1_g[l], ln1_b[l], alpha)

        ids, wts, rnk, sizes_f = _route(lgT, router_bias[l])
        eb = EXPERT_BLOCK
        sizes = sizes_f[:, 0].astype(I32)
        padded = (sizes + eb - 1) // eb * eb
        pend = jnp.cumsum(padded)
        pstart = pend - padded
        n_rows = -(-(n_tok * MOE_TOPK + ne * (eb - 1)) // eb) * eb
        blk_start = jnp.arange(n_rows // eb, dtype=I32) * eb
        blk_expert = jnp.minimum(jnp.sum((pend[None, :] <= blk_start[:, None]).astype(I32), axis=1), ne - 1)
        nvalid = jnp.clip(sizes[blk_expert] - (blk_start - pstart[blk_expert]), 0, eb).astype(I32)
        dest = _dest(pstart.astype(I32), ids, rnk, ne)
        tm = _pick(s, (128,))
        dest_t = dest.reshape(bsz, MOE_TOPK, s // tm, tm).transpose(0, 2, 1, 3).reshape(n_tok // tm, MOE_TOPK, tm)
        wtsT = jnp.swapaxes(wts, 1, 2).reshape(n_tok, MOE_TOPK)

        h2f = h2.reshape(n_tok, d)
        xs = _dispatch(dest_t, h2f, n_rows)
        ys = _experts(xs, blk_expert, nvalid, w_exp_gate[l], w_exp_up[l], w_exp_down[l])
        x = _final(ys, dest_t, wtsT, h2f, x1.reshape(n_tok, d), gate2, w_sh_gate[l].astype(BF16),
                   w_sh_up[l].astype(BF16), w_sh_down[l].astype(BF16), ln2_g[l], ln2_b[l], alpha,
                   s // tm).reshape(bsz, s, d)
    return x
```

```python
import functools
import math

import numpy as np
import jax
import jax.numpy as jnp
from jax import lax
from jax.experimental import pallas as pl
from jax.experimental.pallas import tpu as pltpu

F32 = jnp.float32
BF16 = jnp.bfloat16
I32 = jnp.int32

EPS = 1e-6
IDX_TOPK = 256
RPB_MAX_DIST = 128
MOE_TOPK = 8
N_GROUPS = 8
TOPK_GROUPS = 4
ROUTED_SCALE = 2.5
EXPERT_BLOCK = 256

V7X_VMEM_LIMIT_BYTES = 56 * 1024 * 1024
LANES = 128
INT_MIN = -2 ** 31
NEG_BIG = -1e30
LOG2E = math.log2(math.e)


def _cparams(*sem):
    return pltpu.CompilerParams(dimension_semantics=tuple(sem), vmem_limit_bytes=V7X_VMEM_LIMIT_BYTES)


def _pick(n, prefs):
    for p in prefs:
        if n % p == 0:
            return p
    return n


def _sigmoid(v):
    return 1.0 / (1.0 + jnp.exp(-v))


def _silu(v):
    return v * _sigmoid(v)


def _ln(v):
    mu = jnp.mean(v, axis=-1, keepdims=True)
    d = v - mu
    var = jnp.mean(d * d, axis=-1, keepdims=True)
    return d * lax.rsqrt(var + EPS)


def _rms(v):
    return v * lax.rsqrt(jnp.mean(v * v, axis=-1, keepdims=True) + EPS)


def _dot(a, b):
    return jnp.dot(a, b, preferred_element_type=F32)


def _dot_nt(a, b):
    return lax.dot_general(a, b, (((1,), (1,)), ((), ())), preferred_element_type=F32)


def _dot_tn(a, b):
    return lax.dot_general(a, b, (((0,), (0,)), ((), ())), preferred_element_type=F32)


def _ada_kernel(c_ref, w_ref, b_ref, o_ref):
    c = c_ref[...]
    o_ref[...] = _dot(_silu(c).astype(BF16), w_ref[...].astype(BF16)) + b_ref[...]


def _ada(c, w, b):
    bsz, d = c.shape
    n = w.shape[1]
    rows = 8
    cp = jnp.zeros((rows, d), F32).at[:bsz].set(c)
    tn = _pick(n, (1024, 512, 256, 128))
    out = pl.pallas_call(
        _ada_kernel,
        grid=(n // tn,),
        in_specs=[pl.BlockSpec((rows, d), lambda j: (0, 0)),
                  pl.BlockSpec((d, tn), lambda j: (0, j)),
                  pl.BlockSpec((1, tn), lambda j: (0, j))],
        out_specs=pl.BlockSpec((rows, tn), lambda j: (0, j)),
        out_shape=jax.ShapeDtypeStruct((rows, n), F32),
        compiler_params=_cparams("arbitrary"),
        name="ada",
    )(cp, w, b.reshape(1, n))
    return out[:bsz]


def _inproj_kernel(x_ref, sc_ref, sh_ref, w_ref, o_ref, h_ref):
    @pl.when(pl.program_id(2) == 0)
    def _():
        h = _ln(x_ref[0]) * (1.0 + sc_ref[0]) + sh_ref[0]
        h_ref[...] = h.astype(BF16)

    o_ref[0] = _dot(h_ref[...], w_ref[...]).astype(o_ref.dtype)


def _inproj(x, scale, shift, w):
    bsz, s, d = x.shape
    n = w.shape[1]
    tm = _pick(s, (1024, 512, 256, 128))
    tn = _pick(n, (1024, 512, 256, 128))
    return pl.pallas_call(
        _inproj_kernel,
        grid=(bsz, s // tm, n // tn),
        in_specs=[pl.BlockSpec((1, tm, d), lambda b, i, j: (b, i, 0)),
                  pl.BlockSpec((1, 1, d), lambda b, i, j: (b, 0, 0)),
                  pl.BlockSpec((1, 1, d), lambda b, i, j: (b, 0, 0)),
                  pl.BlockSpec((d, tn), lambda b, i, j: (0, j))],
        out_specs=pl.BlockSpec((1, tm, tn), lambda b, i, j: (b, i, j)),
        out_shape=jax.ShapeDtypeStruct((bsz, s, n), BF16),
        scratch_shapes=[pltpu.VMEM((tm, d), BF16)],
        compiler_params=_cparams("arbitrary", "arbitrary", "arbitrary"),
        name="inproj",
    )(x, scale, shift, w)


def _prep_kernel(a_ref, qg_ref, kvg_ref, ikg_ref, ikb_ref, wuqT_ref, wuk_ref, wqiT_ref,
                 qlatT_ref, qidxT_ref, widxT_ref, kidx_ref, kv_ref, *, qr, kvr, idim, ih, ah, qk):
    a = a_ref[0]
    af = a.astype(F32)
    cq = (_rms(af[:, :qr]) * qg_ref[...]).astype(BF16)
    ckv = af[:, qr:qr + kvr]
    ki = af[:, qr + kvr:qr + kvr + idim]
    kv_ref[0] = (_rms(ckv) * kvg_ref[...]).astype(BF16)
    kidx_ref[0] = (_ln(ki) * ikg_ref[...] + ikb_ref[...]).astype(BF16)
    tail = a[:, qr + kvr:qr + kvr + LANES]
    eye = (lax.broadcasted_iota(I32, (LANES, LANES), 0) == lax.broadcasted_iota(I32, (LANES, LANES), 1))
    tailT = _dot_nt(jnp.where(eye, 1.0, 0.0).astype(BF16), tail)
    widxT_ref[0] = tailT[idim:idim + ih] * float((ih * idim) ** -0.5)
    tm = a.shape[0]
    qT = _dot_nt(wuqT_ref[...], cq)
    for h in range(ah):
        qh = qT[h * qk:(h + 1) * qk].astype(BF16)
        qlatT_ref[0, 0, :, h * tm:(h + 1) * tm] = (_dot(wuk_ref[h], qh) * float(qk ** -0.5 * LOG2E)).astype(BF16)
    qiT = _dot_nt(wqiT_ref[...], cq)
    for h in range(ih):
        qidxT_ref[0, 0, :, h * tm:(h + 1) * tm] = qiT[h * idim:(h + 1) * idim].astype(BF16)


def _prep(proj, a_blk, wa, dims, q_norm_g, kv_norm_g, ikg, ikb, w_uqT, w_uk, w_qidxT, tm):
    bsz, s, _ = proj.shape
    qr, kvr, idim, ih, ah, qk = dims
    assert idim + ih <= LANES and qr + kvr + LANES <= wa
    kern = functools.partial(_prep_kernel, qr=qr, kvr=kvr, idim=idim, ih=ih, ah=ah, qk=qk)
    full = lambda shape: pl.BlockSpec(shape, lambda b, i: (0,) * len(shape))
    return pl.pallas_call(
        kern,
        grid=(bsz, s // tm),
        in_specs=[pl.BlockSpec((1, tm, wa), lambda b, i: (b, i, a_blk)),
                  full((1, qr)), full((1, kvr)), full((1, idim)), full((1, idim)),
                  full(w_uqT.shape), full(w_uk.shape), full(w_qidxT.shape)],
        out_specs=[pl.BlockSpec((1, 1, kvr, ah * tm), lambda b, i: (b, i, 0, 0)),
                   pl.BlockSpec((1, 1, idim, ih * tm), lambda b, i: (b, i, 0, 0)),
                   pl.BlockSpec((1, ih, tm), lambda b, i: (b, 0, i)),
                   pl.BlockSpec((1, tm, idim), lambda b, i: (b, i, 0)),
                   pl.BlockSpec((1, tm, kvr), lambda b, i: (b, i, 0))],
        out_shape=[jax.ShapeDtypeStruct((bsz, s // tm, kvr, ah * tm), BF16),
                   jax.ShapeDtypeStruct((bsz, s // tm, idim, ih * tm), BF16),
                   jax.ShapeDtypeStruct((bsz, ih, s), F32),
                   jax.ShapeDtypeStruct((bsz, s, idim), BF16),
                   jax.ShapeDtypeStruct((bsz, s, kvr), BF16)],
        compiler_params=_cparams("arbitrary", "arbitrary"),
        name="prep",
    )(proj, q_norm_g.reshape(1, qr), kv_norm_g.reshape(1, kvr), ikg.reshape(1, idim), ikb.reshape(1, idim),
      w_uqT, w_uk, w_qidxT)


def _dsa_kernel(qidxT_ref, widxT_ref, qlatT_ref, kidx_ref, kv_ref, kvT_ref, t0_ref, t1_ref, wuvT_ref,
                o_ref, keys_ref, s_ref, p_ref, m_ref, a_ref, acc_ref, *, qb_sz, topk, ih, ah, c):
    QB = qb_sz
    qb = pl.program_id(1)
    nchunks = qb + 1
    qpos = qb * QB + lax.broadcasted_iota(I32, (1, QB), 1)
    kpos = lax.broadcasted_iota(I32, (QB, 1), 0)
    wT = widxT_ref[0]
    hcols = lambda h: slice(h * QB, (h + 1) * QB)

    def score_chunk(kc, carry):
        off = pl.multiple_of(kc * QB, QB)
        s_ref[:, :ih * QB] = _dot(kidx_ref[0, pl.ds(off, QB), :], qidxT_ref[0, 0])
        acc = jnp.zeros((QB, QB), F32)
        for h in range(ih):
            acc = acc + wT[h:h + 1] * jnp.maximum(s_ref[:, hcols(h)], 0.0)
        bits = pltpu.bitcast(acc, I32)
        skey = bits ^ ((bits >> 31) & 0x7FFFFFFF)
        causal = (off + kpos) <= qpos
        keys_ref[pl.ds(off, QB), :] = jnp.where(causal, skey, INT_MIN)
        return carry

    lax.fori_loop(0, nchunks, score_chunk, 0)

    def count_ge(cand):
        def body(kc, cnt):
            off = pl.multiple_of(kc * QB, QB)
            hit = jnp.where(keys_ref[pl.ds(off, QB), :] >= cand, 1, 0)
            return cnt + jnp.sum(hit.reshape(QB // 8, 8, QB), axis=0)

        cnt = lax.fori_loop(0, nchunks, body, jnp.zeros((8, QB), I32))
        return jnp.sum(cnt.astype(F32), axis=0, keepdims=True)

    def bis_cond(st):
        bit, _, done = st
        return jnp.logical_and(bit >= 0, jnp.min(done) < 0.5)

    def bis_body(st):
        bit, c, done = st
        cand = c + jnp.left_shift(jnp.int32(1), bit)
        cnt = count_ge(cand)
        take = jnp.logical_and(cnt >= float(topk), done < 0.5)
        c = jnp.where(take, cand, c)
        done = jnp.where(jnp.logical_and(take, cnt == float(topk)), 1.0, done)
        return bit - 1, c, done

    _, thr, _ = lax.while_loop(
        bis_cond, bis_body,
        (jnp.int32(31), jnp.full((1, QB), INT_MIN, I32), jnp.zeros((1, QB), F32)))

    m_ref[...] = jnp.full(m_ref.shape, NEG_BIG, F32)
    acc_ref[...] = jnp.zeros(acc_ref.shape, F32)

    def attn_chunk(off, bias_ref, diag):
        sel = keys_ref[pl.ds(off, QB), :] >= thr
        if diag:
            sel = jnp.logical_and(sel, (off + kpos) <= qpos)
        madd = jnp.where(sel, 0.0, NEG_BIG)
        s_ref[:, :ah * QB] = _dot(kv_ref[0, pl.ds(off, QB), :], qlatT_ref[0, 0])
        for g in range(ah * QB // LANES):
            cols = slice(g * LANES, (g + 1) * LANES)
            qcols = slice(g * LANES % QB, g * LANES % QB + LANES)
            s = s_ref[:, cols] + madd[:, qcols]
            if bias_ref is not None:
                s = s + bias_ref[:, cols]
            m_prev = m_ref[:, cols]
            m_new = jnp.maximum(m_prev, jnp.max(s, axis=0, keepdims=True))
            p_ref[:, cols] = jnp.exp2(s - m_new).astype(BF16)
            a_ref[:, cols] = jnp.exp2(m_prev - m_new)
            m_ref[:, cols] = m_new
        acc_ref[...] = a_ref[...] * acc_ref[...] + _dot(kvT_ref[0, :, pl.ds(off, QB)], p_ref[...])

    def far_body(kc, carry):
        attn_chunk(pl.multiple_of(kc * QB, QB), None, False)
        return carry

    lax.fori_loop(0, qb - 1, far_body, 0)

    @pl.when(qb >= 1)
    def _():
        attn_chunk(pl.multiple_of((qb - 1) * QB, QB), t1_ref, False)

    attn_chunk(pl.multiple_of(qb * QB, QB), t0_ref, True)

    outs = []
    for h in range(ah):
        o = (acc_ref[:c, hcols(h)] / acc_ref[c:c + 1, hcols(h)]).astype(BF16)
        outs.append(_dot(wuvT_ref[h], o))
    o_ref[0] = jnp.concatenate(outs, axis=0).T.astype(o_ref.dtype)


def _t5_bucket_np(d, nbuckets):
    max_exact = nbuckets // 2
    dd = np.maximum(d, 1).astype(np.float32)
    large = max_exact + (np.log(dd / np.float32(max_exact)) / np.float32(math.log(RPB_MAX_DIST / max_exact))
                         * np.float32(nbuckets - max_exact)).astype(np.int32)
    large = np.minimum(large, nbuckets - 1)
    return np.where(d < max_exact, d, large).astype(np.int32)


def _dsa(q_idxT, w_idxT, q_latT, k_idx, kv_lat, kv_latT, rpb_table, w_uvT, topk, QB):
    bsz, nqb, idim, ihq = q_idxT.shape
    c, ahq = q_latT.shape[2], q_latT.shape[3]
    ca = kv_latT.shape[1]
    ih, ah = ihq // QB, ahq // QB
    s = nqb * QB
    vd = w_uvT.shape[1]
    nb = rpb_table.shape[0]
    assert QB >= RPB_MAX_DIST
    j = np.arange(QB)[:, None]
    i = np.arange(QB)[None, :]
    d0 = np.maximum(i - j, 0)
    d1 = QB + i - j
    rel = (rpb_table.astype(F32) - rpb_table[nb - 1].astype(F32)[None, :]) * LOG2E
    t0 = jnp.transpose(rel[_t5_bucket_np(d0, nb)], (0, 2, 1)).reshape(QB, ah * QB)
    t1 = jnp.transpose(rel[_t5_bucket_np(d1, nb)], (0, 2, 1)).reshape(QB, ah * QB)
    kern = functools.partial(_dsa_kernel, qb_sz=QB, topk=topk, ih=ih, ah=ah, c=c)
    const = lambda shape: pl.BlockSpec(shape, lambda b, i: (0,) * len(shape), pipeline_mode=pl.Buffered(1))
    hw = max(ih, ah) * QB
    return pl.pallas_call(
        kern,
        grid=(bsz, nqb),
        in_specs=[pl.BlockSpec((1, 1, idim, ih * QB), lambda b, i: (b, i, 0, 0)),
                  pl.BlockSpec((1, ih, QB), lambda b, i: (b, 0, i)),
                  pl.BlockSpec((1, 1, c, ah * QB), lambda b, i: (b, i, 0, 0)),
                  pl.BlockSpec((1, s, idim), lambda b, i: (b, 0, 0)),
                  pl.BlockSpec((1, s, c), lambda b, i: (b, 0, 0)),
                  pl.BlockSpec((1, ca, s), lambda b, i: (b, 0, 0)),
                  const((QB, ah * QB)), const((QB, ah * QB)), const(w_uvT.shape)],
        out_specs=pl.BlockSpec((1, QB, ah * vd), lambda b, i: (b, i, 0)),
        out_shape=jax.ShapeDtypeStruct((bsz, s, ah * vd), BF16),
        scratch_shapes=[pltpu.VMEM((s, QB), I32),
                        pltpu.VMEM((QB, hw), F32),
                        pltpu.VMEM((QB, ah * QB), BF16),
                        pltpu.VMEM((1, ah * QB), F32),
                        pltpu.VMEM((1, ah * QB), F32),
                        pltpu.VMEM((ca, ah * QB), F32)],
        compiler_params=_cparams("arbitrary", "arbitrary"),
        name="dsa",
    )(q_idxT, w_idxT, q_latT, k_idx, kv_lat, kv_latT, t0, t1, w_uvT)


HGRN_CHUNK = 64
HGRN_SUB = 16
HGRN_EXP_CLAMP = 80.0


def _hgrn_kernel(hq_ref, hf_ref, hi_ref, hg_ref, lb_ref, g_ref, o_ref, st_ref, *, nh, dk, dv, tt):
    C, SUB = HGRN_CHUNK, HGRN_SUB

    @pl.when(pl.program_id(1) == 0)
    def _():
        st_ref[...] = jnp.zeros(st_ref.shape, F32)

    r = lax.broadcasted_iota(I32, (C, C), 0)
    cc = lax.broadcasted_iota(I32, (C, C), 1)
    tri_mask = r >= cc
    tri = jnp.where(tri_mask, 1.0, 0.0).astype(BF16)
    g = g_ref[...]

    for h in range(nh):
        ko = h * dk
        vo = h * dv
        lb = lb_ref[:, pl.ds(ko, dk)]
        stT = st_ref[h]
        for c in range(tt // C):
            rows = pl.ds(c * C, C)
            hf = hf_ref[0, rows, pl.ds(ko, dk)].astype(F32)
            f = lb + (1.0 - lb) * _sigmoid(hf)
            lf = jnp.log(f)
            t1 = lf.astype(BF16)
            r1 = lf - t1.astype(F32)
            t2 = r1.astype(BF16)
            t3 = (r1 - t2.astype(F32)).astype(BF16)
            b = _dot(tri, t1) + _dot(tri, t2) + _dot(tri, t3)
            kk = 1.0 - f
            hq = hq_ref[0, rows, pl.ds(ko, dk)].astype(F32)
            q = _silu(hq) * float(dk ** -0.5)
            v = hi_ref[0, rows, pl.ds(vo, dv)]
            o_inter = _dot_nt((q * jnp.exp(b)).astype(BF16), stT.astype(BF16))
            parts = []
            for i in range(C // SUB):
                lo, n = i * SUB, (i + 1) * SUB
                bi = b[lo - 1:lo] if i > 0 else jnp.zeros((1, dk), F32)
                qs = (q[lo:n] * jnp.exp(b[lo:n] - bi)).astype(BF16)
                ks = (kk * jnp.exp(jnp.minimum(bi - b, HGRN_EXP_CLAMP))).astype(BF16)
                parts.append(_dot_nt(qs, ks))
            att = jnp.where(tri_mask, jnp.concatenate(parts, axis=0), 0.0)
            o = o_inter + _dot(att.astype(BF16), v)
            b_last = b[C - 1:C]
            k_dec = (kk * jnp.exp(b_last - b)).astype(BF16)
            stT = stT * jnp.exp(b_last) + _dot_tn(v, k_dec)
            hg = hg_ref[0, rows, pl.ds(vo, dv)].astype(F32)
            o_ref[0, rows, pl.ds(vo, dv)] = (_rms(o) * g * _silu(hg)).astype(o_ref.dtype)
        st_ref[h] = stT


def _hgrn(proj, blks, lb, g, nh, dk, dv):
    bsz, s, _ = proj.shape
    tt = _pick(s, (256, 128, 64))
    kern = functools.partial(_hgrn_kernel, nh=nh, dk=dk, dv=dv, tt=tt)
    col = lambda blk, wdt: pl.BlockSpec((1, tt, wdt), lambda b, i: (b, i, blk))
    return pl.pallas_call(
        kern,
        grid=(bsz, s // tt),
        in_specs=[col(blks[0], nh * dk), col(blks[1], nh * dk), col(blks[2], nh * dv), col(blks[3], nh * dv),
                  pl.BlockSpec((1, nh * dk), lambda b, i: (0, 0)),
                  pl.BlockSpec((1, dv), lambda b, i: (0, 0))],
        out_specs=pl.BlockSpec((1, tt, nh * dv), lambda b, i: (b, i, 0)),
        out_shape=jax.ShapeDtypeStruct((bsz, s, nh * dv), BF16),
        scratch_shapes=[pltpu.VMEM((nh, dv, dk), F32)],
        compiler_params=_cparams("arbitrary", "arbitrary"),
        name="hgrn",
    )(proj, proj, proj, proj, lb.reshape(1, nh * dk), g.reshape(1, dv))


def _postmix_kernel(oa_ref, ob_ref, ga_ref, gb_ref, x_ref, g1_ref, sc2_ref, sh2_ref,
                    wa_ref, wb_ref, wo_ref, wrT_ref, lng_ref, lnb_ref,
                    x1_ref, h2_ref, lgT_ref, *, alpha):
    ya = _dot(oa_ref[0], wa_ref[...])
    yb = _dot(ob_ref[0], wb_ref[...])
    mix = _sigmoid(ga_ref[0].astype(F32)) * ya + _sigmoid(gb_ref[0].astype(F32)) * yb
    mixed = _dot(mix.astype(BF16), wo_ref[...])
    x1 = _ln(alpha * x_ref[0] + g1_ref[0] * mixed) * lng_ref[...] + lnb_ref[...]
    x1_ref[0] = x1
    h2 = _ln(x1) * (1.0 + sc2_ref[0]) + sh2_ref[0]
    h2_ref[0] = h2
    lgT_ref[0] = _dot_nt(wrT_ref[...], h2.astype(BF16))


def _postmix(o_a, o_b, proj, ga_blk, gb_blk, x, gate1, scale2, shift2, wa, wb, wo, wrT, lng, lnb, alpha):
    bsz, s, d = x.shape
    ne = wrT.shape[0]
    tm = _pick(s, (256, 128))
    kern = functools.partial(_postmix_kernel, alpha=alpha)
    row = lambda wdt: pl.BlockSpec((1, tm, wdt), lambda b, i: (b, i, 0))
    vec = pl.BlockSpec((1, 1, d), lambda b, i: (b, 0, 0))
    full = lambda a: pl.BlockSpec(a.shape, lambda b, i: (0,) * a.ndim, pipeline_mode=pl.Buffered(1))
    return pl.pallas_call(
        kern,
        grid=(bsz, s // tm),
        in_specs=[row(o_a.shape[2]), row(o_b.shape[2]),
                  pl.BlockSpec((1, tm, d), lambda b, i: (b, i, ga_blk)),
                  pl.BlockSpec((1, tm, d), lambda b, i: (b, i, gb_blk)),
                  row(d), vec, vec, vec,
                  full(wa), full(wb), full(wo), full(wrT),
                  pl.BlockSpec((1, d), lambda b, i: (0, 0)), pl.BlockSpec((1, d), lambda b, i: (0, 0))],
        out_specs=[row(d), row(d), pl.BlockSpec((1, ne, tm), lambda b, i: (b, 0, i))],
        out_shape=[jax.ShapeDtypeStruct((bsz, s, d), F32),
                   jax.ShapeDtypeStruct((bsz, s, d), F32),
                   jax.ShapeDtypeStruct((bsz, ne, s), F32)],
        compiler_params=_cparams("arbitrary", "arbitrary"),
        name="postmix",
    )(o_a, o_b, proj, proj, x, gate1, scale2, shift2, wa, wb, wo, wrT, lng.reshape(1, d), lnb.reshape(1, d))


def _route_kernel(lg_ref, bias_ref, ids_ref, wts_ref, rnk_ref, sizes_ref, upper_ref, carry_ref, *, ne):
    first = jnp.logical_and(pl.program_id(0) == 0, pl.program_id(1) == 0)
    tn = lg_ref.shape[2]

    @pl.when(first)
    def _():
        carry_ref[...] = jnp.zeros(carry_ref.shape, F32)
        r_ = lax.broadcasted_iota(I32, (tn, tn), 0)
        c_ = lax.broadcasted_iota(I32, (tn, tn), 1)
        upper_ref[...] = jnp.where(r_ < c_, 1.0, 0.0).astype(BF16)

    per = ne // N_GROUPS
    s = _sigmoid(lg_ref[0])
    bz = s + bias_ref[...]
    ridx = lax.broadcasted_iota(I32, (per, tn), 0)
    neg_inf = jnp.float32(-jnp.inf)
    gs = []
    for g in range(N_GROUPS):
        blk = bz[g * per:(g + 1) * per]
        m1 = jnp.max(blk, axis=0, keepdims=True)
        first_hit = jnp.min(jnp.where(blk == m1, ridx, per), axis=0, keepdims=True)
        m2 = jnp.max(jnp.where(ridx == first_hit, neg_inf, blk), axis=0, keepdims=True)
        gs.append(m1 + m2)
    emask_rows = []
    for g in range(N_GROUPS):
        rank = jnp.zeros((1, tn), I32)
        for g2 in range(N_GROUPS):
            if g2 == g:
                continue
            beats = (gs[g2] > gs[g]) if g2 > g else (gs[g2] >= gs[g])
            rank = rank + jnp.where(beats, 1, 0)
        emask_rows.append(jnp.broadcast_to(rank < TOPK_GROUPS, (per, tn)))
    emask = jnp.concatenate(emask_rows, axis=0)
    masked = jnp.where(emask, bz, neg_inf)
    eidx = lax.broadcasted_iota(I32, (ne, tn), 0)
    rank = jnp.zeros((ne, tn), I32)
    for e2 in range(ne):
        row = masked[e2:e2 + 1]
        beats = jnp.logical_or(row > masked, jnp.logical_and(row == masked, e2 < eidx))
        rank = rank + jnp.where(beats, 1, 0)
    sel = rank < MOE_TOPK
    sel01 = jnp.where(sel, 1.0, 0.0)
    denom = jnp.sum(jnp.where(sel, s, 0.0), axis=0, keepdims=True)
    wn = s / denom * ROUTED_SCALE
    before = _dot(sel01.astype(BF16), upper_ref[...]) + carry_ref[:, 0:1]
    ids, wts, rnk = [], [], []
    for k in range(MOE_TOPK):
        hit = rank == k
        ids.append(jnp.sum(jnp.where(hit, eidx, 0), axis=0, keepdims=True))
        wts.append(jnp.sum(jnp.where(hit, wn, 0.0), axis=0, keepdims=True))
        rnk.append(jnp.sum(jnp.where(hit, before, 0.0), axis=0, keepdims=True))
    ids_ref[0] = jnp.concatenate(ids, axis=0)
    wts_ref[0] = jnp.concatenate(wts, axis=0)
    rnk_ref[0] = jnp.concatenate(rnk, axis=0).astype(I32)
    carry_ref[...] = carry_ref[...] + jnp.sum(sel01, axis=1, keepdims=True)
    sizes_ref[...] = carry_ref[...]


def _route(lgT, bias):
    bsz, ne, s = lgT.shape
    tn = _pick(s, (1024, 512, 256, 128))
    kern = functools.partial(_route_kernel, ne=ne)
    slot = pl.BlockSpec((1, MOE_TOPK, tn), lambda b, j: (b, 0, j))
    return pl.pallas_call(
        kern,
        grid=(bsz, s // tn),
        in_specs=[pl.BlockSpec((1, ne, tn), lambda b, j: (b, 0, j)),
                  pl.BlockSpec((ne, 1), lambda b, j: (0, 0))],
        out_specs=[slot, slot, slot, pl.BlockSpec((ne, LANES), lambda b, j: (0, 0))],
        out_shape=[jax.ShapeDtypeStruct((bsz, MOE_TOPK, s), I32),
                   jax.ShapeDtypeStruct((bsz, MOE_TOPK, s), F32),
                   jax.ShapeDtypeStruct((bsz, MOE_TOPK, s), I32),
                   jax.ShapeDtypeStruct((ne, LANES), F32)],
        scratch_shapes=[pltpu.VMEM((tn, tn), BF16), pltpu.VMEM((ne, LANES), F32)],
        compiler_params=_cparams("arbitrary", "arbitrary"),
        name="route",
    )(lgT, bias.reshape(ne, 1))


def _dest_kernel(pstart_ref, ids_ref, rnk_ref, o_ref, *, ne):
    ids = ids_ref[0]
    base = jnp.zeros(ids.shape, I32)
    for e in range(ne):
        base = jnp.where(ids == e, pstart_ref[e], base)
    o_ref[0] = base + rnk_ref[0]


def _dest(pstart, ids, rnk, ne):
    bsz, k, s = ids.shape
    tn = _pick(s, (2048, 1024, 512, 256, 128))
    blk = lambda: pl.BlockSpec((1, k, tn), lambda b, j, ps: (b, 0, j))
    return pl.pallas_call(
        functools.partial(_dest_kernel, ne=ne),
        grid_spec=pltpu.PrefetchScalarGridSpec(num_scalar_prefetch=1, grid=(bsz, s // tn),
                                               in_specs=[blk(), blk()], out_specs=blk()),
        out_shape=jax.ShapeDtypeStruct((bsz, k, s), I32),
        compiler_params=_cparams("arbitrary", "arbitrary"),
        name="dest",
    )(pstart, ids, rnk)


def _dispatch_kernel(dst_ref, h_ref, xs_hbm, sem, *, tm, topk):
    def start(t, c):
        for k in range(topk):
            pltpu.make_async_copy(h_ref.at[pl.ds(t, 1)], xs_hbm.at[pl.ds(dst_ref[0, k, t], 1)], sem.at[0]).start()
        return c

    lax.fori_loop(0, tm, start, 0)

    def wait(t, c):
        for k in range(topk):
            pltpu.make_async_copy(h_ref.at[pl.ds(t, 1)], xs_hbm.at[pl.ds(0, 1)], sem.at[0]).wait()
        return c

    lax.fori_loop(0, tm, wait, 0)


def _dispatch(dest, h2, n_rows):
    n, d = h2.shape
    nt, topk, tm = dest.shape
    return pl.pallas_call(
        functools.partial(_dispatch_kernel, tm=tm, topk=topk),
        grid=(nt,),
        in_specs=[pl.BlockSpec((1, topk, tm), lambda i: (i, 0, 0), memory_space=pltpu.SMEM),
                  pl.BlockSpec((tm, d), lambda i: (i, 0))],
        out_specs=pl.BlockSpec(memory_space=pl.ANY),
        out_shape=jax.ShapeDtypeStruct((n_rows, d), h2.dtype),
        scratch_shapes=[pltpu.SemaphoreType.DMA((1,))],
        compiler_params=_cparams("arbitrary"),
        name="dispatch",
    )(dest, h2)


def _experts_kernel(be_ref, nv_ref, x_ref, wg_ref, wu_ref, wd_ref, y_ref, wgb, wub, wdb, *, blk):
    i = pl.program_id(0)
    prev_e = be_ref[jnp.maximum(i - 1, 0)]

    @pl.when(jnp.logical_or(i == 0, be_ref[i] != prev_e))
    def _():
        wgb[...] = wg_ref[0].astype(BF16)
        wub[...] = wu_ref[0].astype(BF16)
        wdb[...] = wd_ref[0].astype(BF16)

    @pl.when(nv_ref[i] > 0)
    def _():
        rows = lax.broadcasted_iota(I32, (blk, 1), 0)
        x = jnp.where(rows < nv_ref[i], x_ref[...], 0.0).astype(BF16)
        act = (_silu(_dot(x, wgb[...])) * _dot(x, wub[...])).astype(BF16)
        y_ref[...] = _dot(act, wdb[...])


def _experts(xs, be, nvalid, wg, wu, wd):
    n_rows, d = xs.shape
    ne, _, f = wg.shape
    blk = EXPERT_BLOCK
    nblocks = n_rows // blk
    grid_spec = pltpu.PrefetchScalarGridSpec(
        num_scalar_prefetch=2,
        grid=(nblocks,),
        in_specs=[pl.BlockSpec((blk, d), lambda i, be, nv: (i, 0)),
                  pl.BlockSpec((1, d, f), lambda i, be, nv: (be[i], 0, 0)),
                  pl.BlockSpec((1, d, f), lambda i, be, nv: (be[i], 0, 0)),
                  pl.BlockSpec((1, f, d), lambda i, be, nv: (be[i], 0, 0))],
        out_specs=pl.BlockSpec((blk, d), lambda i, be, nv: (i, 0)),
        scratch_shapes=[pltpu.VMEM((d, f), BF16), pltpu.VMEM((d, f), BF16), pltpu.VMEM((f, d), BF16)],
    )
    return pl.pallas_call(
        functools.partial(_experts_kernel, blk=blk),
        grid_spec=grid_spec,
        out_shape=jax.ShapeDtypeStruct((n_rows, d), F32),
        compiler_params=_cparams("arbitrary"),
        name="experts",
    )(be, nvalid, xs, wg, wu, wd)


def _final_kernel(dst_ref, dstn_ref, wts_ref, h2_ref, x1_ref, g2_ref, wg_ref, wu_ref, wd_ref, lng_ref, lnb_ref,
                  ys_hbm, o_ref, ybuf, sem, *, alpha, topk, tm, nt):
    i = pl.program_id(0)
    slot = lax.rem(i, 2)

    def gather_start(ids_ref, sl):
        def body(t, c):
            for k in range(topk):
                pltpu.make_async_copy(ys_hbm.at[pl.ds(ids_ref[0, k, t], 1)], ybuf.at[sl, k, pl.ds(t, 1)],
                                      sem.at[sl]).start()
            return c
        lax.fori_loop(0, tm, body, 0)

    @pl.when(i == 0)
    def _():
        gather_start(dst_ref, 0)

    @pl.when(i + 1 < nt)
    def _():
        gather_start(dstn_ref, 1 - slot)

    h = h2_ref[...].astype(BF16)
    y = _dot((_silu(_dot(h, wg_ref[...])) * _dot(h, wu_ref[...])).astype(BF16), wd_ref[...])

    def wait(t, c):
        for k in range(topk):
            pltpu.make_async_copy(ys_hbm.at[pl.ds(0, 1)], ybuf.at[slot, k, pl.ds(t, 1)], sem.at[slot]).wait()
        return c

    lax.fori_loop(0, tm, wait, 0)
    w = wts_ref[...]
    for k in range(topk):
        y = y + w[:, k:k + 1] * ybuf[slot, k]
    o_ref[...] = _ln(alpha * x1_ref[...] + g2_ref[0] * y) * lng_ref[...] + lnb_ref[...]


def _final(ys, dest, wtsT, h2, x1, gate2, wg, wu, wd, lng, lnb, alpha, tiles_per_batch):
    n, d = x1.shape
    nt, topk, tm = dest.shape
    kern = functools.partial(_final_kernel, alpha=alpha, topk=topk, tm=tm, nt=nt)
    row = pl.BlockSpec((tm, d), lambda i: (i, 0))
    full = lambda a: pl.BlockSpec(a.shape, lambda i: (0,) * a.ndim, pipeline_mode=pl.Buffered(1))
    return pl.pallas_call(
        kern,
        grid=(nt,),
        in_specs=[pl.BlockSpec((1, topk, tm), lambda i: (i, 0, 0), memory_space=pltpu.SMEM),
                  pl.BlockSpec((1, topk, tm), lambda i: (jnp.minimum(i + 1, nt - 1), 0, 0), memory_space=pltpu.SMEM),
                  pl.BlockSpec((tm, topk), lambda i: (i, 0)),
                  row, row, pl.BlockSpec((1, 1, d), lambda i: (i // tiles_per_batch, 0, 0)),
                  full(wg), full(wu), full(wd),
                  pl.BlockSpec((1, d), lambda i: (0, 0)), pl.BlockSpec((1, d), lambda i: (0, 0)),
                  pl.BlockSpec(memory_space=pl.ANY)],
        out_specs=row,
        out_shape=jax.ShapeDtypeStruct((n, d), F32),
        scratch_shapes=[pltpu.VMEM((2, topk, tm, d), F32), pltpu.SemaphoreType.DMA((2,))],
        compiler_params=_cparams("arbitrary"),
        name="final",
    )(dest, dest, wtsT, h2, x1, gate2, wg, wu, wd, lng.reshape(1, d), lnb.reshape(1, d), ys)


def _proj_layout(d, qr, kvr, idim, ih, hk, hv):
    src = np.cumsum([0, qr, kvr, idim, ih, hk, hk, hv, hv, d, d])
    wa = -(-(qr + kvr + idim + ih) // LANES) * LANES
    wa = max(wa, 1 << (wa - 1).bit_length())
    pieces = [("a", wa, (int(src[0]), int(src[4]))),
              ("hq", hk, (int(src[4]), int(src[5]))), ("hf", hk, (int(src[5]), int(src[6]))),
              ("hi", hv, (int(src[6]), int(src[7]))), ("hg", hv, (int(src[7]), int(src[8]))),
              ("ga", d, (int(src[8]), int(src[9]))), ("gb", d, (int(src[9]), int(src[10])))]
    pieces.sort(key=lambda p: -p[1])
    off = 0
    layout = {}
    for name, wdt, rng in pieces:
        assert off % wdt == 0
        layout[name] = (off, wdt, rng)
        off += wdt
    return layout, off


def kernel(x, c, rpb_table, hgrn_lb_logits, ada_w, ada_b, w_in, q_norm_g, kv_norm_g, w_uq, w_uk, w_uv, w_qidx,
           idx_k_norm_g, idx_k_norm_b, hgrn_out_norm_g, w_branch_a, w_branch_b, w_o, ln1_g, ln1_b, w_router,
           router_bias, w_exp_gate, w_exp_up, w_exp_down, w_sh_gate, w_sh_up, w_sh_down, ln2_g, ln2_b):
    bsz, s, d = x.shape
    depth = ada_w.shape[0]
    qr = w_uq.shape[1]
    ah, kvr, qk = w_uk.shape[1], w_uk.shape[2], w_uk.shape[3]
    idim = idx_k_norm_g.shape[1]
    ih = w_qidx.shape[2] // idim
    dv = hgrn_out_norm_g.shape[1]
    nh = w_branch_b.shape[1] // dv
    dk = hgrn_lb_logits.shape[1] // nh
    ne = w_router.shape[2]
    topk = min(IDX_TOPK, s // 4)
    alpha = float((2 * depth) ** 0.25)
    n_tok = bsz * s

    lower_bounds = jnp.cumsum(jax.nn.softmax(hgrn_lb_logits.astype(F32), axis=0), axis=0)
    layout, wtot = _proj_layout(d, qr, kvr, idim, ih, nh * dk, nh * dv)

    for l in range(depth):
        mod = _ada(c, ada_w[l], ada_b[l])[:, None, :]
        shift1, scale1, gate1, shift2, scale2, gate2 = jnp.split(mod, 6, axis=-1)

        cols = []
        for off, wdt, (lo, hi) in sorted(layout.values()):
            cols.append(w_in[l][:, lo:hi].astype(BF16))
            if wdt > hi - lo:
                cols.append(jnp.zeros((d, wdt - (hi - lo)), BF16))
        proj = _inproj(x, scale1, shift1, jnp.concatenate(cols, axis=1))
        blk = lambda name: layout[name][0] // layout[name][1]

        qb_sz = _pick(s, (256, 128))
        q_latT, q_idxT, w_idxT, k_idx, kv_lat = _prep(
            proj, blk("a"), layout["a"][1], (qr, kvr, idim, ih, ah, qk), q_norm_g[l], kv_norm_g[l],
            idx_k_norm_g[l], idx_k_norm_b[l], jnp.swapaxes(w_uq[l], 0, 1).astype(BF16),
            w_uk[l].astype(BF16), jnp.swapaxes(w_qidx[l], 0, 1).astype(BF16), qb_sz)
        kv_latT = jnp.concatenate([jnp.swapaxes(kv_lat, 1, 2), jnp.ones((bsz, 1, s), BF16),
                                   jnp.zeros((bsz, 7, s), BF16)], axis=1)
        o_a = _dsa(q_idxT, w_idxT, q_latT, k_idx, kv_lat, kv_latT,
                   rpb_table, jnp.swapaxes(w_uv[l], 1, 2).astype(BF16), topk, qb_sz)

        o_b = _hgrn(proj, (blk("hq"), blk("hf"), blk("hi"), blk("hg")), lower_bounds[l],
                    hgrn_out_norm_g[l], nh, dk, dv)

        x1, h2, lgT = _postmix(o_a, o_b, proj, blk("ga"), blk("gb"), x, gate1, scale2, shift2,
                               w_branch_a[l].astype(BF16), w_branch_b[l].astype(BF16), w_o[l].astype(BF16),
                               jnp.swapaxes(w_router[l], 0, 1).astype(BF16), ln1_g[l], ln1_b[l], alpha)

        ids, wts, rnk, sizes_f = _route(lgT, router_bias[l])
        eb = EXPERT_BLOCK
        sizes = sizes_f[:, 0].astype(I32)
        padded = (sizes + eb - 1) // eb * eb
        pend = jnp.cumsum(padded)
        pstart = pend - padded
        n_rows = -(-(n_tok * MOE_TOPK + ne * (eb - 1)) // eb) * eb
        blk_start = jnp.arange(n_rows // eb, dtype=I32) * eb
        blk_expert = jnp.minimum(jnp.sum((pend[None, :] <= blk_start[:, None]).astype(I32), axis=1), ne - 1)
        onehot = (blk_expert[:, None] == jnp.arange(ne, dtype=I32)[None, :]).astype(I32)
        blk_size = jnp.sum(onehot * sizes[None, :], axis=1)
        blk_pstart = jnp.sum(onehot * pstart[None, :], axis=1)
        nvalid = jnp.clip(blk_size - (blk_start - blk_pstart), 0, eb).astype(I32)
        dest = _dest(pstart.astype(I32), ids, rnk, ne)
        tm = _pick(s, (128,))
        dest_t = dest.reshape(bsz, MOE_TOPK, s // tm, tm).transpose(0, 2, 1, 3).reshape(n_tok // tm, MOE_TOPK, tm)
        wtsT = jnp.swapaxes(wts, 1, 2).reshape(n_tok, MOE_TOPK)

        h2f = h2.reshape(n_tok, d)
        xs = _dispatch(dest_t, h2f, n_rows)
        ys = _experts(xs, blk_expert, nvalid, w_exp_gate[l], w_exp_up[l], w_exp_down[l])
        x = _final(ys, dest_t, wtsT, h2f, x1.reshape(n_tok, d), gate2, w_sh_gate[l].astype(BF16),
                   w_sh_up[l].astype(BF16), w_sh_down[l].astype(BF16), ln2_g[l], ln2_b[l], alpha,
                   s // tm).reshape(bsz, s, d)
    return x
```

```python
import functools
import math

import numpy as np
import jax
import jax.numpy as jnp
from jax import lax
from jax.experimental import pallas as pl
from jax.experimental.pallas import tpu as pltpu

F32 = jnp.float32
BF16 = jnp.bfloat16
I32 = jnp.int32

EPS = 1e-6
IDX_TOPK = 256
RPB_MAX_DIST = 128
MOE_TOPK = 8
N_GROUPS = 8
TOPK_GROUPS = 4
ROUTED_SCALE = 2.5
EXPERT_BLOCK = 256

V7X_VMEM_LIMIT_BYTES = 56 * 1024 * 1024
LANES = 128
INT_MIN = -2 ** 31
NEG_BIG = -1e30
LOG2E = math.log2(math.e)


def _cparams(*sem):
    return pltpu.CompilerParams(dimension_semantics=tuple(sem), vmem_limit_bytes=V7X_VMEM_LIMIT_BYTES)


def _pick(n, prefs):
    for p in prefs:
        if n % p == 0:
            return p
    return n


def _sigmoid(v):
    return 1.0 / (1.0 + jnp.exp(-v))


def _silu(v):
    return v * _sigmoid(v)


def _ln(v):
    mu = jnp.mean(v, axis=-1, keepdims=True)
    d = v - mu
    var = jnp.mean(d * d, axis=-1, keepdims=True)
    return d * lax.rsqrt(var + EPS)


def _rms(v):
    return v * lax.rsqrt(jnp.mean(v * v, axis=-1, keepdims=True) + EPS)


def _dot(a, b):
    return jnp.dot(a, b, preferred_element_type=F32)


def _dot_nt(a, b):
    return lax.dot_general(a, b, (((1,), (1,)), ((), ())), preferred_element_type=F32)


def _dot_tn(a, b):
    return lax.dot_general(a, b, (((0,), (0,)), ((), ())), preferred_element_type=F32)


def _ada_kernel(c_ref, w_ref, b_ref, o_ref):
    c = c_ref[...]
    o_ref[...] = _dot(_silu(c).astype(BF16), w_ref[...].astype(BF16)) + b_ref[...]


def _ada(c, w, b):
    bsz, d = c.shape
    n = w.shape[1]
    rows = 8
    cp = jnp.zeros((rows, d), F32).at[:bsz].set(c)
    tn = _pick(n, (1024, 512, 256, 128))
    out = pl.pallas_call(
        _ada_kernel,
        grid=(n // tn,),
        in_specs=[pl.BlockSpec((rows, d), lambda j: (0, 0)),
                  pl.BlockSpec((d, tn), lambda j: (0, j)),
                  pl.BlockSpec((1, tn), lambda j: (0, j))],
        out_specs=pl.BlockSpec((rows, tn), lambda j: (0, j)),
        out_shape=jax.ShapeDtypeStruct((rows, n), F32),
        compiler_params=_cparams("arbitrary"),
        name="ada",
    )(cp, w, b.reshape(1, n))
    return out[:bsz]


def _inproj_kernel(x_ref, sc_ref, sh_ref, w_ref, o_ref, h_ref):
    @pl.when(pl.program_id(2) == 0)
    def _():
        h = _ln(x_ref[0]) * (1.0 + sc_ref[0]) + sh_ref[0]
        h_ref[...] = h.astype(BF16)

    o_ref[0] = _dot(h_ref[...], w_ref[...]).astype(o_ref.dtype)


def _inproj(x, scale, shift, w):
    bsz, s, d = x.shape
    n = w.shape[1]
    tm = _pick(s, (1024, 512, 256, 128))
    tn = _pick(n, (1024, 512, 256, 128))
    return pl.pallas_call(
        _inproj_kernel,
        grid=(bsz, s // tm, n // tn),
        in_specs=[pl.BlockSpec((1, tm, d), lambda b, i, j: (b, i, 0)),
                  pl.BlockSpec((1, 1, d), lambda b, i, j: (b, 0, 0)),
                  pl.BlockSpec((1, 1, d), lambda b, i, j: (b, 0, 0)),
                  pl.BlockSpec((d, tn), lambda b, i, j: (0, j))],
        out_specs=pl.BlockSpec((1, tm, tn), lambda b, i, j: (b, i, j)),
        out_shape=jax.ShapeDtypeStruct((bsz, s, n), BF16),
        scratch_shapes=[pltpu.VMEM((tm, d), BF16)],
        compiler_params=_cparams("arbitrary", "arbitrary", "arbitrary"),
        name="inproj",
    )(x, scale, shift, w)


def _prep_kernel(a_ref, qg_ref, kvg_ref, ikg_ref, ikb_ref, wuqT_ref, wuk_ref, wqiT_ref,
                 qlatT_ref, qidxT_ref, widxT_ref, kidx_ref, kv_ref, *, qr, kvr, idim, ih, ah, qk):
    a = a_ref[0]
    af = a.astype(F32)
    cq = (_rms(af[:, :qr]) * qg_ref[...]).astype(BF16)
    ckv = af[:, qr:qr + kvr]
    ki = af[:, qr + kvr:qr + kvr + idim]
    kv_ref[0] = (_rms(ckv) * kvg_ref[...]).astype(BF16)
    kidx_ref[0] = (_ln(ki) * ikg_ref[...] + ikb_ref[...]).astype(BF16)
    tail = a[:, qr + kvr:qr + kvr + LANES]
    eye = (lax.broadcasted_iota(I32, (LANES, LANES), 0) == lax.broadcasted_iota(I32, (LANES, LANES), 1))
    tailT = _dot_nt(jnp.where(eye, 1.0, 0.0).astype(BF16), tail)
    widxT_ref[0] = tailT[idim:idim + ih] * float((ih * idim) ** -0.5)
    tm = a.shape[0]
    qT = _dot_nt(wuqT_ref[...], cq)
    for h in range(ah):
        qh = qT[h * qk:(h + 1) * qk].astype(BF16)
        qlatT_ref[0, 0, :, h * tm:(h + 1) * tm] = (_dot(wuk_ref[h], qh) * float(qk ** -0.5 * LOG2E)).astype(BF16)
    qiT = _dot_nt(wqiT_ref[...], cq)
    for h in range(ih):
        qidxT_ref[0, 0, :, h * tm:(h + 1) * tm] = qiT[h * idim:(h + 1) * idim].astype(BF16)


def _prep(proj, a_blk, wa, dims, q_norm_g, kv_norm_g, ikg, ikb, w_uqT, w_uk, w_qidxT, tm):
    bsz, s, _ = proj.shape
    qr, kvr, idim, ih, ah, qk = dims
    assert idim + ih <= LANES and qr + kvr + LANES <= wa
    kern = functools.partial(_prep_kernel, qr=qr, kvr=kvr, idim=idim, ih=ih, ah=ah, qk=qk)
    full = lambda shape: pl.BlockSpec(shape, lambda b, i: (0,) * len(shape))
    return pl.pallas_call(
        kern,
        grid=(bsz, s // tm),
        in_specs=[pl.BlockSpec((1, tm, wa), lambda b, i: (b, i, a_blk)),
                  full((1, qr)), full((1, kvr)), full((1, idim)), full((1, idim)),
                  full(w_uqT.shape), full(w_uk.shape), full(w_qidxT.shape)],
        out_specs=[pl.BlockSpec((1, 1, kvr, ah * tm), lambda b, i: (b, i, 0, 0)),
                   pl.BlockSpec((1, 1, idim, ih * tm), lambda b, i: (b, i, 0, 0)),
                   pl.BlockSpec((1, ih, tm), lambda b, i: (b, 0, i)),
                   pl.BlockSpec((1, tm, idim), lambda b, i: (b, i, 0)),
                   pl.BlockSpec((1, tm, kvr), lambda b, i: (b, i, 0))],
        out_shape=[jax.ShapeDtypeStruct((bsz, s // tm, kvr, ah * tm), BF16),
                   jax.ShapeDtypeStruct((bsz, s // tm, idim, ih * tm), BF16),
                   jax.ShapeDtypeStruct((bsz, ih, s), F32),
                   jax.ShapeDtypeStruct((bsz, s, idim), BF16),
                   jax.ShapeDtypeStruct((bsz, s, kvr), BF16)],
        compiler_params=_cparams("arbitrary", "arbitrary"),
        name="prep",
    )(proj, q_norm_g.reshape(1, qr), kv_norm_g.reshape(1, kvr), ikg.reshape(1, idim), ikb.reshape(1, idim),
      w_uqT, w_uk, w_qidxT)


def _dsa_kernel(qidxT_ref, widxT_ref, qlatT_ref, kidx_ref, kv_ref, kvT_ref, bkt0_ref, bkt1_ref, rel_ref, wuvT_ref,
                o_ref, keys_ref, gmax_ref, s_ref, p_ref, m_ref, a_ref, acc_ref, t0_ref, t1_ref,
                *, qb_sz, topk, ih, ah, c, nb):
    QB = qb_sz
    qb = pl.program_id(1)
    nchunks = qb + 1
    qpos = qb * QB + lax.broadcasted_iota(I32, (1, QB), 1)
    kpos = lax.broadcasted_iota(I32, (QB, 1), 0)
    wT = widxT_ref[0]
    hcols = lambda h: slice(h * QB, (h + 1) * QB)

    @pl.when(jnp.logical_and(pl.program_id(0) == 0, qb == 0))
    def _():
        for bkt_ref, t_ref in ((bkt0_ref, t0_ref), (bkt1_ref, t1_ref)):
            bkt = bkt_ref[...]
            for h in range(ah):
                tile = jnp.zeros((QB, QB), F32)
                for b in range(nb - 1):
                    tile = jnp.where(bkt == b, rel_ref[b, h], tile)
                t_ref[:, hcols(h)] = tile

    def score_chunk(kc, carry):
        off = pl.multiple_of(kc * QB, QB)
        s_ref[:, :ih * QB] = _dot(kidx_ref[0, pl.ds(off, QB), :], qidxT_ref[0, 0])
        acc = jnp.zeros((QB, QB), F32)
        for h in range(ih):
            acc = acc + wT[h:h + 1] * jnp.maximum(s_ref[:, hcols(h)], 0.0)
        bits = pltpu.bitcast(acc, I32)
        skey = bits ^ ((bits >> 31) & 0x7FFFFFFF)
        causal = (off + kpos) <= qpos
        skey = jnp.where(causal, skey, INT_MIN)
        keys_ref[pl.ds(off, QB), :] = skey
        gmax_ref[...] = jnp.maximum(gmax_ref[...], skey)
        return carry

    gmax_ref[...] = jnp.full(gmax_ref.shape, INT_MIN, I32)
    lax.fori_loop(0, nchunks, score_chunk, 0)

    gmax = gmax_ref[...]
    lo0 = jnp.min(gmax, axis=0, keepdims=True) if QB >= topk else jnp.full((1, QB), INT_MIN, I32)
    hi0 = jnp.max(gmax, axis=0, keepdims=True) + 1

    def count_ge(cand):
        def body(kc, cnt):
            off = pl.multiple_of(kc * QB, QB)
            hit = jnp.where(keys_ref[pl.ds(off, QB), :] >= cand, 1, 0)
            return cnt + jnp.sum(hit.reshape(QB // 8, 8, QB), axis=0)

        cnt = lax.fori_loop(0, nchunks, body, jnp.zeros((8, QB), I32))
        return jnp.sum(cnt.astype(F32), axis=0, keepdims=True)

    def bis_cond(st):
        it, _, _, done = st
        return jnp.logical_and(it < 34, jnp.min(done) < 0.5)

    def bis_body(st):
        it, lo, hi, done = st
        cand = (lo >> 1) + (hi >> 1) + (lo & hi & 1)
        cnt = count_ge(cand)
        ge = cnt >= float(topk)
        conv = cand == lo
        fin = jnp.logical_or(conv, cnt == float(topk))
        thr_new = jnp.where(conv, lo, cand)
        lo = jnp.where(fin, thr_new, jnp.where(ge, cand, lo))
        hi = jnp.where(fin, thr_new + 1, jnp.where(ge, hi, cand))
        return it + 1, lo, hi, jnp.where(fin, 1.0, done)

    _, thr, _, _ = lax.while_loop(bis_cond, bis_body, (jnp.int32(0), lo0, hi0, jnp.zeros((1, QB), F32)))

    m_ref[...] = jnp.full(m_ref.shape, NEG_BIG, F32)
    acc_ref[...] = jnp.zeros(acc_ref.shape, F32)

    def attn_chunk(off, bias_ref, diag):
        sel = keys_ref[pl.ds(off, QB), :] >= thr
        if diag:
            sel = jnp.logical_and(sel, (off + kpos) <= qpos)
        madd = jnp.where(sel, 0.0, NEG_BIG)
        s_ref[:, :ah * QB] = _dot(kv_ref[0, pl.ds(off, QB), :], qlatT_ref[0, 0])
        for g in range(ah * QB // LANES):
            cols = slice(g * LANES, (g + 1) * LANES)
            qcols = slice(g * LANES % QB, g * LANES % QB + LANES)
            s = s_ref[:, cols] + madd[:, qcols]
            if bias_ref is not None:
                s = s + bias_ref[:, cols]
            m_prev = m_ref[:, cols]
            m_new = jnp.maximum(m_prev, jnp.max(s, axis=0, keepdims=True))
            p_ref[:, cols] = jnp.exp2(s - m_new).astype(BF16)
            a_ref[:, cols] = jnp.exp2(m_prev - m_new)
            m_ref[:, cols] = m_new
        acc_ref[...] = a_ref[...] * acc_ref[...] + _dot(kvT_ref[0, :, pl.ds(off, QB)], p_ref[...])

    def far_body(kc, carry):
        attn_chunk(pl.multiple_of(kc * QB, QB), None, False)
        return carry

    lax.fori_loop(0, qb - 1, far_body, 0)

    @pl.when(qb >= 1)
    def _():
        attn_chunk(pl.multiple_of((qb - 1) * QB, QB), t1_ref, False)

    attn_chunk(pl.multiple_of(qb * QB, QB), t0_ref, True)

    outs = []
    for h in range(ah):
        o = (acc_ref[:c, hcols(h)] / acc_ref[c:c + 1, hcols(h)]).astype(BF16)
        outs.append(_dot(wuvT_ref[h], o))
    o_ref[0] = jnp.concatenate(outs, axis=0).T.astype(o_ref.dtype)


def _t5_bucket_np(d, nbuckets):
    max_exact = nbuckets // 2
    dd = np.maximum(d, 1).astype(np.float32)
    large = max_exact + (np.log(dd / np.float32(max_exact)) / np.float32(math.log(RPB_MAX_DIST / max_exact))
                         * np.float32(nbuckets - max_exact)).astype(np.int32)
    large = np.minimum(large, nbuckets - 1)
    return np.where(d < max_exact, d, large).astype(np.int32)


def _dsa(q_idxT, w_idxT, q_latT, k_idx, kv_lat, kv_latT, rpb_table, w_uvT, topk, QB):
    bsz, nqb, idim, ihq = q_idxT.shape
    c, ahq = q_latT.shape[2], q_latT.shape[3]
    ca = kv_latT.shape[1]
    ih, ah = ihq // QB, ahq // QB
    s = nqb * QB
    vd = w_uvT.shape[1]
    nb = rpb_table.shape[0]
    assert QB >= RPB_MAX_DIST
    j = np.arange(QB)[:, None]
    i = np.arange(QB)[None, :]
    bkt0 = jnp.asarray(_t5_bucket_np(np.maximum(i - j, 0), nb))
    bkt1 = jnp.asarray(_t5_bucket_np(QB + i - j, nb))
    assert int(_t5_bucket_np(np.array([RPB_MAX_DIST]), nb)[0]) == nb - 1
    rel = (rpb_table.astype(F32) - rpb_table[nb - 1].astype(F32)[None, :]) * LOG2E
    kern = functools.partial(_dsa_kernel, qb_sz=QB, topk=topk, ih=ih, ah=ah, c=c, nb=nb)
    const = lambda shape: pl.BlockSpec(shape, lambda b, i: (0,) * len(shape), pipeline_mode=pl.Buffered(1))
    hw = max(ih, ah) * QB
    return pl.pallas_call(
        kern,
        grid=(bsz, nqb),
        in_specs=[pl.BlockSpec((1, 1, idim, ih * QB), lambda b, i: (b, i, 0, 0)),
                  pl.BlockSpec((1, ih, QB), lambda b, i: (b, 0, i)),
                  pl.BlockSpec((1, 1, c, ah * QB), lambda b, i: (b, i, 0, 0)),
                  pl.BlockSpec((1, s, idim), lambda b, i: (b, 0, 0)),
                  pl.BlockSpec((1, s, c), lambda b, i: (b, 0, 0)),
                  pl.BlockSpec((1, ca, s), lambda b, i: (b, 0, 0)),
                  const((QB, QB)), const((QB, QB)),
                  pl.BlockSpec(memory_space=pltpu.SMEM), const(w_uvT.shape)],
        out_specs=pl.BlockSpec((1, QB, ah * vd), lambda b, i: (b, i, 0)),
        out_shape=jax.ShapeDtypeStruct((bsz, s, ah * vd), BF16),
        scratch_shapes=[pltpu.VMEM((s, QB), I32),
                        pltpu.VMEM((QB, QB), I32),
                        pltpu.VMEM((QB, hw), F32),
                        pltpu.VMEM((QB, ah * QB), BF16),
                        pltpu.VMEM((1, ah * QB), F32),
                        pltpu.VMEM((1, ah * QB), F32),
                        pltpu.VMEM((ca, ah * QB), F32),
                        pltpu.VMEM((QB, ah * QB), F32),
                        pltpu.VMEM((QB, ah * QB), F32)],
        compiler_params=_cparams("arbitrary", "arbitrary"),
        name="dsa",
    )(q_idxT, w_idxT, q_latT, k_idx, kv_lat, kv_latT, bkt0, bkt1, rel, w_uvT)


HGRN_CHUNK = 64
HGRN_SUB = 16
HGRN_EXP_CLAMP = 80.0


def _hgrn_kernel(hq_ref, hf_ref, hi_ref, hg_ref, lb_ref, g_ref, o_ref, st_ref, *, nh, dk, dv, tt):
    C, SUB = HGRN_CHUNK, HGRN_SUB

    @pl.when(pl.program_id(1) == 0)
    def _():
        st_ref[...] = jnp.zeros(st_ref.shape, F32)

    r = lax.broadcasted_iota(I32, (C, C), 0)
    cc = lax.broadcasted_iota(I32, (C, C), 1)
    tri_mask = r >= cc
    tri = jnp.where(tri_mask, 1.0, 0.0).astype(BF16)
    g = g_ref[...]

    for h in range(nh):
        ko = h * dk
        vo = h * dv
        lb = lb_ref[:, pl.ds(ko, dk)]
        stT = st_ref[h]
        for c in range(tt // C):
            rows = pl.ds(c * C, C)
            hf = hf_ref[0, rows, pl.ds(ko, dk)].astype(F32)
            f = lb + (1.0 - lb) * _sigmoid(hf)
            lf = jnp.log(f)
            t1 = lf.astype(BF16)
            r1 = lf - t1.astype(F32)
            t2 = r1.astype(BF16)
            t3 = (r1 - t2.astype(F32)).astype(BF16)
            b = _dot(tri, t1) + _dot(tri, t2) + _dot(tri, t3)
            kk = 1.0 - f
            hq = hq_ref[0, rows, pl.ds(ko, dk)].astype(F32)
            q = _silu(hq) * float(dk ** -0.5)
            v = hi_ref[0, rows, pl.ds(vo, dv)]
            o_inter = _dot_nt((q * jnp.exp(b)).astype(BF16), stT.astype(BF16))
            parts = []
            for i in range(C // SUB):
                lo, n = i * SUB, (i + 1) * SUB
                bi = b[lo - 1:lo] if i > 0 else jnp.zeros((1, dk), F32)
                qs = (q[lo:n] * jnp.exp(b[lo:n] - bi)).astype(BF16)
                ks = (kk * jnp.exp(jnp.minimum(bi - b, HGRN_EXP_CLAMP))).astype(BF16)
                parts.append(_dot_nt(qs, ks))
            att = jnp.where(tri_mask, jnp.concatenate(parts, axis=0), 0.0)
            o = o_inter + _dot(att.astype(BF16), v)
            b_last = b[C - 1:C]
            k_dec = (kk * jnp.exp(b_last - b)).astype(BF16)
            stT = stT * jnp.exp(b_last) + _dot_tn(v, k_dec)
            hg = hg_ref[0, rows, pl.ds(vo, dv)].astype(F32)
            o_ref[0, rows, pl.ds(vo, dv)] = (_rms(o) * g * _silu(hg)).astype(o_ref.dtype)
        st_ref[h] = stT


def _hgrn(proj, blks, lb, g, nh, dk, dv):
    bsz, s, _ = proj.shape
    tt = _pick(s, (256, 128, 64))
    kern = functools.partial(_hgrn_kernel, nh=nh, dk=dk, dv=dv, tt=tt)
    col = lambda blk, wdt: pl.BlockSpec((1, tt, wdt), lambda b, i: (b, i, blk))
    return pl.pallas_call(
        kern,
        grid=(bsz, s // tt),
        in_specs=[col(blks[0], nh * dk), col(blks[1], nh * dk), col(blks[2], nh * dv), col(blks[3], nh * dv),
                  pl.BlockSpec((1, nh * dk), lambda b, i: (0, 0)),
                  pl.BlockSpec((1, dv), lambda b, i: (0, 0))],
        out_specs=pl.BlockSpec((1, tt, nh * dv), lambda b, i: (b, i, 0)),
        out_shape=jax.ShapeDtypeStruct((bsz, s, nh * dv), BF16),
        scratch_shapes=[pltpu.VMEM((nh, dv, dk), F32)],
        compiler_params=_cparams("arbitrary", "arbitrary"),
        name="hgrn",
    )(proj, proj, proj, proj, lb.reshape(1, nh * dk), g.reshape(1, dv))


def _postmix_kernel(oa_ref, ob_ref, ga_ref, gb_ref, x_ref, g1_ref, sc2_ref, sh2_ref,
                    wa_ref, wb_ref, wo_ref, wrT_ref, lng_ref, lnb_ref,
                    x1_ref, h2_ref, lgT_ref, *, alpha):
    ya = _dot(oa_ref[0], wa_ref[...])
    yb = _dot(ob_ref[0], wb_ref[...])
    mix = _sigmoid(ga_ref[0].astype(F32)) * ya + _sigmoid(gb_ref[0].astype(F32)) * yb
    mixed = _dot(mix.astype(BF16), wo_ref[...])
    x1 = _ln(alpha * x_ref[0] + g1_ref[0] * mixed) * lng_ref[...] + lnb_ref[...]
    x1_ref[0] = x1
    h2 = _ln(x1) * (1.0 + sc2_ref[0]) + sh2_ref[0]
    h2_ref[0] = h2
    lgT_ref[0] = _dot_nt(wrT_ref[...], h2.astype(BF16))


def _postmix(o_a, o_b, proj, ga_blk, gb_blk, x, gate1, scale2, shift2, wa, wb, wo, wrT, lng, lnb, alpha):
    bsz, s, d = x.shape
    ne = wrT.shape[0]
    tm = _pick(s, (256, 128))
    kern = functools.partial(_postmix_kernel, alpha=alpha)
    row = lambda wdt: pl.BlockSpec((1, tm, wdt), lambda b, i: (b, i, 0))
    vec = pl.BlockSpec((1, 1, d), lambda b, i: (b, 0, 0))
    full = lambda a: pl.BlockSpec(a.shape, lambda b, i: (0,) * a.ndim, pipeline_mode=pl.Buffered(1))
    return pl.pallas_call(
        kern,
        grid=(bsz, s // tm),
        in_specs=[row(o_a.shape[2]), row(o_b.shape[2]),
                  pl.BlockSpec((1, tm, d), lambda b, i: (b, i, ga_blk)),
                  pl.BlockSpec((1, tm, d), lambda b, i: (b, i, gb_blk)),
                  row(d), vec, vec, vec,
                  full(wa), full(wb), full(wo), full(wrT),
                  pl.BlockSpec((1, d), lambda b, i: (0, 0)), pl.BlockSpec((1, d), lambda b, i: (0, 0))],
        out_specs=[row(d), row(d), pl.BlockSpec((1, ne, tm), lambda b, i: (b, 0, i))],
        out_shape=[jax.ShapeDtypeStruct((bsz, s, d), F32),
                   jax.ShapeDtypeStruct((bsz, s, d), F32),
                   jax.ShapeDtypeStruct((bsz, ne, s), F32)],
        compiler_params=_cparams("arbitrary", "arbitrary"),
        name="postmix",
    )(o_a, o_b, proj, proj, x, gate1, scale2, shift2, wa, wb, wo, wrT, lng.reshape(1, d), lnb.reshape(1, d))


def _route_kernel(lg_ref, bias_ref, ids_ref, wts_ref, rnk_ref, sizes_ref, upper_ref, carry_ref, *, ne):
    first = jnp.logical_and(pl.program_id(0) == 0, pl.program_id(1) == 0)
    tn = lg_ref.shape[2]

    @pl.when(first)
    def _():
        carry_ref[...] = jnp.zeros(carry_ref.shape, F32)
        r_ = lax.broadcasted_iota(I32, (tn, tn), 0)
        c_ = lax.broadcasted_iota(I32, (tn, tn), 1)
        upper_ref[...] = jnp.where(r_ < c_, 1.0, 0.0).astype(BF16)

    per = ne // N_GROUPS
    s = _sigmoid(lg_ref[0])
    bz = s + bias_ref[...]
    ridx = lax.broadcasted_iota(I32, (per, tn), 0)
    neg_inf = jnp.float32(-jnp.inf)
    gs = []
    for g in range(N_GROUPS):
        blk = bz[g * per:(g + 1) * per]
        m1 = jnp.max(blk, axis=0, keepdims=True)
        first_hit = jnp.min(jnp.where(blk == m1, ridx, per), axis=0, keepdims=True)
        m2 = jnp.max(jnp.where(ridx == first_hit, neg_inf, blk), axis=0, keepdims=True)
        gs.append(m1 + m2)
    emask_rows = []
    for g in range(N_GROUPS):
        rank = jnp.zeros((1, tn), I32)
        for g2 in range(N_GROUPS):
            if g2 == g:
                continue
            beats = (gs[g2] > gs[g]) if g2 > g else (gs[g2] >= gs[g])
            rank = rank + jnp.where(beats, 1, 0)
        emask_rows.append(jnp.broadcast_to(rank < TOPK_GROUPS, (per, tn)))
    emask = jnp.concatenate(emask_rows, axis=0)
    masked = jnp.where(emask, bz, neg_inf)
    eidx = lax.broadcasted_iota(I32, (ne, tn), 0)
    rank = jnp.zeros((ne, tn), I32)
    for e2 in range(ne):
        row = masked[e2:e2 + 1]
        beats = jnp.logical_or(row > masked, jnp.logical_and(row == masked, e2 < eidx))
        rank = rank + jnp.where(beats, 1, 0)
    sel = rank < MOE_TOPK
    sel01 = jnp.where(sel, 1.0, 0.0)
    denom = jnp.sum(jnp.where(sel, s, 0.0), axis=0, keepdims=True)
    wn = s / denom * ROUTED_SCALE
    before = _dot(sel01.astype(BF16), upper_ref[...]) + carry_ref[:, 0:1]
    ids, wts, rnk = [], [], []
    for k in range(MOE_TOPK):
        hit = rank == k
        ids.append(jnp.sum(jnp.where(hit, eidx, 0), axis=0, keepdims=True))
        wts.append(jnp.sum(jnp.where(hit, wn, 0.0), axis=0, keepdims=True))
        rnk.append(jnp.sum(jnp.where(hit, before, 0.0), axis=0, keepdims=True))
    ids_ref[0] = jnp.concatenate(ids, axis=0)
    wts_ref[0] = jnp.concatenate(wts, axis=0)
    rnk_ref[0] = jnp.concatenate(rnk, axis=0).astype(I32)
    carry_ref[...] = carry_ref[...] + jnp.sum(sel01, axis=1, keepdims=True)
    sizes_ref[...] = carry_ref[...]


def _route(lgT, bias):
    bsz, ne, s = lgT.shape
    tn = _pick(s, (1024, 512, 256, 128))
    kern = functools.partial(_route_kernel, ne=ne)
    slot = pl.BlockSpec((1, MOE_TOPK, tn), lambda b, j: (b, 0, j))
    return pl.pallas_call(
        kern,
        grid=(bsz, s // tn),
        in_specs=[pl.BlockSpec((1, ne, tn), lambda b, j: (b, 0, j)),
                  pl.BlockSpec((ne, 1), lambda b, j: (0, 0))],
        out_specs=[slot, slot, slot, pl.BlockSpec((ne, LANES), lambda b, j: (0, 0))],
        out_shape=[jax.ShapeDtypeStruct((bsz, MOE_TOPK, s), I32),
                   jax.ShapeDtypeStruct((bsz, MOE_TOPK, s), F32),
                   jax.ShapeDtypeStruct((bsz, MOE_TOPK, s), I32),
                   jax.ShapeDtypeStruct((ne, LANES), F32)],
        scratch_shapes=[pltpu.VMEM((tn, tn), BF16), pltpu.VMEM((ne, LANES), F32)],
        compiler_params=_cparams("arbitrary", "arbitrary"),
        name="route",
    )(lgT, bias.reshape(ne, 1))


def _dest_kernel(pstart_ref, ids_ref, rnk_ref, o_ref, *, ne):
    ids = ids_ref[0]
    base = jnp.zeros(ids.shape, I32)
    for e in range(ne):
        base = jnp.where(ids == e, pstart_ref[e], base)
    o_ref[0] = base + rnk_ref[0]


def _dest(pstart, ids, rnk, ne):
    bsz, k, s = ids.shape
    tn = _pick(s, (2048, 1024, 512, 256, 128))
    blk = lambda: pl.BlockSpec((1, k, tn), lambda b, j, ps: (b, 0, j))
    return pl.pallas_call(
        functools.partial(_dest_kernel, ne=ne),
        grid_spec=pltpu.PrefetchScalarGridSpec(num_scalar_prefetch=1, grid=(bsz, s // tn),
                                               in_specs=[blk(), blk()], out_specs=blk()),
        out_shape=jax.ShapeDtypeStruct((bsz, k, s), I32),
        compiler_params=_cparams("arbitrary", "arbitrary"),
        name="dest",
    )(pstart, ids, rnk)


def _dispatch_kernel(dst_ref, h_ref, xs_hbm, sem, *, tm, topk):
    def start(t, c):
        for k in range(topk):
            pltpu.make_async_copy(h_ref.at[pl.ds(t, 1)], xs_hbm.at[pl.ds(dst_ref[0, k, t], 1)], sem.at[0]).start()
        return c

    lax.fori_loop(0, tm, start, 0)

    def wait(t, c):
        for k in range(topk):
            pltpu.make_async_copy(h_ref.at[pl.ds(t, 1)], xs_hbm.at[pl.ds(0, 1)], sem.at[0]).wait()
        return c

    lax.fori_loop(0, tm, wait, 0)


def _dispatch(dest, h2, n_rows):
    n, d = h2.shape
    nt, topk, tm = dest.shape
    return pl.pallas_call(
        functools.partial(_dispatch_kernel, tm=tm, topk=topk),
        grid=(nt,),
        in_specs=[pl.BlockSpec((1, topk, tm), lambda i: (i, 0, 0), memory_space=pltpu.SMEM),
                  pl.BlockSpec((tm, d), lambda i: (i, 0))],
        out_specs=pl.BlockSpec(memory_space=pl.ANY),
        out_shape=jax.ShapeDtypeStruct((n_rows, d), h2.dtype),
        scratch_shapes=[pltpu.SemaphoreType.DMA((1,))],
        compiler_params=_cparams("arbitrary"),
        name="dispatch",
    )(dest, h2)


def _experts_kernel(be_ref, nv_ref, x_ref, wg_ref, wu_ref, wd_ref, y_ref, wgb, wub, wdb, *, blk):
    i = pl.program_id(0)
    prev_e = be_ref[jnp.maximum(i - 1, 0)]

    @pl.when(jnp.logical_or(i == 0, be_ref[i] != prev_e))
    def _():
        wgb[...] = wg_ref[0].astype(BF16)
        wub[...] = wu_ref[0].astype(BF16)
        wdb[...] = wd_ref[0].astype(BF16)

    @pl.when(nv_ref[i] > 0)
    def _():
        rows = lax.broadcasted_iota(I32, (blk, 1), 0)
        x = jnp.where(rows < nv_ref[i], x_ref[...], 0.0).astype(BF16)
        act = (_silu(_dot(x, wgb[...])) * _dot(x, wub[...])).astype(BF16)
        y_ref[...] = _dot(act, wdb[...])


def _experts(xs, be, nvalid, wg, wu, wd):
    n_rows, d = xs.shape
    ne, _, f = wg.shape
    blk = EXPERT_BLOCK
    nblocks = n_rows // blk
    grid_spec = pltpu.PrefetchScalarGridSpec(
        num_scalar_prefetch=2,
        grid=(nblocks,),
        in_specs=[pl.BlockSpec((blk, d), lambda i, be, nv: (i, 0)),
                  pl.BlockSpec((1, d, f), lambda i, be, nv: (be[i], 0, 0)),
                  pl.BlockSpec((1, d, f), lambda i, be, nv: (be[i], 0, 0)),
                  pl.BlockSpec((1, f, d), lambda i, be, nv: (be[i], 0, 0))],
        out_specs=pl.BlockSpec((blk, d), lambda i, be, nv: (i, 0)),
        scratch_shapes=[pltpu.VMEM((d, f), BF16), pltpu.VMEM((d, f), BF16), pltpu.VMEM((f, d), BF16)],
    )
    return pl.pallas_call(
        functools.partial(_experts_kernel, blk=blk),
        grid_spec=grid_spec,
        out_shape=jax.ShapeDtypeStruct((n_rows, d), F32),
        compiler_params=_cparams("arbitrary"),
        name="experts",
    )(be, nvalid, xs, wg, wu, wd)


def _final_kernel(dst_ref, dstn_ref, wts_ref, h2_ref, x1_ref, g2_ref, wg_ref, wu_ref, wd_ref, lng_ref, lnb_ref,
                  ys_hbm, o_ref, ybuf, sem, *, alpha, topk, tm, nt):
    i = pl.program_id(0)
    slot = lax.rem(i, 2)

    def gather_start(ids_ref, sl):
        def body(t, c):
            for k in range(topk):
                pltpu.make_async_copy(ys_hbm.at[pl.ds(ids_ref[0, k, t], 1)], ybuf.at[sl, k, pl.ds(t, 1)],
                                      sem.at[sl]).start()
            return c
        lax.fori_loop(0, tm, body, 0)

    @pl.when(i == 0)
    def _():
        gather_start(dst_ref, 0)

    for t in range(tm):
        for k in range(topk):
            pltpu.make_async_copy(ys_hbm.at[pl.ds(dstn_ref[0, k, t], 1)], ybuf.at[1 - slot, k, pl.ds(t, 1)],
                                  sem.at[1 - slot]).start()

    h = h2_ref[...].astype(BF16)
    y = _dot((_silu(_dot(h, wg_ref[...])) * _dot(h, wu_ref[...])).astype(BF16), wd_ref[...])

    def wait_slot(sl):
        def body(t, c):
            for k in range(topk):
                pltpu.make_async_copy(ys_hbm.at[pl.ds(0, 1)], ybuf.at[sl, k, pl.ds(t, 1)], sem.at[sl]).wait()
            return c
        lax.fori_loop(0, tm, body, 0)

    wait_slot(slot)
    w = wts_ref[...]
    for k in range(topk):
        y = y + w[:, k:k + 1] * ybuf[slot, k]
    o_ref[...] = _ln(alpha * x1_ref[...] + g2_ref[0] * y) * lng_ref[...] + lnb_ref[...]

    @pl.when(i == nt - 1)
    def _():
        wait_slot(1 - slot)


def _final(ys, dest, wtsT, h2, x1, gate2, wg, wu, wd, lng, lnb, alpha, tiles_per_batch):
    n, d = x1.shape
    nt, topk, tm = dest.shape
    kern = functools.partial(_final_kernel, alpha=alpha, topk=topk, tm=tm, nt=nt)
    row = pl.BlockSpec((tm, d), lambda i: (i, 0))
    full = lambda a: pl.BlockSpec(a.shape, lambda i: (0,) * a.ndim, pipeline_mode=pl.Buffered(1))
    return pl.pallas_call(
        kern,
        grid=(nt,),
        in_specs=[pl.BlockSpec((1, topk, tm), lambda i: (i, 0, 0), memory_space=pltpu.SMEM),
                  pl.BlockSpec((1, topk, tm), lambda i: (jnp.minimum(i + 1, nt - 1), 0, 0), memory_space=pltpu.SMEM),
                  pl.BlockSpec((tm, topk), lambda i: (i, 0)),
                  row, row, pl.BlockSpec((1, 1, d), lambda i: (i // tiles_per_batch, 0, 0)),
                  full(wg), full(wu), full(wd),
                  pl.BlockSpec((1, d), lambda i: (0, 0)), pl.BlockSpec((1, d), lambda i: (0, 0)),
                  pl.BlockSpec(memory_space=pl.ANY)],
        out_specs=row,
        out_shape=jax.ShapeDtypeStruct((n, d), F32),
        scratch_shapes=[pltpu.VMEM((2, topk, tm, d), F32), pltpu.SemaphoreType.DMA((2,))],
        compiler_params=_cparams("arbitrary"),
        name="final",
    )(dest, dest, wtsT, h2, x1, gate2, wg, wu, wd, lng.reshape(1, d), lnb.reshape(1, d), ys)


def _proj_layout(d, qr, kvr, idim, ih, hk, hv):
    src = np.cumsum([0, qr, kvr, idim, ih, hk, hk, hv, hv, d, d])
    wa = -(-(qr + kvr + idim + ih) // LANES) * LANES
    wa = max(wa, 1 << (wa - 1).bit_length())
    pieces = [("a", wa, (int(src[0]), int(src[4]))),
              ("hq", hk, (int(src[4]), int(src[5]))), ("hf", hk, (int(src[5]), int(src[6]))),
              ("hi", hv, (int(src[6]), int(src[7]))), ("hg", hv, (int(src[7]), int(src[8]))),
              ("ga", d, (int(src[8]), int(src[9]))), ("gb", d, (int(src[9]), int(src[10])))]
    pieces.sort(key=lambda p: -p[1])
    off = 0
    layout = {}
    for name, wdt, rng in pieces:
        assert off % wdt == 0
        layout[name] = (off, wdt, rng)
        off += wdt
    return layout, off


def kernel(x, c, rpb_table, hgrn_lb_logits, ada_w, ada_b, w_in, q_norm_g, kv_norm_g, w_uq, w_uk, w_uv, w_qidx,
           idx_k_norm_g, idx_k_norm_b, hgrn_out_norm_g, w_branch_a, w_branch_b, w_o, ln1_g, ln1_b, w_router,
           router_bias, w_exp_gate, w_exp_up, w_exp_down, w_sh_gate, w_sh_up, w_sh_down, ln2_g, ln2_b):
    bsz, s, d = x.shape
    depth = ada_w.shape[0]
    qr = w_uq.shape[1]
    ah, kvr, qk = w_uk.shape[1], w_uk.shape[2], w_uk.shape[3]
    idim = idx_k_norm_g.shape[1]
    ih = w_qidx.shape[2] // idim
    dv = hgrn_out_norm_g.shape[1]
    nh = w_branch_b.shape[1] // dv
    dk = hgrn_lb_logits.shape[1] // nh
    ne = w_router.shape[2]
    topk = min(IDX_TOPK, s // 4)
    alpha = float((2 * depth) ** 0.25)
    n_tok = bsz * s

    lower_bounds = jnp.cumsum(jax.nn.softmax(hgrn_lb_logits.astype(F32), axis=0), axis=0)
    layout, wtot = _proj_layout(d, qr, kvr, idim, ih, nh * dk, nh * dv)

    for l in range(depth):
        mod = _ada(c, ada_w[l], ada_b[l])[:, None, :]
        shift1, scale1, gate1, shift2, scale2, gate2 = jnp.split(mod, 6, axis=-1)

        cols = []
        for off, wdt, (lo, hi) in sorted(layout.values()):
            cols.append(w_in[l][:, lo:hi].astype(BF16))
            if wdt > hi - lo:
                cols.append(jnp.zeros((d, wdt - (hi - lo)), BF16))
        proj = _inproj(x, scale1, shift1, jnp.concatenate(cols, axis=1))
        blk = lambda name: layout[name][0] // layout[name][1]

        qb_sz = _pick(s, (256, 128))
        q_latT, q_idxT, w_idxT, k_idx, kv_lat = _prep(
            proj, blk("a"), layout["a"][1], (qr, kvr, idim, ih, ah, qk), q_norm_g[l], kv_norm_g[l],
            idx_k_norm_g[l], idx_k_norm_b[l], jnp.swapaxes(w_uq[l], 0, 1).astype(BF16),
            w_uk[l].astype(BF16), jnp.swapaxes(w_qidx[l], 0, 1).astype(BF16), qb_sz)
        kv_latT = jnp.concatenate([jnp.swapaxes(kv_lat, 1, 2), jnp.ones((bsz, 1, s), BF16),
                                   jnp.zeros((bsz, 7, s), BF16)], axis=1)
        o_a = _dsa(q_idxT, w_idxT, q_latT, k_idx, kv_lat, kv_latT,
                   rpb_table, jnp.swapaxes(w_uv[l], 1, 2).astype(BF16), topk, qb_sz)

        o_b = _hgrn(proj, (blk("hq"), blk("hf"), blk("hi"), blk("hg")), lower_bounds[l],
                    hgrn_out_norm_g[l], nh, dk, dv)

        x1, h2, lgT = _postmix(o_a, o_b, proj, blk("ga"), blk("gb"), x, gate1, scale2, shift2,
                               w_branch_a[l].astype(BF16), w_branch_b[l].astype(BF16), w_o[l].astype(BF16),
                               jnp.swapaxes(w_router[l], 0, 1).astype(BF16), ln1_g[l], ln1_b[l], alpha)

        ids, wts, rnk, sizes_f = _route(lgT, router_bias[l])
        eb = EXPERT_BLOCK
        sizes = sizes_f[:, 0].astype(I32)
        padded = (sizes + eb - 1) // eb * eb
        pend = jnp.cumsum(padded)
        pstart = pend - padded
        n_rows = -(-(n_tok * MOE_TOPK + ne * (eb - 1)) // eb) * eb
        blk_start = jnp.arange(n_rows // eb, dtype=I32) * eb
        blk_expert = jnp.minimum(jnp.sum((pend[None, :] <= blk_start[:, None]).astype(I32), axis=1), ne - 1)
        onehot = (blk_expert[:, None] == jnp.arange(ne, dtype=I32)[None, :]).astype(I32)
        blk_size = jnp.sum(onehot * sizes[None, :], axis=1)
        blk_pstart = jnp.sum(onehot * pstart[None, :], axis=1)
        nvalid = jnp.clip(blk_size - (blk_start - blk_pstart), 0, eb).astype(I32)
        dest = _dest(pstart.astype(I32), ids, rnk, ne)
        tm = _pick(s, (128,))
        dest_t = dest.reshape(bsz, MOE_TOPK, s // tm, tm).transpose(0, 2, 1, 3).reshape(n_tok // tm, MOE_TOPK, tm)
        wtsT = jnp.swapaxes(wts, 1, 2).reshape(n_tok, MOE_TOPK)

        h2f = h2.reshape(n_tok, d)
        xs = _dispatch(dest_t, h2f, n_rows)
        ys = _experts(xs, blk_expert, nvalid, w_exp_gate[l], w_exp_up[l], w_exp_down[l])
        x = _final(ys, dest_t, wtsT, h2f, x1.reshape(n_tok, d), gate2, w_sh_gate[l].astype(BF16),
                   w_sh_up[l].astype(BF16), w_sh_down[l].astype(BF16), ln2_g[l], ln2_b[l], alpha,
                   s // tm).reshape(bsz, s, d)
    return x
```

```python
import functools
import math

import numpy as np
import jax
import jax.numpy as jnp
from jax import lax
from jax.experimental import pallas as pl
from jax.experimental.pallas import tpu as pltpu

F32 = jnp.float32
BF16 = jnp.bfloat16
I32 = jnp.int32
U32 = jnp.uint32

EPS = 1e-6
IDX_TOPK = 256
RPB_MAX_DIST = 128
MOE_TOPK = 8
N_GROUPS = 8
TOPK_GROUPS = 4
ROUTED_SCALE = 2.5
EXPERT_BLOCK = 256

V7X_VMEM_LIMIT_BYTES = 56 * 1024 * 1024
LANES = 128
INT_MIN = -2 ** 31
NEG_BIG = -1e30
LOG2E = math.log2(math.e)


def _cparams(*sem):
    return pltpu.CompilerParams(dimension_semantics=tuple(sem), vmem_limit_bytes=V7X_VMEM_LIMIT_BYTES)


def _pick(n, prefs):
    for p in prefs:
        if n % p == 0:
            return p
    return n


def _sigmoid(v):
    return 1.0 / (1.0 + jnp.exp(-v))


def _silu(v):
    return v * _sigmoid(v)


def _ln(v):
    mu = jnp.mean(v, axis=-1, keepdims=True)
    d = v - mu
    var = jnp.mean(d * d, axis=-1, keepdims=True)
    return d * lax.rsqrt(var + EPS)


def _rms(v):
    return v * lax.rsqrt(jnp.mean(v * v, axis=-1, keepdims=True) + EPS)


def _dot(a, b):
    return jnp.dot(a, b, preferred_element_type=F32)


def _pack_bf16_pairs(v):
    n = v.shape[1] // 2
    lo = pltpu.bitcast(v[:, :n].astype(BF16).astype(F32), U32) >> 16
    hi = pltpu.bitcast(v[:, n:].astype(BF16).astype(F32), U32) & jnp.uint32(0xFFFF0000)
    return lo | hi


def _unpack_bf16_pairs(u):
    return pltpu.bitcast(u << 16, F32), pltpu.bitcast(u & jnp.uint32(0xFFFF0000), F32)


def _dot_nt(a, b):
    return lax.dot_general(a, b, (((1,), (1,)), ((), ())), preferred_element_type=F32)


def _dot_tn(a, b):
    return lax.dot_general(a, b, (((0,), (0,)), ((), ())), preferred_element_type=F32)


def _ada_kernel(c_ref, w_ref, b_ref, o_ref):
    c = c_ref[...]
    o_ref[...] = _dot(_silu(c).astype(BF16), w_ref[...].astype(BF16)) + b_ref[...]


def _ada(c, w, b):
    bsz, d = c.shape
    n = w.shape[1]
    rows = 8
    cp = jnp.zeros((rows, d), F32).at[:bsz].set(c)
    tn = _pick(n, (1024, 512, 256, 128))
    out = pl.pallas_call(
        _ada_kernel,
        grid=(n // tn,),
        in_specs=[pl.BlockSpec((rows, d), lambda j: (0, 0)),
                  pl.BlockSpec((d, tn), lambda j: (0, j)),
                  pl.BlockSpec((1, tn), lambda j: (0, j))],
        out_specs=pl.BlockSpec((rows, tn), lambda j: (0, j)),
        out_shape=jax.ShapeDtypeStruct((rows, n), F32),
        compiler_params=_cparams("arbitrary"),
        name="ada",
    )(cp, w, b.reshape(1, n))
    return out[:bsz]


def _inproj_kernel(x_ref, sc_ref, sh_ref, w_ref, o_ref, h_ref):
    @pl.when(pl.program_id(2) == 0)
    def _():
        h = _ln(x_ref[0]) * (1.0 + sc_ref[0]) + sh_ref[0]
        h_ref[...] = h.astype(BF16)

    o_ref[0] = _dot(h_ref[...], w_ref[...]).astype(o_ref.dtype)


def _inproj(x, scale, shift, w):
    bsz, s, d = x.shape
    n = w.shape[1]
    tm = _pick(s, (1024, 512, 256, 128))
    tn = _pick(n, (1024, 512, 256, 128))
    return pl.pallas_call(
        _inproj_kernel,
        grid=(bsz, s // tm, n // tn),
        in_specs=[pl.BlockSpec((1, tm, d), lambda b, i, j: (b, i, 0)),
                  pl.BlockSpec((1, 1, d), lambda b, i, j: (b, 0, 0)),
                  pl.BlockSpec((1, 1, d), lambda b, i, j: (b, 0, 0)),
                  pl.BlockSpec((d, tn), lambda b, i, j: (0, j))],
        out_specs=pl.BlockSpec((1, tm, tn), lambda b, i, j: (b, i, j)),
        out_shape=jax.ShapeDtypeStruct((bsz, s, n), BF16),
        scratch_shapes=[pltpu.VMEM((tm, d), BF16)],
        compiler_params=_cparams("arbitrary", "arbitrary", "arbitrary"),
        name="inproj",
    )(x, scale, shift, w)


def _prep_kernel(a_ref, qg_ref, kvg_ref, ikg_ref, ikb_ref, wuqT_ref, wuk_ref, wqiT_ref,
                 qlatT_ref, qidxT_ref, widxT_ref, kidx_ref, kv_ref, *, qr, kvr, idim, ih, ah, qk):
    a = a_ref[0]
    af = a.astype(F32)
    cq = (_rms(af[:, :qr]) * qg_ref[...]).astype(BF16)
    ckv = af[:, qr:qr + kvr]
    ki = af[:, qr + kvr:qr + kvr + idim]
    kv_ref[0] = (_rms(ckv) * kvg_ref[...]).astype(BF16)
    kidx_ref[0] = (_ln(ki) * ikg_ref[...] + ikb_ref[...]).astype(BF16)
    tail = a[:, qr + kvr:qr + kvr + LANES]
    eye = (lax.broadcasted_iota(I32, (LANES, LANES), 0) == lax.broadcasted_iota(I32, (LANES, LANES), 1))
    tailT = _dot_nt(jnp.where(eye, 1.0, 0.0).astype(BF16), tail)
    widxT_ref[0] = tailT[idim:idim + ih] * float((ih * idim) ** -0.5)
    tm = a.shape[0]
    qT = _dot_nt(wuqT_ref[...], cq)
    for h in range(ah):
        qh = qT[h * qk:(h + 1) * qk].astype(BF16)
        qlatT_ref[0, 0, :, h * tm:(h + 1) * tm] = (_dot(wuk_ref[h], qh) * float(qk ** -0.5 * LOG2E)).astype(BF16)
    qiT = _dot_nt(wqiT_ref[...], cq)
    for h in range(ih):
        qidxT_ref[0, 0, :, h * tm:(h + 1) * tm] = qiT[h * idim:(h + 1) * idim].astype(BF16)


def _prep(proj, a_blk, wa, dims, q_norm_g, kv_norm_g, ikg, ikb, w_uqT, w_uk, w_qidxT, tm):
    bsz, s, _ = proj.shape
    qr, kvr, idim, ih, ah, qk = dims
    assert idim + ih <= LANES and qr + kvr + LANES <= wa
    kern = functools.partial(_prep_kernel, qr=qr, kvr=kvr, idim=idim, ih=ih, ah=ah, qk=qk)
    full = lambda shape: pl.BlockSpec(shape, lambda b, i: (0,) * len(shape))
    return pl.pallas_call(
        kern,
        grid=(bsz, s // tm),
        in_specs=[pl.BlockSpec((1, tm, wa), lambda b, i: (b, i, a_blk)),
                  full((1, qr)), full((1, kvr)), full((1, idim)), full((1, idim)),
                  full(w_uqT.shape), full(w_uk.shape), full(w_qidxT.shape)],
        out_specs=[pl.BlockSpec((1, 1, kvr, ah * tm), lambda b, i: (b, i, 0, 0)),
                   pl.BlockSpec((1, 1, idim, ih * tm), lambda b, i: (b, i, 0, 0)),
                   pl.BlockSpec((1, ih, tm), lambda b, i: (b, 0, i)),
                   pl.BlockSpec((1, tm, idim), lambda b, i: (b, i, 0)),
                   pl.BlockSpec((1, tm, kvr), lambda b, i: (b, i, 0))],
        out_shape=[jax.ShapeDtypeStruct((bsz, s // tm, kvr, ah * tm), BF16),
                   jax.ShapeDtypeStruct((bsz, s // tm, idim, ih * tm), BF16),
                   jax.ShapeDtypeStruct((bsz, ih, s), F32),
                   jax.ShapeDtypeStruct((bsz, s, idim), BF16),
                   jax.ShapeDtypeStruct((bsz, s, kvr), BF16)],
        compiler_params=_cparams("arbitrary", "arbitrary"),
        name="prep",
    )(proj, q_norm_g.reshape(1, qr), kv_norm_g.reshape(1, kvr), ikg.reshape(1, idim), ikb.reshape(1, idim),
      w_uqT, w_uk, w_qidxT)


def _dsa_kernel(qidxT_ref, widxT_ref, qlatT_ref, kidx_ref, kv_ref, kvT_ref, bkt0_ref, bkt1_ref, rel_ref, wuvT_ref,
                o_ref, keys_ref, gmax_ref, s_ref, p_ref, m_ref, a_ref, acc_ref, t0_ref, t1_ref,
                *, qb_sz, topk, ih, ah, c, nb):
    QB = qb_sz
    qb = pl.program_id(1)
    nchunks = qb + 1
    qpos = qb * QB + lax.broadcasted_iota(I32, (1, QB), 1)
    kpos = lax.broadcasted_iota(I32, (QB, 1), 0)
    wT = widxT_ref[0]
    hcols = lambda h: slice(h * QB, (h + 1) * QB)

    @pl.when(jnp.logical_and(pl.program_id(0) == 0, qb == 0))
    def _():
        for bkt_ref, t_ref in ((bkt0_ref, t0_ref), (bkt1_ref, t1_ref)):
            bkt = bkt_ref[...]
            for h in range(ah):
                tile = jnp.zeros((QB, QB), F32)
                for b in range(nb - 1):
                    tile = jnp.where(bkt == b, rel_ref[b, h], tile)
                t_ref[:, hcols(h)] = tile

    def score_chunk(kc, carry):
        off = pl.multiple_of(kc * QB, QB)
        s_ref[:, :ih * QB] = _dot(kidx_ref[0, pl.ds(off, QB), :], qidxT_ref[0, 0])
        acc = jnp.zeros((QB, QB), F32)
        for h in range(ih):
            acc = acc + wT[h:h + 1] * jnp.maximum(s_ref[:, hcols(h)], 0.0)
        bits = pltpu.bitcast(acc, I32)
        skey = bits ^ ((bits >> 31) & 0x7FFFFFFF)
        causal = (off + kpos) <= qpos
        skey = jnp.where(causal, skey, INT_MIN)
        keys_ref[pl.ds(off, QB), :] = skey
        gmax_ref[...] = jnp.maximum(gmax_ref[...], skey)
        return carry

    gmax_ref[...] = jnp.full(gmax_ref.shape, INT_MIN, I32)
    lax.fori_loop(0, nchunks, score_chunk, 0)

    gmax = gmax_ref[...]
    lo0 = jnp.min(gmax, axis=0, keepdims=True) if QB >= topk else jnp.full((1, QB), INT_MIN, I32)
    hi0 = jnp.max(gmax, axis=0, keepdims=True) + 1

    def count_ge(cand):
        def body(kc, cnt):
            off = pl.multiple_of(kc * QB, QB)
            hit = jnp.where(keys_ref[pl.ds(off, QB), :] >= cand, 1, 0)
            return cnt + jnp.sum(hit.reshape(QB // 8, 8, QB), axis=0)

        cnt = lax.fori_loop(0, nchunks, body, jnp.zeros((8, QB), I32))
        return jnp.sum(cnt.astype(F32), axis=0, keepdims=True)

    def bis_cond(st):
        it, _, _, done = st
        return jnp.logical_and(it < 34, jnp.min(done) < 0.5)

    def bis_body(st):
        it, lo, hi, done = st
        cand = (lo >> 1) + (hi >> 1) + (lo & hi & 1)
        cnt = count_ge(cand)
        ge = cnt >= float(topk)
        conv = cand == lo
        fin = jnp.logical_or(conv, cnt == float(topk))
        thr_new = jnp.where(conv, lo, cand)
        lo = jnp.where(fin, thr_new, jnp.where(ge, cand, lo))
        hi = jnp.where(fin, thr_new + 1, jnp.where(ge, hi, cand))
        return it + 1, lo, hi, jnp.where(fin, 1.0, done)

    _, thr, _, _ = lax.while_loop(bis_cond, bis_body, (jnp.int32(0), lo0, hi0, jnp.zeros((1, QB), F32)))

    m_ref[...] = jnp.full(m_ref.shape, NEG_BIG, F32)
    acc_ref[...] = jnp.zeros(acc_ref.shape, F32)

    def attn_chunk(off, bias_ref, diag):
        sel = keys_ref[pl.ds(off, QB), :] >= thr
        if diag:
            sel = jnp.logical_and(sel, (off + kpos) <= qpos)
        madd = jnp.where(sel, 0.0, NEG_BIG)
        s_ref[:, :ah * QB] = _dot(kv_ref[0, pl.ds(off, QB), :], qlatT_ref[0, 0])
        for g in range(ah * QB // LANES):
            cols = slice(g * LANES, (g + 1) * LANES)
            qcols = slice(g * LANES % QB, g * LANES % QB + LANES)
            s = s_ref[:, cols] + madd[:, qcols]
            if bias_ref is not None:
                s = s + bias_ref[:, cols]
            m_prev = m_ref[:, cols]
            m_new = jnp.maximum(m_prev, jnp.max(s, axis=0, keepdims=True))
            p_ref[:, cols] = jnp.exp2((s - m_new).astype(BF16))
            a_ref[:, cols] = jnp.exp2(m_prev - m_new)
            m_ref[:, cols] = m_new
        acc_ref[...] = a_ref[...] * acc_ref[...] + _dot(kvT_ref[0, :, pl.ds(off, QB)], p_ref[...])

    def far_body(kc, carry):
        attn_chunk(pl.multiple_of(kc * QB, QB), None, False)
        return carry

    lax.fori_loop(0, qb - 1, far_body, 0)

    @pl.when(qb >= 1)
    def _():
        attn_chunk(pl.multiple_of((qb - 1) * QB, QB), t1_ref, False)

    attn_chunk(pl.multiple_of(qb * QB, QB), t0_ref, True)

    outs = []
    for h in range(ah):
        o = (acc_ref[:c, hcols(h)] / acc_ref[c:c + 1, hcols(h)]).astype(BF16)
        outs.append(_dot(wuvT_ref[h], o))
    o_ref[0] = jnp.concatenate(outs, axis=0).T.astype(o_ref.dtype)


def _t5_bucket_np(d, nbuckets):
    max_exact = nbuckets // 2
    dd = np.maximum(d, 1).astype(np.float32)
    large = max_exact + (np.log(dd / np.float32(max_exact)) / np.float32(math.log(RPB_MAX_DIST / max_exact))
                         * np.float32(nbuckets - max_exact)).astype(np.int32)
    large = np.minimum(large, nbuckets - 1)
    return np.where(d < max_exact, d, large).astype(np.int32)


def _dsa(q_idxT, w_idxT, q_latT, k_idx, kv_lat, kv_latT, rpb_table, w_uvT, topk, QB):
    bsz, nqb, idim, ihq = q_idxT.shape
    c, ahq = q_latT.shape[2], q_latT.shape[3]
    ca = kv_latT.shape[1]
    ih, ah = ihq // QB, ahq // QB
    s = nqb * QB
    vd = w_uvT.shape[1]
    nb = rpb_table.shape[0]
    assert QB >= RPB_MAX_DIST
    j = np.arange(QB)[:, None]
    i = np.arange(QB)[None, :]
    bkt0 = jnp.asarray(_t5_bucket_np(np.maximum(i - j, 0), nb))
    bkt1 = jnp.asarray(_t5_bucket_np(QB + i - j, nb))
    assert int(_t5_bucket_np(np.array([RPB_MAX_DIST]), nb)[0]) == nb - 1
    rel = (rpb_table.astype(F32) - rpb_table[nb - 1].astype(F32)[None, :]) * LOG2E
    kern = functools.partial(_dsa_kernel, qb_sz=QB, topk=topk, ih=ih, ah=ah, c=c, nb=nb)
    const = lambda shape: pl.BlockSpec(shape, lambda b, i: (0,) * len(shape), pipeline_mode=pl.Buffered(1))
    hw = max(ih, ah) * QB
    return pl.pallas_call(
        kern,
        grid=(bsz, nqb),
        in_specs=[pl.BlockSpec((1, 1, idim, ih * QB), lambda b, i: (b, i, 0, 0)),
                  pl.BlockSpec((1, ih, QB), lambda b, i: (b, 0, i)),
                  pl.BlockSpec((1, 1, c, ah * QB), lambda b, i: (b, i, 0, 0)),
                  pl.BlockSpec((1, s, idim), lambda b, i: (b, 0, 0)),
                  pl.BlockSpec((1, s, c), lambda b, i: (b, 0, 0)),
                  pl.BlockSpec((1, ca, s), lambda b, i: (b, 0, 0)),
                  const((QB, QB)), const((QB, QB)),
                  pl.BlockSpec(memory_space=pltpu.SMEM), const(w_uvT.shape)],
        out_specs=pl.BlockSpec((1, QB, ah * vd), lambda b, i: (b, i, 0)),
        out_shape=jax.ShapeDtypeStruct((bsz, s, ah * vd), BF16),
        scratch_shapes=[pltpu.VMEM((s, QB), I32),
                        pltpu.VMEM((QB, QB), I32),
                        pltpu.VMEM((QB, hw), F32),
                        pltpu.VMEM((QB, ah * QB), BF16),
                        pltpu.VMEM((1, ah * QB), F32),
                        pltpu.VMEM((1, ah * QB), F32),
                        pltpu.VMEM((ca, ah * QB), F32),
                        pltpu.VMEM((QB, ah * QB), F32),
                        pltpu.VMEM((QB, ah * QB), F32)],
        compiler_params=_cparams("arbitrary", "arbitrary"),
        name="dsa",
    )(q_idxT, w_idxT, q_latT, k_idx, kv_lat, kv_latT, bkt0, bkt1, rel, w_uvT)


HGRN_CHUNK = 64
HGRN_SUB = 16
HGRN_EXP_CLAMP = 80.0


def _hgrn_kernel(hq_ref, hf_ref, hi_ref, hg_ref, lb_ref, g_ref, o_ref, st_ref, *, nh, dk, dv, tt):
    C, SUB = HGRN_CHUNK, HGRN_SUB

    @pl.when(pl.program_id(1) == 0)
    def _():
        st_ref[...] = jnp.zeros(st_ref.shape, F32)

    r = lax.broadcasted_iota(I32, (C, C), 0)
    cc = lax.broadcasted_iota(I32, (C, C), 1)
    tri_mask = r >= cc
    tri = jnp.where(tri_mask, 1.0, 0.0).astype(BF16)
    g = g_ref[...]

    for h in range(nh):
        ko = h * dk
        vo = h * dv
        lb = lb_ref[:, pl.ds(ko, dk)]
        stT = st_ref[h]
        for c in range(tt // C):
            rows = pl.ds(c * C, C)
            hf = hf_ref[0, rows, pl.ds(ko, dk)].astype(F32)
            f = lb + (1.0 - lb) * _sigmoid(hf)
            lf = jnp.log(f)
            t1 = lf.astype(BF16)
            r1 = lf - t1.astype(F32)
            t2 = r1.astype(BF16)
            t3 = (r1 - t2.astype(F32)).astype(BF16)
            b = _dot(tri, t1) + _dot(tri, t2) + _dot(tri, t3)
            kk = 1.0 - f
            hq = hq_ref[0, rows, pl.ds(ko, dk)].astype(F32)
            q = _silu(hq) * float(dk ** -0.5)
            v = hi_ref[0, rows, pl.ds(vo, dv)]
            o_inter = _dot_nt((q * jnp.exp(b)).astype(BF16), stT.astype(BF16))
            parts = []
            for i in range(C // SUB):
                lo, n = i * SUB, (i + 1) * SUB
                bi = b[lo - 1:lo] if i > 0 else jnp.zeros((1, dk), F32)
                qs = (q[lo:n] * jnp.exp(b[lo:n] - bi)).astype(BF16)
                ks = (kk * jnp.exp(jnp.minimum(bi - b, HGRN_EXP_CLAMP))).astype(BF16)
                parts.append(_dot_nt(qs, ks))
            att = jnp.where(tri_mask, jnp.concatenate(parts, axis=0), 0.0)
            o = o_inter + _dot(att.astype(BF16), v)
            b_last = b[C - 1:C]
            k_dec = (kk * jnp.exp(b_last - b)).astype(BF16)
            stT = stT * jnp.exp(b_last) + _dot_tn(v, k_dec)
            hg = hg_ref[0, rows, pl.ds(vo, dv)].astype(F32)
            o_ref[0, rows, pl.ds(vo, dv)] = (_rms(o) * g * _silu(hg)).astype(o_ref.dtype)
        st_ref[h] = stT


def _hgrn(proj, blks, lb, g, nh, dk, dv):
    bsz, s, _ = proj.shape
    tt = _pick(s, (256, 128, 64))
    kern = functools.partial(_hgrn_kernel, nh=nh, dk=dk, dv=dv, tt=tt)
    col = lambda blk, wdt: pl.BlockSpec((1, tt, wdt), lambda b, i: (b, i, blk))
    return pl.pallas_call(
        kern,
        grid=(bsz, s // tt),
        in_specs=[col(blks[0], nh * dk), col(blks[1], nh * dk), col(blks[2], nh * dv), col(blks[3], nh * dv),
                  pl.BlockSpec((1, nh * dk), lambda b, i: (0, 0)),
                  pl.BlockSpec((1, dv), lambda b, i: (0, 0))],
        out_specs=pl.BlockSpec((1, tt, nh * dv), lambda b, i: (b, i, 0)),
        out_shape=jax.ShapeDtypeStruct((bsz, s, nh * dv), BF16),
        scratch_shapes=[pltpu.VMEM((nh, dv, dk), F32)],
        compiler_params=_cparams("arbitrary", "arbitrary"),
        name="hgrn",
    )(proj, proj, proj, proj, lb.reshape(1, nh * dk), g.reshape(1, dv))


def _postmix_kernel(oa_ref, ob_ref, ga_ref, gb_ref, x_ref, g1_ref, sc2_ref, sh2_ref,
                    wa_ref, wb_ref, wo_ref, wrT_ref, lng_ref, lnb_ref,
                    x1_ref, h2_ref, lgT_ref, *, alpha):
    ya = _dot(oa_ref[0], wa_ref[...])
    yb = _dot(ob_ref[0], wb_ref[...])
    mix = _sigmoid(ga_ref[0].astype(F32)) * ya + _sigmoid(gb_ref[0].astype(F32)) * yb
    mixed = _dot(mix.astype(BF16), wo_ref[...])
    x1 = _ln(alpha * x_ref[0] + g1_ref[0] * mixed) * lng_ref[...] + lnb_ref[...]
    x1_ref[0] = x1
    h2 = _ln(x1) * (1.0 + sc2_ref[0]) + sh2_ref[0]
    h2_ref[0] = _pack_bf16_pairs(h2)
    lgT_ref[0] = _dot_nt(wrT_ref[...], h2.astype(BF16))


def _postmix(o_a, o_b, proj, ga_blk, gb_blk, x, gate1, scale2, shift2, wa, wb, wo, wrT, lng, lnb, alpha):
    bsz, s, d = x.shape
    ne = wrT.shape[0]
    tm = _pick(s, (256, 128))
    kern = functools.partial(_postmix_kernel, alpha=alpha)
    row = lambda wdt: pl.BlockSpec((1, tm, wdt), lambda b, i: (b, i, 0))
    vec = pl.BlockSpec((1, 1, d), lambda b, i: (b, 0, 0))
    full = lambda a: pl.BlockSpec(a.shape, lambda b, i: (0,) * a.ndim, pipeline_mode=pl.Buffered(1))
    return pl.pallas_call(
        kern,
        grid=(bsz, s // tm),
        in_specs=[row(o_a.shape[2]), row(o_b.shape[2]),
                  pl.BlockSpec((1, tm, d), lambda b, i: (b, i, ga_blk)),
                  pl.BlockSpec((1, tm, d), lambda b, i: (b, i, gb_blk)),
                  row(d), vec, vec, vec,
                  full(wa), full(wb), full(wo), full(wrT),
                  pl.BlockSpec((1, d), lambda b, i: (0, 0)), pl.BlockSpec((1, d), lambda b, i: (0, 0))],
        out_specs=[row(d), row(d // 2), pl.BlockSpec((1, ne, tm), lambda b, i: (b, 0, i))],
        out_shape=[jax.ShapeDtypeStruct((bsz, s, d), F32),
                   jax.ShapeDtypeStruct((bsz, s, d // 2), U32),
                   jax.ShapeDtypeStruct((bsz, ne, s), F32)],
        compiler_params=_cparams("arbitrary", "arbitrary"),
        name="postmix",
    )(o_a, o_b, proj, proj, x, gate1, scale2, shift2, wa, wb, wo, wrT, lng.reshape(1, d), lnb.reshape(1, d))


def _route_kernel(lg_ref, bias_ref, ids_ref, wts_ref, rnk_ref, sizes_ref, upper_ref, carry_ref, *, ne):
    first = jnp.logical_and(pl.program_id(0) == 0, pl.program_id(1) == 0)
    tn = lg_ref.shape[2]

    @pl.when(first)
    def _():
        carry_ref[...] = jnp.zeros(carry_ref.shape, F32)
        r_ = lax.broadcasted_iota(I32, (tn, tn), 0)
        c_ = lax.broadcasted_iota(I32, (tn, tn), 1)
        upper_ref[...] = jnp.where(r_ < c_, 1.0, 0.0).astype(BF16)

    per = ne // N_GROUPS
    s = _sigmoid(lg_ref[0])
    bz = s + bias_ref[...]
    ridx = lax.broadcasted_iota(I32, (per, tn), 0)
    neg_inf = jnp.float32(-jnp.inf)
    gs = []
    for g in range(N_GROUPS):
        blk = bz[g * per:(g + 1) * per]
        m1 = jnp.max(blk, axis=0, keepdims=True)
        first_hit = jnp.min(jnp.where(blk == m1, ridx, per), axis=0, keepdims=True)
        m2 = jnp.max(jnp.where(ridx == first_hit, neg_inf, blk), axis=0, keepdims=True)
        gs.append(m1 + m2)
    emask_rows = []
    for g in range(N_GROUPS):
        rank = jnp.zeros((1, tn), I32)
        for g2 in range(N_GROUPS):
            if g2 == g:
                continue
            beats = (gs[g2] > gs[g]) if g2 > g else (gs[g2] >= gs[g])
            rank = rank + jnp.where(beats, 1, 0)
        emask_rows.append(jnp.broadcast_to(rank < TOPK_GROUPS, (per, tn)))
    emask = jnp.concatenate(emask_rows, axis=0)
    masked = jnp.where(emask, bz, neg_inf)
    eidx = lax.broadcasted_iota(I32, (ne, tn), 0)
    rank = jnp.zeros((ne, tn), I32)
    for e2 in range(ne):
        row = masked[e2:e2 + 1]
        beats = jnp.logical_or(row > masked, jnp.logical_and(row == masked, e2 < eidx))
        rank = rank + jnp.where(beats, 1, 0)
    sel = rank < MOE_TOPK
    sel01 = jnp.where(sel, 1.0, 0.0)
    denom = jnp.sum(jnp.where(sel, s, 0.0), axis=0, keepdims=True)
    wn = s / denom * ROUTED_SCALE
    before = _dot(sel01.astype(BF16), upper_ref[...]) + carry_ref[:, 0:1]
    ids, wts, rnk = [], [], []
    for k in range(MOE_TOPK):
        hit = rank == k
        ids.append(jnp.sum(jnp.where(hit, eidx, 0), axis=0, keepdims=True))
        wts.append(jnp.sum(jnp.where(hit, wn, 0.0), axis=0, keepdims=True))
        rnk.append(jnp.sum(jnp.where(hit, before, 0.0), axis=0, keepdims=True))
    ids_ref[0] = jnp.concatenate(ids, axis=0)
    wts_ref[0] = jnp.concatenate(wts, axis=0)
    rnk_ref[0] = jnp.concatenate(rnk, axis=0).astype(I32)
    carry_ref[...] = carry_ref[...] + jnp.sum(sel01, axis=1, keepdims=True)
    sizes_ref[...] = carry_ref[...]


def _route(lgT, bias):
    bsz, ne, s = lgT.shape
    tn = _pick(s, (1024, 512, 256, 128))
    kern = functools.partial(_route_kernel, ne=ne)
    slot = pl.BlockSpec((1, MOE_TOPK, tn), lambda b, j: (b, 0, j))
    return pl.pallas_call(
        kern,
        grid=(bsz, s // tn),
        in_specs=[pl.BlockSpec((1, ne, tn), lambda b, j: (b, 0, j)),
                  pl.BlockSpec((ne, 1), lambda b, j: (0, 0))],
        out_specs=[slot, slot, slot, pl.BlockSpec((ne, LANES), lambda b, j: (0, 0))],
        out_shape=[jax.ShapeDtypeStruct((bsz, MOE_TOPK, s), I32),
                   jax.ShapeDtypeStruct((bsz, MOE_TOPK, s), F32),
                   jax.ShapeDtypeStruct((bsz, MOE_TOPK, s), I32),
                   jax.ShapeDtypeStruct((ne, LANES), F32)],
        scratch_shapes=[pltpu.VMEM((tn, tn), BF16), pltpu.VMEM((ne, LANES), F32)],
        compiler_params=_cparams("arbitrary", "arbitrary"),
        name="route",
    )(lgT, bias.reshape(ne, 1))


def _dest_kernel(pstart_ref, ids_ref, rnk_ref, o_ref, *, ne):
    ids = ids_ref[0]
    base = jnp.zeros(ids.shape, I32)
    for e in range(ne):
        base = jnp.where(ids == e, pstart_ref[e], base)
    o_ref[0] = base + rnk_ref[0]


def _dest(pstart, ids, rnk, ne):
    bsz, k, s = ids.shape
    tn = _pick(s, (2048, 1024, 512, 256, 128))
    blk = lambda: pl.BlockSpec((1, k, tn), lambda b, j, ps: (b, 0, j))
    return pl.pallas_call(
        functools.partial(_dest_kernel, ne=ne),
        grid_spec=pltpu.PrefetchScalarGridSpec(num_scalar_prefetch=1, grid=(bsz, s // tn),
                                               in_specs=[blk(), blk()], out_specs=blk()),
        out_shape=jax.ShapeDtypeStruct((bsz, k, s), I32),
        compiler_params=_cparams("arbitrary", "arbitrary"),
        name="dest",
    )(pstart, ids, rnk)


def _dispatch_kernel(dst_ref, h_ref, xs_hbm, sem, *, tm, topk):
    def start(t, c):
        for k in range(topk):
            pltpu.make_async_copy(h_ref.at[pl.ds(t, 1)], xs_hbm.at[pl.ds(dst_ref[0, k, t], 1)], sem.at[0]).start()
        return c

    lax.fori_loop(0, tm, start, 0)

    def wait(t, c):
        for k in range(topk):
            pltpu.make_async_copy(h_ref.at[pl.ds(t, 1)], xs_hbm.at[pl.ds(0, 1)], sem.at[0]).wait()
        return c

    lax.fori_loop(0, tm, wait, 0)


def _dispatch(dest, h2, n_rows):
    n, d = h2.shape
    nt, topk, tm = dest.shape
    return pl.pallas_call(
        functools.partial(_dispatch_kernel, tm=tm, topk=topk),
        grid=(nt,),
        in_specs=[pl.BlockSpec((1, topk, tm), lambda i: (i, 0, 0), memory_space=pltpu.SMEM),
                  pl.BlockSpec((tm, d), lambda i: (i, 0))],
        out_specs=pl.BlockSpec(memory_space=pl.ANY),
        out_shape=jax.ShapeDtypeStruct((n_rows, d), h2.dtype),
        scratch_shapes=[pltpu.SemaphoreType.DMA((1,))],
        compiler_params=_cparams("arbitrary"),
        name="dispatch",
    )(dest, h2)


def _experts_kernel(be_ref, nv_ref, x_ref, wg_ref, wu_ref, wd_ref, y_ref, wgb, wub, wdb, *, blk):
    i = pl.program_id(0)
    prev_e = be_ref[jnp.maximum(i - 1, 0)]

    @pl.when(jnp.logical_or(i == 0, be_ref[i] != prev_e))
    def _():
        wgb[...] = wg_ref[0].astype(BF16)
        wub[...] = wu_ref[0].astype(BF16)
        wdb[...] = wd_ref[0].astype(BF16)

    @pl.when(nv_ref[i] > 0)
    def _():
        rows = lax.broadcasted_iota(I32, (blk, 1), 0)
        xu = jnp.where(rows < nv_ref[i], x_ref[...], jnp.uint32(0))
        x = jnp.concatenate(_unpack_bf16_pairs(xu), axis=1).astype(BF16)
        act = (_silu(_dot(x, wgb[...])) * _dot(x, wub[...])).astype(BF16)
        y_ref[...] = _pack_bf16_pairs(_dot(act, wdb[...]))


def _experts(xs, be, nvalid, wg, wu, wd):
    n_rows, dh = xs.shape
    ne, d, f = wg.shape
    blk = EXPERT_BLOCK
    nblocks = n_rows // blk
    grid_spec = pltpu.PrefetchScalarGridSpec(
        num_scalar_prefetch=2,
        grid=(nblocks,),
        in_specs=[pl.BlockSpec((blk, dh), lambda i, be, nv: (i, 0)),
                  pl.BlockSpec((1, d, f), lambda i, be, nv: (be[i], 0, 0)),
                  pl.BlockSpec((1, d, f), lambda i, be, nv: (be[i], 0, 0)),
                  pl.BlockSpec((1, f, d), lambda i, be, nv: (be[i], 0, 0))],
        out_specs=pl.BlockSpec((blk, dh), lambda i, be, nv: (i, 0)),
        scratch_shapes=[pltpu.VMEM((d, f), BF16), pltpu.VMEM((d, f), BF16), pltpu.VMEM((f, d), BF16)],
    )
    return pl.pallas_call(
        functools.partial(_experts_kernel, blk=blk),
        grid_spec=grid_spec,
        out_shape=jax.ShapeDtypeStruct((n_rows, dh), U32),
        compiler_params=_cparams("arbitrary"),
        name="experts",
    )(be, nvalid, xs, wg, wu, wd)


def _final_kernel(dst_ref, dstn_ref, wts_ref, h2_ref, x1_ref, g2_ref, wg_ref, wu_ref, wd_ref, lng_ref, lnb_ref,
                  ys_hbm, o_ref, ybuf, sem, *, alpha, topk, tm, nt):
    i = pl.program_id(0)
    slot = lax.rem(i, 2)

    def gather_start(ids_ref, sl):
        def body(t, c):
            for k in range(topk):
                pltpu.make_async_copy(ys_hbm.at[pl.ds(ids_ref[0, k, t], 1)], ybuf.at[sl, k, pl.ds(t, 1)],
                                      sem.at[sl]).start()
            return c
        lax.fori_loop(0, tm, body, 0)

    @pl.when(i == 0)
    def _():
        gather_start(dst_ref, 0)

    for t in range(tm):
        for k in range(topk):
            pltpu.make_async_copy(ys_hbm.at[pl.ds(dstn_ref[0, k, t], 1)], ybuf.at[1 - slot, k, pl.ds(t, 1)],
                                  sem.at[1 - slot]).start()

    h = jnp.concatenate(_unpack_bf16_pairs(h2_ref[...]), axis=1).astype(BF16)
    y = _dot((_silu(_dot(h, wg_ref[...])) * _dot(h, wu_ref[...])).astype(BF16), wd_ref[...])

    def wait_slot(sl):
        def body(t, c):
            for k in range(topk):
                pltpu.make_async_copy(ys_hbm.at[pl.ds(0, 1)], ybuf.at[sl, k, pl.ds(t, 1)], sem.at[sl]).wait()
            return c
        lax.fori_loop(0, tm, body, 0)

    wait_slot(slot)
    w = wts_ref[...]
    ylo = jnp.zeros((tm, y.shape[1] // 2), F32)
    yhi = jnp.zeros((tm, y.shape[1] // 2), F32)
    for k in range(topk):
        lo, hi = _unpack_bf16_pairs(ybuf[slot, k])
        ylo = ylo + w[:, k:k + 1] * lo
        yhi = yhi + w[:, k:k + 1] * hi
    y = y + jnp.concatenate([ylo, yhi], axis=1)
    o_ref[...] = _ln(alpha * x1_ref[...] + g2_ref[0] * y) * lng_ref[...] + lnb_ref[...]

    @pl.when(i == nt - 1)
    def _():
        wait_slot(1 - slot)


def _final(ys, dest, wtsT, h2, x1, gate2, wg, wu, wd, lng, lnb, alpha, tiles_per_batch):
    n, d = x1.shape
    nt, topk, tm = dest.shape
    kern = functools.partial(_final_kernel, alpha=alpha, topk=topk, tm=tm, nt=nt)
    row = pl.BlockSpec((tm, d), lambda i: (i, 0))
    full = lambda a: pl.BlockSpec(a.shape, lambda i: (0,) * a.ndim, pipeline_mode=pl.Buffered(1))
    return pl.pallas_call(
        kern,
        grid=(nt,),
        in_specs=[pl.BlockSpec((1, topk, tm), lambda i: (i, 0, 0), memory_space=pltpu.SMEM),
                  pl.BlockSpec((1, topk, tm), lambda i: (jnp.minimum(i + 1, nt - 1), 0, 0), memory_space=pltpu.SMEM),
                  pl.BlockSpec((tm, topk), lambda i: (i, 0)),
                  pl.BlockSpec((tm, d // 2), lambda i: (i, 0)), row,
                  pl.BlockSpec((1, 1, d), lambda i: (i // tiles_per_batch, 0, 0)),
                  full(wg), full(wu), full(wd),
                  pl.BlockSpec((1, d), lambda i: (0, 0)), pl.BlockSpec((1, d), lambda i: (0, 0)),
                  pl.BlockSpec(memory_space=pl.ANY)],
        out_specs=row,
        out_shape=jax.ShapeDtypeStruct((n, d), F32),
        scratch_shapes=[pltpu.VMEM((2, topk, tm, d // 2), U32), pltpu.SemaphoreType.DMA((2,))],
        compiler_params=_cparams("arbitrary"),
        name="final",
    )(dest, dest, wtsT, h2, x1, gate2, wg, wu, wd, lng.reshape(1, d), lnb.reshape(1, d), ys)


def _proj_layout(d, qr, kvr, idim, ih, hk, hv):
    src = np.cumsum([0, qr, kvr, idim, ih, hk, hk, hv, hv, d, d])
    wa = -(-(qr + kvr + idim + ih) // LANES) * LANES
    wa = max(wa, 1 << (wa - 1).bit_length())
    pieces = [("a", wa, (int(src[0]), int(src[4]))),
              ("hq", hk, (int(src[4]), int(src[5]))), ("hf", hk, (int(src[5]), int(src[6]))),
              ("hi", hv, (int(src[6]), int(src[7]))), ("hg", hv, (int(src[7]), int(src[8]))),
              ("ga", d, (int(src[8]), int(src[9]))), ("gb", d, (int(src[9]), int(src[10])))]
    pieces.sort(key=lambda p: -p[1])
    off = 0
    layout = {}
    for name, wdt, rng in pieces:
        assert off % wdt == 0
        layout[name] = (off, wdt, rng)
        off += wdt
    return layout, off


def kernel(x, c, rpb_table, hgrn_lb_logits, ada_w, ada_b, w_in, q_norm_g, kv_norm_g, w_uq, w_uk, w_uv, w_qidx,
           idx_k_norm_g, idx_k_norm_b, hgrn_out_norm_g, w_branch_a, w_branch_b, w_o, ln1_g, ln1_b, w_router,
           router_bias, w_exp_gate, w_exp_up, w_exp_down, w_sh_gate, w_sh_up, w_sh_down, ln2_g, ln2_b):
    bsz, s, d = x.shape
    depth = ada_w.shape[0]
    qr = w_uq.shape[1]
    ah, kvr, qk = w_uk.shape[1], w_uk.shape[2], w_uk.shape[3]
    idim = idx_k_norm_g.shape[1]
    ih = w_qidx.shape[2] // idim
    dv = hgrn_out_norm_g.shape[1]
    nh = w_branch_b.shape[1] // dv
    dk = hgrn_lb_logits.shape[1] // nh
    ne = w_router.shape[2]
    topk = min(IDX_TOPK, s // 4)
    alpha = float((2 * depth) ** 0.25)
    n_tok = bsz * s

    lower_bounds = jnp.cumsum(jax.nn.softmax(hgrn_lb_logits.astype(F32), axis=0), axis=0)
    layout, wtot = _proj_layout(d, qr, kvr, idim, ih, nh * dk, nh * dv)

    for l in range(depth):
        mod = _ada(c, ada_w[l], ada_b[l])[:, None, :]
        shift1, scale1, gate1, shift2, scale2, gate2 = jnp.split(mod, 6, axis=-1)

        cols = []
        for off, wdt, (lo, hi) in sorted(layout.values()):
            cols.append(w_in[l][:, lo:hi].astype(BF16))
            if wdt > hi - lo:
                cols.append(jnp.zeros((d, wdt - (hi - lo)), BF16))
        proj = _inproj(x, scale1, shift1, jnp.concatenate(cols, axis=1))
        blk = lambda name: layout[name][0] // layout[name][1]

        qb_sz = _pick(s, (256, 128))
        q_latT, q_idxT, w_idxT, k_idx, kv_lat = _prep(
            proj, blk("a"), layout["a"][1], (qr, kvr, idim, ih, ah, qk), q_norm_g[l], kv_norm_g[l],
            idx_k_norm_g[l], idx_k_norm_b[l], jnp.swapaxes(w_uq[l], 0, 1).astype(BF16),
            w_uk[l].astype(BF16), jnp.swapaxes(w_qidx[l], 0, 1).astype(BF16), qb_sz)
        kv_latT = jnp.concatenate([jnp.swapaxes(kv_lat, 1, 2), jnp.ones((bsz, 1, s), BF16),
                                   jnp.zeros((bsz, 7, s), BF16)], axis=1)
        o_a = _dsa(q_idxT, w_idxT, q_latT, k_idx, kv_lat, kv_latT,
                   rpb_table, jnp.swapaxes(w_uv[l], 1, 2).astype(BF16), topk, qb_sz)

        o_b = _hgrn(proj, (blk("hq"), blk("hf"), blk("hi"), blk("hg")), lower_bounds[l],
                    hgrn_out_norm_g[l], nh, dk, dv)

        x1, h2, lgT = _postmix(o_a, o_b, proj, blk("ga"), blk("gb"), x, gate1, scale2, shift2,
                               w_branch_a[l].astype(BF16), w_branch_b[l].astype(BF16), w_o[l].astype(BF16),
                               jnp.swapaxes(w_router[l], 0, 1).astype(BF16), ln1_g[l], ln1_b[l], alpha)

        ids, wts, rnk, sizes_f = _route(lgT, router_bias[l])
        eb = EXPERT_BLOCK
        sizes = sizes_f[:, 0].astype(I32)
        padded = (sizes + eb - 1) // eb * eb
        pend = jnp.cumsum(padded)
        pstart = pend - padded
        n_rows = -(-(n_tok * MOE_TOPK + ne * (eb - 1)) // eb) * eb
        blk_start = jnp.arange(n_rows // eb, dtype=I32) * eb
        blk_expert = jnp.minimum(jnp.sum((pend[None, :] <= blk_start[:, None]).astype(I32), axis=1), ne - 1)
        onehot = (blk_expert[:, None] == jnp.arange(ne, dtype=I32)[None, :]).astype(I32)
        blk_size = jnp.sum(onehot * sizes[None, :], axis=1)
        blk_pstart = jnp.sum(onehot * pstart[None, :], axis=1)
        nvalid = jnp.clip(blk_size - (blk_start - blk_pstart), 0, eb).astype(I32)
        dest = _dest(pstart.astype(I32), ids, rnk, ne)
        tm = _pick(s, (128,))
        dest_t = dest.reshape(bsz, MOE_TOPK, s // tm, tm).transpose(0, 2, 1, 3).reshape(n_tok // tm, MOE_TOPK, tm)
        wtsT = jnp.swapaxes(wts, 1, 2).reshape(n_tok, MOE_TOPK)

        h2f = h2.reshape(n_tok, d // 2)
        xs = _dispatch(dest_t, h2f, n_rows)
        ys = _experts(xs, blk_expert, nvalid, w_exp_gate[l], w_exp_up[l], w_exp_down[l])
        x = _final(ys, dest_t, wtsT, h2f, x1.reshape(n_tok, d), gate2, w_sh_gate[l].astype(BF16),
                   w_sh_up[l].astype(BF16), w_sh_down[l].astype(BF16), ln2_g[l], ln2_b[l], alpha,
                   s // tm).reshape(bsz, s, d)
    return x
```

```python
import functools
import math

import numpy as np
import jax
import jax.numpy as jnp
from jax import lax
from jax.experimental import pallas as pl
from jax.experimental.pallas import tpu as pltpu

F32 = jnp.float32
BF16 = jnp.bfloat16
I32 = jnp.int32
U32 = jnp.uint32

EPS = 1e-6
IDX_TOPK = 256
RPB_MAX_DIST = 128
MOE_TOPK = 8
N_GROUPS = 8
TOPK_GROUPS = 4
ROUTED_SCALE = 2.5
EXPERT_BLOCK = 256

V7X_VMEM_LIMIT_BYTES = 56 * 1024 * 1024
LANES = 128
INT_MIN = -2 ** 31
NEG_BIG = -1e30
LOG2E = math.log2(math.e)


def _cparams(*sem):
    return pltpu.CompilerParams(dimension_semantics=tuple(sem), vmem_limit_bytes=V7X_VMEM_LIMIT_BYTES)


def _pick(n, prefs):
    for p in prefs:
        if n % p == 0:
            return p
    return n


def _sigmoid(v):
    return 1.0 / (1.0 + jnp.exp(-v))


def _silu(v):
    return v * _sigmoid(v)


def _ln(v):
    mu = jnp.mean(v, axis=-1, keepdims=True)
    d = v - mu
    var = jnp.mean(d * d, axis=-1, keepdims=True)
    return d * lax.rsqrt(var + EPS)


def _rms(v):
    return v * lax.rsqrt(jnp.mean(v * v, axis=-1, keepdims=True) + EPS)


def _dot(a, b):
    return jnp.dot(a, b, preferred_element_type=F32)


def _pack_bf16_pairs(v):
    n = v.shape[1] // 2
    lo = pltpu.bitcast(v[:, :n].astype(BF16).astype(F32), U32) >> 16
    hi = pltpu.bitcast(v[:, n:].astype(BF16).astype(F32), U32) & jnp.uint32(0xFFFF0000)
    return lo | hi


def _unpack_bf16_pairs(u):
    return pltpu.bitcast(u << 16, F32), pltpu.bitcast(u & jnp.uint32(0xFFFF0000), F32)


def _dot_nt(a, b):
    return lax.dot_general(a, b, (((1,), (1,)), ((), ())), preferred_element_type=F32)


def _dot_tn(a, b):
    return lax.dot_general(a, b, (((0,), (0,)), ((), ())), preferred_element_type=F32)


def _ada_kernel(c_ref, w_ref, b_ref, o_ref):
    c = c_ref[...]
    o_ref[...] = _dot(_silu(c).astype(BF16), w_ref[...].astype(BF16)) + b_ref[...]


def _ada(c, w, b):
    bsz, d = c.shape
    n = w.shape[1]
    rows = 8
    cp = jnp.zeros((rows, d), F32).at[:bsz].set(c)
    tn = _pick(n, (1024, 512, 256, 128))
    out = pl.pallas_call(
        _ada_kernel,
        grid=(n // tn,),
        in_specs=[pl.BlockSpec((rows, d), lambda j: (0, 0)),
                  pl.BlockSpec((d, tn), lambda j: (0, j)),
                  pl.BlockSpec((1, tn), lambda j: (0, j))],
        out_specs=pl.BlockSpec((rows, tn), lambda j: (0, j)),
        out_shape=jax.ShapeDtypeStruct((rows, n), F32),
        compiler_params=_cparams("arbitrary"),
        name="ada",
    )(cp, w, b.reshape(1, n))
    return out[:bsz]


def _inproj_kernel(x_ref, sc_ref, sh_ref, w_ref, o_ref, h_ref):
    @pl.when(pl.program_id(2) == 0)
    def _():
        h = _ln(x_ref[0]) * (1.0 + sc_ref[0]) + sh_ref[0]
        h_ref[...] = h.astype(BF16)

    o_ref[0] = _dot(h_ref[...], w_ref[...]).astype(o_ref.dtype)


def _inproj(x, scale, shift, w):
    bsz, s, d = x.shape
    n = w.shape[1]
    tm = _pick(s, (1024, 512, 256, 128))
    tn = _pick(n, (1024, 512, 256, 128))
    return pl.pallas_call(
        _inproj_kernel,
        grid=(bsz, s // tm, n // tn),
        in_specs=[pl.BlockSpec((1, tm, d), lambda b, i, j: (b, i, 0)),
                  pl.BlockSpec((1, 1, d), lambda b, i, j: (b, 0, 0)),
                  pl.BlockSpec((1, 1, d), lambda b, i, j: (b, 0, 0)),
                  pl.BlockSpec((d, tn), lambda b, i, j: (0, j))],
        out_specs=pl.BlockSpec((1, tm, tn), lambda b, i, j: (b, i, j)),
        out_shape=jax.ShapeDtypeStruct((bsz, s, n), BF16),
        scratch_shapes=[pltpu.VMEM((tm, d), BF16)],
        compiler_params=_cparams("arbitrary", "arbitrary", "arbitrary"),
        name="inproj",
    )(x, scale, shift, w)


def _prep_kernel(a_ref, qg_ref, kvg_ref, ikg_ref, ikb_ref, wuqT_ref, wuk_ref, wqiT_ref,
                 qlatT_ref, qidxT_ref, widxT_ref, kidx_ref, kv_ref, *, qr, kvr, idim, ih, ah, qk):
    a = a_ref[0]
    af = a.astype(F32)
    cq = (_rms(af[:, :qr]) * qg_ref[...]).astype(BF16)
    ckv = af[:, qr:qr + kvr]
    ki = af[:, qr + kvr:qr + kvr + idim]
    kv_ref[0] = (_rms(ckv) * kvg_ref[...]).astype(BF16)
    kidx_ref[0] = (_ln(ki) * ikg_ref[...] + ikb_ref[...]).astype(BF16)
    tail = a[:, qr + kvr:qr + kvr + LANES]
    eye = (lax.broadcasted_iota(I32, (LANES, LANES), 0) == lax.broadcasted_iota(I32, (LANES, LANES), 1))
    tailT = _dot_nt(jnp.where(eye, 1.0, 0.0).astype(BF16), tail)
    widxT_ref[0] = tailT[idim:idim + ih] * float((ih * idim) ** -0.5)
    tm = a.shape[0]
    qT = _dot_nt(wuqT_ref[...], cq)
    for h in range(ah):
        qh = qT[h * qk:(h + 1) * qk].astype(BF16)
        qlatT_ref[0, 0, :, h * tm:(h + 1) * tm] = (_dot(wuk_ref[h], qh) * float(qk ** -0.5 * LOG2E)).astype(BF16)
    qiT = _dot_nt(wqiT_ref[...], cq)
    for h in range(ih):
        qidxT_ref[0, 0, :, h * tm:(h + 1) * tm] = qiT[h * idim:(h + 1) * idim].astype(BF16)


def _prep(proj, a_blk, wa, dims, q_norm_g, kv_norm_g, ikg, ikb, w_uqT, w_uk, w_qidxT, tm):
    bsz, s, _ = proj.shape
    qr, kvr, idim, ih, ah, qk = dims
    assert idim + ih <= LANES and qr + kvr + LANES <= wa
    kern = functools.partial(_prep_kernel, qr=qr, kvr=kvr, idim=idim, ih=ih, ah=ah, qk=qk)
    full = lambda shape: pl.BlockSpec(shape, lambda b, i: (0,) * len(shape))
    return pl.pallas_call(
        kern,
        grid=(bsz, s // tm),
        in_specs=[pl.BlockSpec((1, tm, wa), lambda b, i: (b, i, a_blk)),
                  full((1, qr)), full((1, kvr)), full((1, idim)), full((1, idim)),
                  full(w_uqT.shape), full(w_uk.shape), full(w_qidxT.shape)],
        out_specs=[pl.BlockSpec((1, 1, kvr, ah * tm), lambda b, i: (b, i, 0, 0)),
                   pl.BlockSpec((1, 1, idim, ih * tm), lambda b, i: (b, i, 0, 0)),
                   pl.BlockSpec((1, ih, tm), lambda b, i: (b, 0, i)),
                   pl.BlockSpec((1, tm, idim), lambda b, i: (b, i, 0)),
                   pl.BlockSpec((1, tm, kvr), lambda b, i: (b, i, 0))],
        out_shape=[jax.ShapeDtypeStruct((bsz, s // tm, kvr, ah * tm), BF16),
                   jax.ShapeDtypeStruct((bsz, s // tm, idim, ih * tm), BF16),
                   jax.ShapeDtypeStruct((bsz, ih, s), F32),
                   jax.ShapeDtypeStruct((bsz, s, idim), BF16),
                   jax.ShapeDtypeStruct((bsz, s, kvr), BF16)],
        compiler_params=_cparams("arbitrary", "arbitrary"),
        name="prep",
    )(proj, q_norm_g.reshape(1, qr), kv_norm_g.reshape(1, kvr), ikg.reshape(1, idim), ikb.reshape(1, idim),
      w_uqT, w_uk, w_qidxT)


def _dsa_kernel(qidxT_ref, widxT_ref, qlatT_ref, kidx_ref, kv_ref, kvT_ref, bkt0_ref, bkt1_ref, rel_ref, wuvT_ref,
                o_ref, keys_ref, gmax_ref, s_ref, p_ref, m_ref, a_ref, acc_ref, t0_ref, t1_ref,
                *, qb_sz, topk, ih, ah, c, nb):
    QB = qb_sz
    qb = pl.program_id(1)
    nchunks = qb + 1
    qpos = qb * QB + lax.broadcasted_iota(I32, (1, QB), 1)
    kpos = lax.broadcasted_iota(I32, (QB, 1), 0)
    wT = widxT_ref[0]
    hcols = lambda h: slice(h * QB, (h + 1) * QB)

    @pl.when(jnp.logical_and(pl.program_id(0) == 0, qb == 0))
    def _():
        for bkt_ref, t_ref in ((bkt0_ref, t0_ref), (bkt1_ref, t1_ref)):
            bkt = bkt_ref[...]
            for h in range(ah):
                tile = jnp.zeros((QB, QB), F32)
                for b in range(nb - 1):
                    tile = jnp.where(bkt == b, rel_ref[b, h], tile)
                t_ref[:, hcols(h)] = tile

    def score_chunk(kc, carry):
        off = pl.multiple_of(kc * QB, QB)
        s_ref[:, :ih * QB] = _dot(kidx_ref[0, pl.ds(off, QB), :], qidxT_ref[0, 0])
        acc = jnp.zeros((QB, QB), F32)
        for h in range(ih):
            acc = acc + wT[h:h + 1] * jnp.maximum(s_ref[:, hcols(h)], 0.0)
        bits = pltpu.bitcast(acc, I32)
        skey = bits ^ ((bits >> 31) & 0x7FFFFFFF)
        causal = (off + kpos) <= qpos
        skey = jnp.where(causal, skey, INT_MIN)
        keys_ref[pl.ds(off, QB), :] = skey
        gmax_ref[...] = jnp.maximum(gmax_ref[...], skey)
        return carry

    gmax_ref[...] = jnp.full(gmax_ref.shape, INT_MIN, I32)
    lax.fori_loop(0, nchunks, score_chunk, 0)

    gmax = gmax_ref[...]
    lo0 = jnp.min(gmax, axis=0, keepdims=True) if QB >= topk else jnp.full((1, QB), INT_MIN, I32)
    hi0 = jnp.max(gmax, axis=0, keepdims=True) + 1

    def count_ge(cand):
        def body(kc, cnt):
            off = pl.multiple_of(kc * QB, QB)
            hit = jnp.where(keys_ref[pl.ds(off, QB), :] >= cand, 1, 0)
            return cnt + jnp.sum(hit.reshape(QB // 8, 8, QB), axis=0)

        cnt = lax.fori_loop(0, nchunks, body, jnp.zeros((8, QB), I32))
        return jnp.sum(cnt.astype(F32), axis=0, keepdims=True)

    def bis_cond(st):
        it, _, _, done = st
        return jnp.logical_and(it < 34, jnp.min(done) < 0.5)

    def bis_body(st):
        it, lo, hi, done = st
        cand = (lo >> 1) + (hi >> 1) + (lo & hi & 1)
        cnt = count_ge(cand)
        ge = cnt >= float(topk)
        conv = cand == lo
        fin = jnp.logical_or(conv, cnt == float(topk))
        thr_new = jnp.where(conv, lo, cand)
        lo = jnp.where(fin, thr_new, jnp.where(ge, cand, lo))
        hi = jnp.where(fin, thr_new + 1, jnp.where(ge, hi, cand))
        return it + 1, lo, hi, jnp.where(fin, 1.0, done)

    _, thr, _, _ = lax.while_loop(bis_cond, bis_body, (jnp.int32(0), lo0, hi0, jnp.zeros((1, QB), F32)))

    m_ref[...] = jnp.full(m_ref.shape, NEG_BIG, F32)
    acc_ref[...] = jnp.zeros(acc_ref.shape, F32)

    def attn_chunk(off, bias_ref, diag):
        sel = keys_ref[pl.ds(off, QB), :] >= thr
        if diag:
            sel = jnp.logical_and(sel, (off + kpos) <= qpos)
        madd = jnp.where(sel, 0.0, NEG_BIG)
        s_ref[:, :ah * QB] = _dot(kv_ref[0, pl.ds(off, QB), :], qlatT_ref[0, 0])
        for g in range(ah * QB // LANES):
            cols = slice(g * LANES, (g + 1) * LANES)
            qcols = slice(g * LANES % QB, g * LANES % QB + LANES)
            s = s_ref[:, cols] + madd[:, qcols]
            if bias_ref is not None:
                s = s + bias_ref[:, cols]
            m_prev = m_ref[:, cols]
            m_new = jnp.maximum(m_prev, jnp.max(s, axis=0, keepdims=True))
            p_ref[:, cols] = jnp.exp2(s - m_new).astype(BF16)
            a_ref[:, cols] = jnp.exp2(m_prev - m_new)
            m_ref[:, cols] = m_new
        acc_ref[...] = a_ref[...] * acc_ref[...] + _dot(kvT_ref[0, :, pl.ds(off, QB)], p_ref[...])

    def far_body(kc, carry):
        attn_chunk(pl.multiple_of(kc * QB, QB), None, False)
        return carry

    lax.fori_loop(0, qb - 1, far_body, 0)

    @pl.when(qb >= 1)
    def _():
        attn_chunk(pl.multiple_of((qb - 1) * QB, QB), t1_ref, False)

    attn_chunk(pl.multiple_of(qb * QB, QB), t0_ref, True)

    outs = []
    for h in range(ah):
        o = (acc_ref[:c, hcols(h)] / acc_ref[c:c + 1, hcols(h)]).astype(BF16)
        outs.append(_dot(wuvT_ref[h], o))
    o_ref[0] = jnp.concatenate(outs, axis=0).T.astype(o_ref.dtype)


def _t5_bucket_np(d, nbuckets):
    max_exact = nbuckets // 2
    dd = np.maximum(d, 1).astype(np.float32)
    large = max_exact + (np.log(dd / np.float32(max_exact)) / np.float32(math.log(RPB_MAX_DIST / max_exact))
                         * np.float32(nbuckets - max_exact)).astype(np.int32)
    large = np.minimum(large, nbuckets - 1)
    return np.where(d < max_exact, d, large).astype(np.int32)


def _dsa(q_idxT, w_idxT, q_latT, k_idx, kv_lat, kv_latT, rpb_table, w_uvT, topk, QB):
    bsz, nqb, idim, ihq = q_idxT.shape
    c, ahq = q_latT.shape[2], q_latT.shape[3]
    ca = kv_latT.shape[1]
    ih, ah = ihq // QB, ahq // QB
    s = nqb * QB
    vd = w_uvT.shape[1]
    nb = rpb_table.shape[0]
    assert QB >= RPB_MAX_DIST
    j = np.arange(QB)[:, None]
    i = np.arange(QB)[None, :]
    bkt0 = jnp.asarray(_t5_bucket_np(np.maximum(i - j, 0), nb))
    bkt1 = jnp.asarray(_t5_bucket_np(QB + i - j, nb))
    assert int(_t5_bucket_np(np.array([RPB_MAX_DIST]), nb)[0]) == nb - 1
    rel = (rpb_table.astype(F32) - rpb_table[nb - 1].astype(F32)[None, :]) * LOG2E
    kern = functools.partial(_dsa_kernel, qb_sz=QB, topk=topk, ih=ih, ah=ah, c=c, nb=nb)
    const = lambda shape: pl.BlockSpec(shape, lambda b, i: (0,) * len(shape), pipeline_mode=pl.Buffered(1))
    hw = max(ih, ah) * QB
    return pl.pallas_call(
        kern,
        grid=(bsz, nqb),
        in_specs=[pl.BlockSpec((1, 1, idim, ih * QB), lambda b, i: (b, i, 0, 0)),
                  pl.BlockSpec((1, ih, QB), lambda b, i: (b, 0, i)),
                  pl.BlockSpec((1, 1, c, ah * QB), lambda b, i: (b, i, 0, 0)),
                  pl.BlockSpec((1, s, idim), lambda b, i: (b, 0, 0)),
                  pl.BlockSpec((1, s, c), lambda b, i: (b, 0, 0)),
                  pl.BlockSpec((1, ca, s), lambda b, i: (b, 0, 0)),
                  const((QB, QB)), const((QB, QB)),
                  pl.BlockSpec(memory_space=pltpu.SMEM), const(w_uvT.shape)],
        out_specs=pl.BlockSpec((1, QB, ah * vd), lambda b, i: (b, i, 0)),
        out_shape=jax.ShapeDtypeStruct((bsz, s, ah * vd), BF16),
        scratch_shapes=[pltpu.VMEM((s, QB), I32),
                        pltpu.VMEM((QB, QB), I32),
                        pltpu.VMEM((QB, hw), F32),
                        pltpu.VMEM((QB, ah * QB), BF16),
                        pltpu.VMEM((1, ah * QB), F32),
                        pltpu.VMEM((1, ah * QB), F32),
                        pltpu.VMEM((ca, ah * QB), F32),
                        pltpu.VMEM((QB, ah * QB), F32),
                        pltpu.VMEM((QB, ah * QB), F32)],
        compiler_params=_cparams("arbitrary", "arbitrary"),
        name="dsa",
    )(q_idxT, w_idxT, q_latT, k_idx, kv_lat, kv_latT, bkt0, bkt1, rel, w_uvT)


HGRN_CHUNK = 64
HGRN_SUB = 16
HGRN_EXP_CLAMP = 80.0


def _dot_exact(a, b, dims):
    return lax.dot_general(a, b, (dims, ((), ())), precision=lax.Precision.HIGHEST, preferred_element_type=F32)


def _hgrn_kernel(hq_ref, hf_ref, hi_ref, hg_ref, lb_ref, g_ref, o_ref, st_ref, st0_ref, *, nh, dk, dv, tt):
    C, SUB = HGRN_CHUNK, HGRN_SUB

    @pl.when(pl.program_id(1) == 0)
    def _():
        st_ref[...] = jnp.zeros(st_ref.shape, F32)

    r = lax.broadcasted_iota(I32, (C, C), 0)
    cc = lax.broadcasted_iota(I32, (C, C), 1)
    tri_mask = r >= cc
    tri = jnp.where(tri_mask, 1.0, 0.0).astype(BF16)
    g = g_ref[...]
    st0_ref[...] = st_ref[...]
    decay = jnp.zeros((1, dk), F32)

    for h in range(nh):
        ko = h * dk
        vo = h * dv
        lb = lb_ref[:, pl.ds(ko, dk)]
        stT = st_ref[h]
        for c in range(tt // C):
            rows = pl.ds(c * C, C)
            hf = hf_ref[0, rows, pl.ds(ko, dk)].astype(F32)
            f = lb + (1.0 - lb) * _sigmoid(hf)
            lf = jnp.log(f)
            t1 = lf.astype(BF16)
            r1 = lf - t1.astype(F32)
            t2 = r1.astype(BF16)
            t3 = (r1 - t2.astype(F32)).astype(BF16)
            b = _dot(tri, t1) + _dot(tri, t2) + _dot(tri, t3)
            kk = 1.0 - f
            hq = hq_ref[0, rows, pl.ds(ko, dk)].astype(F32)
            q = _silu(hq) * float(dk ** -0.5)
            v = hi_ref[0, rows, pl.ds(vo, dv)]
            o_inter = _dot_nt((q * jnp.exp(b)).astype(BF16), stT.astype(BF16))
            parts = []
            for i in range(C // SUB):
                lo, n = i * SUB, (i + 1) * SUB
                bi = b[lo - 1:lo] if i > 0 else jnp.zeros((1, dk), F32)
                decay = jnp.maximum(decay, bi - b[n - 1:n])
                qs = (q[lo:n] * jnp.exp(b[lo:n] - bi)).astype(BF16)
                ks = (kk * jnp.exp(jnp.minimum(bi - b, HGRN_EXP_CLAMP))).astype(BF16)
                parts.append(_dot_nt(qs, ks))
            att = jnp.where(tri_mask, jnp.concatenate(parts, axis=0), 0.0)
            o = o_inter + _dot(att.astype(BF16), v)
            b_last = b[C - 1:C]
            k_dec = (kk * jnp.exp(b_last - b)).astype(BF16)
            stT = stT * jnp.exp(b_last) + _dot_tn(v, k_dec)
            hg = hg_ref[0, rows, pl.ds(vo, dv)].astype(F32)
            o_ref[0, rows, pl.ds(vo, dv)] = (_rms(o) * g * _silu(hg)).astype(o_ref.dtype)
        st_ref[h] = stT

    @pl.when(jnp.max(decay) > HGRN_EXP_CLAMP)
    def _():
        row = lax.broadcasted_iota(I32, (SUB, 1), 0)

        def head(h, carry):
            ko = pl.multiple_of(h * dk, dk)
            vo = pl.multiple_of(h * dv, dv)
            lb = lb_ref[:, pl.ds(ko, dk)]

            def slab(j, S):
                rows = pl.ds(pl.multiple_of(j * SUB, SUB), SUB)
                f = lb + (1.0 - lb) * _sigmoid(hf_ref[0, rows, pl.ds(ko, dk)].astype(F32))
                kk = 1.0 - f
                hq = hq_ref[0, rows, pl.ds(ko, dk)].astype(F32)
                q = _silu(hq) * float(dk ** -0.5)
                v = hi_ref[0, rows, pl.ds(vo, dv)].astype(F32)
                o = jnp.zeros((SUB, dv), F32)
                for r in range(SUB):
                    S = S * f[r:r + 1] + _dot_exact(jnp.where(row == r, v, 0.0), kk, ((0,), (0,)))
                    o = jnp.where(row == r, _dot_exact(q, S, ((1,), (1,))), o)
                hg = hg_ref[0, rows, pl.ds(vo, dv)].astype(F32)
                o_ref[0, rows, pl.ds(vo, dv)] = (_rms(o) * g * _silu(hg)).astype(o_ref.dtype)
                return S

            st_ref[h] = lax.fori_loop(0, tt // SUB, slab, st0_ref[h])
            return carry

        lax.fori_loop(0, nh, head, 0)


def _hgrn(proj, blks, lb, g, nh, dk, dv):
    bsz, s, _ = proj.shape
    tt = _pick(s, (256, 128, 64))
    kern = functools.partial(_hgrn_kernel, nh=nh, dk=dk, dv=dv, tt=tt)
    col = lambda blk, wdt: pl.BlockSpec((1, tt, wdt), lambda b, i: (b, i, blk))
    return pl.pallas_call(
        kern,
        grid=(bsz, s // tt),
        in_specs=[col(blks[0], nh * dk), col(blks[1], nh * dk), col(blks[2], nh * dv), col(blks[3], nh * dv),
                  pl.BlockSpec((1, nh * dk), lambda b, i: (0, 0)),
                  pl.BlockSpec((1, dv), lambda b, i: (0, 0))],
        out_specs=pl.BlockSpec((1, tt, nh * dv), lambda b, i: (b, i, 0)),
        out_shape=jax.ShapeDtypeStruct((bsz, s, nh * dv), BF16),
        scratch_shapes=[pltpu.VMEM((nh, dv, dk), F32), pltpu.VMEM((nh, dv, dk), F32)],
        compiler_params=_cparams("arbitrary", "arbitrary"),
        name="hgrn",
    )(proj, proj, proj, proj, lb.reshape(1, nh * dk), g.reshape(1, dv))


def _postmix_kernel(oa_ref, ob_ref, ga_ref, gb_ref, x_ref, g1_ref, sc2_ref, sh2_ref,
                    wa_ref, wb_ref, wo_ref, wrT_ref, lng_ref, lnb_ref,
                    x1_ref, h2_ref, lgT_ref, *, alpha):
    ya = _dot(oa_ref[0], wa_ref[...])
    yb = _dot(ob_ref[0], wb_ref[...])
    mix = _sigmoid(ga_ref[0].astype(F32)) * ya + _sigmoid(gb_ref[0].astype(F32)) * yb
    mixed = _dot(mix.astype(BF16), wo_ref[...])
    x1 = _ln(alpha * x_ref[0] + g1_ref[0] * mixed) * lng_ref[...] + lnb_ref[...]
    x1_ref[0] = x1
    h2 = _ln(x1) * (1.0 + sc2_ref[0]) + sh2_ref[0]
    h2_ref[0] = _pack_bf16_pairs(h2)
    lgT_ref[0] = _dot_nt(wrT_ref[...], h2.astype(BF16))


def _postmix(o_a, o_b, proj, ga_blk, gb_blk, x, gate1, scale2, shift2, wa, wb, wo, wrT, lng, lnb, alpha):
    bsz, s, d = x.shape
    ne = wrT.shape[0]
    tm = _pick(s, (256, 128))
    kern = functools.partial(_postmix_kernel, alpha=alpha)
    row = lambda wdt: pl.BlockSpec((1, tm, wdt), lambda b, i: (b, i, 0))
    vec = pl.BlockSpec((1, 1, d), lambda b, i: (b, 0, 0))
    full = lambda a: pl.BlockSpec(a.shape, lambda b, i: (0,) * a.ndim, pipeline_mode=pl.Buffered(1))
    return pl.pallas_call(
        kern,
        grid=(bsz, s // tm),
        in_specs=[row(o_a.shape[2]), row(o_b.shape[2]),
                  pl.BlockSpec((1, tm, d), lambda b, i: (b, i, ga_blk)),
                  pl.BlockSpec((1, tm, d), lambda b, i: (b, i, gb_blk)),
                  row(d), vec, vec, vec,
                  full(wa), full(wb), full(wo), full(wrT),
                  pl.BlockSpec((1, d), lambda b, i: (0, 0)), pl.BlockSpec((1, d), lambda b, i: (0, 0))],
        out_specs=[row(d), row(d // 2), pl.BlockSpec((1, ne, tm), lambda b, i: (b, 0, i))],
        out_shape=[jax.ShapeDtypeStruct((bsz, s, d), F32),
                   jax.ShapeDtypeStruct((bsz, s, d // 2), U32),
                   jax.ShapeDtypeStruct((bsz, ne, s), F32)],
        compiler_params=_cparams("arbitrary", "arbitrary"),
        name="postmix",
    )(o_a, o_b, proj, proj, x, gate1, scale2, shift2, wa, wb, wo, wrT, lng.reshape(1, d), lnb.reshape(1, d))


def _route_kernel(lg_ref, bias_ref, ids_ref, wts_ref, rnk_ref, sizes_ref, upper_ref, carry_ref, *, ne):
    first = jnp.logical_and(pl.program_id(0) == 0, pl.program_id(1) == 0)
    tn = lg_ref.shape[2]

    @pl.when(first)
    def _():
        carry_ref[...] = jnp.zeros(carry_ref.shape, F32)
        r_ = lax.broadcasted_iota(I32, (tn, tn), 0)
        c_ = lax.broadcasted_iota(I32, (tn, tn), 1)
        upper_ref[...] = jnp.where(r_ < c_, 1.0, 0.0).astype(BF16)

    per = ne // N_GROUPS
    s = _sigmoid(lg_ref[0])
    bz = s + bias_ref[...]
    ridx = lax.broadcasted_iota(I32, (per, tn), 0)
    neg_inf = jnp.float32(-jnp.inf)
    gs = []
    for g in range(N_GROUPS):
        blk = bz[g * per:(g + 1) * per]
        m1 = jnp.max(blk, axis=0, keepdims=True)
        first_hit = jnp.min(jnp.where(blk == m1, ridx, per), axis=0, keepdims=True)
        m2 = jnp.max(jnp.where(ridx == first_hit, neg_inf, blk), axis=0, keepdims=True)
        gs.append(m1 + m2)
    emask_rows = []
    for g in range(N_GROUPS):
        rank = jnp.zeros((1, tn), I32)
        for g2 in range(N_GROUPS):
            if g2 == g:
                continue
            beats = (gs[g2] > gs[g]) if g2 > g else (gs[g2] >= gs[g])
            rank = rank + jnp.where(beats, 1, 0)
        emask_rows.append(jnp.broadcast_to(rank < TOPK_GROUPS, (per, tn)))
    emask = jnp.concatenate(emask_rows, axis=0)
    masked = jnp.where(emask, bz, neg_inf)
    eidx = lax.broadcasted_iota(I32, (ne, tn), 0)
    rank = jnp.zeros((ne, tn), I32)
    for e2 in range(ne):
        row = masked[e2:e2 + 1]
        beats = jnp.logical_or(row > masked, jnp.logical_and(row == masked, e2 < eidx))
        rank = rank + jnp.where(beats, 1, 0)
    sel = rank < MOE_TOPK
    sel01 = jnp.where(sel, 1.0, 0.0)
    denom = jnp.sum(jnp.where(sel, s, 0.0), axis=0, keepdims=True)
    wn = s / denom * ROUTED_SCALE
    before = _dot(sel01.astype(BF16), upper_ref[...]) + carry_ref[:, 0:1]
    ids, wts, rnk = [], [], []
    for k in range(MOE_TOPK):
        hit = rank == k
        ids.append(jnp.sum(jnp.where(hit, eidx, 0), axis=0, keepdims=True))
        wts.append(jnp.sum(jnp.where(hit, wn, 0.0), axis=0, keepdims=True))
        rnk.append(jnp.sum(jnp.where(hit, before, 0.0), axis=0, keepdims=True))
    ids_ref[0] = jnp.concatenate(ids, axis=0)
    wts_ref[0] = jnp.concatenate(wts, axis=0)
    rnk_ref[0] = jnp.concatenate(rnk, axis=0).astype(I32)
    carry_ref[...] = carry_ref[...] + jnp.sum(sel01, axis=1, keepdims=True)
    sizes_ref[...] = carry_ref[...]


def _route(lgT, bias):
    bsz, ne, s = lgT.shape
    tn = _pick(s, (1024, 512, 256, 128))
    kern = functools.partial(_route_kernel, ne=ne)
    slot = pl.BlockSpec((1, MOE_TOPK, tn), lambda b, j: (b, 0, j))
    return pl.pallas_call(
        kern,
        grid=(bsz, s // tn),
        in_specs=[pl.BlockSpec((1, ne, tn), lambda b, j: (b, 0, j)),
                  pl.BlockSpec((ne, 1), lambda b, j: (0, 0))],
        out_specs=[slot, slot, slot, pl.BlockSpec((ne, LANES), lambda b, j: (0, 0))],
        out_shape=[jax.ShapeDtypeStruct((bsz, MOE_TOPK, s), I32),
                   jax.ShapeDtypeStruct((bsz, MOE_TOPK, s), F32),
                   jax.ShapeDtypeStruct((bsz, MOE_TOPK, s), I32),
                   jax.ShapeDtypeStruct((ne, LANES), F32)],
        scratch_shapes=[pltpu.VMEM((tn, tn), BF16), pltpu.VMEM((ne, LANES), F32)],
        compiler_params=_cparams("arbitrary", "arbitrary"),
        name="route",
    )(lgT, bias.reshape(ne, 1))


def _dest_kernel(pstart_ref, ids_ref, rnk_ref, o_ref, *, ne):
    ids = ids_ref[0]
    base = jnp.zeros(ids.shape, I32)
    for e in range(ne):
        base = jnp.where(ids == e, pstart_ref[e], base)
    o_ref[0] = base + rnk_ref[0]


def _dest(pstart, ids, rnk, ne):
    bsz, k, s = ids.shape
    tn = _pick(s, (2048, 1024, 512, 256, 128))
    blk = lambda: pl.BlockSpec((1, k, tn), lambda b, j, ps: (b, 0, j))
    return pl.pallas_call(
        functools.partial(_dest_kernel, ne=ne),
        grid_spec=pltpu.PrefetchScalarGridSpec(num_scalar_prefetch=1, grid=(bsz, s // tn),
                                               in_specs=[blk(), blk()], out_specs=blk()),
        out_shape=jax.ShapeDtypeStruct((bsz, k, s), I32),
        compiler_params=_cparams("arbitrary", "arbitrary"),
        name="dest",
    )(pstart, ids, rnk)


def _dispatch_kernel(dst_ref, h_ref, xs_hbm, sem, *, tm, topk):
    def start(t, c):
        for k in range(topk):
            pltpu.make_async_copy(h_ref.at[pl.ds(t, 1)], xs_hbm.at[pl.ds(dst_ref[0, k, t], 1)], sem.at[0]).start()
        return c

    lax.fori_loop(0, tm, start, 0)

    def wait(t, c):
        for k in range(topk):
            pltpu.make_async_copy(h_ref.at[pl.ds(t, 1)], xs_hbm.at[pl.ds(0, 1)], sem.at[0]).wait()
        return c

    lax.fori_loop(0, tm, wait, 0)


def _dispatch(dest, h2, n_rows):
    n, d = h2.shape
    nt, topk, tm = dest.shape
    return pl.pallas_call(
        functools.partial(_dispatch_kernel, tm=tm, topk=topk),
        grid=(nt,),
        in_specs=[pl.BlockSpec((1, topk, tm), lambda i: (i, 0, 0), memory_space=pltpu.SMEM),
                  pl.BlockSpec((tm, d), lambda i: (i, 0))],
        out_specs=pl.BlockSpec(memory_space=pl.ANY),
        out_shape=jax.ShapeDtypeStruct((n_rows, d), h2.dtype),
        scratch_shapes=[pltpu.SemaphoreType.DMA((1,))],
        compiler_params=_cparams("arbitrary"),
        name="dispatch",
    )(dest, h2)


def _experts_kernel(be_ref, nv_ref, x_ref, wg_ref, wu_ref, wd_ref, y_ref, wgb, wub, wdb, *, blk):
    i = pl.program_id(0)
    prev_e = be_ref[jnp.maximum(i - 1, 0)]

    @pl.when(jnp.logical_or(i == 0, be_ref[i] != prev_e))
    def _():
        wgb[...] = wg_ref[0].astype(BF16)
        wub[...] = wu_ref[0].astype(BF16)
        wdb[...] = wd_ref[0].astype(BF16)

    @pl.when(nv_ref[i] > 0)
    def _():
        rows = lax.broadcasted_iota(I32, (blk, 1), 0)
        xu = jnp.where(rows < nv_ref[i], x_ref[...], jnp.uint32(0))
        x = jnp.concatenate(_unpack_bf16_pairs(xu), axis=1).astype(BF16)
        act = (_silu(_dot(x, wgb[...])) * _dot(x, wub[...])).astype(BF16)
        y_ref[...] = _pack_bf16_pairs(_dot(act, wdb[...]))


def _experts(xs, be, nvalid, wg, wu, wd):
    n_rows, dh = xs.shape
    ne, d, f = wg.shape
    blk = EXPERT_BLOCK
    nblocks = n_rows // blk
    grid_spec = pltpu.PrefetchScalarGridSpec(
        num_scalar_prefetch=2,
        grid=(nblocks,),
        in_specs=[pl.BlockSpec((blk, dh), lambda i, be, nv: (i, 0)),
                  pl.BlockSpec((1, d, f), lambda i, be, nv: (be[i], 0, 0)),
                  pl.BlockSpec((1, d, f), lambda i, be, nv: (be[i], 0, 0)),
                  pl.BlockSpec((1, f, d), lambda i, be, nv: (be[i], 0, 0))],
        out_specs=pl.BlockSpec((blk, dh), lambda i, be, nv: (i, 0)),
        scratch_shapes=[pltpu.VMEM((d, f), BF16), pltpu.VMEM((d, f), BF16), pltpu.VMEM((f, d), BF16)],
    )
    return pl.pallas_call(
        functools.partial(_experts_kernel, blk=blk),
        grid_spec=grid_spec,
        out_shape=jax.ShapeDtypeStruct((n_rows, dh), U32),
        compiler_params=_cparams("arbitrary"),
        name="experts",
    )(be, nvalid, xs, wg, wu, wd)


def _final_kernel(dst_ref, dstn_ref, wts_ref, h2_ref, x1_ref, g2_ref, wg_ref, wu_ref, wd_ref, lng_ref, lnb_ref,
                  ys_hbm, o_ref, ybuf, sem, *, alpha, topk, tm, nt):
    i = pl.program_id(0)
    slot = lax.rem(i, 2)

    def gather_start(ids_ref, sl):
        def body(t, c):
            for k in range(topk):
                pltpu.make_async_copy(ys_hbm.at[pl.ds(ids_ref[0, k, t], 1)], ybuf.at[sl, k, pl.ds(t, 1)],
                                      sem.at[sl]).start()
            return c
        lax.fori_loop(0, tm, body, 0)

    @pl.when(i == 0)
    def _():
        gather_start(dst_ref, 0)

    for t in range(tm):
        for k in range(topk):
            pltpu.make_async_copy(ys_hbm.at[pl.ds(dstn_ref[0, k, t], 1)], ybuf.at[1 - slot, k, pl.ds(t, 1)],
                                  sem.at[1 - slot]).start()

    h = jnp.concatenate(_unpack_bf16_pairs(h2_ref[...]), axis=1).astype(BF16)
    y = _dot((_silu(_dot(h, wg_ref[...])) * _dot(h, wu_ref[...])).astype(BF16), wd_ref[...])

    def wait_slot(sl):
        def body(t, c):
            for k in range(topk):
                pltpu.make_async_copy(ys_hbm.at[pl.ds(0, 1)], ybuf.at[sl, k, pl.ds(t, 1)], sem.at[sl]).wait()
            return c
        lax.fori_loop(0, tm, body, 0)

    wait_slot(slot)
    w = wts_ref[...]
    ylo = jnp.zeros((tm, y.shape[1] // 2), F32)
    yhi = jnp.zeros((tm, y.shape[1] // 2), F32)
    for k in range(topk):
        lo, hi = _unpack_bf16_pairs(ybuf[slot, k])
        ylo = ylo + w[:, k:k + 1] * lo
        yhi = yhi + w[:, k:k + 1] * hi
    y = y + jnp.concatenate([ylo, yhi], axis=1)
    o_ref[...] = _ln(alpha * x1_ref[...] + g2_ref[0] * y) * lng_ref[...] + lnb_ref[...]

    @pl.when(i == nt - 1)
    def _():
        wait_slot(1 - slot)


def _final(ys, dest, wtsT, h2, x1, gate2, wg, wu, wd, lng, lnb, alpha, tiles_per_batch):
    n, d = x1.shape
    nt, topk, tm = dest.shape
    kern = functools.partial(_final_kernel, alpha=alpha, topk=topk, tm=tm, nt=nt)
    row = pl.BlockSpec((tm, d), lambda i: (i, 0))
    full = lambda a: pl.BlockSpec(a.shape, lambda i: (0,) * a.ndim, pipeline_mode=pl.Buffered(1))
    return pl.pallas_call(
        kern,
        grid=(nt,),
        in_specs=[pl.BlockSpec((1, topk, tm), lambda i: (i, 0, 0), memory_space=pltpu.SMEM),
                  pl.BlockSpec((1, topk, tm), lambda i: (jnp.minimum(i + 1, nt - 1), 0, 0), memory_space=pltpu.SMEM),
                  pl.BlockSpec((tm, topk), lambda i: (i, 0)),
                  pl.BlockSpec((tm, d // 2), lambda i: (i, 0)), row,
                  pl.BlockSpec((1, 1, d), lambda i: (i // tiles_per_batch, 0, 0)),
                  full(wg), full(wu), full(wd),
                  pl.BlockSpec((1, d), lambda i: (0, 0)), pl.BlockSpec((1, d), lambda i: (0, 0)),
                  pl.BlockSpec(memory_space=pl.ANY)],
        out_specs=row,
        out_shape=jax.ShapeDtypeStruct((n, d), F32),
        scratch_shapes=[pltpu.VMEM((2, topk, tm, d // 2), U32), pltpu.SemaphoreType.DMA((2,))],
        compiler_params=_cparams("arbitrary"),
        name="final",
    )(dest, dest, wtsT, h2, x1, gate2, wg, wu, wd, lng.reshape(1, d), lnb.reshape(1, d), ys)


def _proj_layout(d, qr, kvr, idim, ih, hk, hv):
    src = np.cumsum([0, qr, kvr, idim, ih, hk, hk, hv, hv, d, d])
    wa = -(-(qr + kvr + idim + ih) // LANES) * LANES
    wa = max(wa, 1 << (wa - 1).bit_length())
    pieces = [("a", wa, (int(src[0]), int(src[4]))),
              ("hq", hk, (int(src[4]), int(src[5]))), ("hf", hk, (int(src[5]), int(src[6]))),
              ("hi", hv, (int(src[6]), int(src[7]))), ("hg", hv, (int(src[7]), int(src[8]))),
              ("ga", d, (int(src[8]), int(src[9]))), ("gb", d, (int(src[9]), int(src[10])))]
    pieces.sort(key=lambda p: -p[1])
    off = 0
    layout = {}
    for name, wdt, rng in pieces:
        assert off % wdt == 0
        layout[name] = (off, wdt, rng)
        off += wdt
    return layout, off


def kernel(x, c, rpb_table, hgrn_lb_logits, ada_w, ada_b, w_in, q_norm_g, kv_norm_g, w_uq, w_uk, w_uv, w_qidx,
           idx_k_norm_g, idx_k_norm_b, hgrn_out_norm_g, w_branch_a, w_branch_b, w_o, ln1_g, ln1_b, w_router,
           router_bias, w_exp_gate, w_exp_up, w_exp_down, w_sh_gate, w_sh_up, w_sh_down, ln2_g, ln2_b):
    bsz, s, d = x.shape
    depth = ada_w.shape[0]
    qr = w_uq.shape[1]
    ah, kvr, qk = w_uk.shape[1], w_uk.shape[2], w_uk.shape[3]
    idim = idx_k_norm_g.shape[1]
    ih = w_qidx.shape[2] // idim
    dv = hgrn_out_norm_g.shape[1]
    nh = w_branch_b.shape[1] // dv
    dk = hgrn_lb_logits.shape[1] // nh
    ne = w_router.shape[2]
    topk = min(IDX_TOPK, s // 4)
    alpha = float((2 * depth) ** 0.25)
    n_tok = bsz * s

    lower_bounds = jnp.cumsum(jax.nn.softmax(hgrn_lb_logits.astype(F32), axis=0), axis=0)
    layout, wtot = _proj_layout(d, qr, kvr, idim, ih, nh * dk, nh * dv)

    for l in range(depth):
        mod = _ada(c, ada_w[l], ada_b[l])[:, None, :]
        shift1, scale1, gate1, shift2, scale2, gate2 = jnp.split(mod, 6, axis=-1)

        cols = []
        for off, wdt, (lo, hi) in sorted(layout.values()):
            cols.append(w_in[l][:, lo:hi].astype(BF16))
            if wdt > hi - lo:
                cols.append(jnp.zeros((d, wdt - (hi - lo)), BF16))
        proj = _inproj(x, scale1, shift1, jnp.concatenate(cols, axis=1))
        blk = lambda name: layout[name][0] // layout[name][1]

        qb_sz = _pick(s, (256, 128))
        q_latT, q_idxT, w_idxT, k_idx, kv_lat = _prep(
            proj, blk("a"), layout["a"][1], (qr, kvr, idim, ih, ah, qk), q_norm_g[l], kv_norm_g[l],
            idx_k_norm_g[l], idx_k_norm_b[l], jnp.swapaxes(w_uq[l], 0, 1).astype(BF16),
            w_uk[l].astype(BF16), jnp.swapaxes(w_qidx[l], 0, 1).astype(BF16), qb_sz)
        kv_latT = jnp.concatenate([jnp.swapaxes(kv_lat, 1, 2), jnp.ones((bsz, 1, s), BF16),
                                   jnp.zeros((bsz, 7, s), BF16)], axis=1)
        o_a = _dsa(q_idxT, w_idxT, q_latT, k_idx, kv_lat, kv_latT,
                   rpb_table, jnp.swapaxes(w_uv[l], 1, 2).astype(BF16), topk, qb_sz)

        o_b = _hgrn(proj, (blk("hq"), blk("hf"), blk("hi"), blk("hg")), lower_bounds[l],
                    hgrn_out_norm_g[l], nh, dk, dv)

        x1, h2, lgT = _postmix(o_a, o_b, proj, blk("ga"), blk("gb"), x, gate1, scale2, shift2,
                               w_branch_a[l].astype(BF16), w_branch_b[l].astype(BF16), w_o[l].astype(BF16),
                               jnp.swapaxes(w_router[l], 0, 1).astype(BF16), ln1_g[l], ln1_b[l], alpha)

        ids, wts, rnk, sizes_f = _route(lgT, router_bias[l])
        eb = EXPERT_BLOCK
        sizes = sizes_f[:, 0].astype(I32)
        padded = (sizes + eb - 1) // eb * eb
        pend = jnp.cumsum(padded)
        pstart = pend - padded
        n_rows = -(-(n_tok * MOE_TOPK + ne * (eb - 1)) // eb) * eb
        blk_start = jnp.arange(n_rows // eb, dtype=I32) * eb
        blk_expert = jnp.minimum(jnp.sum((pend[None, :] <= blk_start[:, None]).astype(I32), axis=1), ne - 1)
        onehot = (blk_expert[:, None] == jnp.arange(ne, dtype=I32)[None, :]).astype(I32)
        blk_size = jnp.sum(onehot * sizes[None, :], axis=1)
        blk_pstart = jnp.sum(onehot * pstart[None, :], axis=1)
        nvalid = jnp.clip(blk_size - (blk_start - blk_pstart), 0, eb).astype(I32)
        dest = _dest(pstart.astype(I32), ids, rnk, ne)
        tm = _pick(s, (128,))
        dest_t = dest.reshape(bsz, MOE_TOPK, s // tm, tm).transpose(0, 2, 1, 3).reshape(n_tok // tm, MOE_TOPK, tm)
        wtsT = jnp.swapaxes(wts, 1, 2).reshape(n_tok, MOE_TOPK)

        h2f = h2.reshape(n_tok, d // 2)
        xs = _dispatch(dest_t, h2f, n_rows)
        ys = _experts(xs, blk_expert, nvalid, w_exp_gate[l], w_exp_up[l], w_exp_down[l])
        x = _final(ys, dest_t, wtsT, h2f, x1.reshape(n_tok, d), gate2, w_sh_gate[l].astype(BF16),
                   w_sh_up[l].astype(BF16), w_sh_down[l].astype(BF16), ln2_g[l], ln2_b[l], alpha,
                   s // tm).reshape(bsz, s, d)
    return x
```

```python
import functools
import math

import numpy as np
import jax
import jax.numpy as jnp
from jax import lax
from jax.experimental import pallas as pl
from jax.experimental.pallas import tpu as pltpu

F32 = jnp.float32
BF16 = jnp.bfloat16
I32 = jnp.int32
U32 = jnp.uint32

EPS = 1e-6
IDX_TOPK = 256
RPB_MAX_DIST = 128
MOE_TOPK = 8
N_GROUPS = 8
TOPK_GROUPS = 4
ROUTED_SCALE = 2.5
EXPERT_BLOCK = 512

V7X_VMEM_LIMIT_BYTES = 56 * 1024 * 1024
LANES = 128
INT_MIN = -2 ** 31
NEG_BIG = -1e30
LOG2E = math.log2(math.e)


def _cparams(*sem):
    return pltpu.CompilerParams(dimension_semantics=tuple(sem), vmem_limit_bytes=V7X_VMEM_LIMIT_BYTES)


def _pick(n, prefs):
    for p in prefs:
        if n % p == 0:
            return p
    return n


def _sigmoid(v):
    return 1.0 / (1.0 + jnp.exp(-v))


def _silu(v):
    return v * _sigmoid(v)


def _ln(v):
    mu = jnp.mean(v, axis=-1, keepdims=True)
    d = v - mu
    var = jnp.mean(d * d, axis=-1, keepdims=True)
    return d * lax.rsqrt(var + EPS)


def _rms(v):
    return v * lax.rsqrt(jnp.mean(v * v, axis=-1, keepdims=True) + EPS)


def _dot(a, b):
    return jnp.dot(a, b, preferred_element_type=F32)


def _pack_bf16_pairs(v):
    n = v.shape[1] // 2
    lo = pltpu.bitcast(v[:, :n].astype(BF16).astype(F32), U32) >> 16
    hi = pltpu.bitcast(v[:, n:].astype(BF16).astype(F32), U32) & jnp.uint32(0xFFFF0000)
    return lo | hi


def _unpack_bf16_pairs(u):
    return pltpu.bitcast(u << 16, F32), pltpu.bitcast(u & jnp.uint32(0xFFFF0000), F32)


def _dot_nt(a, b):
    return lax.dot_general(a, b, (((1,), (1,)), ((), ())), preferred_element_type=F32)


def _dot_tn(a, b):
    return lax.dot_general(a, b, (((0,), (0,)), ((), ())), preferred_element_type=F32)


def _ada_kernel(c_ref, w_ref, b_ref, o_ref):
    c = c_ref[...]
    o_ref[...] = _dot(_silu(c).astype(BF16), w_ref[...].astype(BF16)) + b_ref[...]


def _ada(c, w, b):
    bsz, d = c.shape
    n = w.shape[1]
    rows = 8
    cp = jnp.zeros((rows, d), F32).at[:bsz].set(c)
    tn = _pick(n, (1024, 512, 256, 128))
    out = pl.pallas_call(
        _ada_kernel,
        grid=(n // tn,),
        in_specs=[pl.BlockSpec((rows, d), lambda j: (0, 0)),
                  pl.BlockSpec((d, tn), lambda j: (0, j)),
                  pl.BlockSpec((1, tn), lambda j: (0, j))],
        out_specs=pl.BlockSpec((rows, tn), lambda j: (0, j)),
        out_shape=jax.ShapeDtypeStruct((rows, n), F32),
        compiler_params=_cparams("arbitrary"),
        name="ada",
    )(cp, w, b.reshape(1, n))
    return out[:bsz]


def _inproj_kernel(x_ref, sc_ref, sh_ref, w_ref, o_ref, h_ref):
    @pl.when(pl.program_id(2) == 0)
    def _():
        h = _ln(x_ref[0]) * (1.0 + sc_ref[0]) + sh_ref[0]
        h_ref[...] = h.astype(BF16)

    o_ref[0] = _dot(h_ref[...], w_ref[...]).astype(o_ref.dtype)


def _inproj(x, scale, shift, w):
    bsz, s, d = x.shape
    n = w.shape[1]
    tm = _pick(s, (1024, 512, 256, 128))
    tn = _pick(n, (1024, 512, 256, 128))
    return pl.pallas_call(
        _inproj_kernel,
        grid=(bsz, s // tm, n // tn),
        in_specs=[pl.BlockSpec((1, tm, d), lambda b, i, j: (b, i, 0)),
                  pl.BlockSpec((1, 1, d), lambda b, i, j: (b, 0, 0)),
                  pl.BlockSpec((1, 1, d), lambda b, i, j: (b, 0, 0)),
                  pl.BlockSpec((d, tn), lambda b, i, j: (0, j))],
        out_specs=pl.BlockSpec((1, tm, tn), lambda b, i, j: (b, i, j)),
        out_shape=jax.ShapeDtypeStruct((bsz, s, n), BF16),
        scratch_shapes=[pltpu.VMEM((tm, d), BF16)],
        compiler_params=_cparams("arbitrary", "arbitrary", "arbitrary"),
        name="inproj",
    )(x, scale, shift, w)


def _prep_kernel(a_ref, qg_ref, kvg_ref, ikg_ref, ikb_ref, wuqT_ref, wuk_ref, wqiT_ref,
                 qlatT_ref, qidxT_ref, widxT_ref, kidx_ref, kv_ref, *, qr, kvr, idim, ih, ah, qk):
    a = a_ref[0]
    af = a.astype(F32)
    cq = (_rms(af[:, :qr]) * qg_ref[...]).astype(BF16)
    ckv = af[:, qr:qr + kvr]
    ki = af[:, qr + kvr:qr + kvr + idim]
    kv_ref[0] = (_rms(ckv) * kvg_ref[...]).astype(BF16)
    kidx_ref[0] = (_ln(ki) * ikg_ref[...] + ikb_ref[...]).astype(BF16)
    tail = a[:, qr + kvr:qr + kvr + LANES]
    eye = (lax.broadcasted_iota(I32, (LANES, LANES), 0) == lax.broadcasted_iota(I32, (LANES, LANES), 1))
    tailT = _dot_nt(jnp.where(eye, 1.0, 0.0).astype(BF16), tail)
    widxT_ref[0] = tailT[idim:idim + ih] * float((ih * idim) ** -0.5)
    tm = a.shape[0]
    qT = _dot_nt(wuqT_ref[...], cq)
    for h in range(ah):
        qh = qT[h * qk:(h + 1) * qk].astype(BF16)
        qlatT_ref[0, 0, :, h * tm:(h + 1) * tm] = (_dot(wuk_ref[h], qh) * float(qk ** -0.5 * LOG2E)).astype(BF16)
    qiT = _dot_nt(wqiT_ref[...], cq)
    for h in range(ih):
        qidxT_ref[0, 0, :, h * tm:(h + 1) * tm] = qiT[h * idim:(h + 1) * idim].astype(BF16)


def _prep(proj, a_blk, wa, dims, q_norm_g, kv_norm_g, ikg, ikb, w_uqT, w_uk, w_qidxT, tm):
    bsz, s, _ = proj.shape
    qr, kvr, idim, ih, ah, qk = dims
    assert idim + ih <= LANES and qr + kvr + LANES <= wa
    kern = functools.partial(_prep_kernel, qr=qr, kvr=kvr, idim=idim, ih=ih, ah=ah, qk=qk)
    full = lambda shape: pl.BlockSpec(shape, lambda b, i: (0,) * len(shape))
    return pl.pallas_call(
        kern,
        grid=(bsz, s // tm),
        in_specs=[pl.BlockSpec((1, tm, wa), lambda b, i: (b, i, a_blk)),
                  full((1, qr)), full((1, kvr)), full((1, idim)), full((1, idim)),
                  full(w_uqT.shape), full(w_uk.shape), full(w_qidxT.shape)],
        out_specs=[pl.BlockSpec((1, 1, kvr, ah * tm), lambda b, i: (b, i, 0, 0)),
                   pl.BlockSpec((1, 1, idim, ih * tm), lambda b, i: (b, i, 0, 0)),
                   pl.BlockSpec((1, ih, tm), lambda b, i: (b, 0, i)),
                   pl.BlockSpec((1, tm, idim), lambda b, i: (b, i, 0)),
                   pl.BlockSpec((1, tm, kvr), lambda b, i: (b, i, 0))],
        out_shape=[jax.ShapeDtypeStruct((bsz, s // tm, kvr, ah * tm), BF16),
                   jax.ShapeDtypeStruct((bsz, s // tm, idim, ih * tm), BF16),
                   jax.ShapeDtypeStruct((bsz, ih, s), F32),
                   jax.ShapeDtypeStruct((bsz, s, idim), BF16),
                   jax.ShapeDtypeStruct((bsz, s, kvr), BF16)],
        compiler_params=_cparams("arbitrary", "arbitrary"),
        name="prep",
    )(proj, q_norm_g.reshape(1, qr), kv_norm_g.reshape(1, kvr), ikg.reshape(1, idim), ikb.reshape(1, idim),
      w_uqT, w_uk, w_qidxT)


def _dsa_kernel(qidxT_ref, widxT_ref, qlatT_ref, kidx_ref, kv_ref, kvT_ref, bkt0_ref, bkt1_ref, rel_ref, wuvT_ref,
                o_ref, keys_ref, gmax_ref, s_ref, p_ref, m_ref, a_ref, acc_ref, t0_ref, t1_ref,
                *, qb_sz, topk, ih, ah, c, nb):
    QB = qb_sz
    qb = pl.program_id(1)
    nchunks = qb + 1
    qpos = qb * QB + lax.broadcasted_iota(I32, (1, QB), 1)
    kpos = lax.broadcasted_iota(I32, (QB, 1), 0)
    wT = widxT_ref[0]
    hcols = lambda h: slice(h * QB, (h + 1) * QB)

    @pl.when(jnp.logical_and(pl.program_id(0) == 0, qb == 0))
    def _():
        for bkt_ref, t_ref in ((bkt0_ref, t0_ref), (bkt1_ref, t1_ref)):
            bkt = bkt_ref[...]
            for h in range(ah):
                tile = jnp.zeros((QB, QB), F32)
                for b in range(nb - 1):
                    tile = jnp.where(bkt == b, rel_ref[b, h], tile)
                t_ref[:, hcols(h)] = tile

    def score_chunk(kc, carry):
        off = pl.multiple_of(kc * QB, QB)
        s_ref[:, :ih * QB] = _dot(kidx_ref[0, pl.ds(off, QB), :], qidxT_ref[0, 0])
        acc = jnp.zeros((QB, QB), F32)
        for h in range(ih):
            acc = acc + wT[h:h + 1] * jnp.maximum(s_ref[:, hcols(h)], 0.0)
        bits = pltpu.bitcast(acc, I32)
        skey = bits ^ ((bits >> 31) & 0x7FFFFFFF)
        causal = (off + kpos) <= qpos
        skey = jnp.where(causal, skey, INT_MIN)
        keys_ref[pl.ds(off, QB), :] = skey
        gmax_ref[...] = jnp.maximum(gmax_ref[...], skey)
        return carry

    gmax_ref[...] = jnp.full(gmax_ref.shape, INT_MIN, I32)
    lax.fori_loop(0, nchunks, score_chunk, 0)

    gmax = gmax_ref[...]
    lo0 = jnp.min(gmax, axis=0, keepdims=True) if QB >= topk else jnp.full((1, QB), INT_MIN, I32)
    hi0 = jnp.max(gmax, axis=0, keepdims=True) + 1

    def count_ge(cand):
        def body(kc, cnt):
            off = pl.multiple_of(kc * QB, QB)
            hit = jnp.where(keys_ref[pl.ds(off, QB), :] >= cand, 1, 0)
            return cnt + jnp.sum(hit.reshape(QB // 8, 8, QB), axis=0)

        cnt = lax.fori_loop(0, nchunks, body, jnp.zeros((8, QB), I32))
        return jnp.sum(cnt.astype(F32), axis=0, keepdims=True)

    def bis_cond(st):
        it, _, _, done = st
        return jnp.logical_and(it < 34, jnp.min(done) < 0.5)

    def bis_body(st):
        it, lo, hi, done = st
        cand = (lo >> 1) + (hi >> 1) + (lo & hi & 1)
        cnt = count_ge(cand)
        ge = cnt >= float(topk)
        conv = cand == lo
        fin = jnp.logical_or(conv, cnt == float(topk))
        thr_new = jnp.where(conv, lo, cand)
        lo = jnp.where(fin, thr_new, jnp.where(ge, cand, lo))
        hi = jnp.where(fin, thr_new + 1, jnp.where(ge, hi, cand))
        return it + 1, lo, hi, jnp.where(fin, 1.0, done)

    _, thr, _, _ = lax.while_loop(bis_cond, bis_body, (jnp.int32(0), lo0, hi0, jnp.zeros((1, QB), F32)))

    m_ref[...] = jnp.full(m_ref.shape, NEG_BIG, F32)
    acc_ref[...] = jnp.zeros(acc_ref.shape, F32)

    def attn_chunk(off, bias_ref, diag):
        sel = keys_ref[pl.ds(off, QB), :] >= thr
        if diag:
            sel = jnp.logical_and(sel, (off + kpos) <= qpos)
        madd = jnp.where(sel, 0.0, NEG_BIG)
        s_ref[:, :ah * QB] = _dot(kv_ref[0, pl.ds(off, QB), :], qlatT_ref[0, 0])
        for g in range(ah * QB // LANES):
            cols = slice(g * LANES, (g + 1) * LANES)
            qcols = slice(g * LANES % QB, g * LANES % QB + LANES)
            s = s_ref[:, cols] + madd[:, qcols]
            if bias_ref is not None:
                s = s + bias_ref[:, cols]
            m_prev = m_ref[:, cols]
            m_new = jnp.maximum(m_prev, jnp.max(s, axis=0, keepdims=True))
            p_ref[:, cols] = jnp.exp2(s - m_new).astype(BF16)
            a_ref[:, cols] = jnp.exp2(m_prev - m_new)
            m_ref[:, cols] = m_new
        acc_ref[...] = a_ref[...] * acc_ref[...] + _dot(kvT_ref[0, :, pl.ds(off, QB)], p_ref[...])

    def far_body(kc, carry):
        attn_chunk(pl.multiple_of(kc * QB, QB), None, False)
        return carry

    lax.fori_loop(0, qb - 1, far_body, 0)

    @pl.when(qb >= 1)
    def _():
        attn_chunk(pl.multiple_of((qb - 1) * QB, QB), t1_ref, False)

    attn_chunk(pl.multiple_of(qb * QB, QB), t0_ref, True)

    outs = []
    for h in range(ah):
        o = (acc_ref[:c, hcols(h)] / acc_ref[c:c + 1, hcols(h)]).astype(BF16)
        outs.append(_dot(wuvT_ref[h], o))
    o_ref[0] = jnp.concatenate(outs, axis=0).T.astype(o_ref.dtype)


def _t5_bucket_np(d, nbuckets):
    max_exact = nbuckets // 2
    dd = np.maximum(d, 1).astype(np.float32)
    large = max_exact + (np.log(dd / np.float32(max_exact)) / np.float32(math.log(RPB_MAX_DIST / max_exact))
                         * np.float32(nbuckets - max_exact)).astype(np.int32)
    large = np.minimum(large, nbuckets - 1)
    return np.where(d < max_exact, d, large).astype(np.int32)


def _dsa(q_idxT, w_idxT, q_latT, k_idx, kv_lat, kv_latT, rpb_table, w_uvT, topk, QB):
    bsz, nqb, idim, ihq = q_idxT.shape
    c, ahq = q_latT.shape[2], q_latT.shape[3]
    ca = kv_latT.shape[1]
    ih, ah = ihq // QB, ahq // QB
    s = nqb * QB
    vd = w_uvT.shape[1]
    nb = rpb_table.shape[0]
    assert QB >= RPB_MAX_DIST
    j = np.arange(QB)[:, None]
    i = np.arange(QB)[None, :]
    bkt0 = jnp.asarray(_t5_bucket_np(np.maximum(i - j, 0), nb))
    bkt1 = jnp.asarray(_t5_bucket_np(QB + i - j, nb))
    assert int(_t5_bucket_np(np.array([RPB_MAX_DIST]), nb)[0]) == nb - 1
    rel = (rpb_table.astype(F32) - rpb_table[nb - 1].astype(F32)[None, :]) * LOG2E
    kern = functools.partial(_dsa_kernel, qb_sz=QB, topk=topk, ih=ih, ah=ah, c=c, nb=nb)
    const = lambda shape: pl.BlockSpec(shape, lambda b, i: (0,) * len(shape), pipeline_mode=pl.Buffered(1))
    hw = max(ih, ah) * QB
    return pl.pallas_call(
        kern,
        grid=(bsz, nqb),
        in_specs=[pl.BlockSpec((1, 1, idim, ih * QB), lambda b, i: (b, i, 0, 0)),
                  pl.BlockSpec((1, ih, QB), lambda b, i: (b, 0, i)),
                  pl.BlockSpec((1, 1, c, ah * QB), lambda b, i: (b, i, 0, 0)),
                  pl.BlockSpec((1, s, idim), lambda b, i: (b, 0, 0)),
                  pl.BlockSpec((1, s, c), lambda b, i: (b, 0, 0)),
                  pl.BlockSpec((1, ca, s), lambda b, i: (b, 0, 0)),
                  const((QB, QB)), const((QB, QB)),
                  pl.BlockSpec(memory_space=pltpu.SMEM), const(w_uvT.shape)],
        out_specs=pl.BlockSpec((1, QB, ah * vd), lambda b, i: (b, i, 0)),
        out_shape=jax.ShapeDtypeStruct((bsz, s, ah * vd), BF16),
        scratch_shapes=[pltpu.VMEM((s, QB), I32),
                        pltpu.VMEM((QB, QB), I32),
                        pltpu.VMEM((QB, hw), F32),
                        pltpu.VMEM((QB, ah * QB), BF16),
                        pltpu.VMEM((1, ah * QB), F32),
                        pltpu.VMEM((1, ah * QB), F32),
                        pltpu.VMEM((ca, ah * QB), F32),
                        pltpu.VMEM((QB, ah * QB), F32),
                        pltpu.VMEM((QB, ah * QB), F32)],
        compiler_params=_cparams("arbitrary", "arbitrary"),
        name="dsa",
    )(q_idxT, w_idxT, q_latT, k_idx, kv_lat, kv_latT, bkt0, bkt1, rel, w_uvT)


HGRN_CHUNK = 64
HGRN_SUB = 16
HGRN_EXP_CLAMP = 80.0


def _dot_exact(a, b, dims):
    return lax.dot_general(a, b, (dims, ((), ())), precision=lax.Precision.HIGHEST, preferred_element_type=F32)


def _hgrn_kernel(hq_ref, hf_ref, hi_ref, hg_ref, lb_ref, g_ref, o_ref, st_ref, st0_ref, *, nh, dk, dv, tt):
    C, SUB = HGRN_CHUNK, HGRN_SUB

    @pl.when(pl.program_id(1) == 0)
    def _():
        st_ref[...] = jnp.zeros(st_ref.shape, F32)

    r = lax.broadcasted_iota(I32, (C, C), 0)
    cc = lax.broadcasted_iota(I32, (C, C), 1)
    tri_mask = r >= cc
    tri = jnp.where(tri_mask, 1.0, 0.0).astype(BF16)
    g = g_ref[...]
    st0_ref[...] = st_ref[...]
    decay = jnp.zeros((1, nh * dk), F32)
    lb = lb_ref[...]

    for c in range(tt // C):
        rows = pl.ds(c * C, C)
        f = lb + (1.0 - lb) * _sigmoid(hf_ref[0, rows, :].astype(F32))
        lf = jnp.log(f)
        t1 = lf.astype(BF16)
        r1 = lf - t1.astype(F32)
        t2 = r1.astype(BF16)
        t3 = (r1 - t2.astype(F32)).astype(BF16)
        b = _dot(tri, t1) + _dot(tri, t2) + _dot(tri, t3)
        kk = 1.0 - f
        hq = hq_ref[0, rows, :].astype(F32)
        q = _silu(hq) * float(dk ** -0.5)
        qe = (q * jnp.exp(b)).astype(BF16)
        b_last = b[C - 1:C]
        k_dec = (kk * jnp.exp(b_last - b)).astype(BF16)
        dec_last = jnp.exp(b_last)
        qs, ks = [], []
        for i in range(C // SUB):
            lo, n = i * SUB, (i + 1) * SUB
            bi = b[lo - 1:lo] if i > 0 else jnp.zeros((1, nh * dk), F32)
            decay = jnp.maximum(decay, bi - b[n - 1:n])
            qs.append((q[lo:n] * jnp.exp(b[lo:n] - bi)).astype(BF16))
            ks.append((kk * jnp.exp(jnp.minimum(bi - b, HGRN_EXP_CLAMP))).astype(BF16))
        v_all = hi_ref[0, rows, :]
        outs = []
        for h in range(nh):
            kc = slice(h * dk, (h + 1) * dk)
            v = v_all[:, h * dv:(h + 1) * dv]
            stT = st_ref[h]
            att = jnp.concatenate([_dot_nt(qs[i][:, kc], ks[i][:, kc]) for i in range(C // SUB)], axis=0)
            att = jnp.where(tri_mask, att, 0.0)
            o = _dot_nt(qe[:, kc], stT.astype(BF16)) + _dot(att.astype(BF16), v)
            st_ref[h] = stT * dec_last[:, kc] + _dot_tn(v, k_dec[:, kc])
            outs.append(_rms(o) * g)
        o_all = jnp.concatenate(outs, axis=1) * _silu(hg_ref[0, rows, :].astype(F32))
        o_ref[0, rows, :] = o_all.astype(o_ref.dtype)

    @pl.when(jnp.max(decay) > HGRN_EXP_CLAMP)
    def _():
        row = lax.broadcasted_iota(I32, (SUB, 1), 0)

        def head(h, carry):
            ko = pl.multiple_of(h * dk, dk)
            vo = pl.multiple_of(h * dv, dv)
            lb = lb_ref[:, pl.ds(ko, dk)]

            def slab(j, S):
                rows = pl.ds(pl.multiple_of(j * SUB, SUB), SUB)
                f = lb + (1.0 - lb) * _sigmoid(hf_ref[0, rows, pl.ds(ko, dk)].astype(F32))
                kk = 1.0 - f
                hq = hq_ref[0, rows, pl.ds(ko, dk)].astype(F32)
                q = _silu(hq) * float(dk ** -0.5)
                v = hi_ref[0, rows, pl.ds(vo, dv)].astype(F32)
                o = jnp.zeros((SUB, dv), F32)
                for r in range(SUB):
                    S = S * f[r:r + 1] + _dot_exact(jnp.where(row == r, v, 0.0), kk, ((0,), (0,)))
                    o = jnp.where(row == r, _dot_exact(q, S, ((1,), (1,))), o)
                hg = hg_ref[0, rows, pl.ds(vo, dv)].astype(F32)
                o_ref[0, rows, pl.ds(vo, dv)] = (_rms(o) * g * _silu(hg)).astype(o_ref.dtype)
                return S

            st_ref[h] = lax.fori_loop(0, tt // SUB, slab, st0_ref[h])
            return carry

        lax.fori_loop(0, nh, head, 0)


def _hgrn(proj, blks, lb, g, nh, dk, dv):
    bsz, s, _ = proj.shape
    tt = _pick(s, (256, 128, 64))
    kern = functools.partial(_hgrn_kernel, nh=nh, dk=dk, dv=dv, tt=tt)
    col = lambda blk, wdt: pl.BlockSpec((1, tt, wdt), lambda b, i: (b, i, blk))
    return pl.pallas_call(
        kern,
        grid=(bsz, s // tt),
        in_specs=[col(blks[0], nh * dk), col(blks[1], nh * dk), col(blks[2], nh * dv), col(blks[3], nh * dv),
                  pl.BlockSpec((1, nh * dk), lambda b, i: (0, 0)),
                  pl.BlockSpec((1, dv), lambda b, i: (0, 0))],
        out_specs=pl.BlockSpec((1, tt, nh * dv), lambda b, i: (b, i, 0)),
        out_shape=jax.ShapeDtypeStruct((bsz, s, nh * dv), BF16),
        scratch_shapes=[pltpu.VMEM((nh, dv, dk), F32), pltpu.VMEM((nh, dv, dk), F32)],
        compiler_params=_cparams("arbitrary", "arbitrary"),
        name="hgrn",
    )(proj, proj, proj, proj, lb.reshape(1, nh * dk), g.reshape(1, dv))


def _postmix_kernel(oa_ref, ob_ref, ga_ref, gb_ref, x_ref, g1_ref, sc2_ref, sh2_ref,
                    wa_ref, wb_ref, wo_ref, wrT_ref, lng_ref, lnb_ref,
                    x1_ref, h2_ref, lgT_ref, *, alpha):
    ya = _dot(oa_ref[0], wa_ref[...])
    yb = _dot(ob_ref[0], wb_ref[...])
    mix = _sigmoid(ga_ref[0].astype(F32)) * ya + _sigmoid(gb_ref[0].astype(F32)) * yb
    mixed = _dot(mix.astype(BF16), wo_ref[...])
    x1 = _ln(alpha * x_ref[0] + g1_ref[0] * mixed) * lng_ref[...] + lnb_ref[...]
    x1_ref[0] = x1
    h2 = _ln(x1) * (1.0 + sc2_ref[0]) + sh2_ref[0]
    h2_ref[0] = _pack_bf16_pairs(h2)
    lgT_ref[0] = _dot_nt(wrT_ref[...], h2.astype(BF16))


def _postmix(o_a, o_b, proj, ga_blk, gb_blk, x, gate1, scale2, shift2, wa, wb, wo, wrT, lng, lnb, alpha):
    bsz, s, d = x.shape
    ne = wrT.shape[0]
    tm = _pick(s, (256, 128))
    kern = functools.partial(_postmix_kernel, alpha=alpha)
    row = lambda wdt: pl.BlockSpec((1, tm, wdt), lambda b, i: (b, i, 0))
    vec = pl.BlockSpec((1, 1, d), lambda b, i: (b, 0, 0))
    full = lambda a: pl.BlockSpec(a.shape, lambda b, i: (0,) * a.ndim, pipeline_mode=pl.Buffered(1))
    return pl.pallas_call(
        kern,
        grid=(bsz, s // tm),
        in_specs=[row(o_a.shape[2]), row(o_b.shape[2]),
                  pl.BlockSpec((1, tm, d), lambda b, i: (b, i, ga_blk)),
                  pl.BlockSpec((1, tm, d), lambda b, i: (b, i, gb_blk)),
                  row(d), vec, vec, vec,
                  full(wa), full(wb), full(wo), full(wrT),
                  pl.BlockSpec((1, d), lambda b, i: (0, 0)), pl.BlockSpec((1, d), lambda b, i: (0, 0))],
        out_specs=[row(d), row(d // 2), pl.BlockSpec((1, ne, tm), lambda b, i: (b, 0, i))],
        out_shape=[jax.ShapeDtypeStruct((bsz, s, d), F32),
                   jax.ShapeDtypeStruct((bsz, s, d // 2), U32),
                   jax.ShapeDtypeStruct((bsz, ne, s), F32)],
        compiler_params=_cparams("arbitrary", "arbitrary"),
        name="postmix",
    )(o_a, o_b, proj, proj, x, gate1, scale2, shift2, wa, wb, wo, wrT, lng.reshape(1, d), lnb.reshape(1, d))


def _route_kernel(lg_ref, bias_ref, ids_ref, wts_ref, rnk_ref, sizes_ref, upper_ref, carry_ref, *, ne):
    first = jnp.logical_and(pl.program_id(0) == 0, pl.program_id(1) == 0)
    tn = lg_ref.shape[2]

    @pl.when(first)
    def _():
        carry_ref[...] = jnp.zeros(carry_ref.shape, F32)
        r_ = lax.broadcasted_iota(I32, (tn, tn), 0)
        c_ = lax.broadcasted_iota(I32, (tn, tn), 1)
        upper_ref[...] = jnp.where(r_ < c_, 1.0, 0.0).astype(BF16)

    per = ne // N_GROUPS
    s = _sigmoid(lg_ref[0])
    bz = s + bias_ref[...]
    ridx = lax.broadcasted_iota(I32, (per, tn), 0)
    neg_inf = jnp.float32(-jnp.inf)
    gs = []
    for g in range(N_GROUPS):
        blk = bz[g * per:(g + 1) * per]
        m1 = jnp.max(blk, axis=0, keepdims=True)
        first_hit = jnp.min(jnp.where(blk == m1, ridx, per), axis=0, keepdims=True)
        m2 = jnp.max(jnp.where(ridx == first_hit, neg_inf, blk), axis=0, keepdims=True)
        gs.append(m1 + m2)
    emask_rows = []
    for g in range(N_GROUPS):
        rank = jnp.zeros((1, tn), I32)
        for g2 in range(N_GROUPS):
            if g2 == g:
                continue
            beats = (gs[g2] > gs[g]) if g2 > g else (gs[g2] >= gs[g])
            rank = rank + jnp.where(beats, 1, 0)
        emask_rows.append(jnp.broadcast_to(rank < TOPK_GROUPS, (per, tn)))
    emask = jnp.concatenate(emask_rows, axis=0)
    masked = jnp.where(emask, bz, neg_inf)
    eidx = lax.broadcasted_iota(I32, (ne, tn), 0)
    rank = jnp.zeros((ne, tn), I32)
    for e2 in range(ne):
        row = masked[e2:e2 + 1]
        beats = jnp.logical_or(row > masked, jnp.logical_and(row == masked, e2 < eidx))
        rank = rank + jnp.where(beats, 1, 0)
    sel = rank < MOE_TOPK
    sel01 = jnp.where(sel, 1.0, 0.0)
    denom = jnp.sum(jnp.where(sel, s, 0.0), axis=0, keepdims=True)
    wn = s / denom * ROUTED_SCALE
    before = _dot(sel01.astype(BF16), upper_ref[...]) + carry_ref[:, 0:1]
    ids, wts, rnk = [], [], []
    for k in range(MOE_TOPK):
        hit = rank == k
        ids.append(jnp.sum(jnp.where(hit, eidx, 0), axis=0, keepdims=True))
        wts.append(jnp.sum(jnp.where(hit, wn, 0.0), axis=0, keepdims=True))
        rnk.append(jnp.sum(jnp.where(hit, before, 0.0), axis=0, keepdims=True))
    ids_ref[0] = jnp.concatenate(ids, axis=0)
    wts_ref[0] = jnp.concatenate(wts, axis=0)
    rnk_ref[0] = jnp.concatenate(rnk, axis=0).astype(I32)
    carry_ref[...] = carry_ref[...] + jnp.sum(sel01, axis=1, keepdims=True)
    sizes_ref[...] = carry_ref[...]


def _route(lgT, bias):
    bsz, ne, s = lgT.shape
    tn = _pick(s, (1024, 512, 256, 128))
    kern = functools.partial(_route_kernel, ne=ne)
    slot = pl.BlockSpec((1, MOE_TOPK, tn), lambda b, j: (b, 0, j))
    return pl.pallas_call(
        kern,
        grid=(bsz, s // tn),
        in_specs=[pl.BlockSpec((1, ne, tn), lambda b, j: (b, 0, j)),
                  pl.BlockSpec((ne, 1), lambda b, j: (0, 0))],
        out_specs=[slot, slot, slot, pl.BlockSpec((ne, LANES), lambda b, j: (0, 0))],
        out_shape=[jax.ShapeDtypeStruct((bsz, MOE_TOPK, s), I32),
                   jax.ShapeDtypeStruct((bsz, MOE_TOPK, s), F32),
                   jax.ShapeDtypeStruct((bsz, MOE_TOPK, s), I32),
                   jax.ShapeDtypeStruct((ne, LANES), F32)],
        scratch_shapes=[pltpu.VMEM((tn, tn), BF16), pltpu.VMEM((ne, LANES), F32)],
        compiler_params=_cparams("arbitrary", "arbitrary"),
        name="route",
    )(lgT, bias.reshape(ne, 1))


def _dest_kernel(pstart_ref, ids_ref, rnk_ref, o_ref, *, ne):
    ids = ids_ref[0]
    base = jnp.zeros(ids.shape, I32)
    for e in range(ne):
        base = jnp.where(ids == e, pstart_ref[e], base)
    o_ref[0] = base + rnk_ref[0]


def _dest(pstart, ids, rnk, ne):
    bsz, k, s = ids.shape
    tn = _pick(s, (2048, 1024, 512, 256, 128))
    blk = lambda: pl.BlockSpec((1, k, tn), lambda b, j, ps: (b, 0, j))
    return pl.pallas_call(
        functools.partial(_dest_kernel, ne=ne),
        grid_spec=pltpu.PrefetchScalarGridSpec(num_scalar_prefetch=1, grid=(bsz, s // tn),
                                               in_specs=[blk(), blk()], out_specs=blk()),
        out_shape=jax.ShapeDtypeStruct((bsz, k, s), I32),
        compiler_params=_cparams("arbitrary", "arbitrary"),
        name="dest",
    )(pstart, ids, rnk)


def _dispatch_kernel(dst_ref, h_ref, xs_hbm, sem, *, tm, topk):
    def start(t, c):
        for k in range(topk):
            pltpu.make_async_copy(h_ref.at[pl.ds(t, 1)], xs_hbm.at[pl.ds(dst_ref[0, k, t], 1)], sem.at[0]).start()
        return c

    lax.fori_loop(0, tm, start, 0)

    def wait(t, c):
        for k in range(topk):
            pltpu.make_async_copy(h_ref.at[pl.ds(t, 1)], xs_hbm.at[pl.ds(0, 1)], sem.at[0]).wait()
        return c

    lax.fori_loop(0, tm, wait, 0)


def _dispatch(dest, h2, n_rows):
    n, d = h2.shape
    nt, topk, tm = dest.shape
    return pl.pallas_call(
        functools.partial(_dispatch_kernel, tm=tm, topk=topk),
        grid=(nt,),
        in_specs=[pl.BlockSpec((1, topk, tm), lambda i: (i, 0, 0), memory_space=pltpu.SMEM),
                  pl.BlockSpec((tm, d), lambda i: (i, 0))],
        out_specs=pl.BlockSpec(memory_space=pl.ANY),
        out_shape=jax.ShapeDtypeStruct((n_rows, d), h2.dtype),
        scratch_shapes=[pltpu.SemaphoreType.DMA((1,))],
        compiler_params=_cparams("arbitrary"),
        name="dispatch",
    )(dest, h2)


def _experts_kernel(be_ref, nv_ref, x_ref, wg_ref, wu_ref, wd_ref, y_ref, wgb, wub, wdb, *, blk):
    i = pl.program_id(0)
    prev_e = be_ref[jnp.maximum(i - 1, 0)]

    @pl.when(jnp.logical_or(i == 0, be_ref[i] != prev_e))
    def _():
        wgb[...] = wg_ref[0].astype(BF16)
        wub[...] = wu_ref[0].astype(BF16)
        wdb[...] = wd_ref[0].astype(BF16)

    @pl.when(nv_ref[i] > 0)
    def _():
        rows = lax.broadcasted_iota(I32, (blk, 1), 0)
        xu = jnp.where(rows < nv_ref[i], x_ref[...], jnp.uint32(0))
        x = jnp.concatenate(_unpack_bf16_pairs(xu), axis=1).astype(BF16)
        act = (_silu(_dot(x, wgb[...])) * _dot(x, wub[...])).astype(BF16)
        y_ref[...] = _pack_bf16_pairs(_dot(act, wdb[...]))


def _experts(xs, be, nvalid, wg, wu, wd):
    n_rows, dh = xs.shape
    ne, d, f = wg.shape
    blk = EXPERT_BLOCK
    nblocks = n_rows // blk
    grid_spec = pltpu.PrefetchScalarGridSpec(
        num_scalar_prefetch=2,
        grid=(nblocks,),
        in_specs=[pl.BlockSpec((blk, dh), lambda i, be, nv: (i, 0)),
                  pl.BlockSpec((1, d, f), lambda i, be, nv: (be[i], 0, 0)),
                  pl.BlockSpec((1, d, f), lambda i, be, nv: (be[i], 0, 0)),
                  pl.BlockSpec((1, f, d), lambda i, be, nv: (be[i], 0, 0))],
        out_specs=pl.BlockSpec((blk, dh), lambda i, be, nv: (i, 0)),
        scratch_shapes=[pltpu.VMEM((d, f), BF16), pltpu.VMEM((d, f), BF16), pltpu.VMEM((f, d), BF16)],
    )
    return pl.pallas_call(
        functools.partial(_experts_kernel, blk=blk),
        grid_spec=grid_spec,
        out_shape=jax.ShapeDtypeStruct((n_rows, dh), U32),
        compiler_params=_cparams("arbitrary"),
        name="experts",
    )(be, nvalid, xs, wg, wu, wd)


def _final_kernel(dst_ref, dstn_ref, wts_ref, h2_ref, x1_ref, g2_ref, wg_ref, wu_ref, wd_ref, lng_ref, lnb_ref,
                  ys_hbm, o_ref, ybuf, sem, *, alpha, topk, tm, nt):
    i = pl.program_id(0)
    slot = lax.rem(i, 2)

    def gather_start(ids_ref, sl):
        def body(t, c):
            for k in range(topk):
                pltpu.make_async_copy(ys_hbm.at[pl.ds(ids_ref[0, k, t], 1)], ybuf.at[sl, k, pl.ds(t, 1)],
                                      sem.at[sl]).start()
            return c
        lax.fori_loop(0, tm, body, 0)

    @pl.when(i == 0)
    def _():
        gather_start(dst_ref, 0)

    for t in range(tm):
        for k in range(topk):
            pltpu.make_async_copy(ys_hbm.at[pl.ds(dstn_ref[0, k, t], 1)], ybuf.at[1 - slot, k, pl.ds(t, 1)],
                                  sem.at[1 - slot]).start()

    h = jnp.concatenate(_unpack_bf16_pairs(h2_ref[...]), axis=1).astype(BF16)
    y = _dot((_silu(_dot(h, wg_ref[...])) * _dot(h, wu_ref[...])).astype(BF16), wd_ref[...])

    def wait_slot(sl):
        def body(t, c):
            for k in range(topk):
                pltpu.make_async_copy(ys_hbm.at[pl.ds(0, 1)], ybuf.at[sl, k, pl.ds(t, 1)], sem.at[sl]).wait()
            return c
        lax.fori_loop(0, tm, body, 0)

    wait_slot(slot)
    w = wts_ref[...]
    ylo = jnp.zeros((tm, y.shape[1] // 2), F32)
    yhi = jnp.zeros((tm, y.shape[1] // 2), F32)
    for k in range(topk):
        lo, hi = _unpack_bf16_pairs(ybuf[slot, k])
        ylo = ylo + w[:, k:k + 1] * lo
        yhi = yhi + w[:, k:k + 1] * hi
    y = y + jnp.concatenate([ylo, yhi], axis=1)
    o_ref[...] = _ln(alpha * x1_ref[...] + g2_ref[0] * y) * lng_ref[...] + lnb_ref[...]

    @pl.when(i == nt - 1)
    def _():
        wait_slot(1 - slot)


def _final(ys, dest, wtsT, h2, x1, gate2, wg, wu, wd, lng, lnb, alpha, tiles_per_batch):
    n, d = x1.shape
    nt, topk, tm = dest.shape
    kern = functools.partial(_final_kernel, alpha=alpha, topk=topk, tm=tm, nt=nt)
    row = pl.BlockSpec((tm, d), lambda i: (i, 0))
    full = lambda a: pl.BlockSpec(a.shape, lambda i: (0,) * a.ndim, pipeline_mode=pl.Buffered(1))
    return pl.pallas_call(
        kern,
        grid=(nt,),
        in_specs=[pl.BlockSpec((1, topk, tm), lambda i: (i, 0, 0), memory_space=pltpu.SMEM),
                  pl.BlockSpec((1, topk, tm), lambda i: (jnp.minimum(i + 1, nt - 1), 0, 0), memory_space=pltpu.SMEM),
                  pl.BlockSpec((tm, topk), lambda i: (i, 0)),
                  pl.BlockSpec((tm, d // 2), lambda i: (i, 0)), row,
                  pl.BlockSpec((1, 1, d), lambda i: (i // tiles_per_batch, 0, 0)),
                  full(wg), full(wu), full(wd),
                  pl.BlockSpec((1, d), lambda i: (0, 0)), pl.BlockSpec((1, d), lambda i: (0, 0)),
                  pl.BlockSpec(memory_space=pl.ANY)],
        out_specs=row,
        out_shape=jax.ShapeDtypeStruct((n, d), F32),
        scratch_shapes=[pltpu.VMEM((2, topk, tm, d // 2), U32), pltpu.SemaphoreType.DMA((2,))],
        compiler_params=_cparams("arbitrary"),
        name="final",
    )(dest, dest, wtsT, h2, x1, gate2, wg, wu, wd, lng.reshape(1, d), lnb.reshape(1, d), ys)


def _proj_layout(d, qr, kvr, idim, ih, hk, hv):
    src = np.cumsum([0, qr, kvr, idim, ih, hk, hk, hv, hv, d, d])
    wa = -(-(qr + kvr + idim + ih) // LANES) * LANES
    wa = max(wa, 1 << (wa - 1).bit_length())
    pieces = [("a", wa, (int(src[0]), int(src[4]))),
              ("hq", hk, (int(src[4]), int(src[5]))), ("hf", hk, (int(src[5]), int(src[6]))),
              ("hi", hv, (int(src[6]), int(src[7]))), ("hg", hv, (int(src[7]), int(src[8]))),
              ("ga", d, (int(src[8]), int(src[9]))), ("gb", d, (int(src[9]), int(src[10])))]
    pieces.sort(key=lambda p: -p[1])
    off = 0
    layout = {}
    for name, wdt, rng in pieces:
        assert off % wdt == 0
        layout[name] = (off, wdt, rng)
        off += wdt
    return layout, off


def kernel(x, c, rpb_table, hgrn_lb_logits, ada_w, ada_b, w_in, q_norm_g, kv_norm_g, w_uq, w_uk, w_uv, w_qidx,
           idx_k_norm_g, idx_k_norm_b, hgrn_out_norm_g, w_branch_a, w_branch_b, w_o, ln1_g, ln1_b, w_router,
           router_bias, w_exp_gate, w_exp_up, w_exp_down, w_sh_gate, w_sh_up, w_sh_down, ln2_g, ln2_b):
    bsz, s, d = x.shape
    depth = ada_w.shape[0]
    qr = w_uq.shape[1]
    ah, kvr, qk = w_uk.shape[1], w_uk.shape[2], w_uk.shape[3]
    idim = idx_k_norm_g.shape[1]
    ih = w_qidx.shape[2] // idim
    dv = hgrn_out_norm_g.shape[1]
    nh = w_branch_b.shape[1] // dv
    dk = hgrn_lb_logits.shape[1] // nh
    ne = w_router.shape[2]
    topk = min(IDX_TOPK, s // 4)
    alpha = float((2 * depth) ** 0.25)
    n_tok = bsz * s

    lower_bounds = jnp.cumsum(jax.nn.softmax(hgrn_lb_logits.astype(F32), axis=0), axis=0)
    layout, wtot = _proj_layout(d, qr, kvr, idim, ih, nh * dk, nh * dv)

    for l in range(depth):
        mod = _ada(c, ada_w[l], ada_b[l])[:, None, :]
        shift1, scale1, gate1, shift2, scale2, gate2 = jnp.split(mod, 6, axis=-1)

        cols = []
        for off, wdt, (lo, hi) in sorted(layout.values()):
            cols.append(w_in[l][:, lo:hi].astype(BF16))
            if wdt > hi - lo:
                cols.append(jnp.zeros((d, wdt - (hi - lo)), BF16))
        proj = _inproj(x, scale1, shift1, jnp.concatenate(cols, axis=1))
        blk = lambda name: layout[name][0] // layout[name][1]

        qb_sz = _pick(s, (256, 128))
        q_latT, q_idxT, w_idxT, k_idx, kv_lat = _prep(
            proj, blk("a"), layout["a"][1], (qr, kvr, idim, ih, ah, qk), q_norm_g[l], kv_norm_g[l],
            idx_k_norm_g[l], idx_k_norm_b[l], jnp.swapaxes(w_uq[l], 0, 1).astype(BF16),
            w_uk[l].astype(BF16), jnp.swapaxes(w_qidx[l], 0, 1).astype(BF16), qb_sz)
        kv_latT = jnp.concatenate([jnp.swapaxes(kv_lat, 1, 2), jnp.ones((bsz, 1, s), BF16),
                                   jnp.zeros((bsz, 7, s), BF16)], axis=1)
        o_a = _dsa(q_idxT, w_idxT, q_latT, k_idx, kv_lat, kv_latT,
                   rpb_table, jnp.swapaxes(w_uv[l], 1, 2).astype(BF16), topk, qb_sz)

        o_b = _hgrn(proj, (blk("hq"), blk("hf"), blk("hi"), blk("hg")), lower_bounds[l],
                    hgrn_out_norm_g[l], nh, dk, dv)

        x1, h2, lgT = _postmix(o_a, o_b, proj, blk("ga"), blk("gb"), x, gate1, scale2, shift2,
                               w_branch_a[l].astype(BF16), w_branch_b[l].astype(BF16), w_o[l].astype(BF16),
                               jnp.swapaxes(w_router[l], 0, 1).astype(BF16), ln1_g[l], ln1_b[l], alpha)

        ids, wts, rnk, sizes_f = _route(lgT, router_bias[l])
        eb = EXPERT_BLOCK
        sizes = sizes_f[:, 0].astype(I32)
        padded = (sizes + eb - 1) // eb * eb
        pend = jnp.cumsum(padded)
        pstart = pend - padded
        n_rows = -(-(n_tok * MOE_TOPK + ne * (eb - 1)) // eb) * eb
        blk_start = jnp.arange(n_rows // eb, dtype=I32) * eb
        blk_expert = jnp.minimum(jnp.sum((pend[None, :] <= blk_start[:, None]).astype(I32), axis=1), ne - 1)
        onehot = (blk_expert[:, None] == jnp.arange(ne, dtype=I32)[None, :]).astype(I32)
        blk_size = jnp.sum(onehot * sizes[None, :], axis=1)
        blk_pstart = jnp.sum(onehot * pstart[None, :], axis=1)
        nvalid = jnp.clip(blk_size - (blk_start - blk_pstart), 0, eb).astype(I32)
        dest = _dest(pstart.astype(I32), ids, rnk, ne)
        tm = _pick(s, (128,))
        dest_t = dest.reshape(bsz, MOE_TOPK, s // tm, tm).transpose(0, 2, 1, 3).reshape(n_tok // tm, MOE_TOPK, tm)
        wtsT = jnp.swapaxes(wts, 1, 2).reshape(n_tok, MOE_TOPK)

        h2f = h2.reshape(n_tok, d // 2)
        xs = _dispatch(dest_t, h2f, n_rows)
        ys = _experts(xs, blk_expert, nvalid, w_exp_gate[l], w_exp_up[l], w_exp_down[l])
        x = _final(ys, dest_t, wtsT, h2f, x1.reshape(n_tok, d), gate2, w_sh_gate[l].astype(BF16),
                   w_sh_up[l].astype(BF16), w_sh_down[l].astype(BF16), ln2_g[l], ln2_b[l], alpha,
                   s // tm).reshape(bsz, s, d)
    return x
```

```python
import functools
import math

import numpy as np
import jax
import jax.numpy as jnp
from jax import lax
from jax.experimental import pallas as pl
from jax.experimental.pallas import tpu as pltpu

F32 = jnp.float32
BF16 = jnp.bfloat16
I32 = jnp.int32
U32 = jnp.uint32

EPS = 1e-6
IDX_TOPK = 256
RPB_MAX_DIST = 128
MOE_TOPK = 8
N_GROUPS = 8
TOPK_GROUPS = 4
ROUTED_SCALE = 2.5
EXPERT_BLOCK = 512

V7X_VMEM_LIMIT_BYTES = 56 * 1024 * 1024
LANES = 128
INT_MIN = -2 ** 31
NEG_BIG = -1e30
LOG2E = math.log2(math.e)


def _cparams(*sem):
    return pltpu.CompilerParams(dimension_semantics=tuple(sem), vmem_limit_bytes=V7X_VMEM_LIMIT_BYTES)


def _pick(n, prefs):
    for p in prefs:
        if n % p == 0:
            return p
    return n


def _sigmoid(v):
    return 1.0 / (1.0 + jnp.exp(-v))


def _silu(v):
    return v * _sigmoid(v)


def _ln(v):
    mu = jnp.mean(v, axis=-1, keepdims=True)
    d = v - mu
    var = jnp.mean(d * d, axis=-1, keepdims=True)
    return d * lax.rsqrt(var + EPS)


def _rms(v):
    return v * lax.rsqrt(jnp.mean(v * v, axis=-1, keepdims=True) + EPS)


def _dot(a, b):
    return jnp.dot(a, b, preferred_element_type=F32)


def _pack_bf16_pairs(v):
    n = v.shape[1] // 2
    lo = pltpu.bitcast(v[:, :n].astype(BF16).astype(F32), U32) >> 16
    hi = pltpu.bitcast(v[:, n:].astype(BF16).astype(F32), U32) & jnp.uint32(0xFFFF0000)
    return lo | hi


def _unpack_bf16_pairs(u):
    return pltpu.bitcast(u << 16, F32), pltpu.bitcast(u & jnp.uint32(0xFFFF0000), F32)


def _dot_nt(a, b):
    return lax.dot_general(a, b, (((1,), (1,)), ((), ())), preferred_element_type=F32)


def _dot_tn(a, b):
    return lax.dot_general(a, b, (((0,), (0,)), ((), ())), preferred_element_type=F32)


def _ada_kernel(c_ref, w_ref, b_ref, o_ref):
    c = c_ref[...]
    o_ref[...] = _dot(_silu(c).astype(BF16), w_ref[...].astype(BF16)) + b_ref[...]


def _ada(c, w, b):
    bsz, d = c.shape
    n = w.shape[1]
    rows = 8
    cp = jnp.zeros((rows, d), F32).at[:bsz].set(c)
    tn = _pick(n, (1024, 512, 256, 128))
    out = pl.pallas_call(
        _ada_kernel,
        grid=(n // tn,),
        in_specs=[pl.BlockSpec((rows, d), lambda j: (0, 0)),
                  pl.BlockSpec((d, tn), lambda j: (0, j)),
                  pl.BlockSpec((1, tn), lambda j: (0, j))],
        out_specs=pl.BlockSpec((rows, tn), lambda j: (0, j)),
        out_shape=jax.ShapeDtypeStruct((rows, n), F32),
        compiler_params=_cparams("arbitrary"),
        name="ada",
    )(cp, w, b.reshape(1, n))
    return out[:bsz]


def _inproj_kernel(x_ref, sc_ref, sh_ref, w_ref, o_ref, h_ref):
    @pl.when(pl.program_id(2) == 0)
    def _():
        h = _ln(x_ref[0]) * (1.0 + sc_ref[0]) + sh_ref[0]
        h_ref[...] = h.astype(BF16)

    o_ref[0] = _dot(h_ref[...], w_ref[...]).astype(o_ref.dtype)


def _inproj(x, scale, shift, w):
    bsz, s, d = x.shape
    n = w.shape[1]
    tm = _pick(s, (1024, 512, 256, 128))
    tn = _pick(n, (1024, 512, 256, 128))
    return pl.pallas_call(
        _inproj_kernel,
        grid=(bsz, s // tm, n // tn),
        in_specs=[pl.BlockSpec((1, tm, d), lambda b, i, j: (b, i, 0)),
                  pl.BlockSpec((1, 1, d), lambda b, i, j: (b, 0, 0)),
                  pl.BlockSpec((1, 1, d), lambda b, i, j: (b, 0, 0)),
                  pl.BlockSpec((d, tn), lambda b, i, j: (0, j))],
        out_specs=pl.BlockSpec((1, tm, tn), lambda b, i, j: (b, i, j)),
        out_shape=jax.ShapeDtypeStruct((bsz, s, n), BF16),
        scratch_shapes=[pltpu.VMEM((tm, d), BF16)],
        compiler_params=_cparams("arbitrary", "arbitrary", "arbitrary"),
        name="inproj",
    )(x, scale, shift, w)


def _prep_kernel(a_ref, qg_ref, kvg_ref, ikg_ref, ikb_ref, wuqT_ref, wuk_ref, wqiT_ref,
                 qlatT_ref, qidxT_ref, widxT_ref, kidx_ref, kv_ref, *, qr, kvr, idim, ih, ah, qk):
    a = a_ref[0]
    af = a.astype(F32)
    cq = (_rms(af[:, :qr]) * qg_ref[...]).astype(BF16)
    ckv = af[:, qr:qr + kvr]
    ki = af[:, qr + kvr:qr + kvr + idim]
    kv_ref[0] = (_rms(ckv) * kvg_ref[...]).astype(BF16)
    kidx_ref[0] = (_ln(ki) * ikg_ref[...] + ikb_ref[...]).astype(BF16)
    tail = a[:, qr + kvr:qr + kvr + LANES]
    eye = (lax.broadcasted_iota(I32, (LANES, LANES), 0) == lax.broadcasted_iota(I32, (LANES, LANES), 1))
    tailT = _dot_nt(jnp.where(eye, 1.0, 0.0).astype(BF16), tail)
    widxT_ref[0] = tailT[idim:idim + ih] * float((ih * idim) ** -0.5)
    tm = a.shape[0]
    qT = _dot_nt(wuqT_ref[...], cq)
    for h in range(ah):
        qh = qT[h * qk:(h + 1) * qk].astype(BF16)
        qlatT_ref[0, 0, :, h * tm:(h + 1) * tm] = (_dot(wuk_ref[h], qh) * float(qk ** -0.5 * LOG2E)).astype(BF16)
    qiT = _dot_nt(wqiT_ref[...], cq)
    for h in range(ih):
        qidxT_ref[0, 0, :, h * tm:(h + 1) * tm] = qiT[h * idim:(h + 1) * idim].astype(BF16)


def _prep(proj, a_blk, wa, dims, q_norm_g, kv_norm_g, ikg, ikb, w_uqT, w_uk, w_qidxT, tm):
    bsz, s, _ = proj.shape
    qr, kvr, idim, ih, ah, qk = dims
    assert idim + ih <= LANES and qr + kvr + LANES <= wa
    kern = functools.partial(_prep_kernel, qr=qr, kvr=kvr, idim=idim, ih=ih, ah=ah, qk=qk)
    full = lambda shape: pl.BlockSpec(shape, lambda b, i: (0,) * len(shape))
    return pl.pallas_call(
        kern,
        grid=(bsz, s // tm),
        in_specs=[pl.BlockSpec((1, tm, wa), lambda b, i: (b, i, a_blk)),
                  full((1, qr)), full((1, kvr)), full((1, idim)), full((1, idim)),
                  full(w_uqT.shape), full(w_uk.shape), full(w_qidxT.shape)],
        out_specs=[pl.BlockSpec((1, 1, kvr, ah * tm), lambda b, i: (b, i, 0, 0)),
                   pl.BlockSpec((1, 1, idim, ih * tm), lambda b, i: (b, i, 0, 0)),
                   pl.BlockSpec((1, ih, tm), lambda b, i: (b, 0, i)),
                   pl.BlockSpec((1, tm, idim), lambda b, i: (b, i, 0)),
                   pl.BlockSpec((1, tm, kvr), lambda b, i: (b, i, 0))],
        out_shape=[jax.ShapeDtypeStruct((bsz, s // tm, kvr, ah * tm), BF16),
                   jax.ShapeDtypeStruct((bsz, s // tm, idim, ih * tm), BF16),
                   jax.ShapeDtypeStruct((bsz, ih, s), F32),
                   jax.ShapeDtypeStruct((bsz, s, idim), BF16),
                   jax.ShapeDtypeStruct((bsz, s, kvr), BF16)],
        compiler_params=_cparams("arbitrary", "arbitrary"),
        name="prep",
    )(proj, q_norm_g.reshape(1, qr), kv_norm_g.reshape(1, kvr), ikg.reshape(1, idim), ikb.reshape(1, idim),
      w_uqT, w_uk, w_qidxT)


def _dsa_kernel(qidxT_ref, widxT_ref, qlatT_ref, kidx_ref, kv_ref, kvT_ref, bkt0_ref, bkt1_ref, rel_ref, wuvT_ref,
                o_ref, keys_ref, gmax_ref, s_ref, p_ref, m_ref, a_ref, acc_ref, t0_ref, t1_ref, lstrict_ref, taken_ref,
                *, qb_sz, topk, ih, ah, c, nb):
    QB = qb_sz
    qb = pl.program_id(1)
    nchunks = qb + 1
    qpos = qb * QB + lax.broadcasted_iota(I32, (1, QB), 1)
    kpos = lax.broadcasted_iota(I32, (QB, 1), 0)
    wT = widxT_ref[0]
    hcols = lambda h: slice(h * QB, (h + 1) * QB)

    @pl.when(jnp.logical_and(pl.program_id(0) == 0, qb == 0))
    def _():
        for bkt_ref, t_ref in ((bkt0_ref, t0_ref), (bkt1_ref, t1_ref)):
            bkt = bkt_ref[...]
            for h in range(ah):
                tile = jnp.zeros((QB, QB), F32)
                for b in range(nb - 1):
                    tile = jnp.where(bkt == b, rel_ref[b, h], tile)
                t_ref[:, hcols(h)] = tile
        lstrict_ref[...] = jnp.where(lax.broadcasted_iota(I32, (QB, QB), 1) < lax.broadcasted_iota(I32, (QB, QB), 0),
                                     1.0, 0.0).astype(BF16)

    def score_chunk(kc, carry):
        off = pl.multiple_of(kc * QB, QB)
        s_ref[:, :ih * QB] = _dot(kidx_ref[0, pl.ds(off, QB), :], qidxT_ref[0, 0])
        acc = jnp.zeros((QB, QB), F32)
        for h in range(ih):
            acc = acc + wT[h:h + 1] * jnp.maximum(s_ref[:, hcols(h)], 0.0)
        bits = pltpu.bitcast(acc, I32)
        skey = bits ^ ((bits >> 31) & 0x7FFFFFFF)
        causal = (off + kpos) <= qpos
        skey = jnp.where(causal, skey, INT_MIN)
        keys_ref[pl.ds(off, QB), :] = skey
        gmax_ref[...] = jnp.maximum(gmax_ref[...], skey)
        return carry

    gmax_ref[...] = jnp.full(gmax_ref.shape, INT_MIN, I32)
    lax.fori_loop(0, nchunks, score_chunk, 0)

    gmax = gmax_ref[...]
    lo0 = jnp.min(gmax, axis=0, keepdims=True) if QB >= topk else jnp.full((1, QB), INT_MIN, I32)
    hi0 = jnp.max(gmax, axis=0, keepdims=True) + 1

    def count_ge(cand):
        def body(kc, cnt):
            off = pl.multiple_of(kc * QB, QB)
            hit = jnp.where(keys_ref[pl.ds(off, QB), :] >= cand, 1, 0)
            return cnt + jnp.sum(hit.reshape(QB // 8, 8, QB), axis=0)

        cnt = lax.fori_loop(0, nchunks, body, jnp.zeros((8, QB), I32))
        return jnp.sum(cnt.astype(F32), axis=0, keepdims=True)

    def bis_cond(st):
        it, _, _, done = st
        return jnp.logical_and(it < 34, jnp.min(done) < 0.5)

    def bis_body(st):
        it, lo, hi, done = st
        cand = (lo >> 1) + (hi >> 1) + (lo & hi & 1)
        cnt = count_ge(cand)
        ge = cnt >= float(topk)
        conv = cand == lo
        fin = jnp.logical_or(conv, cnt == float(topk))
        thr_new = jnp.where(conv, lo, cand)
        lo = jnp.where(fin, thr_new, jnp.where(ge, cand, lo))
        hi = jnp.where(fin, thr_new + 1, jnp.where(ge, hi, cand))
        return it + 1, lo, hi, jnp.where(fin, 1.0, done)

    _, thr, _, _ = lax.while_loop(bis_cond, bis_body, (jnp.int32(0), lo0, hi0, jnp.zeros((1, QB), F32)))
    need = float(topk) - count_ge(thr + 1)

    m_ref[...] = jnp.full(m_ref.shape, NEG_BIG, F32)
    acc_ref[...] = jnp.zeros(acc_ref.shape, F32)

    def attn_chunk(off, bias_ref, diag, taken):
        kt = keys_ref[pl.ds(off, QB), :]
        eq = kt == thr
        eq01 = jnp.where(eq, 1.0, 0.0)
        before = _dot(lstrict_ref[...], eq01.astype(BF16)) + taken
        sel = jnp.logical_or(kt > thr, jnp.logical_and(eq, before < need))
        taken = taken + jnp.sum(eq01, axis=0, keepdims=True)
        if diag:
            sel = jnp.logical_and(sel, (off + kpos) <= qpos)
        madd = jnp.where(sel, 0.0, NEG_BIG)
        s_ref[:, :ah * QB] = _dot(kv_ref[0, pl.ds(off, QB), :], qlatT_ref[0, 0])
        for g in range(ah * QB // LANES):
            cols = slice(g * LANES, (g + 1) * LANES)
            qcols = slice(g * LANES % QB, g * LANES % QB + LANES)
            s = s_ref[:, cols] + madd[:, qcols]
            if bias_ref is not None:
                s = s + bias_ref[:, cols]
            m_prev = m_ref[:, cols]
            m_new = jnp.maximum(m_prev, jnp.max(s, axis=0, keepdims=True))
            p_ref[:, cols] = jnp.exp2(s - m_new).astype(BF16)
            a_ref[:, cols] = jnp.exp2(m_prev - m_new)
            m_ref[:, cols] = m_new
        acc_ref[...] = a_ref[...] * acc_ref[...] + _dot(kvT_ref[0, :, pl.ds(off, QB)], p_ref[...])
        return taken

    def far_body(kc, taken):
        return attn_chunk(pl.multiple_of(kc * QB, QB), None, False, taken)

    taken_ref[...] = lax.fori_loop(0, qb - 1, far_body, jnp.zeros((1, QB), F32))

    @pl.when(qb >= 1)
    def _():
        taken_ref[...] = attn_chunk(pl.multiple_of((qb - 1) * QB, QB), t1_ref, False, taken_ref[...])

    attn_chunk(pl.multiple_of(qb * QB, QB), t0_ref, True, taken_ref[...])

    outs = []
    for h in range(ah):
        o = (acc_ref[:c, hcols(h)] / acc_ref[c:c + 1, hcols(h)]).astype(BF16)
        outs.append(_dot(wuvT_ref[h], o))
    o_ref[0] = jnp.concatenate(outs, axis=0).T.astype(o_ref.dtype)


def _t5_bucket_np(d, nbuckets):
    max_exact = nbuckets // 2
    dd = np.maximum(d, 1).astype(np.float32)
    large = max_exact + (np.log(dd / np.float32(max_exact)) / np.float32(math.log(RPB_MAX_DIST / max_exact))
                         * np.float32(nbuckets - max_exact)).astype(np.int32)
    large = np.minimum(large, nbuckets - 1)
    return np.where(d < max_exact, d, large).astype(np.int32)


def _dsa(q_idxT, w_idxT, q_latT, k_idx, kv_lat, kv_latT, rpb_table, w_uvT, topk, QB):
    bsz, nqb, idim, ihq = q_idxT.shape
    c, ahq = q_latT.shape[2], q_latT.shape[3]
    ca = kv_latT.shape[1]
    ih, ah = ihq // QB, ahq // QB
    s = nqb * QB
    vd = w_uvT.shape[1]
    nb = rpb_table.shape[0]
    assert QB >= RPB_MAX_DIST
    j = np.arange(QB)[:, None]
    i = np.arange(QB)[None, :]
    bkt0 = jnp.asarray(_t5_bucket_np(np.maximum(i - j, 0), nb))
    bkt1 = jnp.asarray(_t5_bucket_np(QB + i - j, nb))
    assert int(_t5_bucket_np(np.array([RPB_MAX_DIST]), nb)[0]) == nb - 1
    rel = (rpb_table.astype(F32) - rpb_table[nb - 1].astype(F32)[None, :]) * LOG2E
    kern = functools.partial(_dsa_kernel, qb_sz=QB, topk=topk, ih=ih, ah=ah, c=c, nb=nb)
    const = lambda shape: pl.BlockSpec(shape, lambda b, i: (0,) * len(shape), pipeline_mode=pl.Buffered(1))
    hw = max(ih, ah) * QB
    return pl.pallas_call(
        kern,
        grid=(bsz, nqb),
        in_specs=[pl.BlockSpec((1, 1, idim, ih * QB), lambda b, i: (b, i, 0, 0)),
                  pl.BlockSpec((1, ih, QB), lambda b, i: (b, 0, i)),
                  pl.BlockSpec((1, 1, c, ah * QB), lambda b, i: (b, i, 0, 0)),
                  pl.BlockSpec((1, s, idim), lambda b, i: (b, 0, 0)),
                  pl.BlockSpec((1, s, c), lambda b, i: (b, 0, 0)),
                  pl.BlockSpec((1, ca, s), lambda b, i: (b, 0, 0)),
                  const((QB, QB)), const((QB, QB)),
                  pl.BlockSpec(memory_space=pltpu.SMEM), const(w_uvT.shape)],
        out_specs=pl.BlockSpec((1, QB, ah * vd), lambda b, i: (b, i, 0)),
        out_shape=jax.ShapeDtypeStruct((bsz, s, ah * vd), BF16),
        scratch_shapes=[pltpu.VMEM((s, QB), I32),
                        pltpu.VMEM((QB, QB), I32),
                        pltpu.VMEM((QB, hw), F32),
                        pltpu.VMEM((QB, ah * QB), BF16),
                        pltpu.VMEM((1, ah * QB), F32),
                        pltpu.VMEM((1, ah * QB), F32),
                        pltpu.VMEM((ca, ah * QB), F32),
                        pltpu.VMEM((QB, ah * QB), F32),
                        pltpu.VMEM((QB, ah * QB), F32),
                        pltpu.VMEM((QB, QB), BF16),
                        pltpu.VMEM((1, QB), F32)],
        compiler_params=_cparams("arbitrary", "arbitrary"),
        name="dsa",
    )(q_idxT, w_idxT, q_latT, k_idx, kv_lat, kv_latT, bkt0, bkt1, rel, w_uvT)


HGRN_CHUNK = 64
HGRN_SUB = 16
HGRN_EXP_CLAMP = 80.0


def _dot_exact(a, b, dims):
    return lax.dot_general(a, b, (dims, ((), ())), precision=lax.Precision.HIGHEST, preferred_element_type=F32)


def _hgrn_kernel(hq_ref, hf_ref, hi_ref, hg_ref, lb_ref, g_ref, o_ref, st_ref, st0_ref, *, nh, dk, dv, tt):
    C, SUB = HGRN_CHUNK, HGRN_SUB

    @pl.when(pl.program_id(1) == 0)
    def _():
        st_ref[...] = jnp.zeros(st_ref.shape, F32)

    r = lax.broadcasted_iota(I32, (C, C), 0)
    cc = lax.broadcasted_iota(I32, (C, C), 1)
    tri_mask = r >= cc
    tri = jnp.where(tri_mask, 1.0, 0.0).astype(BF16)
    g = g_ref[...]
    st0_ref[...] = st_ref[...]
    decay = jnp.zeros((1, nh * dk), F32)
    lb = lb_ref[...]

    for c in range(tt // C):
        rows = pl.ds(c * C, C)
        f = lb + (1.0 - lb) * _sigmoid(hf_ref[0, rows, :].astype(F32))
        lf = jnp.log(f)
        t1 = lf.astype(BF16)
        r1 = lf - t1.astype(F32)
        t2 = r1.astype(BF16)
        t3 = (r1 - t2.astype(F32)).astype(BF16)
        b = _dot(tri, t1) + _dot(tri, t2) + _dot(tri, t3)
        kk = 1.0 - f
        hq = hq_ref[0, rows, :].astype(F32)
        q = _silu(hq) * float(dk ** -0.5)
        qe = (q * jnp.exp(b)).astype(BF16)
        b_last = b[C - 1:C]
        k_dec = (kk * jnp.exp(b_last - b)).astype(BF16)
        dec_last = jnp.exp(b_last)
        qs, ks = [], []
        for i in range(C // SUB):
            lo, n = i * SUB, (i + 1) * SUB
            bi = b[lo - 1:lo] if i > 0 else jnp.zeros((1, nh * dk), F32)
            decay = jnp.maximum(decay, bi - b[n - 1:n])
            qs.append((q[lo:n] * jnp.exp(b[lo:n] - bi)).astype(BF16))
            ks.append((kk * jnp.exp(jnp.minimum(bi - b, HGRN_EXP_CLAMP))).astype(BF16))
        v_all = hi_ref[0, rows, :]
        outs = []
        for h in range(nh):
            kc = slice(h * dk, (h + 1) * dk)
            v = v_all[:, h * dv:(h + 1) * dv]
            stT = st_ref[h]
            att = jnp.concatenate([_dot_nt(qs[i][:, kc], ks[i][:, kc]) for i in range(C // SUB)], axis=0)
            att = jnp.where(tri_mask, att, 0.0)
            o = _dot_nt(qe[:, kc], stT.astype(BF16)) + _dot(att.astype(BF16), v)
            st_ref[h] = stT * dec_last[:, kc] + _dot_tn(v, k_dec[:, kc])
            outs.append(_rms(o) * g)
        o_all = jnp.concatenate(outs, axis=1) * _silu(hg_ref[0, rows, :].astype(F32))
        o_ref[0, rows, :] = o_all.astype(o_ref.dtype)

    @pl.when(jnp.max(decay) > HGRN_EXP_CLAMP)
    def _():
        row = lax.broadcasted_iota(I32, (SUB, 1), 0)

        def head(h, carry):
            ko = pl.multiple_of(h * dk, dk)
            vo = pl.multiple_of(h * dv, dv)
            lb = lb_ref[:, pl.ds(ko, dk)]

            def slab(j, S):
                rows = pl.ds(pl.multiple_of(j * SUB, SUB), SUB)
                f = lb + (1.0 - lb) * _sigmoid(hf_ref[0, rows, pl.ds(ko, dk)].astype(F32))
                kk = 1.0 - f
                hq = hq_ref[0, rows, pl.ds(ko, dk)].astype(F32)
                q = _silu(hq) * float(dk ** -0.5)
                v = hi_ref[0, rows, pl.ds(vo, dv)].astype(F32)
                o = jnp.zeros((SUB, dv), F32)
                for r in range(SUB):
                    S = S * f[r:r + 1] + _dot_exact(jnp.where(row == r, v, 0.0), kk, ((0,), (0,)))
                    o = jnp.where(row == r, _dot_exact(q, S, ((1,), (1,))), o)
                hg = hg_ref[0, rows, pl.ds(vo, dv)].astype(F32)
                o_ref[0, rows, pl.ds(vo, dv)] = (_rms(o) * g * _silu(hg)).astype(o_ref.dtype)
                return S

            st_ref[h] = lax.fori_loop(0, tt // SUB, slab, st0_ref[h])
            return carry

        lax.fori_loop(0, nh, head, 0)


def _hgrn(proj, blks, lb, g, nh, dk, dv):
    bsz, s, _ = proj.shape
    tt = _pick(s, (256, 128, 64))
    kern = functools.partial(_hgrn_kernel, nh=nh, dk=dk, dv=dv, tt=tt)
    col = lambda blk, wdt: pl.BlockSpec((1, tt, wdt), lambda b, i: (b, i, blk))
    return pl.pallas_call(
        kern,
        grid=(bsz, s // tt),
        in_specs=[col(blks[0], nh * dk), col(blks[1], nh * dk), col(blks[2], nh * dv), col(blks[3], nh * dv),
                  pl.BlockSpec((1, nh * dk), lambda b, i: (0, 0)),
                  pl.BlockSpec((1, dv), lambda b, i: (0, 0))],
        out_specs=pl.BlockSpec((1, tt, nh * dv), lambda b, i: (b, i, 0)),
        out_shape=jax.ShapeDtypeStruct((bsz, s, nh * dv), BF16),
        scratch_shapes=[pltpu.VMEM((nh, dv, dk), F32), pltpu.VMEM((nh, dv, dk), F32)],
        compiler_params=_cparams("arbitrary", "arbitrary"),
        name="hgrn",
    )(proj, proj, proj, proj, lb.reshape(1, nh * dk), g.reshape(1, dv))


def _postmix_kernel(oa_ref, ob_ref, ga_ref, gb_ref, x_ref, g1_ref, sc2_ref, sh2_ref,
                    wa_ref, wb_ref, wo_ref, wrT_ref, lng_ref, lnb_ref,
                    x1_ref, h2_ref, lgT_ref, *, alpha):
    ya = _dot(oa_ref[0], wa_ref[...])
    yb = _dot(ob_ref[0], wb_ref[...])
    mix = _sigmoid(ga_ref[0].astype(F32)) * ya + _sigmoid(gb_ref[0].astype(F32)) * yb
    mixed = _dot(mix.astype(BF16), wo_ref[...])
    x1 = _ln(alpha * x_ref[0] + g1_ref[0] * mixed) * lng_ref[...] + lnb_ref[...]
    x1_ref[0] = x1
    h2 = _ln(x1) * (1.0 + sc2_ref[0]) + sh2_ref[0]
    h2_ref[0] = _pack_bf16_pairs(h2)
    lgT_ref[0] = _dot_nt(wrT_ref[...], h2.astype(BF16))


def _postmix(o_a, o_b, proj, ga_blk, gb_blk, x, gate1, scale2, shift2, wa, wb, wo, wrT, lng, lnb, alpha):
    bsz, s, d = x.shape
    ne = wrT.shape[0]
    tm = _pick(s, (256, 128))
    kern = functools.partial(_postmix_kernel, alpha=alpha)
    row = lambda wdt: pl.BlockSpec((1, tm, wdt), lambda b, i: (b, i, 0))
    vec = pl.BlockSpec((1, 1, d), lambda b, i: (b, 0, 0))
    full = lambda a: pl.BlockSpec(a.shape, lambda b, i: (0,) * a.ndim, pipeline_mode=pl.Buffered(1))
    return pl.pallas_call(
        kern,
        grid=(bsz, s // tm),
        in_specs=[row(o_a.shape[2]), row(o_b.shape[2]),
                  pl.BlockSpec((1, tm, d), lambda b, i: (b, i, ga_blk)),
                  pl.BlockSpec((1, tm, d), lambda b, i: (b, i, gb_blk)),
                  row(d), vec, vec, vec,
                  full(wa), full(wb), full(wo), full(wrT),
                  pl.BlockSpec((1, d), lambda b, i: (0, 0)), pl.BlockSpec((1, d), lambda b, i: (0, 0))],
        out_specs=[row(d), row(d // 2), pl.BlockSpec((1, ne, tm), lambda b, i: (b, 0, i))],
        out_shape=[jax.ShapeDtypeStruct((bsz, s, d), F32),
                   jax.ShapeDtypeStruct((bsz, s, d // 2), U32),
                   jax.ShapeDtypeStruct((bsz, ne, s), F32)],
        compiler_params=_cparams("arbitrary", "arbitrary"),
        name="postmix",
    )(o_a, o_b, proj, proj, x, gate1, scale2, shift2, wa, wb, wo, wrT, lng.reshape(1, d), lnb.reshape(1, d))


def _route_kernel(lg_ref, bias_ref, ids_ref, wts_ref, rnk_ref, sizes_ref, upper_ref, carry_ref, *, ne):
    first = jnp.logical_and(pl.program_id(0) == 0, pl.program_id(1) == 0)
    tn = lg_ref.shape[2]

    @pl.when(first)
    def _():
        carry_ref[...] = jnp.zeros(carry_ref.shape, F32)
        r_ = lax.broadcasted_iota(I32, (tn, tn), 0)
        c_ = lax.broadcasted_iota(I32, (tn, tn), 1)
        upper_ref[...] = jnp.where(r_ < c_, 1.0, 0.0).astype(BF16)

    per = ne // N_GROUPS
    s = _sigmoid(lg_ref[0])
    bz = s + bias_ref[...]
    ridx = lax.broadcasted_iota(I32, (per, tn), 0)
    neg_inf = jnp.float32(-jnp.inf)
    gs = []
    for g in range(N_GROUPS):
        blk = bz[g * per:(g + 1) * per]
        m1 = jnp.max(blk, axis=0, keepdims=True)
        first_hit = jnp.min(jnp.where(blk == m1, ridx, per), axis=0, keepdims=True)
        m2 = jnp.max(jnp.where(ridx == first_hit, neg_inf, blk), axis=0, keepdims=True)
        gs.append(m1 + m2)
    emask_rows = []
    for g in range(N_GROUPS):
        rank = jnp.zeros((1, tn), I32)
        for g2 in range(N_GROUPS):
            if g2 == g:
                continue
            beats = (gs[g2] > gs[g]) if g2 > g else (gs[g2] >= gs[g])
            rank = rank + jnp.where(beats, 1, 0)
        emask_rows.append(jnp.broadcast_to(rank < TOPK_GROUPS, (per, tn)))
    emask = jnp.concatenate(emask_rows, axis=0)
    masked = jnp.where(emask, bz, neg_inf)
    eidx = lax.broadcasted_iota(I32, (ne, tn), 0)
    rank = jnp.zeros((ne, tn), I32)
    for e2 in range(ne):
        row = masked[e2:e2 + 1]
        beats = jnp.logical_or(row > masked, jnp.logical_and(row == masked, e2 < eidx))
        rank = rank + jnp.where(beats, 1, 0)
    sel = rank < MOE_TOPK
    sel01 = jnp.where(sel, 1.0, 0.0)
    denom = jnp.sum(jnp.where(sel, s, 0.0), axis=0, keepdims=True)
    wn = s / denom * ROUTED_SCALE
    before = _dot(sel01.astype(BF16), upper_ref[...]) + carry_ref[:, 0:1]
    ids, wts, rnk = [], [], []
    for k in range(MOE_TOPK):
        hit = rank == k
        ids.append(jnp.sum(jnp.where(hit, eidx, 0), axis=0, keepdims=True))
        wts.append(jnp.sum(jnp.where(hit, wn, 0.0), axis=0, keepdims=True))
        rnk.append(jnp.sum(jnp.where(hit, before, 0.0), axis=0, keepdims=True))
    ids_ref[0] = jnp.concatenate(ids, axis=0)
    wts_ref[0] = jnp.concatenate(wts, axis=0)
    rnk_ref[0] = jnp.concatenate(rnk, axis=0).astype(I32)
    carry_ref[...] = carry_ref[...] + jnp.sum(sel01, axis=1, keepdims=True)
    sizes_ref[...] = carry_ref[...]


def _route(lgT, bias):
    bsz, ne, s = lgT.shape
    tn = _pick(s, (1024, 512, 256, 128))
    kern = functools.partial(_route_kernel, ne=ne)
    slot = pl.BlockSpec((1, MOE_TOPK, tn), lambda b, j: (b, 0, j))
    return pl.pallas_call(
        kern,
        grid=(bsz, s // tn),
        in_specs=[pl.BlockSpec((1, ne, tn), lambda b, j: (b, 0, j)),
                  pl.BlockSpec((ne, 1), lambda b, j: (0, 0))],
        out_specs=[slot, slot, slot, pl.BlockSpec((ne, LANES), lambda b, j: (0, 0))],
        out_shape=[jax.ShapeDtypeStruct((bsz, MOE_TOPK, s), I32),
                   jax.ShapeDtypeStruct((bsz, MOE_TOPK, s), F32),
                   jax.ShapeDtypeStruct((bsz, MOE_TOPK, s), I32),
                   jax.ShapeDtypeStruct((ne, LANES), F32)],
        scratch_shapes=[pltpu.VMEM((tn, tn), BF16), pltpu.VMEM((ne, LANES), F32)],
        compiler_params=_cparams("arbitrary", "arbitrary"),
        name="route",
    )(lgT, bias.reshape(ne, 1))


def _dest_kernel(pstart_ref, ids_ref, rnk_ref, o_ref, *, ne):
    ids = ids_ref[0]
    base = jnp.zeros(ids.shape, I32)
    for e in range(ne):
        base = jnp.where(ids == e, pstart_ref[e], base)
    o_ref[0] = base + rnk_ref[0]


def _dest(pstart, ids, rnk, ne):
    bsz, k, s = ids.shape
    tn = _pick(s, (2048, 1024, 512, 256, 128))
    blk = lambda: pl.BlockSpec((1, k, tn), lambda b, j, ps: (b, 0, j))
    return pl.pallas_call(
        functools.partial(_dest_kernel, ne=ne),
        grid_spec=pltpu.PrefetchScalarGridSpec(num_scalar_prefetch=1, grid=(bsz, s // tn),
                                               in_specs=[blk(), blk()], out_specs=blk()),
        out_shape=jax.ShapeDtypeStruct((bsz, k, s), I32),
        compiler_params=_cparams("arbitrary", "arbitrary"),
        name="dest",
    )(pstart, ids, rnk)


def _dispatch_kernel(dst_ref, h_ref, xs_hbm, sem, *, tm, topk):
    def start(t, c):
        for k in range(topk):
            pltpu.make_async_copy(h_ref.at[pl.ds(t, 1)], xs_hbm.at[pl.ds(dst_ref[0, k, t], 1)], sem.at[0]).start()
        return c

    lax.fori_loop(0, tm, start, 0)

    def wait(t, c):
        for k in range(topk):
            pltpu.make_async_copy(h_ref.at[pl.ds(t, 1)], xs_hbm.at[pl.ds(0, 1)], sem.at[0]).wait()
        return c

    lax.fori_loop(0, tm, wait, 0)


def _dispatch(dest, h2, n_rows):
    n, d = h2.shape
    nt, topk, tm = dest.shape
    return pl.pallas_call(
        functools.partial(_dispatch_kernel, tm=tm, topk=topk),
        grid=(nt,),
        in_specs=[pl.BlockSpec((1, topk, tm), lambda i: (i, 0, 0), memory_space=pltpu.SMEM),
                  pl.BlockSpec((tm, d), lambda i: (i, 0))],
        out_specs=pl.BlockSpec(memory_space=pl.ANY),
        out_shape=jax.ShapeDtypeStruct((n_rows, d), h2.dtype),
        scratch_shapes=[pltpu.SemaphoreType.DMA((1,))],
        compiler_params=_cparams("arbitrary"),
        name="dispatch",
    )(dest, h2)


def _experts_kernel(be_ref, nv_ref, x_ref, wg_ref, wu_ref, wd_ref, y_ref, wgb, wub, wdb, *, blk):
    i = pl.program_id(0)
    prev_e = be_ref[jnp.maximum(i - 1, 0)]

    @pl.when(jnp.logical_or(i == 0, be_ref[i] != prev_e))
    def _():
        wgb[...] = wg_ref[0].astype(BF16)
        wub[...] = wu_ref[0].astype(BF16)
        wdb[...] = wd_ref[0].astype(BF16)

    @pl.when(nv_ref[i] > 0)
    def _():
        rows = lax.broadcasted_iota(I32, (blk, 1), 0)
        xu = jnp.where(rows < nv_ref[i], x_ref[...], jnp.uint32(0))
        x = jnp.concatenate(_unpack_bf16_pairs(xu), axis=1).astype(BF16)
        act = (_silu(_dot(x, wgb[...])) * _dot(x, wub[...])).astype(BF16)
        y_ref[...] = _pack_bf16_pairs(_dot(act, wdb[...]))


def _experts(xs, be, nvalid, wg, wu, wd):
    n_rows, dh = xs.shape
    ne, d, f = wg.shape
    blk = EXPERT_BLOCK
    nblocks = n_rows // blk
    grid_spec = pltpu.PrefetchScalarGridSpec(
        num_scalar_prefetch=2,
        grid=(nblocks,),
        in_specs=[pl.BlockSpec((blk, dh), lambda i, be, nv: (i, 0)),
                  pl.BlockSpec((1, d, f), lambda i, be, nv: (be[i], 0, 0)),
                  pl.BlockSpec((1, d, f), lambda i, be, nv: (be[i], 0, 0)),
                  pl.BlockSpec((1, f, d), lambda i, be, nv: (be[i], 0, 0))],
        out_specs=pl.BlockSpec((blk, dh), lambda i, be, nv: (i, 0)),
        scratch_shapes=[pltpu.VMEM((d, f), BF16), pltpu.VMEM((d, f), BF16), pltpu.VMEM((f, d), BF16)],
    )
    return pl.pallas_call(
        functools.partial(_experts_kernel, blk=blk),
        grid_spec=grid_spec,
        out_shape=jax.ShapeDtypeStruct((n_rows, dh), U32),
        compiler_params=_cparams("arbitrary"),
        name="experts",
    )(be, nvalid, xs, wg, wu, wd)


def _final_kernel(dst_ref, dstn_ref, wts_ref, h2_ref, x1_ref, g2_ref, wg_ref, wu_ref, wd_ref, lng_ref, lnb_ref,
                  ys_hbm, o_ref, ybuf, sem, *, alpha, topk, tm, nt):
    i = pl.program_id(0)
    slot = lax.rem(i, 2)

    def gather_start(ids_ref, sl):
        def body(t, c):
            for k in range(topk):
                pltpu.make_async_copy(ys_hbm.at[pl.ds(ids_ref[0, k, t], 1)], ybuf.at[sl, k, pl.ds(t, 1)],
                                      sem.at[sl]).start()
            return c
        lax.fori_loop(0, tm, body, 0)

    @pl.when(i == 0)
    def _():
        gather_start(dst_ref, 0)

    for t in range(tm):
        for k in range(topk):
            pltpu.make_async_copy(ys_hbm.at[pl.ds(dstn_ref[0, k, t], 1)], ybuf.at[1 - slot, k, pl.ds(t, 1)],
                                  sem.at[1 - slot]).start()

    h = jnp.concatenate(_unpack_bf16_pairs(h2_ref[...]), axis=1).astype(BF16)
    y = _dot((_silu(_dot(h, wg_ref[...])) * _dot(h, wu_ref[...])).astype(BF16), wd_ref[...])

    def wait_slot(sl):
        def body(t, c):
            for k in range(topk):
                pltpu.make_async_copy(ys_hbm.at[pl.ds(0, 1)], ybuf.at[sl, k, pl.ds(t, 1)], sem.at[sl]).wait()
            return c
        lax.fori_loop(0, tm, body, 0)

    wait_slot(slot)
    w = wts_ref[...]
    ylo = jnp.zeros((tm, y.shape[1] // 2), F32)
    yhi = jnp.zeros((tm, y.shape[1] // 2), F32)
    for k in range(topk):
        lo, hi = _unpack_bf16_pairs(ybuf[slot, k])
        ylo = ylo + w[:, k:k + 1] * lo
        yhi = yhi + w[:, k:k + 1] * hi
    y = y + jnp.concatenate([ylo, yhi], axis=1)
    o_ref[...] = _ln(alpha * x1_ref[...] + g2_ref[0] * y) * lng_ref[...] + lnb_ref[...]

    @pl.when(i == nt - 1)
    def _():
        wait_slot(1 - slot)


def _final(ys, dest, wtsT, h2, x1, gate2, wg, wu, wd, lng, lnb, alpha, tiles_per_batch):
    n, d = x1.shape
    nt, topk, tm = dest.shape
    kern = functools.partial(_final_kernel, alpha=alpha, topk=topk, tm=tm, nt=nt)
    row = pl.BlockSpec((tm, d), lambda i: (i, 0))
    full = lambda a: pl.BlockSpec(a.shape, lambda i: (0,) * a.ndim, pipeline_mode=pl.Buffered(1))
    return pl.pallas_call(
        kern,
        grid=(nt,),
        in_specs=[pl.BlockSpec((1, topk, tm), lambda i: (i, 0, 0), memory_space=pltpu.SMEM),
                  pl.BlockSpec((1, topk, tm), lambda i: (jnp.minimum(i + 1, nt - 1), 0, 0), memory_space=pltpu.SMEM),
                  pl.BlockSpec((tm, topk), lambda i: (i, 0)),
                  pl.BlockSpec((tm, d // 2), lambda i: (i, 0)), row,
                  pl.BlockSpec((1, 1, d), lambda i: (i // tiles_per_batch, 0, 0)),
                  full(wg), full(wu), full(wd),
                  pl.BlockSpec((1, d), lambda i: (0, 0)), pl.BlockSpec((1, d), lambda i: (0, 0)),
                  pl.BlockSpec(memory_space=pl.ANY)],
        out_specs=row,
        out_shape=jax.ShapeDtypeStruct((n, d), F32),
        scratch_shapes=[pltpu.VMEM((2, topk, tm, d // 2), U32), pltpu.SemaphoreType.DMA((2,))],
        compiler_params=_cparams("arbitrary"),
        name="final",
    )(dest, dest, wtsT, h2, x1, gate2, wg, wu, wd, lng.reshape(1, d), lnb.reshape(1, d), ys)


def _proj_layout(d, qr, kvr, idim, ih, hk, hv):
    src = np.cumsum([0, qr, kvr, idim, ih, hk, hk, hv, hv, d, d])
    wa = -(-(qr + kvr + idim + ih) // LANES) * LANES
    wa = max(wa, 1 << (wa - 1).bit_length())
    pieces = [("a", wa, (int(src[0]), int(src[4]))),
              ("hq", hk, (int(src[4]), int(src[5]))), ("hf", hk, (int(src[5]), int(src[6]))),
              ("hi", hv, (int(src[6]), int(src[7]))), ("hg", hv, (int(src[7]), int(src[8]))),
              ("ga", d, (int(src[8]), int(src[9]))), ("gb", d, (int(src[9]), int(src[10])))]
    pieces.sort(key=lambda p: -p[1])
    off = 0
    layout = {}
    for name, wdt, rng in pieces:
        assert off % wdt == 0
        layout[name] = (off, wdt, rng)
        off += wdt
    return layout, off


def kernel(x, c, rpb_table, hgrn_lb_logits, ada_w, ada_b, w_in, q_norm_g, kv_norm_g, w_uq, w_uk, w_uv, w_qidx,
           idx_k_norm_g, idx_k_norm_b, hgrn_out_norm_g, w_branch_a, w_branch_b, w_o, ln1_g, ln1_b, w_router,
           router_bias, w_exp_gate, w_exp_up, w_exp_down, w_sh_gate, w_sh_up, w_sh_down, ln2_g, ln2_b):
    bsz, s, d = x.shape
    depth = ada_w.shape[0]
    qr = w_uq.shape[1]
    ah, kvr, qk = w_uk.shape[1], w_uk.shape[2], w_uk.shape[3]
    idim = idx_k_norm_g.shape[1]
    ih = w_qidx.shape[2] // idim
    dv = hgrn_out_norm_g.shape[1]
    nh = w_branch_b.shape[1] // dv
    dk = hgrn_lb_logits.shape[1] // nh
    ne = w_router.shape[2]
    topk = min(IDX_TOPK, s // 4)
    alpha = float((2 * depth) ** 0.25)
    n_tok = bsz * s

    lower_bounds = jnp.cumsum(jax.nn.softmax(hgrn_lb_logits.astype(F32), axis=0), axis=0)
    layout, wtot = _proj_layout(d, qr, kvr, idim, ih, nh * dk, nh * dv)

    for l in range(depth):
        mod = _ada(c, ada_w[l], ada_b[l])[:, None, :]
        shift1, scale1, gate1, shift2, scale2, gate2 = jnp.split(mod, 6, axis=-1)

        cols = []
        for off, wdt, (lo, hi) in sorted(layout.values()):
            cols.append(w_in[l][:, lo:hi].astype(BF16))
            if wdt > hi - lo:
                cols.append(jnp.zeros((d, wdt - (hi - lo)), BF16))
        proj = _inproj(x, scale1, shift1, jnp.concatenate(cols, axis=1))
        blk = lambda name: layout[name][0] // layout[name][1]

        qb_sz = _pick(s, (256, 128))
        q_latT, q_idxT, w_idxT, k_idx, kv_lat = _prep(
            proj, blk("a"), layout["a"][1], (qr, kvr, idim, ih, ah, qk), q_norm_g[l], kv_norm_g[l],
            idx_k_norm_g[l], idx_k_norm_b[l], jnp.swapaxes(w_uq[l], 0, 1).astype(BF16),
            w_uk[l].astype(BF16), jnp.swapaxes(w_qidx[l], 0, 1).astype(BF16), qb_sz)
        kv_latT = jnp.concatenate([jnp.swapaxes(kv_lat, 1, 2), jnp.ones((bsz, 1, s), BF16),
                                   jnp.zeros((bsz, 7, s), BF16)], axis=1)
        o_a = _dsa(q_idxT, w_idxT, q_latT, k_idx, kv_lat, kv_latT,
                   rpb_table, jnp.swapaxes(w_uv[l], 1, 2).astype(BF16), topk, qb_sz)

        o_b = _hgrn(proj, (blk("hq"), blk("hf"), blk("hi"), blk("hg")), lower_bounds[l],
                    hgrn_out_norm_g[l], nh, dk, dv)

        x1, h2, lgT = _postmix(o_a, o_b, proj, blk("ga"), blk("gb"), x, gate1, scale2, shift2,
                               w_branch_a[l].astype(BF16), w_branch_b[l].astype(BF16), w_o[l].astype(BF16),
                               jnp.swapaxes(w_router[l], 0, 1).astype(BF16), ln1_g[l], ln1_b[l], alpha)

        ids, wts, rnk, sizes_f = _route(lgT, router_bias[l])
        eb = EXPERT_BLOCK
        sizes = sizes_f[:, 0].astype(I32)
        padded = (sizes + eb - 1) // eb * eb
        pend = jnp.cumsum(padded)
        pstart = pend - padded
        n_rows = -(-(n_tok * MOE_TOPK + ne * (eb - 1)) // eb) * eb
        blk_start = jnp.arange(n_rows // eb, dtype=I32) * eb
        blk_expert = jnp.minimum(jnp.sum((pend[None, :] <= blk_start[:, None]).astype(I32), axis=1), ne - 1)
        onehot = (blk_expert[:, None] == jnp.arange(ne, dtype=I32)[None, :]).astype(I32)
        blk_size = jnp.sum(onehot * sizes[None, :], axis=1)
        blk_pstart = jnp.sum(onehot * pstart[None, :], axis=1)
        nvalid = jnp.clip(blk_size - (blk_start - blk_pstart), 0, eb).astype(I32)
        dest = _dest(pstart.astype(I32), ids, rnk, ne)
        tm = _pick(s, (128,))
        dest_t = dest.reshape(bsz, MOE_TOPK, s // tm, tm).transpose(0, 2, 1, 3).reshape(n_tok // tm, MOE_TOPK, tm)
        wtsT = jnp.swapaxes(wts, 1, 2).reshape(n_tok, MOE_TOPK)

        h2f = h2.reshape(n_tok, d // 2)
        xs = _dispatch(dest_t, h2f, n_rows)
        ys = _experts(xs, blk_expert, nvalid, w_exp_gate[l], w_exp_up[l], w_exp_down[l])
        x = _final(ys, dest_t, wtsT, h2f, x1.reshape(n_tok, d), gate2, w_sh_gate[l].astype(BF16),
                   w_sh_up[l].astype(BF16), w_sh_down[l].astype(BF16), ln2_g[l], ln2_b[l], alpha,
                   s // tm).reshape(bsz, s, d)
    return x
```

```python
import functools
import math

import numpy as np
import jax
import jax.numpy as jnp
from jax import lax
from jax.experimental import pallas as pl
from jax.experimental.pallas import tpu as pltpu

F32 = jnp.float32
BF16 = jnp.bfloat16
I32 = jnp.int32
U32 = jnp.uint32

EPS = 1e-6
IDX_TOPK = 256
RPB_MAX_DIST = 128
MOE_TOPK = 8
N_GROUPS = 8
TOPK_GROUPS = 4
ROUTED_SCALE = 2.5
EXPERT_BLOCK = 512

V7X_VMEM_LIMIT_BYTES = 56 * 1024 * 1024
LANES = 128
INT_MIN = -2 ** 31
NEG_BIG = -1e30
LOG2E = math.log2(math.e)


def _cparams(*sem):
    return pltpu.CompilerParams(dimension_semantics=tuple(sem), vmem_limit_bytes=V7X_VMEM_LIMIT_BYTES)


def _pick(n, prefs):
    for p in prefs:
        if n % p == 0:
            return p
    return n


def _sigmoid(v):
    return 1.0 / (1.0 + jnp.exp(-v))


def _silu(v):
    return v * _sigmoid(v)


def _ln(v):
    mu = jnp.mean(v, axis=-1, keepdims=True)
    d = v - mu
    var = jnp.mean(d * d, axis=-1, keepdims=True)
    return d * lax.rsqrt(var + EPS)


def _rms(v):
    return v * lax.rsqrt(jnp.mean(v * v, axis=-1, keepdims=True) + EPS)


def _dot(a, b):
    return jnp.dot(a, b, preferred_element_type=F32)


def _pack_bf16_pairs(v):
    n = v.shape[1] // 2
    lo = pltpu.bitcast(v[:, :n].astype(BF16).astype(F32), U32) >> 16
    hi = pltpu.bitcast(v[:, n:].astype(BF16).astype(F32), U32) & jnp.uint32(0xFFFF0000)
    return lo | hi


def _unpack_bf16_pairs(u):
    return pltpu.bitcast(u << 16, F32), pltpu.bitcast(u & jnp.uint32(0xFFFF0000), F32)


def _dot_nt(a, b):
    return lax.dot_general(a, b, (((1,), (1,)), ((), ())), preferred_element_type=F32)


def _dot_tn(a, b):
    return lax.dot_general(a, b, (((0,), (0,)), ((), ())), preferred_element_type=F32)


def _ada_kernel(c_ref, w_ref, b_ref, o_ref):
    c = c_ref[...]
    o_ref[...] = _dot(_silu(c).astype(BF16), w_ref[...].astype(BF16)) + b_ref[...]


def _ada(c, w, b):
    bsz, d = c.shape
    n = w.shape[1]
    rows = 8
    cp = jnp.zeros((rows, d), F32).at[:bsz].set(c)
    tn = _pick(n, (1024, 512, 256, 128))
    out = pl.pallas_call(
        _ada_kernel,
        grid=(n // tn,),
        in_specs=[pl.BlockSpec((rows, d), lambda j: (0, 0)),
                  pl.BlockSpec((d, tn), lambda j: (0, j)),
                  pl.BlockSpec((1, tn), lambda j: (0, j))],
        out_specs=pl.BlockSpec((rows, tn), lambda j: (0, j)),
        out_shape=jax.ShapeDtypeStruct((rows, n), F32),
        compiler_params=_cparams("arbitrary"),
        name="ada",
    )(cp, w, b.reshape(1, n))
    return out[:bsz]


def _inproj_kernel(x_ref, sc_ref, sh_ref, w_ref, o_ref, h_ref):
    @pl.when(pl.program_id(2) == 0)
    def _():
        h = _ln(x_ref[0]) * (1.0 + sc_ref[0]) + sh_ref[0]
        h_ref[...] = h.astype(BF16)

    o_ref[0] = _dot(h_ref[...], w_ref[...]).astype(o_ref.dtype)


def _inproj(x, scale, shift, w):
    bsz, s, d = x.shape
    n = w.shape[1]
    tm = _pick(s, (1024, 512, 256, 128))
    tn = _pick(n, (1024, 512, 256, 128))
    return pl.pallas_call(
        _inproj_kernel,
        grid=(bsz, s // tm, n // tn),
        in_specs=[pl.BlockSpec((1, tm, d), lambda b, i, j: (b, i, 0)),
                  pl.BlockSpec((1, 1, d), lambda b, i, j: (b, 0, 0)),
                  pl.BlockSpec((1, 1, d), lambda b, i, j: (b, 0, 0)),
                  pl.BlockSpec((d, tn), lambda b, i, j: (0, j))],
        out_specs=pl.BlockSpec((1, tm, tn), lambda b, i, j: (b, i, j)),
        out_shape=jax.ShapeDtypeStruct((bsz, s, n), BF16),
        scratch_shapes=[pltpu.VMEM((tm, d), BF16)],
        compiler_params=_cparams("arbitrary", "arbitrary", "arbitrary"),
        name="inproj",
    )(x, scale, shift, w)


def _prep_kernel(a_ref, qg_ref, kvg_ref, ikg_ref, ikb_ref, wuqT_ref, wuk_ref, wqiT_ref,
                 qlatT_ref, qidxT_ref, widxT_ref, kidx_ref, kv_ref, *, qr, kvr, idim, ih, ah, qk):
    a = a_ref[0]
    af = a.astype(F32)
    cq = (_rms(af[:, :qr]) * qg_ref[...]).astype(BF16)
    ckv = af[:, qr:qr + kvr]
    ki = af[:, qr + kvr:qr + kvr + idim]
    kv_ref[0] = (_rms(ckv) * kvg_ref[...]).astype(BF16)
    kidx_ref[0] = (_ln(ki) * ikg_ref[...] + ikb_ref[...]).astype(BF16)
    tail = a[:, qr + kvr:qr + kvr + LANES]
    eye = (lax.broadcasted_iota(I32, (LANES, LANES), 0) == lax.broadcasted_iota(I32, (LANES, LANES), 1))
    tailT = _dot_nt(jnp.where(eye, 1.0, 0.0).astype(BF16), tail)
    widxT_ref[0] = tailT[idim:idim + ih] * float((ih * idim) ** -0.5)
    tm = a.shape[0]
    qT = _dot_nt(wuqT_ref[...], cq)
    for h in range(ah):
        qh = qT[h * qk:(h + 1) * qk].astype(BF16)
        qlatT_ref[0, 0, :, h * tm:(h + 1) * tm] = (_dot(wuk_ref[h], qh) * float(qk ** -0.5 * LOG2E)).astype(BF16)
    qiT = _dot_nt(wqiT_ref[...], cq)
    for h in range(ih):
        qidxT_ref[0, 0, :, h * tm:(h + 1) * tm] = qiT[h * idim:(h + 1) * idim].astype(BF16)


def _prep(proj, a_blk, wa, dims, q_norm_g, kv_norm_g, ikg, ikb, w_uqT, w_uk, w_qidxT, tm):
    bsz, s, _ = proj.shape
    qr, kvr, idim, ih, ah, qk = dims
    assert idim + ih <= LANES and qr + kvr + LANES <= wa
    kern = functools.partial(_prep_kernel, qr=qr, kvr=kvr, idim=idim, ih=ih, ah=ah, qk=qk)
    full = lambda shape: pl.BlockSpec(shape, lambda b, i: (0,) * len(shape))
    return pl.pallas_call(
        kern,
        grid=(bsz, s // tm),
        in_specs=[pl.BlockSpec((1, tm, wa), lambda b, i: (b, i, a_blk)),
                  full((1, qr)), full((1, kvr)), full((1, idim)), full((1, idim)),
                  full(w_uqT.shape), full(w_uk.shape), full(w_qidxT.shape)],
        out_specs=[pl.BlockSpec((1, 1, kvr, ah * tm), lambda b, i: (b, i, 0, 0)),
                   pl.BlockSpec((1, 1, idim, ih * tm), lambda b, i: (b, i, 0, 0)),
                   pl.BlockSpec((1, ih, tm), lambda b, i: (b, 0, i)),
                   pl.BlockSpec((1, tm, idim), lambda b, i: (b, i, 0)),
                   pl.BlockSpec((1, tm, kvr), lambda b, i: (b, i, 0))],
        out_shape=[jax.ShapeDtypeStruct((bsz, s // tm, kvr, ah * tm), BF16),
                   jax.ShapeDtypeStruct((bsz, s // tm, idim, ih * tm), BF16),
                   jax.ShapeDtypeStruct((bsz, ih, s), F32),
                   jax.ShapeDtypeStruct((bsz, s, idim), BF16),
                   jax.ShapeDtypeStruct((bsz, s, kvr), BF16)],
        compiler_params=_cparams("arbitrary", "arbitrary"),
        name="prep",
    )(proj, q_norm_g.reshape(1, qr), kv_norm_g.reshape(1, kvr), ikg.reshape(1, idim), ikb.reshape(1, idim),
      w_uqT, w_uk, w_qidxT)


def _dsa_kernel(qidxT_ref, widxT_ref, qlatT_ref, kidx_ref, kv_ref, kvT_ref, bkt0_ref, bkt1_ref, rel_ref, wuvT_ref,
                o_ref, keys_ref, gmax_ref, s_ref, p_ref, m_ref, a_ref, acc_ref, t0_ref, t1_ref, lstrict_ref, taken_ref,
                *, qb_sz, topk, ih, ah, c, nb):
    QB = qb_sz
    qb = pl.program_id(1)
    nchunks = qb + 1
    qpos = qb * QB + lax.broadcasted_iota(I32, (1, QB), 1)
    kpos = lax.broadcasted_iota(I32, (QB, 1), 0)
    wT = widxT_ref[0]
    hcols = lambda h: slice(h * QB, (h + 1) * QB)

    @pl.when(jnp.logical_and(pl.program_id(0) == 0, qb == 0))
    def _():
        for bkt_ref, t_ref in ((bkt0_ref, t0_ref), (bkt1_ref, t1_ref)):
            bkt = bkt_ref[...]
            for h in range(ah):
                tile = jnp.zeros((QB, QB), F32)
                for b in range(nb - 1):
                    tile = jnp.where(bkt == b, rel_ref[b, h], tile)
                t_ref[:, hcols(h)] = tile
        lstrict_ref[...] = jnp.where(lax.broadcasted_iota(I32, (QB, QB), 1) < lax.broadcasted_iota(I32, (QB, QB), 0),
                                     1.0, 0.0).astype(BF16)

    def score_chunk(kc, carry):
        off = pl.multiple_of(kc * QB, QB)
        s_ref[:, :ih * QB] = _dot(kidx_ref[0, pl.ds(off, QB), :], qidxT_ref[0, 0])
        acc = jnp.zeros((QB, QB), F32)
        for h in range(ih):
            acc = acc + wT[h:h + 1] * jnp.maximum(s_ref[:, hcols(h)], 0.0)
        bits = pltpu.bitcast(acc, I32)
        skey = bits ^ ((bits >> 31) & 0x7FFFFFFF)
        causal = (off + kpos) <= qpos
        skey = jnp.where(causal, skey, INT_MIN)
        keys_ref[pl.ds(off, QB), :] = skey
        gmax_ref[...] = jnp.maximum(gmax_ref[...], skey)
        return carry

    gmax_ref[...] = jnp.full(gmax_ref.shape, INT_MIN, I32)
    lax.fori_loop(0, nchunks, score_chunk, 0)

    gmax = gmax_ref[...]
    lo0 = jnp.min(gmax, axis=0, keepdims=True) if QB >= topk else jnp.full((1, QB), INT_MIN, I32)
    hi0 = jnp.max(gmax, axis=0, keepdims=True) + 1

    def count_ge(cand):
        def body(kc, cnt):
            off = pl.multiple_of(kc * QB, QB)
            hit = jnp.where(keys_ref[pl.ds(off, QB), :] >= cand, 1, 0)
            return cnt + jnp.sum(hit.reshape(QB // 8, 8, QB), axis=0)

        cnt = lax.fori_loop(0, nchunks, body, jnp.zeros((8, QB), I32))
        return jnp.sum(cnt.astype(F32), axis=0, keepdims=True)

    def bis_cond(st):
        it, _, _, done = st
        return jnp.logical_and(it < 34, jnp.min(done) < 0.5)

    def bis_body(st):
        it, lo, hi, done = st
        cand = (lo >> 1) + (hi >> 1) + (lo & hi & 1)
        cnt = count_ge(cand)
        ge = cnt >= float(topk)
        conv = cand == lo
        fin = jnp.logical_or(conv, cnt == float(topk))
        thr_new = jnp.where(conv, lo, cand)
        lo = jnp.where(fin, thr_new, jnp.where(ge, cand, lo))
        hi = jnp.where(fin, thr_new + 1, jnp.where(ge, hi, cand))
        return it + 1, lo, hi, jnp.where(fin, 1.0, done)

    _, thr, _, _ = lax.while_loop(bis_cond, bis_body, (jnp.int32(0), lo0, hi0, jnp.zeros((1, QB), F32)))
    need = float(topk) - count_ge(thr + 1)

    m_ref[...] = jnp.full(m_ref.shape, NEG_BIG, F32)
    acc_ref[...] = jnp.zeros(acc_ref.shape, F32)

    def attn_chunk(off, bias_ref, diag, taken):
        kt = keys_ref[pl.ds(off, QB), :]
        eq = kt == thr
        eq01 = jnp.where(eq, 1.0, 0.0)
        before = _dot(lstrict_ref[...], eq01.astype(BF16)) + taken
        sel = jnp.logical_or(kt > thr, jnp.logical_and(eq, before < need))
        taken = taken + jnp.sum(eq01, axis=0, keepdims=True)
        if diag:
            sel = jnp.logical_and(sel, (off + kpos) <= qpos)
        madd = jnp.where(sel, 0.0, NEG_BIG)
        s_ref[:, :ah * QB] = _dot(kv_ref[0, pl.ds(off, QB), :], qlatT_ref[0, 0])
        for g in range(ah * QB // LANES):
            cols = slice(g * LANES, (g + 1) * LANES)
            qcols = slice(g * LANES % QB, g * LANES % QB + LANES)
            s = s_ref[:, cols] + madd[:, qcols]
            if bias_ref is not None:
                s = s + bias_ref[:, cols]
            m_prev = m_ref[:, cols]
            m_new = jnp.maximum(m_prev, jnp.max(s, axis=0, keepdims=True))
            p_ref[:, cols] = jnp.exp2(s - m_new).astype(BF16)
            a_ref[:, cols] = jnp.exp2(m_prev - m_new)
            m_ref[:, cols] = m_new
        acc_ref[...] = a_ref[...] * acc_ref[...] + _dot(kvT_ref[0, :, pl.ds(off, QB)], p_ref[...])
        return taken

    def far_body(kc, taken):
        return attn_chunk(pl.multiple_of(kc * QB, QB), None, False, taken)

    taken_ref[...] = lax.fori_loop(0, qb - 1, far_body, jnp.zeros((1, QB), F32))

    @pl.when(qb >= 1)
    def _():
        taken_ref[...] = attn_chunk(pl.multiple_of((qb - 1) * QB, QB), t1_ref, False, taken_ref[...])

    attn_chunk(pl.multiple_of(qb * QB, QB), t0_ref, True, taken_ref[...])

    outs = []
    for h in range(ah):
        o = (acc_ref[:c, hcols(h)] / acc_ref[c:c + 1, hcols(h)]).astype(BF16)
        outs.append(_dot(wuvT_ref[h], o))
    o_ref[0] = jnp.concatenate(outs, axis=0).T.astype(o_ref.dtype)


def _t5_bucket_np(d, nbuckets):
    max_exact = nbuckets // 2
    dd = np.maximum(d, 1).astype(np.float32)
    large = max_exact + (np.log(dd / np.float32(max_exact)) / np.float32(math.log(RPB_MAX_DIST / max_exact))
                         * np.float32(nbuckets - max_exact)).astype(np.int32)
    large = np.minimum(large, nbuckets - 1)
    return np.where(d < max_exact, d, large).astype(np.int32)


def _dsa(q_idxT, w_idxT, q_latT, k_idx, kv_lat, kv_latT, rpb_table, w_uvT, topk, QB):
    bsz, nqb, idim, ihq = q_idxT.shape
    c, ahq = q_latT.shape[2], q_latT.shape[3]
    ca = kv_latT.shape[1]
    ih, ah = ihq // QB, ahq // QB
    s = nqb * QB
    vd = w_uvT.shape[1]
    nb = rpb_table.shape[0]
    assert QB >= RPB_MAX_DIST
    j = np.arange(QB)[:, None]
    i = np.arange(QB)[None, :]
    bkt0 = jnp.asarray(_t5_bucket_np(np.maximum(i - j, 0), nb))
    bkt1 = jnp.asarray(_t5_bucket_np(QB + i - j, nb))
    assert int(_t5_bucket_np(np.array([RPB_MAX_DIST]), nb)[0]) == nb - 1
    rel = (rpb_table.astype(F32) - rpb_table[nb - 1].astype(F32)[None, :]) * LOG2E
    kern = functools.partial(_dsa_kernel, qb_sz=QB, topk=topk, ih=ih, ah=ah, c=c, nb=nb)
    const = lambda shape: pl.BlockSpec(shape, lambda b, i: (0,) * len(shape), pipeline_mode=pl.Buffered(1))
    hw = max(ih, ah) * QB
    return pl.pallas_call(
        kern,
        grid=(bsz, nqb),
        in_specs=[pl.BlockSpec((1, 1, idim, ih * QB), lambda b, i: (b, i, 0, 0)),
                  pl.BlockSpec((1, ih, QB), lambda b, i: (b, 0, i)),
                  pl.BlockSpec((1, 1, c, ah * QB), lambda b, i: (b, i, 0, 0)),
                  pl.BlockSpec((1, s, idim), lambda b, i: (b, 0, 0)),
                  pl.BlockSpec((1, s, c), lambda b, i: (b, 0, 0)),
                  pl.BlockSpec((1, ca, s), lambda b, i: (b, 0, 0)),
                  const((QB, QB)), const((QB, QB)),
                  pl.BlockSpec(memory_space=pltpu.SMEM), const(w_uvT.shape)],
        out_specs=pl.BlockSpec((1, QB, ah * vd), lambda b, i: (b, i, 0)),
        out_shape=jax.ShapeDtypeStruct((bsz, s, ah * vd), BF16),
        scratch_shapes=[pltpu.VMEM((s, QB), I32),
                        pltpu.VMEM((QB, QB), I32),
                        pltpu.VMEM((QB, hw), F32),
                        pltpu.VMEM((QB, ah * QB), BF16),
                        pltpu.VMEM((1, ah * QB), F32),
                        pltpu.VMEM((1, ah * QB), F32),
                        pltpu.VMEM((ca, ah * QB), F32),
                        pltpu.VMEM((QB, ah * QB), F32),
                        pltpu.VMEM((QB, ah * QB), F32),
                        pltpu.VMEM((QB, QB), BF16),
                        pltpu.VMEM((1, QB), F32)],
        compiler_params=_cparams("arbitrary", "arbitrary"),
        name="dsa",
    )(q_idxT, w_idxT, q_latT, k_idx, kv_lat, kv_latT, bkt0, bkt1, rel, w_uvT)


HGRN_CHUNK = 64
HGRN_SUB = 16
HGRN_EXP_CLAMP = 80.0


def _dot_exact(a, b, dims):
    return lax.dot_general(a, b, (dims, ((), ())), precision=lax.Precision.HIGHEST, preferred_element_type=F32)


def _hgrn_kernel(hq_ref, hf_ref, hi_ref, hg_ref, lb_ref, g_ref, o_ref, st_ref, st0_ref, *, nh, dk, dv, tt):
    C, SUB = HGRN_CHUNK, HGRN_SUB

    @pl.when(pl.program_id(1) == 0)
    def _():
        st_ref[...] = jnp.zeros(st_ref.shape, F32)

    r = lax.broadcasted_iota(I32, (C, C), 0)
    cc = lax.broadcasted_iota(I32, (C, C), 1)
    tri_mask = r >= cc
    tri = jnp.where(tri_mask, 1.0, 0.0).astype(BF16)
    g = g_ref[...]
    st0_ref[...] = st_ref[...]
    decay = jnp.zeros((1, nh * dk), F32)
    lb = lb_ref[...]

    for c in range(tt // C):
        rows = pl.ds(c * C, C)
        f = lb + (1.0 - lb) * _sigmoid(hf_ref[0, rows, :].astype(F32))
        lf = jnp.log(f)
        t1 = lf.astype(BF16)
        r1 = lf - t1.astype(F32)
        t2 = r1.astype(BF16)
        t3 = (r1 - t2.astype(F32)).astype(BF16)
        b = _dot(tri, t1) + _dot(tri, t2) + _dot(tri, t3)
        kk = 1.0 - f
        hq = hq_ref[0, rows, :].astype(F32)
        q = _silu(hq) * float(dk ** -0.5)
        qe = (q * jnp.exp(b)).astype(BF16)
        b_last = b[C - 1:C]
        k_dec = (kk * jnp.exp(b_last - b)).astype(BF16)
        dec_last = jnp.exp(b_last)
        qs, ks = [], []
        for i in range(C // SUB):
            lo, n = i * SUB, (i + 1) * SUB
            bi = b[lo - 1:lo] if i > 0 else jnp.zeros((1, nh * dk), F32)
            decay = jnp.maximum(decay, bi - b[n - 1:n])
            qs.append((q[lo:n] * jnp.exp(b[lo:n] - bi)).astype(BF16))
            ks.append((kk * jnp.exp(jnp.minimum(bi - b, HGRN_EXP_CLAMP))).astype(BF16))
        v_all = hi_ref[0, rows, :]
        outs = []
        for h in range(nh):
            kc = slice(h * dk, (h + 1) * dk)
            v = v_all[:, h * dv:(h + 1) * dv]
            stT = st_ref[h]
            att = jnp.concatenate([_dot_nt(qs[i][:, kc], ks[i][:, kc]) for i in range(C // SUB)], axis=0)
            att = jnp.where(tri_mask, att, 0.0)
            o = _dot_nt(qe[:, kc], stT.astype(BF16)) + _dot(att.astype(BF16), v)
            st_ref[h] = stT * dec_last[:, kc] + _dot_tn(v, k_dec[:, kc])
            outs.append(_rms(o) * g)
        o_all = jnp.concatenate(outs, axis=1) * _silu(hg_ref[0, rows, :].astype(F32))
        o_ref[0, rows, :] = o_all.astype(o_ref.dtype)

    @pl.when(jnp.max(decay) > HGRN_EXP_CLAMP)
    def _():
        row = lax.broadcasted_iota(I32, (SUB, 1), 0)

        def head(h, carry):
            ko = pl.multiple_of(h * dk, dk)
            vo = pl.multiple_of(h * dv, dv)
            lb = lb_ref[:, pl.ds(ko, dk)]

            def slab(j, S):
                rows = pl.ds(pl.multiple_of(j * SUB, SUB), SUB)
                f = lb + (1.0 - lb) * _sigmoid(hf_ref[0, rows, pl.ds(ko, dk)].astype(F32))
                kk = 1.0 - f
                hq = hq_ref[0, rows, pl.ds(ko, dk)].astype(F32)
                q = _silu(hq) * float(dk ** -0.5)
                v = hi_ref[0, rows, pl.ds(vo, dv)].astype(F32)
                o = jnp.zeros((SUB, dv), F32)
                for r in range(SUB):
                    S = S * f[r:r + 1] + _dot_exact(jnp.where(row == r, v, 0.0), kk, ((0,), (0,)))
                    o = jnp.where(row == r, _dot_exact(q, S, ((1,), (1,))), o)
                hg = hg_ref[0, rows, pl.ds(vo, dv)].astype(F32)
                o_ref[0, rows, pl.ds(vo, dv)] = (_rms(o) * g * _silu(hg)).astype(o_ref.dtype)
                return S

            st_ref[h] = lax.fori_loop(0, tt // SUB, slab, st0_ref[h])
            return carry

        lax.fori_loop(0, nh, head, 0)


def _hgrn(proj, blks, lb, g, nh, dk, dv):
    bsz, s, _ = proj.shape
    tt = _pick(s, (256, 128, 64))
    kern = functools.partial(_hgrn_kernel, nh=nh, dk=dk, dv=dv, tt=tt)
    col = lambda blk, wdt: pl.BlockSpec((1, tt, wdt), lambda b, i: (b, i, blk))
    return pl.pallas_call(
        kern,
        grid=(bsz, s // tt),
        in_specs=[col(blks[0], nh * dk), col(blks[1], nh * dk), col(blks[2], nh * dv), col(blks[3], nh * dv),
                  pl.BlockSpec((1, nh * dk), lambda b, i: (0, 0)),
                  pl.BlockSpec((1, dv), lambda b, i: (0, 0))],
        out_specs=pl.BlockSpec((1, tt, nh * dv), lambda b, i: (b, i, 0)),
        out_shape=jax.ShapeDtypeStruct((bsz, s, nh * dv), BF16),
        scratch_shapes=[pltpu.VMEM((nh, dv, dk), F32), pltpu.VMEM((nh, dv, dk), F32)],
        compiler_params=_cparams("arbitrary", "arbitrary"),
        name="hgrn",
    )(proj, proj, proj, proj, lb.reshape(1, nh * dk), g.reshape(1, dv))


def _postmix_kernel(oa_ref, ob_ref, ga_ref, gb_ref, x_ref, g1_ref, sc2_ref, sh2_ref,
                    wa_ref, wb_ref, wo_ref, wrT_ref, lng_ref, lnb_ref,
                    x1_ref, h2_ref, lgT_ref, *, alpha):
    ya = _dot(oa_ref[0], wa_ref[...])
    yb = _dot(ob_ref[0], wb_ref[...])
    mix = _sigmoid(ga_ref[0].astype(F32)) * ya + _sigmoid(gb_ref[0].astype(F32)) * yb
    mixed = _dot(mix.astype(BF16), wo_ref[...])
    x1 = _ln(alpha * x_ref[0] + g1_ref[0] * mixed) * lng_ref[...] + lnb_ref[...]
    x1_ref[0] = x1
    h2 = _ln(x1) * (1.0 + sc2_ref[0]) + sh2_ref[0]
    h2_ref[0] = _pack_bf16_pairs(h2)
    lgT_ref[0] = _dot_nt(wrT_ref[...], h2.astype(BF16))


def _postmix(o_a, o_b, proj, ga_blk, gb_blk, x, gate1, scale2, shift2, wa, wb, wo, wrT, lng, lnb, alpha):
    bsz, s, d = x.shape
    ne = wrT.shape[0]
    tm = _pick(s, (256, 128))
    kern = functools.partial(_postmix_kernel, alpha=alpha)
    row = lambda wdt: pl.BlockSpec((1, tm, wdt), lambda b, i: (b, i, 0))
    vec = pl.BlockSpec((1, 1, d), lambda b, i: (b, 0, 0))
    full = lambda a: pl.BlockSpec(a.shape, lambda b, i: (0,) * a.ndim, pipeline_mode=pl.Buffered(1))
    return pl.pallas_call(
        kern,
        grid=(bsz, s // tm),
        in_specs=[row(o_a.shape[2]), row(o_b.shape[2]),
                  pl.BlockSpec((1, tm, d), lambda b, i: (b, i, ga_blk)),
                  pl.BlockSpec((1, tm, d), lambda b, i: (b, i, gb_blk)),
                  row(d), vec, vec, vec,
                  full(wa), full(wb), full(wo), full(wrT),
                  pl.BlockSpec((1, d), lambda b, i: (0, 0)), pl.BlockSpec((1, d), lambda b, i: (0, 0))],
        out_specs=[row(d), row(d // 2), pl.BlockSpec((1, ne, tm), lambda b, i: (b, 0, i))],
        out_shape=[jax.ShapeDtypeStruct((bsz, s, d), F32),
                   jax.ShapeDtypeStruct((bsz, s, d // 2), U32),
                   jax.ShapeDtypeStruct((bsz, ne, s), F32)],
        compiler_params=_cparams("arbitrary", "arbitrary"),
        name="postmix",
    )(o_a, o_b, proj, proj, x, gate1, scale2, shift2, wa, wb, wo, wrT, lng.reshape(1, d), lnb.reshape(1, d))


def _route_kernel(lg_ref, bias_ref, ids_ref, wts_ref, rnk_ref, sizes_ref, upper_ref, carry_ref, *, ne):
    first = jnp.logical_and(pl.program_id(0) == 0, pl.program_id(1) == 0)
    tn = lg_ref.shape[2]

    @pl.when(first)
    def _():
        carry_ref[...] = jnp.zeros(carry_ref.shape, F32)
        r_ = lax.broadcasted_iota(I32, (tn, tn), 0)
        c_ = lax.broadcasted_iota(I32, (tn, tn), 1)
        upper_ref[...] = jnp.where(r_ < c_, 1.0, 0.0).astype(BF16)

    per = ne // N_GROUPS
    s = _sigmoid(lg_ref[0])
    bz = s + bias_ref[...]
    ridx = lax.broadcasted_iota(I32, (per, tn), 0)
    neg_inf = jnp.float32(-jnp.inf)
    gs = []
    for g in range(N_GROUPS):
        blk = bz[g * per:(g + 1) * per]
        m1 = jnp.max(blk, axis=0, keepdims=True)
        first_hit = jnp.min(jnp.where(blk == m1, ridx, per), axis=0, keepdims=True)
        m2 = jnp.max(jnp.where(ridx == first_hit, neg_inf, blk), axis=0, keepdims=True)
        gs.append(m1 + m2)
    emask_rows = []
    for g in range(N_GROUPS):
        rank = jnp.zeros((1, tn), I32)
        for g2 in range(N_GROUPS):
            if g2 == g:
                continue
            beats = (gs[g2] > gs[g]) if g2 > g else (gs[g2] >= gs[g])
            rank = rank + jnp.where(beats, 1, 0)
        emask_rows.append(jnp.broadcast_to(rank < TOPK_GROUPS, (per, tn)))
    emask = jnp.concatenate(emask_rows, axis=0)
    masked = jnp.where(emask, bz, neg_inf)
    eidx = lax.broadcasted_iota(I32, (ne, tn), 0)
    rank = jnp.zeros((ne, tn), I32)
    for e2 in range(ne):
        row = masked[e2:e2 + 1]
        beats = jnp.logical_or(row > masked, jnp.logical_and(row == masked, e2 < eidx))
        rank = rank + jnp.where(beats, 1, 0)
    sel = rank < MOE_TOPK
    sel01 = jnp.where(sel, 1.0, 0.0)
    denom = jnp.sum(jnp.where(sel, s, 0.0), axis=0, keepdims=True)
    wn = s / denom * ROUTED_SCALE
    before = _dot(sel01.astype(BF16), upper_ref[...]) + carry_ref[:, 0:1]
    ids, wts, rnk = [], [], []
    for k in range(MOE_TOPK):
        hit = rank == k
        ids.append(jnp.sum(jnp.where(hit, eidx, 0), axis=0, keepdims=True))
        wts.append(jnp.sum(jnp.where(hit, wn, 0.0), axis=0, keepdims=True))
        rnk.append(jnp.sum(jnp.where(hit, before, 0.0), axis=0, keepdims=True))
    ids_ref[0] = jnp.concatenate(ids, axis=0)
    wts_ref[0] = jnp.concatenate(wts, axis=0)
    rnk_ref[0] = jnp.concatenate(rnk, axis=0).astype(I32)
    carry_ref[...] = carry_ref[...] + jnp.sum(sel01, axis=1, keepdims=True)
    sizes_ref[...] = carry_ref[...]


def _route(lgT, bias):
    bsz, ne, s = lgT.shape
    tn = _pick(s, (1024, 512, 256, 128))
    kern = functools.partial(_route_kernel, ne=ne)
    slot = pl.BlockSpec((1, MOE_TOPK, tn), lambda b, j: (b, 0, j))
    return pl.pallas_call(
        kern,
        grid=(bsz, s // tn),
        in_specs=[pl.BlockSpec((1, ne, tn), lambda b, j: (b, 0, j)),
                  pl.BlockSpec((ne, 1), lambda b, j: (0, 0))],
        out_specs=[slot, slot, slot, pl.BlockSpec((ne, LANES), lambda b, j: (0, 0))],
        out_shape=[jax.ShapeDtypeStruct((bsz, MOE_TOPK, s), I32),
                   jax.ShapeDtypeStruct((bsz, MOE_TOPK, s), F32),
                   jax.ShapeDtypeStruct((bsz, MOE_TOPK, s), I32),
                   jax.ShapeDtypeStruct((ne, LANES), F32)],
        scratch_shapes=[pltpu.VMEM((tn, tn), BF16), pltpu.VMEM((ne, LANES), F32)],
        compiler_params=_cparams("arbitrary", "arbitrary"),
        name="route",
    )(lgT, bias.reshape(ne, 1))


def _dest_kernel(pstart_ref, ids_ref, rnk_ref, o_ref, *, ne):
    ids = ids_ref[0]
    base = jnp.zeros(ids.shape, I32)
    for e in range(ne):
        base = jnp.where(ids == e, pstart_ref[e], base)
    o_ref[0] = base + rnk_ref[0]


def _dest(pstart, ids, rnk, ne):
    bsz, k, s = ids.shape
    tn = _pick(s, (2048, 1024, 512, 256, 128))
    blk = lambda: pl.BlockSpec((1, k, tn), lambda b, j, ps: (b, 0, j))
    return pl.pallas_call(
        functools.partial(_dest_kernel, ne=ne),
        grid_spec=pltpu.PrefetchScalarGridSpec(num_scalar_prefetch=1, grid=(bsz, s // tn),
                                               in_specs=[blk(), blk()], out_specs=blk()),
        out_shape=jax.ShapeDtypeStruct((bsz, k, s), I32),
        compiler_params=_cparams("arbitrary", "arbitrary"),
        name="dest",
    )(pstart, ids, rnk)


def _dispatch_kernel(dst_ref, h_ref, xs_hbm, sem, *, tm, topk):
    for t in range(tm):
        for k in range(topk):
            pltpu.make_async_copy(h_ref.at[pl.ds(t, 1)], xs_hbm.at[pl.ds(dst_ref[0, k, t], 1)], sem.at[0]).start()

    def wait(t, c):
        for k in range(topk):
            pltpu.make_async_copy(h_ref.at[pl.ds(t, 1)], xs_hbm.at[pl.ds(0, 1)], sem.at[0]).wait()
        return c

    lax.fori_loop(0, tm, wait, 0)


def _dispatch(dest, h2, n_rows):
    n, d = h2.shape
    nt, topk, tm = dest.shape
    return pl.pallas_call(
        functools.partial(_dispatch_kernel, tm=tm, topk=topk),
        grid=(nt,),
        in_specs=[pl.BlockSpec((1, topk, tm), lambda i: (i, 0, 0), memory_space=pltpu.SMEM),
                  pl.BlockSpec((tm, d), lambda i: (i, 0))],
        out_specs=pl.BlockSpec(memory_space=pl.ANY),
        out_shape=jax.ShapeDtypeStruct((n_rows, d), h2.dtype),
        scratch_shapes=[pltpu.SemaphoreType.DMA((1,))],
        compiler_params=_cparams("arbitrary"),
        name="dispatch",
    )(dest, h2)


def _experts_kernel(be_ref, nv_ref, x_ref, wg_ref, wu_ref, wd_ref, y_ref, wgb, wub, wdb, *, blk):
    i = pl.program_id(0)
    prev_e = be_ref[jnp.maximum(i - 1, 0)]

    @pl.when(jnp.logical_or(i == 0, be_ref[i] != prev_e))
    def _():
        wgb[...] = wg_ref[0].astype(BF16)
        wub[...] = wu_ref[0].astype(BF16)
        wdb[...] = wd_ref[0].astype(BF16)

    @pl.when(nv_ref[i] > 0)
    def _():
        rows = lax.broadcasted_iota(I32, (blk, 1), 0)
        xu = jnp.where(rows < nv_ref[i], x_ref[...], jnp.uint32(0))
        x = jnp.concatenate(_unpack_bf16_pairs(xu), axis=1).astype(BF16)
        act = (_silu(_dot(x, wgb[...])) * _dot(x, wub[...])).astype(BF16)
        y_ref[...] = _pack_bf16_pairs(_dot(act, wdb[...]))


def _experts(xs, be, nvalid, wg, wu, wd):
    n_rows, dh = xs.shape
    ne, d, f = wg.shape
    blk = EXPERT_BLOCK
    nblocks = n_rows // blk
    grid_spec = pltpu.PrefetchScalarGridSpec(
        num_scalar_prefetch=2,
        grid=(nblocks,),
        in_specs=[pl.BlockSpec((blk, dh), lambda i, be, nv: (i, 0)),
                  pl.BlockSpec((1, d, f), lambda i, be, nv: (be[i], 0, 0)),
                  pl.BlockSpec((1, d, f), lambda i, be, nv: (be[i], 0, 0)),
                  pl.BlockSpec((1, f, d), lambda i, be, nv: (be[i], 0, 0))],
        out_specs=pl.BlockSpec((blk, dh), lambda i, be, nv: (i, 0)),
        scratch_shapes=[pltpu.VMEM((d, f), BF16), pltpu.VMEM((d, f), BF16), pltpu.VMEM((f, d), BF16)],
    )
    return pl.pallas_call(
        functools.partial(_experts_kernel, blk=blk),
        grid_spec=grid_spec,
        out_shape=jax.ShapeDtypeStruct((n_rows, dh), U32),
        compiler_params=_cparams("arbitrary"),
        name="experts",
    )(be, nvalid, xs, wg, wu, wd)


def _final_kernel(dst_ref, dstn_ref, wts_ref, h2_ref, x1_ref, g2_ref, wg_ref, wu_ref, wd_ref, lng_ref, lnb_ref,
                  ys_hbm, o_ref, ybuf, sem, *, alpha, topk, tm, nt):
    i = pl.program_id(0)
    slot = lax.rem(i, 2)

    def gather_start(ids_ref, sl):
        def body(t, c):
            for k in range(topk):
                pltpu.make_async_copy(ys_hbm.at[pl.ds(ids_ref[0, k, t], 1)], ybuf.at[sl, k, pl.ds(t, 1)],
                                      sem.at[sl]).start()
            return c
        lax.fori_loop(0, tm, body, 0)

    @pl.when(i == 0)
    def _():
        gather_start(dst_ref, 0)

    for t in range(tm):
        for k in range(topk):
            pltpu.make_async_copy(ys_hbm.at[pl.ds(dstn_ref[0, k, t], 1)], ybuf.at[1 - slot, k, pl.ds(t, 1)],
                                  sem.at[1 - slot]).start()

    h = jnp.concatenate(_unpack_bf16_pairs(h2_ref[...]), axis=1).astype(BF16)
    y = _dot((_silu(_dot(h, wg_ref[...])) * _dot(h, wu_ref[...])).astype(BF16), wd_ref[...])

    def wait_slot(sl):
        def body(t, c):
            for k in range(topk):
                pltpu.make_async_copy(ys_hbm.at[pl.ds(0, 1)], ybuf.at[sl, k, pl.ds(t, 1)], sem.at[sl]).wait()
            return c
        lax.fori_loop(0, tm, body, 0)

    wait_slot(slot)
    w = wts_ref[...]
    ylo = jnp.zeros((tm, y.shape[1] // 2), F32)
    yhi = jnp.zeros((tm, y.shape[1] // 2), F32)
    for k in range(topk):
        lo, hi = _unpack_bf16_pairs(ybuf[slot, k])
        ylo = ylo + w[:, k:k + 1] * lo
        yhi = yhi + w[:, k:k + 1] * hi
    y = y + jnp.concatenate([ylo, yhi], axis=1)
    o_ref[...] = _ln(alpha * x1_ref[...] + g2_ref[0] * y) * lng_ref[...] + lnb_ref[...]

    @pl.when(i == nt - 1)
    def _():
        wait_slot(1 - slot)


def _final(ys, dest, wtsT, h2, x1, gate2, wg, wu, wd, lng, lnb, alpha, tiles_per_batch):
    n, d = x1.shape
    nt, topk, tm = dest.shape
    kern = functools.partial(_final_kernel, alpha=alpha, topk=topk, tm=tm, nt=nt)
    row = pl.BlockSpec((tm, d), lambda i: (i, 0))
    full = lambda a: pl.BlockSpec(a.shape, lambda i: (0,) * a.ndim, pipeline_mode=pl.Buffered(1))
    return pl.pallas_call(
        kern,
        grid=(nt,),
        in_specs=[pl.BlockSpec((1, topk, tm), lambda i: (i, 0, 0), memory_space=pltpu.SMEM),
                  pl.BlockSpec((1, topk, tm), lambda i: (jnp.minimum(i + 1, nt - 1), 0, 0), memory_space=pltpu.SMEM),
                  pl.BlockSpec((tm, topk), lambda i: (i, 0)),
                  pl.BlockSpec((tm, d // 2), lambda i: (i, 0)), row,
                  pl.BlockSpec((1, 1, d), lambda i: (i // tiles_per_batch, 0, 0)),
                  full(wg), full(wu), full(wd),
                  pl.BlockSpec((1, d), lambda i: (0, 0)), pl.BlockSpec((1, d), lambda i: (0, 0)),
                  pl.BlockSpec(memory_space=pl.ANY)],
        out_specs=row,
        out_shape=jax.ShapeDtypeStruct((n, d), F32),
        scratch_shapes=[pltpu.VMEM((2, topk, tm, d // 2), U32), pltpu.SemaphoreType.DMA((2,))],
        compiler_params=_cparams("arbitrary"),
        name="final",
    )(dest, dest, wtsT, h2, x1, gate2, wg, wu, wd, lng.reshape(1, d), lnb.reshape(1, d), ys)


def _proj_layout(d, qr, kvr, idim, ih, hk, hv):
    src = np.cumsum([0, qr, kvr, idim, ih, hk, hk, hv, hv, d, d])
    wa = -(-(qr + kvr + idim + ih) // LANES) * LANES
    wa = max(wa, 1 << (wa - 1).bit_length())
    pieces = [("a", wa, (int(src[0]), int(src[4]))),
              ("hq", hk, (int(src[4]), int(src[5]))), ("hf", hk, (int(src[5]), int(src[6]))),
              ("hi", hv, (int(src[6]), int(src[7]))), ("hg", hv, (int(src[7]), int(src[8]))),
              ("ga", d, (int(src[8]), int(src[9]))), ("gb", d, (int(src[9]), int(src[10])))]
    pieces.sort(key=lambda p: -p[1])
    off = 0
    layout = {}
    for name, wdt, rng in pieces:
        assert off % wdt == 0
        layout[name] = (off, wdt, rng)
        off += wdt
    return layout, off


def kernel(x, c, rpb_table, hgrn_lb_logits, ada_w, ada_b, w_in, q_norm_g, kv_norm_g, w_uq, w_uk, w_uv, w_qidx,
           idx_k_norm_g, idx_k_norm_b, hgrn_out_norm_g, w_branch_a, w_branch_b, w_o, ln1_g, ln1_b, w_router,
           router_bias, w_exp_gate, w_exp_up, w_exp_down, w_sh_gate, w_sh_up, w_sh_down, ln2_g, ln2_b):
    bsz, s, d = x.shape
    depth = ada_w.shape[0]
    qr = w_uq.shape[1]
    ah, kvr, qk = w_uk.shape[1], w_uk.shape[2], w_uk.shape[3]
    idim = idx_k_norm_g.shape[1]
    ih = w_qidx.shape[2] // idim
    dv = hgrn_out_norm_g.shape[1]
    nh = w_branch_b.shape[1] // dv
    dk = hgrn_lb_logits.shape[1] // nh
    ne = w_router.shape[2]
    topk = min(IDX_TOPK, s // 4)
    alpha = float((2 * depth) ** 0.25)
    n_tok = bsz * s

    lower_bounds = jnp.cumsum(jax.nn.softmax(hgrn_lb_logits.astype(F32), axis=0), axis=0)
    layout, wtot = _proj_layout(d, qr, kvr, idim, ih, nh * dk, nh * dv)

    for l in range(depth):
        mod = _ada(c, ada_w[l], ada_b[l])[:, None, :]
        shift1, scale1, gate1, shift2, scale2, gate2 = jnp.split(mod, 6, axis=-1)

        cols = []
        for off, wdt, (lo, hi) in sorted(layout.values()):
            cols.append(w_in[l][:, lo:hi].astype(BF16))
            if wdt > hi - lo:
                cols.append(jnp.zeros((d, wdt - (hi - lo)), BF16))
        proj = _inproj(x, scale1, shift1, jnp.concatenate(cols, axis=1))
        blk = lambda name: layout[name][0] // layout[name][1]

        qb_sz = _pick(s, (256, 128))
        q_latT, q_idxT, w_idxT, k_idx, kv_lat = _prep(
            proj, blk("a"), layout["a"][1], (qr, kvr, idim, ih, ah, qk), q_norm_g[l], kv_norm_g[l],
            idx_k_norm_g[l], idx_k_norm_b[l], jnp.swapaxes(w_uq[l], 0, 1).astype(BF16),
            w_uk[l].astype(BF16), jnp.swapaxes(w_qidx[l], 0, 1).astype(BF16), qb_sz)
        kv_latT = jnp.concatenate([jnp.swapaxes(kv_lat, 1, 2), jnp.ones((bsz, 1, s), BF16),
                                   jnp.zeros((bsz, 7, s), BF16)], axis=1)
        o_a = _dsa(q_idxT, w_idxT, q_latT, k_idx, kv_lat, kv_latT,
                   rpb_table, jnp.swapaxes(w_uv[l], 1, 2).astype(BF16), topk, qb_sz)

        o_b = _hgrn(proj, (blk("hq"), blk("hf"), blk("hi"), blk("hg")), lower_bounds[l],
                    hgrn_out_norm_g[l], nh, dk, dv)

        x1, h2, lgT = _postmix(o_a, o_b, proj, blk("ga"), blk("gb"), x, gate1, scale2, shift2,
                               w_branch_a[l].astype(BF16), w_branch_b[l].astype(BF16), w_o[l].astype(BF16),
                               jnp.swapaxes(w_router[l], 0, 1).astype(BF16), ln1_g[l], ln1_b[l], alpha)

        ids, wts, rnk, sizes_f = _route(lgT, router_bias[l])
        eb = EXPERT_BLOCK
        sizes = sizes_f[:, 0].astype(I32)
        padded = (sizes + eb - 1) // eb * eb
        pend = jnp.cumsum(padded)
        pstart = pend - padded
        n_rows = -(-(n_tok * MOE_TOPK + ne * (eb - 1)) // eb) * eb
        blk_start = jnp.arange(n_rows // eb, dtype=I32) * eb
        blk_expert = jnp.minimum(jnp.sum((pend[None, :] <= blk_start[:, None]).astype(I32), axis=1), ne - 1)
        onehot = (blk_expert[:, None] == jnp.arange(ne, dtype=I32)[None, :]).astype(I32)
        blk_size = jnp.sum(onehot * sizes[None, :], axis=1)
        blk_pstart = jnp.sum(onehot * pstart[None, :], axis=1)
        nvalid = jnp.clip(blk_size - (blk_start - blk_pstart), 0, eb).astype(I32)
        dest = _dest(pstart.astype(I32), ids, rnk, ne)
        tm = _pick(s, (128,))
        dest_t = dest.reshape(bsz, MOE_TOPK, s // tm, tm).transpose(0, 2, 1, 3).reshape(n_tok // tm, MOE_TOPK, tm)
        wtsT = jnp.swapaxes(wts, 1, 2).reshape(n_tok, MOE_TOPK)

        h2f = h2.reshape(n_tok, d // 2)
        xs = _dispatch(dest_t, h2f, n_rows)
        ys = _experts(xs, blk_expert, nvalid, w_exp_gate[l], w_exp_up[l], w_exp_down[l])
        x = _final(ys, dest_t, wtsT, h2f, x1.reshape(n_tok, d), gate2, w_sh_gate[l].astype(BF16),
                   w_sh_up[l].astype(BF16), w_sh_down[l].astype(BF16), ln2_g[l], ln2_b[l], alpha,
                   s // tm).reshape(bsz, s, d)
    return x
```

```python
import functools
import math

import numpy as np
import jax
import jax.numpy as jnp
from jax import lax
from jax.experimental import pallas as pl
from jax.experimental.pallas import tpu as pltpu

F32 = jnp.float32
BF16 = jnp.bfloat16
I32 = jnp.int32
U32 = jnp.uint32

EPS = 1e-6
IDX_TOPK = 256
RPB_MAX_DIST = 128
MOE_TOPK = 8
N_GROUPS = 8
TOPK_GROUPS = 4
ROUTED_SCALE = 2.5
EXPERT_BLOCK = 512

V7X_VMEM_LIMIT_BYTES = 56 * 1024 * 1024
LANES = 128
INT_MIN = -2 ** 31
NEG_BIG = -1e30
LOG2E = math.log2(math.e)


def _cparams(*sem):
    return pltpu.CompilerParams(dimension_semantics=tuple(sem), vmem_limit_bytes=V7X_VMEM_LIMIT_BYTES)


def _pick(n, prefs):
    for p in prefs:
        if n % p == 0:
            return p
    return n


def _sigmoid(v):
    return 1.0 / (1.0 + jnp.exp(-v))


def _silu(v):
    return v * _sigmoid(v)


def _ln(v):
    mu = jnp.mean(v, axis=-1, keepdims=True)
    d = v - mu
    var = jnp.mean(d * d, axis=-1, keepdims=True)
    return d * lax.rsqrt(var + EPS)


def _rms(v):
    return v * lax.rsqrt(jnp.mean(v * v, axis=-1, keepdims=True) + EPS)


def _dot(a, b):
    return jnp.dot(a, b, preferred_element_type=F32)


def _pack_bf16_pairs(v):
    n = v.shape[1] // 2
    lo = pltpu.bitcast(v[:, :n].astype(BF16).astype(F32), U32) >> 16
    hi = pltpu.bitcast(v[:, n:].astype(BF16).astype(F32), U32) & jnp.uint32(0xFFFF0000)
    return lo | hi


def _unpack_bf16_pairs(u):
    return pltpu.bitcast(u << 16, F32), pltpu.bitcast(u & jnp.uint32(0xFFFF0000), F32)


def _dot_nt(a, b):
    return lax.dot_general(a, b, (((1,), (1,)), ((), ())), preferred_element_type=F32)


def _dot_tn(a, b):
    return lax.dot_general(a, b, (((0,), (0,)), ((), ())), preferred_element_type=F32)


def _ada_kernel(c_ref, w_ref, b_ref, o_ref):
    c = c_ref[...]
    o_ref[...] = _dot(_silu(c).astype(BF16), w_ref[...].astype(BF16)) + b_ref[...]


def _ada(c, w, b):
    bsz, d = c.shape
    n = w.shape[1]
    rows = 8
    cp = jnp.zeros((rows, d), F32).at[:bsz].set(c)
    tn = _pick(n, (1024, 512, 256, 128))
    out = pl.pallas_call(
        _ada_kernel,
        grid=(n // tn,),
        in_specs=[pl.BlockSpec((rows, d), lambda j: (0, 0)),
                  pl.BlockSpec((d, tn), lambda j: (0, j)),
                  pl.BlockSpec((1, tn), lambda j: (0, j))],
        out_specs=pl.BlockSpec((rows, tn), lambda j: (0, j)),
        out_shape=jax.ShapeDtypeStruct((rows, n), F32),
        compiler_params=_cparams("arbitrary"),
        name="ada",
    )(cp, w, b.reshape(1, n))
    return out[:bsz]


def _inproj_kernel(x_ref, sc_ref, sh_ref, w_ref, o_ref, h_ref):
    @pl.when(pl.program_id(2) == 0)
    def _():
        h = _ln(x_ref[0]) * (1.0 + sc_ref[0]) + sh_ref[0]
        h_ref[...] = h.astype(BF16)

    o_ref[0] = _dot(h_ref[...], w_ref[...]).astype(o_ref.dtype)


def _inproj(x, scale, shift, w):
    bsz, s, d = x.shape
    n = w.shape[1]
    tm = _pick(s, (1024, 512, 256, 128))
    tn = _pick(n, (1024, 512, 256, 128))
    return pl.pallas_call(
        _inproj_kernel,
        grid=(bsz, s // tm, n // tn),
        in_specs=[pl.BlockSpec((1, tm, d), lambda b, i, j: (b, i, 0)),
                  pl.BlockSpec((1, 1, d), lambda b, i, j: (b, 0, 0)),
                  pl.BlockSpec((1, 1, d), lambda b, i, j: (b, 0, 0)),
                  pl.BlockSpec((d, tn), lambda b, i, j: (0, j))],
        out_specs=pl.BlockSpec((1, tm, tn), lambda b, i, j: (b, i, j)),
        out_shape=jax.ShapeDtypeStruct((bsz, s, n), BF16),
        scratch_shapes=[pltpu.VMEM((tm, d), BF16)],
        compiler_params=_cparams("arbitrary", "arbitrary", "arbitrary"),
        name="inproj",
    )(x, scale, shift, w)


def _prep_kernel(a_ref, qg_ref, kvg_ref, ikg_ref, ikb_ref, wuqT_ref, wuk_ref, wqiT_ref,
                 qlatT_ref, qidxT_ref, widxT_ref, kidx_ref, kv_ref, *, qr, kvr, idim, ih, ah, qk):
    a = a_ref[0]
    af = a.astype(F32)
    cq = (_rms(af[:, :qr]) * qg_ref[...]).astype(BF16)
    ckv = af[:, qr:qr + kvr]
    ki = af[:, qr + kvr:qr + kvr + idim]
    kv_ref[0] = (_rms(ckv) * kvg_ref[...]).astype(BF16)
    kidx_ref[0] = (_ln(ki) * ikg_ref[...] + ikb_ref[...]).astype(BF16)
    tail = a[:, qr + kvr:qr + kvr + LANES]
    eye = (lax.broadcasted_iota(I32, (LANES, LANES), 0) == lax.broadcasted_iota(I32, (LANES, LANES), 1))
    tailT = _dot_nt(jnp.where(eye, 1.0, 0.0).astype(BF16), tail)
    widxT_ref[0] = tailT[idim:idim + ih] * float((ih * idim) ** -0.5)
    tm = a.shape[0]
    qT = _dot_nt(wuqT_ref[...], cq)
    for h in range(ah):
        qh = qT[h * qk:(h + 1) * qk].astype(BF16)
        qlatT_ref[0, 0, :, h * tm:(h + 1) * tm] = (_dot(wuk_ref[h], qh) * float(qk ** -0.5 * LOG2E)).astype(BF16)
    qiT = _dot_nt(wqiT_ref[...], cq)
    for h in range(ih):
        qidxT_ref[0, 0, :, h * tm:(h + 1) * tm] = qiT[h * idim:(h + 1) * idim].astype(BF16)


def _prep(proj, a_blk, wa, dims, q_norm_g, kv_norm_g, ikg, ikb, w_uqT, w_uk, w_qidxT, tm):
    bsz, s, _ = proj.shape
    qr, kvr, idim, ih, ah, qk = dims
    assert idim + ih <= LANES and qr + kvr + LANES <= wa
    kern = functools.partial(_prep_kernel, qr=qr, kvr=kvr, idim=idim, ih=ih, ah=ah, qk=qk)
    full = lambda shape: pl.BlockSpec(shape, lambda b, i: (0,) * len(shape))
    return pl.pallas_call(
        kern,
        grid=(bsz, s // tm),
        in_specs=[pl.BlockSpec((1, tm, wa), lambda b, i: (b, i, a_blk)),
                  full((1, qr)), full((1, kvr)), full((1, idim)), full((1, idim)),
                  full(w_uqT.shape), full(w_uk.shape), full(w_qidxT.shape)],
        out_specs=[pl.BlockSpec((1, 1, kvr, ah * tm), lambda b, i: (b, i, 0, 0)),
                   pl.BlockSpec((1, 1, idim, ih * tm), lambda b, i: (b, i, 0, 0)),
                   pl.BlockSpec((1, ih, tm), lambda b, i: (b, 0, i)),
                   pl.BlockSpec((1, tm, idim), lambda b, i: (b, i, 0)),
                   pl.BlockSpec((1, tm, kvr), lambda b, i: (b, i, 0))],
        out_shape=[jax.ShapeDtypeStruct((bsz, s // tm, kvr, ah * tm), BF16),
                   jax.ShapeDtypeStruct((bsz, s // tm, idim, ih * tm), BF16),
                   jax.ShapeDtypeStruct((bsz, ih, s), F32),
                   jax.ShapeDtypeStruct((bsz, s, idim), BF16),
                   jax.ShapeDtypeStruct((bsz, s, kvr), BF16)],
        compiler_params=_cparams("arbitrary", "arbitrary"),
        name="prep",
    )(proj, q_norm_g.reshape(1, qr), kv_norm_g.reshape(1, kvr), ikg.reshape(1, idim), ikb.reshape(1, idim),
      w_uqT, w_uk, w_qidxT)


def _dsa_kernel(qidxT_ref, widxT_ref, qlatT_ref, kidx_ref, kv_ref, kvT_ref, bkt0_ref, bkt1_ref, rel_ref, wuvT_ref,
                o_ref, keys_ref, gmax_ref, s_ref, p_ref, m_ref, a_ref, acc_ref, t0_ref, t1_ref, lstrict_ref, taken_ref, s2_ref, p2_ref, a2_ref,
                *, qb_sz, topk, ih, ah, c, nb):
    QB = qb_sz
    qb = pl.program_id(1)
    nchunks = qb + 1
    qpos = qb * QB + lax.broadcasted_iota(I32, (1, QB), 1)
    kpos = lax.broadcasted_iota(I32, (QB, 1), 0)
    wT = widxT_ref[0]
    hcols = lambda h: slice(h * QB, (h + 1) * QB)

    @pl.when(jnp.logical_and(pl.program_id(0) == 0, qb == 0))
    def _():
        for bkt_ref, t_ref in ((bkt0_ref, t0_ref), (bkt1_ref, t1_ref)):
            bkt = bkt_ref[...]
            for h in range(ah):
                tile = jnp.zeros((QB, QB), F32)
                for b in range(nb - 1):
                    tile = jnp.where(bkt == b, rel_ref[b, h], tile)
                t_ref[:, hcols(h)] = tile
        lstrict_ref[...] = jnp.where(lax.broadcasted_iota(I32, (QB, QB), 1) < lax.broadcasted_iota(I32, (QB, QB), 0),
                                     1.0, 0.0).astype(BF16)

    def score_chunk(kc, carry):
        off = pl.multiple_of(kc * QB, QB)
        s_ref[:, :ih * QB] = _dot(kidx_ref[0, pl.ds(off, QB), :], qidxT_ref[0, 0])
        acc = jnp.zeros((QB, QB), F32)
        for h in range(ih):
            acc = acc + wT[h:h + 1] * jnp.maximum(s_ref[:, hcols(h)], 0.0)
        bits = pltpu.bitcast(acc, I32)
        skey = bits ^ ((bits >> 31) & 0x7FFFFFFF)
        causal = (off + kpos) <= qpos
        skey = jnp.where(causal, skey, INT_MIN)
        keys_ref[pl.ds(off, QB), :] = skey
        gmax_ref[...] = jnp.maximum(gmax_ref[...], skey)
        return carry

    gmax_ref[...] = jnp.full(gmax_ref.shape, INT_MIN, I32)
    lax.fori_loop(0, nchunks, score_chunk, 0)

    gmax = gmax_ref[...]
    lo0 = jnp.min(gmax, axis=0, keepdims=True) if QB >= topk else jnp.full((1, QB), INT_MIN, I32)
    hi0 = jnp.max(gmax, axis=0, keepdims=True) + 1

    def count_ge(cand):
        def body(kc, cnt):
            off = pl.multiple_of(kc * QB, QB)
            hit = jnp.where(keys_ref[pl.ds(off, QB), :] >= cand, 1, 0)
            return cnt + jnp.sum(hit.reshape(QB // 8, 8, QB), axis=0)

        cnt = lax.fori_loop(0, nchunks, body, jnp.zeros((8, QB), I32))
        return jnp.sum(cnt.astype(F32), axis=0, keepdims=True)

    def bis_cond(st):
        it, _, _, done = st
        return jnp.logical_and(it < 34, jnp.min(done) < 0.5)

    def bis_body(st):
        it, lo, hi, done = st
        cand = (lo >> 1) + (hi >> 1) + (lo & hi & 1)
        cnt = count_ge(cand)
        ge = cnt >= float(topk)
        conv = cand == lo
        fin = jnp.logical_or(conv, cnt == float(topk))
        thr_new = jnp.where(conv, lo, cand)
        lo = jnp.where(fin, thr_new, jnp.where(ge, cand, lo))
        hi = jnp.where(fin, thr_new + 1, jnp.where(ge, hi, cand))
        return it + 1, lo, hi, jnp.where(fin, 1.0, done)

    _, thr, _, _ = lax.while_loop(bis_cond, bis_body, (jnp.int32(0), lo0, hi0, jnp.zeros((1, QB), F32)))
    need = float(topk) - count_ge(thr + 1)

    m_ref[...] = jnp.full(m_ref.shape, NEG_BIG, F32)
    acc_ref[...] = jnp.zeros(acc_ref.shape, F32)

    def attn_chunk(off, bias_ref, diag, taken, s_ref=s_ref, p_ref=p_ref, a_ref=a_ref):
        kt = keys_ref[pl.ds(off, QB), :]
        eq = kt == thr
        eq01 = jnp.where(eq, 1.0, 0.0)
        before = _dot(lstrict_ref[...], eq01.astype(BF16)) + taken
        sel = jnp.logical_or(kt > thr, jnp.logical_and(eq, before < need))
        taken = taken + jnp.sum(eq01, axis=0, keepdims=True)
        if diag:
            sel = jnp.logical_and(sel, (off + kpos) <= qpos)
        madd = jnp.where(sel, 0.0, NEG_BIG)
        s_ref[:, :ah * QB] = _dot(kv_ref[0, pl.ds(off, QB), :], qlatT_ref[0, 0])
        for g in range(ah * QB // LANES):
            cols = slice(g * LANES, (g + 1) * LANES)
            qcols = slice(g * LANES % QB, g * LANES % QB + LANES)
            s = s_ref[:, cols] + madd[:, qcols]
            if bias_ref is not None:
                s = s + bias_ref[:, cols]
            m_prev = m_ref[:, cols]
            m_new = jnp.maximum(m_prev, jnp.max(s, axis=0, keepdims=True))
            p_ref[:, cols] = jnp.exp2(s - m_new).astype(BF16)
            a_ref[:, cols] = jnp.exp2(m_prev - m_new)
            m_ref[:, cols] = m_new
        acc_ref[...] = a_ref[...] * acc_ref[...] + _dot(kvT_ref[0, :, pl.ds(off, QB)], p_ref[...])
        return taken

    def far_pair(kp, taken):
        off = pl.multiple_of(kp * (2 * QB), 2 * QB)
        taken = attn_chunk(off, None, False, taken)
        return attn_chunk(off + QB, None, False, taken, s2_ref, p2_ref, a2_ref)

    nfar = qb - 1
    taken_ref[...] = lax.fori_loop(0, nfar // 2, far_pair, jnp.zeros((1, QB), F32))

    @pl.when(jnp.logical_and(nfar >= 1, lax.rem(nfar, 2) == 1))
    def _():
        taken_ref[...] = attn_chunk(pl.multiple_of((nfar - 1) * QB, QB), None, False, taken_ref[...])

    @pl.when(qb >= 1)
    def _():
        taken_ref[...] = attn_chunk(pl.multiple_of((qb - 1) * QB, QB), t1_ref, False, taken_ref[...])

    attn_chunk(pl.multiple_of(qb * QB, QB), t0_ref, True, taken_ref[...])

    outs = []
    for h in range(ah):
        o = (acc_ref[:c, hcols(h)] / acc_ref[c:c + 1, hcols(h)]).astype(BF16)
        outs.append(_dot(wuvT_ref[h], o))
    o_ref[0] = jnp.concatenate(outs, axis=0).T.astype(o_ref.dtype)


def _t5_bucket_np(d, nbuckets):
    max_exact = nbuckets // 2
    dd = np.maximum(d, 1).astype(np.float32)
    large = max_exact + (np.log(dd / np.float32(max_exact)) / np.float32(math.log(RPB_MAX_DIST / max_exact))
                         * np.float32(nbuckets - max_exact)).astype(np.int32)
    large = np.minimum(large, nbuckets - 1)
    return np.where(d < max_exact, d, large).astype(np.int32)


def _dsa(q_idxT, w_idxT, q_latT, k_idx, kv_lat, kv_latT, rpb_table, w_uvT, topk, QB):
    bsz, nqb, idim, ihq = q_idxT.shape
    c, ahq = q_latT.shape[2], q_latT.shape[3]
    ca = kv_latT.shape[1]
    ih, ah = ihq // QB, ahq // QB
    s = nqb * QB
    vd = w_uvT.shape[1]
    nb = rpb_table.shape[0]
    assert QB >= RPB_MAX_DIST
    j = np.arange(QB)[:, None]
    i = np.arange(QB)[None, :]
    bkt0 = jnp.asarray(_t5_bucket_np(np.maximum(i - j, 0), nb))
    bkt1 = jnp.asarray(_t5_bucket_np(QB + i - j, nb))
    assert int(_t5_bucket_np(np.array([RPB_MAX_DIST]), nb)[0]) == nb - 1
    rel = (rpb_table.astype(F32) - rpb_table[nb - 1].astype(F32)[None, :]) * LOG2E
    kern = functools.partial(_dsa_kernel, qb_sz=QB, topk=topk, ih=ih, ah=ah, c=c, nb=nb)
    const = lambda shape: pl.BlockSpec(shape, lambda b, i: (0,) * len(shape), pipeline_mode=pl.Buffered(1))
    hw = max(ih, ah) * QB
    return pl.pallas_call(
        kern,
        grid=(bsz, nqb),
        in_specs=[pl.BlockSpec((1, 1, idim, ih * QB), lambda b, i: (b, i, 0, 0)),
                  pl.BlockSpec((1, ih, QB), lambda b, i: (b, 0, i)),
                  pl.BlockSpec((1, 1, c, ah * QB), lambda b, i: (b, i, 0, 0)),
                  pl.BlockSpec((1, s, idim), lambda b, i: (b, 0, 0), pipeline_mode=pl.Buffered(1)),
                  pl.BlockSpec((1, s, c), lambda b, i: (b, 0, 0), pipeline_mode=pl.Buffered(1)),
                  pl.BlockSpec((1, ca, s), lambda b, i: (b, 0, 0), pipeline_mode=pl.Buffered(1)),
                  const((QB, QB)), const((QB, QB)),
                  pl.BlockSpec(memory_space=pltpu.SMEM), const(w_uvT.shape)],
        out_specs=pl.BlockSpec((1, QB, ah * vd), lambda b, i: (b, i, 0)),
        out_shape=jax.ShapeDtypeStruct((bsz, s, ah * vd), BF16),
        scratch_shapes=[pltpu.VMEM((s, QB), I32),
                        pltpu.VMEM((QB, QB), I32),
                        pltpu.VMEM((QB, hw), F32),
                        pltpu.VMEM((QB, ah * QB), BF16),
                        pltpu.VMEM((1, ah * QB), F32),
                        pltpu.VMEM((1, ah * QB), F32),
                        pltpu.VMEM((ca, ah * QB), F32),
                        pltpu.VMEM((QB, ah * QB), F32),
                        pltpu.VMEM((QB, ah * QB), F32),
                        pltpu.VMEM((QB, QB), BF16),
                        pltpu.VMEM((1, QB), F32),
                        pltpu.VMEM((QB, ah * QB), F32),
                        pltpu.VMEM((QB, ah * QB), BF16),
                        pltpu.VMEM((1, ah * QB), F32)],
        compiler_params=_cparams("arbitrary", "arbitrary"),
        name="dsa",
    )(q_idxT, w_idxT, q_latT, k_idx, kv_lat, kv_latT, bkt0, bkt1, rel, w_uvT)


HGRN_CHUNK = 64
HGRN_SUB = 16
HGRN_EXP_CLAMP = 80.0


def _dot_exact(a, b, dims):
    return lax.dot_general(a, b, (dims, ((), ())), precision=lax.Precision.HIGHEST, preferred_element_type=F32)


def _hgrn_kernel(hq_ref, hf_ref, hi_ref, hg_ref, lb_ref, g_ref, o_ref, st_ref, st0_ref, *, nh, dk, dv, tt):
    C, SUB = HGRN_CHUNK, HGRN_SUB

    @pl.when(pl.program_id(1) == 0)
    def _():
        st_ref[...] = jnp.zeros(st_ref.shape, F32)

    r = lax.broadcasted_iota(I32, (C, C), 0)
    cc = lax.broadcasted_iota(I32, (C, C), 1)
    tri_mask = r >= cc
    tri = jnp.where(tri_mask, 1.0, 0.0).astype(BF16)
    g = g_ref[...]
    st0_ref[...] = st_ref[...]
    decay = jnp.zeros((1, nh * dk), F32)
    lb = lb_ref[...]

    for c in range(tt // C):
        rows = pl.ds(c * C, C)
        f = lb + (1.0 - lb) * _sigmoid(hf_ref[0, rows, :].astype(F32))
        lf = jnp.log(f)
        t1 = lf.astype(BF16)
        r1 = lf - t1.astype(F32)
        t2 = r1.astype(BF16)
        t3 = (r1 - t2.astype(F32)).astype(BF16)
        b = _dot(tri, t1) + _dot(tri, t2) + _dot(tri, t3)
        kk = 1.0 - f
        hq = hq_ref[0, rows, :].astype(F32)
        q = _silu(hq) * float(dk ** -0.5)
        qe = (q * jnp.exp(b)).astype(BF16)
        b_last = b[C - 1:C]
        k_dec = (kk * jnp.exp(b_last - b)).astype(BF16)
        dec_last = jnp.exp(b_last)
        qs, ks = [], []
        for i in range(C // SUB):
            lo, n = i * SUB, (i + 1) * SUB
            bi = b[lo - 1:lo] if i > 0 else jnp.zeros((1, nh * dk), F32)
            decay = jnp.maximum(decay, bi - b[n - 1:n])
            qs.append((q[lo:n] * jnp.exp(b[lo:n] - bi)).astype(BF16))
            ks.append((kk * jnp.exp(jnp.minimum(bi - b, HGRN_EXP_CLAMP))).astype(BF16))
        v_all = hi_ref[0, rows, :]
        outs = []
        for h in range(nh):
            kc = slice(h * dk, (h + 1) * dk)
            v = v_all[:, h * dv:(h + 1) * dv]
            stT = st_ref[h]
            att = jnp.concatenate([_dot_nt(qs[i][:, kc], ks[i][:, kc]) for i in range(C // SUB)], axis=0)
            att = jnp.where(tri_mask, att, 0.0)
            o = _dot_nt(qe[:, kc], stT.astype(BF16)) + _dot(att.astype(BF16), v)
            st_ref[h] = stT * dec_last[:, kc] + _dot_tn(v, k_dec[:, kc])
            outs.append(_rms(o) * g)
        o_all = jnp.concatenate(outs, axis=1) * _silu(hg_ref[0, rows, :].astype(F32))
        o_ref[0, rows, :] = o_all.astype(o_ref.dtype)

    @pl.when(jnp.max(decay) > HGRN_EXP_CLAMP)
    def _():
        row = lax.broadcasted_iota(I32, (SUB, 1), 0)

        def head(h, carry):
            ko = pl.multiple_of(h * dk, dk)
            vo = pl.multiple_of(h * dv, dv)
            lb = lb_ref[:, pl.ds(ko, dk)]

            def slab(j, S):
                rows = pl.ds(pl.multiple_of(j * SUB, SUB), SUB)
                f = lb + (1.0 - lb) * _sigmoid(hf_ref[0, rows, pl.ds(ko, dk)].astype(F32))
                kk = 1.0 - f
                hq = hq_ref[0, rows, pl.ds(ko, dk)].astype(F32)
                q = _silu(hq) * float(dk ** -0.5)
                v = hi_ref[0, rows, pl.ds(vo, dv)].astype(F32)
                o = jnp.zeros((SUB, dv), F32)
                for r in range(SUB):
                    S = S * f[r:r + 1] + _dot_exact(jnp.where(row == r, v, 0.0), kk, ((0,), (0,)))
                    o = jnp.where(row == r, _dot_exact(q, S, ((1,), (1,))), o)
                hg = hg_ref[0, rows, pl.ds(vo, dv)].astype(F32)
                o_ref[0, rows, pl.ds(vo, dv)] = (_rms(o) * g * _silu(hg)).astype(o_ref.dtype)
                return S

            st_ref[h] = lax.fori_loop(0, tt // SUB, slab, st0_ref[h])
            return carry

        lax.fori_loop(0, nh, head, 0)


def _hgrn(proj, blks, lb, g, nh, dk, dv):
    bsz, s, _ = proj.shape
    tt = _pick(s, (256, 128, 64))
    kern = functools.partial(_hgrn_kernel, nh=nh, dk=dk, dv=dv, tt=tt)
    col = lambda blk, wdt: pl.BlockSpec((1, tt, wdt), lambda b, i: (b, i, blk))
    return pl.pallas_call(
        kern,
        grid=(bsz, s // tt),
        in_specs=[col(blks[0], nh * dk), col(blks[1], nh * dk), col(blks[2], nh * dv), col(blks[3], nh * dv),
                  pl.BlockSpec((1, nh * dk), lambda b, i: (0, 0)),
                  pl.BlockSpec((1, dv), lambda b, i: (0, 0))],
        out_specs=pl.BlockSpec((1, tt, nh * dv), lambda b, i: (b, i, 0)),
        out_shape=jax.ShapeDtypeStruct((bsz, s, nh * dv), BF16),
        scratch_shapes=[pltpu.VMEM((nh, dv, dk), F32), pltpu.VMEM((nh, dv, dk), F32)],
        compiler_params=_cparams("arbitrary", "arbitrary"),
        name="hgrn",
    )(proj, proj, proj, proj, lb.reshape(1, nh * dk), g.reshape(1, dv))


def _postmix_kernel(oa_ref, ob_ref, ga_ref, gb_ref, x_ref, g1_ref, sc2_ref, sh2_ref,
                    wa_ref, wb_ref, wo_ref, wrT_ref, lng_ref, lnb_ref,
                    x1_ref, h2_ref, lgT_ref, *, alpha):
    ya = _dot(oa_ref[0], wa_ref[...])
    yb = _dot(ob_ref[0], wb_ref[...])
    mix = _sigmoid(ga_ref[0].astype(F32)) * ya + _sigmoid(gb_ref[0].astype(F32)) * yb
    mixed = _dot(mix.astype(BF16), wo_ref[...])
    x1 = _ln(alpha * x_ref[0] + g1_ref[0] * mixed) * lng_ref[...] + lnb_ref[...]
    x1_ref[0] = x1
    h2 = _ln(x1) * (1.0 + sc2_ref[0]) + sh2_ref[0]
    h2_ref[0] = _pack_bf16_pairs(h2)
    lgT_ref[0] = _dot_nt(wrT_ref[...], h2.astype(BF16))


def _postmix(o_a, o_b, proj, ga_blk, gb_blk, x, gate1, scale2, shift2, wa, wb, wo, wrT, lng, lnb, alpha):
    bsz, s, d = x.shape
    ne = wrT.shape[0]
    tm = _pick(s, (256, 128))
    kern = functools.partial(_postmix_kernel, alpha=alpha)
    row = lambda wdt: pl.BlockSpec((1, tm, wdt), lambda b, i: (b, i, 0))
    vec = pl.BlockSpec((1, 1, d), lambda b, i: (b, 0, 0))
    full = lambda a: pl.BlockSpec(a.shape, lambda b, i: (0,) * a.ndim, pipeline_mode=pl.Buffered(1))
    return pl.pallas_call(
        kern,
        grid=(bsz, s // tm),
        in_specs=[row(o_a.shape[2]), row(o_b.shape[2]),
                  pl.BlockSpec((1, tm, d), lambda b, i: (b, i, ga_blk)),
                  pl.BlockSpec((1, tm, d), lambda b, i: (b, i, gb_blk)),
                  row(d), vec, vec, vec,
                  full(wa), full(wb), full(wo), full(wrT),
                  pl.BlockSpec((1, d), lambda b, i: (0, 0)), pl.BlockSpec((1, d), lambda b, i: (0, 0))],
        out_specs=[row(d), row(d // 2), pl.BlockSpec((1, ne, tm), lambda b, i: (b, 0, i))],
        out_shape=[jax.ShapeDtypeStruct((bsz, s, d), F32),
                   jax.ShapeDtypeStruct((bsz, s, d // 2), U32),
                   jax.ShapeDtypeStruct((bsz, ne, s), F32)],
        compiler_params=_cparams("arbitrary", "arbitrary"),
        name="postmix",
    )(o_a, o_b, proj, proj, x, gate1, scale2, shift2, wa, wb, wo, wrT, lng.reshape(1, d), lnb.reshape(1, d))


def _route_kernel(lg_ref, bias_ref, ids_ref, wts_ref, rnk_ref, sizes_ref, upper_ref, carry_ref, *, ne):
    first = jnp.logical_and(pl.program_id(0) == 0, pl.program_id(1) == 0)
    tn = lg_ref.shape[2]

    @pl.when(first)
    def _():
        carry_ref[...] = jnp.zeros(carry_ref.shape, F32)
        r_ = lax.broadcasted_iota(I32, (tn, tn), 0)
        c_ = lax.broadcasted_iota(I32, (tn, tn), 1)
        upper_ref[...] = jnp.where(r_ < c_, 1.0, 0.0).astype(BF16)

    per = ne // N_GROUPS
    s = _sigmoid(lg_ref[0])
    bz = s + bias_ref[...]
    ridx = lax.broadcasted_iota(I32, (per, tn), 0)
    neg_inf = jnp.float32(-jnp.inf)
    gs = []
    for g in range(N_GROUPS):
        blk = bz[g * per:(g + 1) * per]
        m1 = jnp.max(blk, axis=0, keepdims=True)
        first_hit = jnp.min(jnp.where(blk == m1, ridx, per), axis=0, keepdims=True)
        m2 = jnp.max(jnp.where(ridx == first_hit, neg_inf, blk), axis=0, keepdims=True)
        gs.append(m1 + m2)
    emask_rows = []
    for g in range(N_GROUPS):
        rank = jnp.zeros((1, tn), I32)
        for g2 in range(N_GROUPS):
            if g2 == g:
                continue
            beats = (gs[g2] > gs[g]) if g2 > g else (gs[g2] >= gs[g])
            rank = rank + jnp.where(beats, 1, 0)
        emask_rows.append(jnp.broadcast_to(rank < TOPK_GROUPS, (per, tn)))
    emask = jnp.concatenate(emask_rows, axis=0)
    masked = jnp.where(emask, bz, neg_inf)
    eidx = lax.broadcasted_iota(I32, (ne, tn), 0)
    rank = jnp.zeros((ne, tn), I32)
    for e2 in range(ne):
        row = masked[e2:e2 + 1]
        beats = jnp.logical_or(row > masked, jnp.logical_and(row == masked, e2 < eidx))
        rank = rank + jnp.where(beats, 1, 0)
    sel = rank < MOE_TOPK
    sel01 = jnp.where(sel, 1.0, 0.0)
    denom = jnp.sum(jnp.where(sel, s, 0.0), axis=0, keepdims=True)
    wn = s / denom * ROUTED_SCALE
    before = _dot(sel01.astype(BF16), upper_ref[...]) + carry_ref[:, 0:1]
    ids, wts, rnk = [], [], []
    for k in range(MOE_TOPK):
        hit = rank == k
        ids.append(jnp.sum(jnp.where(hit, eidx, 0), axis=0, keepdims=True))
        wts.append(jnp.sum(jnp.where(hit, wn, 0.0), axis=0, keepdims=True))
        rnk.append(jnp.sum(jnp.where(hit, before, 0.0), axis=0, keepdims=True))
    ids_ref[0] = jnp.concatenate(ids, axis=0)
    wts_ref[0] = jnp.concatenate(wts, axis=0)
    rnk_ref[0] = jnp.concatenate(rnk, axis=0).astype(I32)
    carry_ref[...] = carry_ref[...] + jnp.sum(sel01, axis=1, keepdims=True)
    sizes_ref[...] = carry_ref[...]


def _route(lgT, bias):
    bsz, ne, s = lgT.shape
    tn = _pick(s, (1024, 512, 256, 128))
    kern = functools.partial(_route_kernel, ne=ne)
    slot = pl.BlockSpec((1, MOE_TOPK, tn), lambda b, j: (b, 0, j))
    return pl.pallas_call(
        kern,
        grid=(bsz, s // tn),
        in_specs=[pl.BlockSpec((1, ne, tn), lambda b, j: (b, 0, j)),
                  pl.BlockSpec((ne, 1), lambda b, j: (0, 0))],
        out_specs=[slot, slot, slot, pl.BlockSpec((ne, LANES), lambda b, j: (0, 0))],
        out_shape=[jax.ShapeDtypeStruct((bsz, MOE_TOPK, s), I32),
                   jax.ShapeDtypeStruct((bsz, MOE_TOPK, s), F32),
                   jax.ShapeDtypeStruct((bsz, MOE_TOPK, s), I32),
                   jax.ShapeDtypeStruct((ne, LANES), F32)],
        scratch_shapes=[pltpu.VMEM((tn, tn), BF16), pltpu.VMEM((ne, LANES), F32)],
        compiler_params=_cparams("arbitrary", "arbitrary"),
        name="route",
    )(lgT, bias.reshape(ne, 1))


def _dest_kernel(pstart_ref, ids_ref, rnk_ref, o_ref, *, ne):
    ids = ids_ref[0]
    base = jnp.zeros(ids.shape, I32)
    for e in range(ne):
        base = jnp.where(ids == e, pstart_ref[e], base)
    o_ref[0] = base + rnk_ref[0]


def _dest(pstart, ids, rnk, ne):
    bsz, k, s = ids.shape
    tn = _pick(s, (2048, 1024, 512, 256, 128))
    blk = lambda: pl.BlockSpec((1, k, tn), lambda b, j, ps: (b, 0, j))
    return pl.pallas_call(
        functools.partial(_dest_kernel, ne=ne),
        grid_spec=pltpu.PrefetchScalarGridSpec(num_scalar_prefetch=1, grid=(bsz, s // tn),
                                               in_specs=[blk(), blk()], out_specs=blk()),
        out_shape=jax.ShapeDtypeStruct((bsz, k, s), I32),
        compiler_params=_cparams("arbitrary", "arbitrary"),
        name="dest",
    )(pstart, ids, rnk)


def _dispatch_kernel(dst_ref, h_ref, xs_hbm, sem, *, tm, topk):
    def start(t, c):
        for k in range(topk):
            pltpu.make_async_copy(h_ref.at[pl.ds(t, 1)], xs_hbm.at[pl.ds(dst_ref[0, k, t], 1)], sem.at[0]).start()
        return c

    lax.fori_loop(0, tm, start, 0)

    def wait(t, c):
        for k in range(topk):
            pltpu.make_async_copy(h_ref.at[pl.ds(t, 1)], xs_hbm.at[pl.ds(0, 1)], sem.at[0]).wait()
        return c

    lax.fori_loop(0, tm, wait, 0)


def _dispatch(dest, h2, n_rows):
    n, d = h2.shape
    nt, topk, tm = dest.shape
    return pl.pallas_call(
        functools.partial(_dispatch_kernel, tm=tm, topk=topk),
        grid=(nt,),
        in_specs=[pl.BlockSpec((1, topk, tm), lambda i: (i, 0, 0), memory_space=pltpu.SMEM),
                  pl.BlockSpec((tm, d), lambda i: (i, 0))],
        out_specs=pl.BlockSpec(memory_space=pl.ANY),
        out_shape=jax.ShapeDtypeStruct((n_rows, d), h2.dtype),
        scratch_shapes=[pltpu.SemaphoreType.DMA((1,))],
        compiler_params=_cparams("arbitrary"),
        name="dispatch",
    )(dest, h2)


def _experts_kernel(be_ref, nv_ref, x_ref, wg_ref, wu_ref, wd_ref, y_ref, wgb, wub, wdb, *, blk):
    i = pl.program_id(0)
    prev_e = be_ref[jnp.maximum(i - 1, 0)]

    @pl.when(jnp.logical_or(i == 0, be_ref[i] != prev_e))
    def _():
        wgb[...] = wg_ref[0].astype(BF16)
        wub[...] = wu_ref[0].astype(BF16)
        wdb[...] = wd_ref[0].astype(BF16)

    @pl.when(nv_ref[i] > 0)
    def _():
        rows = lax.broadcasted_iota(I32, (blk, 1), 0)
        xu = jnp.where(rows < nv_ref[i], x_ref[...], jnp.uint32(0))
        x = jnp.concatenate(_unpack_bf16_pairs(xu), axis=1).astype(BF16)
        act = (_silu(_dot(x, wgb[...])) * _dot(x, wub[...])).astype(BF16)
        y_ref[...] = _pack_bf16_pairs(_dot(act, wdb[...]))


def _experts(xs, be, nvalid, wg, wu, wd):
    n_rows, dh = xs.shape
    ne, d, f = wg.shape
    blk = EXPERT_BLOCK
    nblocks = n_rows // blk
    grid_spec = pltpu.PrefetchScalarGridSpec(
        num_scalar_prefetch=2,
        grid=(nblocks,),
        in_specs=[pl.BlockSpec((blk, dh), lambda i, be, nv: (i, 0)),
                  pl.BlockSpec((1, d, f), lambda i, be, nv: (be[i], 0, 0)),
                  pl.BlockSpec((1, d, f), lambda i, be, nv: (be[i], 0, 0)),
                  pl.BlockSpec((1, f, d), lambda i, be, nv: (be[i], 0, 0))],
        out_specs=pl.BlockSpec((blk, dh), lambda i, be, nv: (i, 0)),
        scratch_shapes=[pltpu.VMEM((d, f), BF16), pltpu.VMEM((d, f), BF16), pltpu.VMEM((f, d), BF16)],
    )
    return pl.pallas_call(
        functools.partial(_experts_kernel, blk=blk),
        grid_spec=grid_spec,
        out_shape=jax.ShapeDtypeStruct((n_rows, dh), U32),
        compiler_params=_cparams("arbitrary"),
        name="experts",
    )(be, nvalid, xs, wg, wu, wd)


def _final_kernel(dst_ref, dstn_ref, wts_ref, h2_ref, x1_ref, g2_ref, wg_ref, wu_ref, wd_ref, lng_ref, lnb_ref,
                  ys_hbm, o_ref, ybuf, sem, *, alpha, topk, tm, nt):
    i = pl.program_id(0)
    slot = lax.rem(i, 2)

    def gather_start(ids_ref, sl):
        def body(t, c):
            for k in range(topk):
                pltpu.make_async_copy(ys_hbm.at[pl.ds(ids_ref[0, k, t], 1)], ybuf.at[sl, k, pl.ds(t, 1)],
                                      sem.at[sl]).start()
            return c
        lax.fori_loop(0, tm, body, 0)

    @pl.when(i == 0)
    def _():
        gather_start(dst_ref, 0)

    for t in range(tm):
        for k in range(topk):
            pltpu.make_async_copy(ys_hbm.at[pl.ds(dstn_ref[0, k, t], 1)], ybuf.at[1 - slot, k, pl.ds(t, 1)],
                                  sem.at[1 - slot]).start()

    h = jnp.concatenate(_unpack_bf16_pairs(h2_ref[...]), axis=1).astype(BF16)
    y = _dot((_silu(_dot(h, wg_ref[...])) * _dot(h, wu_ref[...])).astype(BF16), wd_ref[...])

    def wait_slot(sl):
        def body(t, c):
            for k in range(topk):
                pltpu.make_async_copy(ys_hbm.at[pl.ds(0, 1)], ybuf.at[sl, k, pl.ds(t, 1)], sem.at[sl]).wait()
            return c
        lax.fori_loop(0, tm, body, 0)

    wait_slot(slot)
    w = wts_ref[...]
    ylo = jnp.zeros((tm, y.shape[1] // 2), F32)
    yhi = jnp.zeros((tm, y.shape[1] // 2), F32)
    for k in range(topk):
        lo, hi = _unpack_bf16_pairs(ybuf[slot, k])
        ylo = ylo + w[:, k:k + 1] * lo
        yhi = yhi + w[:, k:k + 1] * hi
    y = y + jnp.concatenate([ylo, yhi], axis=1)
    o_ref[...] = _ln(alpha * x1_ref[...] + g2_ref[0] * y) * lng_ref[...] + lnb_ref[...]

    @pl.when(i == nt - 1)
    def _():
        wait_slot(1 - slot)


def _final(ys, dest, wtsT, h2, x1, gate2, wg, wu, wd, lng, lnb, alpha, tiles_per_batch):
    n, d = x1.shape
    nt, topk, tm = dest.shape
    kern = functools.partial(_final_kernel, alpha=alpha, topk=topk, tm=tm, nt=nt)
    row = pl.BlockSpec((tm, d), lambda i: (i, 0))
    full = lambda a: pl.BlockSpec(a.shape, lambda i: (0,) * a.ndim, pipeline_mode=pl.Buffered(1))
    return pl.pallas_call(
        kern,
        grid=(nt,),
        in_specs=[pl.BlockSpec((1, topk, tm), lambda i: (i, 0, 0), memory_space=pltpu.SMEM),
                  pl.BlockSpec((1, topk, tm), lambda i: (jnp.minimum(i + 1, nt - 1), 0, 0), memory_space=pltpu.SMEM),
                  pl.BlockSpec((tm, topk), lambda i: (i, 0)),
                  pl.BlockSpec((tm, d // 2), lambda i: (i, 0)), row,
                  pl.BlockSpec((1, 1, d), lambda i: (i // tiles_per_batch, 0, 0)),
                  full(wg), full(wu), full(wd),
                  pl.BlockSpec((1, d), lambda i: (0, 0)), pl.BlockSpec((1, d), lambda i: (0, 0)),
                  pl.BlockSpec(memory_space=pl.ANY)],
        out_specs=row,
        out_shape=jax.ShapeDtypeStruct((n, d), F32),
        scratch_shapes=[pltpu.VMEM((2, topk, tm, d // 2), U32), pltpu.SemaphoreType.DMA((2,))],
        compiler_params=_cparams("arbitrary"),
        name="final",
    )(dest, dest, wtsT, h2, x1, gate2, wg, wu, wd, lng.reshape(1, d), lnb.reshape(1, d), ys)


def _proj_layout(d, qr, kvr, idim, ih, hk, hv):
    src = np.cumsum([0, qr, kvr, idim, ih, hk, hk, hv, hv, d, d])
    wa = -(-(qr + kvr + idim + ih) // LANES) * LANES
    wa = max(wa, 1 << (wa - 1).bit_length())
    pieces = [("a", wa, (int(src[0]), int(src[4]))),
              ("hq", hk, (int(src[4]), int(src[5]))), ("hf", hk, (int(src[5]), int(src[6]))),
              ("hi", hv, (int(src[6]), int(src[7]))), ("hg", hv, (int(src[7]), int(src[8]))),
              ("ga", d, (int(src[8]), int(src[9]))), ("gb", d, (int(src[9]), int(src[10])))]
    pieces.sort(key=lambda p: -p[1])
    off = 0
    layout = {}
    for name, wdt, rng in pieces:
        assert off % wdt == 0
        layout[name] = (off, wdt, rng)
        off += wdt
    return layout, off


def kernel(x, c, rpb_table, hgrn_lb_logits, ada_w, ada_b, w_in, q_norm_g, kv_norm_g, w_uq, w_uk, w_uv, w_qidx,
           idx_k_norm_g, idx_k_norm_b, hgrn_out_norm_g, w_branch_a, w_branch_b, w_o, ln1_g, ln1_b, w_router,
           router_bias, w_exp_gate, w_exp_up, w_exp_down, w_sh_gate, w_sh_up, w_sh_down, ln2_g, ln2_b):
    bsz, s, d = x.shape
    depth = ada_w.shape[0]
    qr = w_uq.shape[1]
    ah, kvr, qk = w_uk.shape[1], w_uk.shape[2], w_uk.shape[3]
    idim = idx_k_norm_g.shape[1]
    ih = w_qidx.shape[2] // idim
    dv = hgrn_out_norm_g.shape[1]
    nh = w_branch_b.shape[1] // dv
    dk = hgrn_lb_logits.shape[1] // nh
    ne = w_router.shape[2]
    topk = min(IDX_TOPK, s // 4)
    alpha = float((2 * depth) ** 0.25)
    n_tok = bsz * s

    lower_bounds = jnp.cumsum(jax.nn.softmax(hgrn_lb_logits.astype(F32), axis=0), axis=0)
    layout, wtot = _proj_layout(d, qr, kvr, idim, ih, nh * dk, nh * dv)

    for l in range(depth):
        mod = _ada(c, ada_w[l], ada_b[l])[:, None, :]
        shift1, scale1, gate1, shift2, scale2, gate2 = jnp.split(mod, 6, axis=-1)

        cols = []
        for off, wdt, (lo, hi) in sorted(layout.values()):
            cols.append(w_in[l][:, lo:hi].astype(BF16))
            if wdt > hi - lo:
                cols.append(jnp.zeros((d, wdt - (hi - lo)), BF16))
        proj = _inproj(x, scale1, shift1, jnp.concatenate(cols, axis=1))
        blk = lambda name: layout[name][0] // layout[name][1]

        qb_sz = _pick(s, (256, 128))
        q_latT, q_idxT, w_idxT, k_idx, kv_lat = _prep(
            proj, blk("a"), layout["a"][1], (qr, kvr, idim, ih, ah, qk), q_norm_g[l], kv_norm_g[l],
            idx_k_norm_g[l], idx_k_norm_b[l], jnp.swapaxes(w_uq[l], 0, 1).astype(BF16),
            w_uk[l].astype(BF16), jnp.swapaxes(w_qidx[l], 0, 1).astype(BF16), qb_sz)
        kv_latT = jnp.concatenate([jnp.swapaxes(kv_lat, 1, 2), jnp.ones((bsz, 1, s), BF16),
                                   jnp.zeros((bsz, 7, s), BF16)], axis=1)
        o_a = _dsa(q_idxT, w_idxT, q_latT, k_idx, kv_lat, kv_latT,
                   rpb_table, jnp.swapaxes(w_uv[l], 1, 2).astype(BF16), topk, qb_sz)

        o_b = _hgrn(proj, (blk("hq"), blk("hf"), blk("hi"), blk("hg")), lower_bounds[l],
                    hgrn_out_norm_g[l], nh, dk, dv)

        x1, h2, lgT = _postmix(o_a, o_b, proj, blk("ga"), blk("gb"), x, gate1, scale2, shift2,
                               w_branch_a[l].astype(BF16), w_branch_b[l].astype(BF16), w_o[l].astype(BF16),
                               jnp.swapaxes(w_router[l], 0, 1).astype(BF16), ln1_g[l], ln1_b[l], alpha)

        ids, wts, rnk, sizes_f = _route(lgT, router_bias[l])
        eb = EXPERT_BLOCK
        sizes = sizes_f[:, 0].astype(I32)
        padded = (sizes + eb - 1) // eb * eb
        pend = jnp.cumsum(padded)
        pstart = pend - padded
        n_rows = -(-(n_tok * MOE_TOPK + ne * (eb - 1)) // eb) * eb
        blk_start = jnp.arange(n_rows // eb, dtype=I32) * eb
        blk_expert = jnp.minimum(jnp.sum((pend[None, :] <= blk_start[:, None]).astype(I32), axis=1), ne - 1)
        onehot = (blk_expert[:, None] == jnp.arange(ne, dtype=I32)[None, :]).astype(I32)
        blk_size = jnp.sum(onehot * sizes[None, :], axis=1)
        blk_pstart = jnp.sum(onehot * pstart[None, :], axis=1)
        nvalid = jnp.clip(blk_size - (blk_start - blk_pstart), 0, eb).astype(I32)
        dest = _dest(pstart.astype(I32), ids, rnk, ne)
        tm = _pick(s, (128,))
        dest_t = dest.reshape(bsz, MOE_TOPK, s // tm, tm).transpose(0, 2, 1, 3).reshape(n_tok // tm, MOE_TOPK, tm)
        wtsT = jnp.swapaxes(wts, 1, 2).reshape(n_tok, MOE_TOPK)

        h2f = h2.reshape(n_tok, d // 2)
        xs = _dispatch(dest_t, h2f, n_rows)
        ys = _experts(xs, blk_expert, nvalid, w_exp_gate[l], w_exp_up[l], w_exp_down[l])
        x = _final(ys, dest_t, wtsT, h2f, x1.reshape(n_tok, d), gate2, w_sh_gate[l].astype(BF16),
                   w_sh_up[l].astype(BF16), w_sh_down[l].astype(BF16), ln2_g[l], ln2_b[l], alpha,
                   s // tm).reshape(bsz, s, d)
    return x
```

```python
import functools
import math

import numpy as np
import jax
import jax.numpy as jnp
from jax import lax
from jax.experimental import pallas as pl
from jax.experimental.pallas import tpu as pltpu

F32 = jnp.float32
BF16 = jnp.bfloat16
I32 = jnp.int32
U32 = jnp.uint32

EPS = 1e-6
IDX_TOPK = 256
RPB_MAX_DIST = 128
MOE_TOPK = 8
N_GROUPS = 8
TOPK_GROUPS = 4
ROUTED_SCALE = 2.5
EXPERT_BLOCK = 512

V7X_VMEM_LIMIT_BYTES = 56 * 1024 * 1024
LANES = 128
INT_MIN = -2 ** 31
NEG_BIG = -1e30
DSA_FAR_GROUP = 4
LOG2E = math.log2(math.e)


def _cparams(*sem):
    return pltpu.CompilerParams(dimension_semantics=tuple(sem), vmem_limit_bytes=V7X_VMEM_LIMIT_BYTES)


def _pick(n, prefs):
    for p in prefs:
        if n % p == 0:
            return p
    return n


def _sigmoid(v):
    return 1.0 / (1.0 + jnp.exp(-v))


def _silu(v):
    return v * _sigmoid(v)


def _ln(v):
    mu = jnp.mean(v, axis=-1, keepdims=True)
    d = v - mu
    var = jnp.mean(d * d, axis=-1, keepdims=True)
    return d * lax.rsqrt(var + EPS)


def _rms(v):
    return v * lax.rsqrt(jnp.mean(v * v, axis=-1, keepdims=True) + EPS)


def _dot(a, b):
    return jnp.dot(a, b, preferred_element_type=F32)


def _pack_bf16_pairs(v):
    n = v.shape[1] // 2
    lo = pltpu.bitcast(v[:, :n].astype(BF16).astype(F32), U32) >> 16
    hi = pltpu.bitcast(v[:, n:].astype(BF16).astype(F32), U32) & jnp.uint32(0xFFFF0000)
    return lo | hi


def _unpack_bf16_pairs(u):
    return pltpu.bitcast(u << 16, F32), pltpu.bitcast(u & jnp.uint32(0xFFFF0000), F32)


def _dot_nt(a, b):
    return lax.dot_general(a, b, (((1,), (1,)), ((), ())), preferred_element_type=F32)


def _dot_tn(a, b):
    return lax.dot_general(a, b, (((0,), (0,)), ((), ())), preferred_element_type=F32)


def _ada_kernel(c_ref, w_ref, b_ref, o_ref):
    c = c_ref[...]
    o_ref[...] = _dot(_silu(c).astype(BF16), w_ref[...].astype(BF16)) + b_ref[...]


def _ada(c, w, b):
    bsz, d = c.shape
    n = w.shape[1]
    rows = 8
    cp = jnp.zeros((rows, d), F32).at[:bsz].set(c)
    tn = _pick(n, (1024, 512, 256, 128))
    out = pl.pallas_call(
        _ada_kernel,
        grid=(n // tn,),
        in_specs=[pl.BlockSpec((rows, d), lambda j: (0, 0)),
                  pl.BlockSpec((d, tn), lambda j: (0, j)),
                  pl.BlockSpec((1, tn), lambda j: (0, j))],
        out_specs=pl.BlockSpec((rows, tn), lambda j: (0, j)),
        out_shape=jax.ShapeDtypeStruct((rows, n), F32),
        compiler_params=_cparams("arbitrary"),
        name="ada",
    )(cp, w, b.reshape(1, n))
    return out[:bsz]


def _inproj_kernel(x_ref, sc_ref, sh_ref, w_ref, o_ref, h_ref):
    @pl.when(pl.program_id(2) == 0)
    def _():
        h = _ln(x_ref[0]) * (1.0 + sc_ref[0]) + sh_ref[0]
        h_ref[...] = h.astype(BF16)

    o_ref[0] = _dot(h_ref[...], w_ref[...]).astype(o_ref.dtype)


def _inproj(x, scale, shift, w):
    bsz, s, d = x.shape
    n = w.shape[1]
    tm = _pick(s, (1024, 512, 256, 128))
    tn = _pick(n, (1024, 512, 256, 128))
    return pl.pallas_call(
        _inproj_kernel,
        grid=(bsz, s // tm, n // tn),
        in_specs=[pl.BlockSpec((1, tm, d), lambda b, i, j: (b, i, 0)),
                  pl.BlockSpec((1, 1, d), lambda b, i, j: (b, 0, 0)),
                  pl.BlockSpec((1, 1, d), lambda b, i, j: (b, 0, 0)),
                  pl.BlockSpec((d, tn), lambda b, i, j: (0, j))],
        out_specs=pl.BlockSpec((1, tm, tn), lambda b, i, j: (b, i, j)),
        out_shape=jax.ShapeDtypeStruct((bsz, s, n), BF16),
        scratch_shapes=[pltpu.VMEM((tm, d), BF16)],
        compiler_params=_cparams("arbitrary", "arbitrary", "arbitrary"),
        name="inproj",
    )(x, scale, shift, w)


def _prep_kernel(a_ref, qg_ref, kvg_ref, ikg_ref, ikb_ref, wuqT_ref, wuk_ref, wqiT_ref,
                 qlatT_ref, qidxT_ref, widxT_ref, kidx_ref, kv_ref, *, qr, kvr, idim, ih, ah, qk):
    a = a_ref[0]
    af = a.astype(F32)
    cq = (_rms(af[:, :qr]) * qg_ref[...]).astype(BF16)
    ckv = af[:, qr:qr + kvr]
    ki = af[:, qr + kvr:qr + kvr + idim]
    kv_ref[0] = (_rms(ckv) * kvg_ref[...]).astype(BF16)
    kidx_ref[0] = (_ln(ki) * ikg_ref[...] + ikb_ref[...]).astype(BF16)
    tail = a[:, qr + kvr:qr + kvr + LANES]
    eye = (lax.broadcasted_iota(I32, (LANES, LANES), 0) == lax.broadcasted_iota(I32, (LANES, LANES), 1))
    tailT = _dot_nt(jnp.where(eye, 1.0, 0.0).astype(BF16), tail)
    widxT_ref[0] = tailT[idim:idim + ih] * float((ih * idim) ** -0.5)
    tm = a.shape[0]
    qT = _dot_nt(wuqT_ref[...], cq)
    for h in range(ah):
        qh = qT[h * qk:(h + 1) * qk].astype(BF16)
        qlatT_ref[0, 0, :, h * tm:(h + 1) * tm] = (_dot(wuk_ref[h], qh) * float(qk ** -0.5 * LOG2E)).astype(BF16)
    qiT = _dot_nt(wqiT_ref[...], cq)
    for h in range(ih):
        qidxT_ref[0, 0, :, h * tm:(h + 1) * tm] = qiT[h * idim:(h + 1) * idim].astype(BF16)


def _prep(proj, a_blk, wa, dims, q_norm_g, kv_norm_g, ikg, ikb, w_uqT, w_uk, w_qidxT, tm):
    bsz, s, _ = proj.shape
    qr, kvr, idim, ih, ah, qk = dims
    assert idim + ih <= LANES and qr + kvr + LANES <= wa
    kern = functools.partial(_prep_kernel, qr=qr, kvr=kvr, idim=idim, ih=ih, ah=ah, qk=qk)
    full = lambda shape: pl.BlockSpec(shape, lambda b, i: (0,) * len(shape))
    return pl.pallas_call(
        kern,
        grid=(bsz, s // tm),
        in_specs=[pl.BlockSpec((1, tm, wa), lambda b, i: (b, i, a_blk)),
                  full((1, qr)), full((1, kvr)), full((1, idim)), full((1, idim)),
                  full(w_uqT.shape), full(w_uk.shape), full(w_qidxT.shape)],
        out_specs=[pl.BlockSpec((1, 1, kvr, ah * tm), lambda b, i: (b, i, 0, 0)),
                   pl.BlockSpec((1, 1, idim, ih * tm), lambda b, i: (b, i, 0, 0)),
                   pl.BlockSpec((1, ih, tm), lambda b, i: (b, 0, i)),
                   pl.BlockSpec((1, tm, idim), lambda b, i: (b, i, 0)),
                   pl.BlockSpec((1, tm, kvr), lambda b, i: (b, i, 0))],
        out_shape=[jax.ShapeDtypeStruct((bsz, s // tm, kvr, ah * tm), BF16),
                   jax.ShapeDtypeStruct((bsz, s // tm, idim, ih * tm), BF16),
                   jax.ShapeDtypeStruct((bsz, ih, s), F32),
                   jax.ShapeDtypeStruct((bsz, s, idim), BF16),
                   jax.ShapeDtypeStruct((bsz, s, kvr), BF16)],
        compiler_params=_cparams("arbitrary", "arbitrary"),
        name="prep",
    )(proj, q_norm_g.reshape(1, qr), kv_norm_g.reshape(1, kvr), ikg.reshape(1, idim), ikb.reshape(1, idim),
      w_uqT, w_uk, w_qidxT)


def _dsa_kernel(qidxT_ref, widxT_ref, qlatT_ref, kidx_ref, kv_ref, kvT_ref, bkt0_ref, bkt1_ref, rel_ref, wuvT_ref,
                o_ref, keys_ref, gmax_ref, s_ref, p_ref, m_ref, a_ref, acc_ref, t0_ref, t1_ref, lstrict_ref, taken_ref, sx_ref, px_ref, ax_ref,
                *, qb_sz, topk, ih, ah, c, nb):
    QB = qb_sz
    qb = pl.program_id(1)
    nchunks = qb + 1
    qpos = qb * QB + lax.broadcasted_iota(I32, (1, QB), 1)
    kpos = lax.broadcasted_iota(I32, (QB, 1), 0)
    wT = widxT_ref[0]
    hcols = lambda h: slice(h * QB, (h + 1) * QB)

    @pl.when(jnp.logical_and(pl.program_id(0) == 0, qb == 0))
    def _():
        for bkt_ref, t_ref in ((bkt0_ref, t0_ref), (bkt1_ref, t1_ref)):
            bkt = bkt_ref[...]
            for h in range(ah):
                tile = jnp.zeros((QB, QB), F32)
                for b in range(nb - 1):
                    tile = jnp.where(bkt == b, rel_ref[b, h], tile)
                t_ref[:, hcols(h)] = tile
        lstrict_ref[...] = jnp.where(lax.broadcasted_iota(I32, (QB, QB), 1) < lax.broadcasted_iota(I32, (QB, QB), 0),
                                     1.0, 0.0).astype(BF16)

    def score_chunk(kc, carry):
        off = pl.multiple_of(kc * QB, QB)
        s_ref[:, :ih * QB] = _dot(kidx_ref[0, pl.ds(off, QB), :], qidxT_ref[0, 0])
        acc = jnp.zeros((QB, QB), F32)
        for h in range(ih):
            acc = acc + wT[h:h + 1] * jnp.maximum(s_ref[:, hcols(h)], 0.0)
        bits = pltpu.bitcast(acc, I32)
        skey = bits ^ ((bits >> 31) & 0x7FFFFFFF)
        causal = (off + kpos) <= qpos
        skey = jnp.where(causal, skey, INT_MIN)
        keys_ref[pl.ds(off, QB), :] = skey
        gmax_ref[...] = jnp.maximum(gmax_ref[...], skey)
        return carry

    gmax_ref[...] = jnp.full(gmax_ref.shape, INT_MIN, I32)
    lax.fori_loop(0, nchunks, score_chunk, 0)

    gmax = gmax_ref[...]
    lo0 = jnp.min(gmax, axis=0, keepdims=True) if QB >= topk else jnp.full((1, QB), INT_MIN, I32)
    hi0 = jnp.max(gmax, axis=0, keepdims=True) + 1

    def count_ge(cand):
        def body(kc, cnt):
            off = pl.multiple_of(kc * QB, QB)
            hit = jnp.where(keys_ref[pl.ds(off, QB), :] >= cand, 1, 0)
            return cnt + jnp.sum(hit.reshape(QB // 8, 8, QB), axis=0)

        cnt = lax.fori_loop(0, nchunks, body, jnp.zeros((8, QB), I32))
        return jnp.sum(cnt.astype(F32), axis=0, keepdims=True)

    def bis_cond(st):
        it, _, _, done = st
        return jnp.logical_and(it < 34, jnp.min(done) < 0.5)

    def bis_body(st):
        it, lo, hi, done = st
        cand = (lo >> 1) + (hi >> 1) + (lo & hi & 1)
        cnt = count_ge(cand)
        ge = cnt >= float(topk)
        conv = cand == lo
        fin = jnp.logical_or(conv, cnt == float(topk))
        thr_new = jnp.where(conv, lo, cand)
        lo = jnp.where(fin, thr_new, jnp.where(ge, cand, lo))
        hi = jnp.where(fin, thr_new + 1, jnp.where(ge, hi, cand))
        return it + 1, lo, hi, jnp.where(fin, 1.0, done)

    _, thr, _, _ = lax.while_loop(bis_cond, bis_body, (jnp.int32(0), lo0, hi0, jnp.zeros((1, QB), F32)))
    need = float(topk) - count_ge(thr + 1)

    m_ref[...] = jnp.full(m_ref.shape, NEG_BIG, F32)
    acc_ref[...] = jnp.zeros(acc_ref.shape, F32)

    def attn_chunk(off, bias_ref, diag, taken, s_ref=s_ref, p_ref=p_ref, a_ref=a_ref):
        kt = keys_ref[pl.ds(off, QB), :]
        eq = kt == thr
        eq01 = jnp.where(eq, 1.0, 0.0)
        before = _dot(lstrict_ref[...], eq01.astype(BF16)) + taken
        sel = jnp.logical_or(kt > thr, jnp.logical_and(eq, before < need))
        taken = taken + jnp.sum(eq01, axis=0, keepdims=True)
        if diag:
            sel = jnp.logical_and(sel, (off + kpos) <= qpos)
        madd = jnp.where(sel, 0.0, NEG_BIG)
        s_ref[:, :ah * QB] = _dot(kv_ref[0, pl.ds(off, QB), :], qlatT_ref[0, 0])
        for g in range(ah * QB // LANES):
            cols = slice(g * LANES, (g + 1) * LANES)
            qcols = slice(g * LANES % QB, g * LANES % QB + LANES)
            s = s_ref[:, cols] + madd[:, qcols]
            if bias_ref is not None:
                s = s + bias_ref[:, cols]
            m_prev = m_ref[:, cols]
            m_new = jnp.maximum(m_prev, jnp.max(s, axis=0, keepdims=True))
            p_ref[:, cols] = jnp.exp2(s - m_new).astype(BF16)
            a_ref[:, cols] = jnp.exp2(m_prev - m_new)
            m_ref[:, cols] = m_new
        acc_ref[...] = a_ref[...] * acc_ref[...] + _dot(kvT_ref[0, :, pl.ds(off, QB)], p_ref[...])
        return taken

    G = DSA_FAR_GROUP

    def far_group(kg, taken):
        off = pl.multiple_of(kg * (G * QB), G * QB)
        taken = attn_chunk(off, None, False, taken)
        for j in range(1, G):
            taken = attn_chunk(off + j * QB, None, False, taken, sx_ref.at[j - 1], px_ref.at[j - 1], ax_ref.at[j - 1])
        return taken

    nfar = jnp.maximum(qb - 1, 0)
    ngroups = nfar // G
    taken = lax.fori_loop(0, ngroups, far_group, jnp.zeros((1, QB), F32))

    def far_single(r, taken):
        return attn_chunk(pl.multiple_of((ngroups * G + r) * QB, QB), None, False, taken)

    taken_ref[...] = lax.fori_loop(0, nfar - ngroups * G, far_single, taken)

    @pl.when(qb >= 1)
    def _():
        taken_ref[...] = attn_chunk(pl.multiple_of((qb - 1) * QB, QB), t1_ref, False, taken_ref[...])

    attn_chunk(pl.multiple_of(qb * QB, QB), t0_ref, True, taken_ref[...])

    outs = []
    for h in range(ah):
        o = (acc_ref[:c, hcols(h)] / acc_ref[c:c + 1, hcols(h)]).astype(BF16)
        outs.append(_dot(wuvT_ref[h], o))
    o_ref[0] = jnp.concatenate(outs, axis=0).T.astype(o_ref.dtype)


def _t5_bucket_np(d, nbuckets):
    max_exact = nbuckets // 2
    dd = np.maximum(d, 1).astype(np.float32)
    large = max_exact + (np.log(dd / np.float32(max_exact)) / np.float32(math.log(RPB_MAX_DIST / max_exact))
                         * np.float32(nbuckets - max_exact)).astype(np.int32)
    large = np.minimum(large, nbuckets - 1)
    return np.where(d < max_exact, d, large).astype(np.int32)


def _dsa(q_idxT, w_idxT, q_latT, k_idx, kv_lat, kv_latT, rpb_table, w_uvT, topk, QB):
    bsz, nqb, idim, ihq = q_idxT.shape
    c, ahq = q_latT.shape[2], q_latT.shape[3]
    ca = kv_latT.shape[1]
    ih, ah = ihq // QB, ahq // QB
    s = nqb * QB
    vd = w_uvT.shape[1]
    nb = rpb_table.shape[0]
    assert QB >= RPB_MAX_DIST
    j = np.arange(QB)[:, None]
    i = np.arange(QB)[None, :]
    bkt0 = jnp.asarray(_t5_bucket_np(np.maximum(i - j, 0), nb))
    bkt1 = jnp.asarray(_t5_bucket_np(QB + i - j, nb))
    assert int(_t5_bucket_np(np.array([RPB_MAX_DIST]), nb)[0]) == nb - 1
    rel = (rpb_table.astype(F32) - rpb_table[nb - 1].astype(F32)[None, :]) * LOG2E
    kern = functools.partial(_dsa_kernel, qb_sz=QB, topk=topk, ih=ih, ah=ah, c=c, nb=nb)
    const = lambda shape: pl.BlockSpec(shape, lambda b, i: (0,) * len(shape), pipeline_mode=pl.Buffered(1))
    hw = max(ih, ah) * QB
    return pl.pallas_call(
        kern,
        grid=(bsz, nqb),
        in_specs=[pl.BlockSpec((1, 1, idim, ih * QB), lambda b, i: (b, i, 0, 0)),
                  pl.BlockSpec((1, ih, QB), lambda b, i: (b, 0, i)),
                  pl.BlockSpec((1, 1, c, ah * QB), lambda b, i: (b, i, 0, 0)),
                  pl.BlockSpec((1, s, idim), lambda b, i: (b, 0, 0), pipeline_mode=pl.Buffered(1)),
                  pl.BlockSpec((1, s, c), lambda b, i: (b, 0, 0), pipeline_mode=pl.Buffered(1)),
                  pl.BlockSpec((1, ca, s), lambda b, i: (b, 0, 0), pipeline_mode=pl.Buffered(1)),
                  const((QB, QB)), const((QB, QB)),
                  pl.BlockSpec(memory_space=pltpu.SMEM), const(w_uvT.shape)],
        out_specs=pl.BlockSpec((1, QB, ah * vd), lambda b, i: (b, i, 0)),
        out_shape=jax.ShapeDtypeStruct((bsz, s, ah * vd), BF16),
        scratch_shapes=[pltpu.VMEM((s, QB), I32),
                        pltpu.VMEM((QB, QB), I32),
                        pltpu.VMEM((QB, hw), F32),
                        pltpu.VMEM((QB, ah * QB), BF16),
                        pltpu.VMEM((1, ah * QB), F32),
                        pltpu.VMEM((1, ah * QB), F32),
                        pltpu.VMEM((ca, ah * QB), F32),
                        pltpu.VMEM((QB, ah * QB), F32),
                        pltpu.VMEM((QB, ah * QB), F32),
                        pltpu.VMEM((QB, QB), BF16),
                        pltpu.VMEM((1, QB), F32),
                        pltpu.VMEM((DSA_FAR_GROUP - 1, QB, ah * QB), F32),
                        pltpu.VMEM((DSA_FAR_GROUP - 1, QB, ah * QB), BF16),
                        pltpu.VMEM((DSA_FAR_GROUP - 1, 1, ah * QB), F32)],
        compiler_params=_cparams("arbitrary", "arbitrary"),
        name="dsa",
    )(q_idxT, w_idxT, q_latT, k_idx, kv_lat, kv_latT, bkt0, bkt1, rel, w_uvT)


HGRN_CHUNK = 64
HGRN_SUB = 16
HGRN_EXP_CLAMP = 80.0


def _dot_exact(a, b, dims):
    return lax.dot_general(a, b, (dims, ((), ())), precision=lax.Precision.HIGHEST, preferred_element_type=F32)


def _hgrn_kernel(hq_ref, hf_ref, hi_ref, hg_ref, lb_ref, g_ref, o_ref, st_ref, st0_ref, *, nh, dk, dv, tt):
    C, SUB = HGRN_CHUNK, HGRN_SUB

    @pl.when(pl.program_id(1) == 0)
    def _():
        st_ref[...] = jnp.zeros(st_ref.shape, F32)

    r = lax.broadcasted_iota(I32, (C, C), 0)
    cc = lax.broadcasted_iota(I32, (C, C), 1)
    tri_mask = r >= cc
    tri = jnp.where(tri_mask, 1.0, 0.0).astype(BF16)
    g = g_ref[...]
    st0_ref[...] = st_ref[...]
    decay = jnp.zeros((1, nh * dk), F32)
    lb = lb_ref[...]

    for c in range(tt // C):
        rows = pl.ds(c * C, C)
        f = lb + (1.0 - lb) * _sigmoid(hf_ref[0, rows, :].astype(F32))
        lf = jnp.log(f)
        t1 = lf.astype(BF16)
        r1 = lf - t1.astype(F32)
        t2 = r1.astype(BF16)
        t3 = (r1 - t2.astype(F32)).astype(BF16)
        b = _dot(tri, t1) + _dot(tri, t2) + _dot(tri, t3)
        kk = 1.0 - f
        hq = hq_ref[0, rows, :].astype(F32)
        q = _silu(hq) * float(dk ** -0.5)
        qe = (q * jnp.exp(b)).astype(BF16)
        b_last = b[C - 1:C]
        k_dec = (kk * jnp.exp(b_last - b)).astype(BF16)
        dec_last = jnp.exp(b_last)
        qs, ks = [], []
        for i in range(C // SUB):
            lo, n = i * SUB, (i + 1) * SUB
            bi = b[lo - 1:lo] if i > 0 else jnp.zeros((1, nh * dk), F32)
            decay = jnp.maximum(decay, bi - b[n - 1:n])
            qs.append((q[lo:n] * jnp.exp(b[lo:n] - bi)).astype(BF16))
            ks.append((kk * jnp.exp(jnp.minimum(bi - b, HGRN_EXP_CLAMP))).astype(BF16))
        v_all = hi_ref[0, rows, :]
        outs = []
        for h in range(nh):
            kc = slice(h * dk, (h + 1) * dk)
            v = v_all[:, h * dv:(h + 1) * dv]
            stT = st_ref[h]
            att = jnp.concatenate([_dot_nt(qs[i][:, kc], ks[i][:, kc]) for i in range(C // SUB)], axis=0)
            att = jnp.where(tri_mask, att, 0.0)
            o = _dot_nt(qe[:, kc], stT.astype(BF16)) + _dot(att.astype(BF16), v)
            st_ref[h] = stT * dec_last[:, kc] + _dot_tn(v, k_dec[:, kc])
            outs.append(_rms(o) * g)
        o_all = jnp.concatenate(outs, axis=1) * _silu(hg_ref[0, rows, :].astype(F32))
        o_ref[0, rows, :] = o_all.astype(o_ref.dtype)

    @pl.when(jnp.max(decay) > HGRN_EXP_CLAMP)
    def _():
        row = lax.broadcasted_iota(I32, (SUB, 1), 0)

        def head(h, carry):
            ko = pl.multiple_of(h * dk, dk)
            vo = pl.multiple_of(h * dv, dv)
            lb = lb_ref[:, pl.ds(ko, dk)]

            def slab(j, S):
                rows = pl.ds(pl.multiple_of(j * SUB, SUB), SUB)
                f = lb + (1.0 - lb) * _sigmoid(hf_ref[0, rows, pl.ds(ko, dk)].astype(F32))
                kk = 1.0 - f
                hq = hq_ref[0, rows, pl.ds(ko, dk)].astype(F32)
                q = _silu(hq) * float(dk ** -0.5)
                v = hi_ref[0, rows, pl.ds(vo, dv)].astype(F32)
                o = jnp.zeros((SUB, dv), F32)
                for r in range(SUB):
                    S = S * f[r:r + 1] + _dot_exact(jnp.where(row == r, v, 0.0), kk, ((0,), (0,)))
                    o = jnp.where(row == r, _dot_exact(q, S, ((1,), (1,))), o)
                hg = hg_ref[0, rows, pl.ds(vo, dv)].astype(F32)
                o_ref[0, rows, pl.ds(vo, dv)] = (_rms(o) * g * _silu(hg)).astype(o_ref.dtype)
                return S

            st_ref[h] = lax.fori_loop(0, tt // SUB, slab, st0_ref[h])
            return carry

        lax.fori_loop(0, nh, head, 0)


def _hgrn(proj, blks, lb, g, nh, dk, dv):
    bsz, s, _ = proj.shape
    tt = _pick(s, (256, 128, 64))
    kern = functools.partial(_hgrn_kernel, nh=nh, dk=dk, dv=dv, tt=tt)
    col = lambda blk, wdt: pl.BlockSpec((1, tt, wdt), lambda b, i: (b, i, blk))
    return pl.pallas_call(
        kern,
        grid=(bsz, s // tt),
        in_specs=[col(blks[0], nh * dk), col(blks[1], nh * dk), col(blks[2], nh * dv), col(blks[3], nh * dv),
                  pl.BlockSpec((1, nh * dk), lambda b, i: (0, 0)),
                  pl.BlockSpec((1, dv), lambda b, i: (0, 0))],
        out_specs=pl.BlockSpec((1, tt, nh * dv), lambda b, i: (b, i, 0)),
        out_shape=jax.ShapeDtypeStruct((bsz, s, nh * dv), BF16),
        scratch_shapes=[pltpu.VMEM((nh, dv, dk), F32), pltpu.VMEM((nh, dv, dk), F32)],
        compiler_params=_cparams("arbitrary", "arbitrary"),
        name="hgrn",
    )(proj, proj, proj, proj, lb.reshape(1, nh * dk), g.reshape(1, dv))


def _postmix_kernel(oa_ref, ob_ref, ga_ref, gb_ref, x_ref, g1_ref, sc2_ref, sh2_ref,
                    wa_ref, wb_ref, wo_ref, wrT_ref, lng_ref, lnb_ref,
                    x1_ref, h2_ref, lgT_ref, *, alpha):
    ya = _dot(oa_ref[0], wa_ref[...])
    yb = _dot(ob_ref[0], wb_ref[...])
    mix = _sigmoid(ga_ref[0].astype(F32)) * ya + _sigmoid(gb_ref[0].astype(F32)) * yb
    mixed = _dot(mix.astype(BF16), wo_ref[...])
    x1 = _ln(alpha * x_ref[0] + g1_ref[0] * mixed) * lng_ref[...] + lnb_ref[...]
    x1_ref[0] = x1
    h2 = _ln(x1) * (1.0 + sc2_ref[0]) + sh2_ref[0]
    h2_ref[0] = _pack_bf16_pairs(h2)
    lgT_ref[0] = _dot_nt(wrT_ref[...], h2.astype(BF16))


def _postmix(o_a, o_b, proj, ga_blk, gb_blk, x, gate1, scale2, shift2, wa, wb, wo, wrT, lng, lnb, alpha):
    bsz, s, d = x.shape
    ne = wrT.shape[0]
    tm = _pick(s, (256, 128))
    kern = functools.partial(_postmix_kernel, alpha=alpha)
    row = lambda wdt: pl.BlockSpec((1, tm, wdt), lambda b, i: (b, i, 0))
    vec = pl.BlockSpec((1, 1, d), lambda b, i: (b, 0, 0))
    full = lambda a: pl.BlockSpec(a.shape, lambda b, i: (0,) * a.ndim, pipeline_mode=pl.Buffered(1))
    return pl.pallas_call(
        kern,
        grid=(bsz, s // tm),
        in_specs=[row(o_a.shape[2]), row(o_b.shape[2]),
                  pl.BlockSpec((1, tm, d), lambda b, i: (b, i, ga_blk)),
                  pl.BlockSpec((1, tm, d), lambda b, i: (b, i, gb_blk)),
                  row(d), vec, vec, vec,
                  full(wa), full(wb), full(wo), full(wrT),
                  pl.BlockSpec((1, d), lambda b, i: (0, 0)), pl.BlockSpec((1, d), lambda b, i: (0, 0))],
        out_specs=[row(d), row(d // 2), pl.BlockSpec((1, ne, tm), lambda b, i: (b, 0, i))],
        out_shape=[jax.ShapeDtypeStruct((bsz, s, d), F32),
                   jax.ShapeDtypeStruct((bsz, s, d // 2), U32),
                   jax.ShapeDtypeStruct((bsz, ne, s), F32)],
        compiler_params=_cparams("arbitrary", "arbitrary"),
        name="postmix",
    )(o_a, o_b, proj, proj, x, gate1, scale2, shift2, wa, wb, wo, wrT, lng.reshape(1, d), lnb.reshape(1, d))


def _route_kernel(lg_ref, bias_ref, ids_ref, wts_ref, rnk_ref, sizes_ref, upper_ref, carry_ref, *, ne):
    first = jnp.logical_and(pl.program_id(0) == 0, pl.program_id(1) == 0)
    tn = lg_ref.shape[2]

    @pl.when(first)
    def _():
        carry_ref[...] = jnp.zeros(carry_ref.shape, F32)
        r_ = lax.broadcasted_iota(I32, (tn, tn), 0)
        c_ = lax.broadcasted_iota(I32, (tn, tn), 1)
        upper_ref[...] = jnp.where(r_ < c_, 1.0, 0.0).astype(BF16)

    per = ne // N_GROUPS
    s = _sigmoid(lg_ref[0])
    bz = s + bias_ref[...]
    ridx = lax.broadcasted_iota(I32, (per, tn), 0)
    neg_inf = jnp.float32(-jnp.inf)
    gs = []
    for g in range(N_GROUPS):
        blk = bz[g * per:(g + 1) * per]
        m1 = jnp.max(blk, axis=0, keepdims=True)
        first_hit = jnp.min(jnp.where(blk == m1, ridx, per), axis=0, keepdims=True)
        m2 = jnp.max(jnp.where(ridx == first_hit, neg_inf, blk), axis=0, keepdims=True)
        gs.append(m1 + m2)
    emask_rows = []
    for g in range(N_GROUPS):
        rank = jnp.zeros((1, tn), I32)
        for g2 in range(N_GROUPS):
            if g2 == g:
                continue
            beats = (gs[g2] > gs[g]) if g2 > g else (gs[g2] >= gs[g])
            rank = rank + jnp.where(beats, 1, 0)
        emask_rows.append(jnp.broadcast_to(rank < TOPK_GROUPS, (per, tn)))
    emask = jnp.concatenate(emask_rows, axis=0)
    masked = jnp.where(emask, bz, neg_inf)
    eidx = lax.broadcasted_iota(I32, (ne, tn), 0)
    rank = jnp.zeros((ne, tn), I32)
    for e2 in range(ne):
        row = masked[e2:e2 + 1]
        beats = jnp.logical_or(row > masked, jnp.logical_and(row == masked, e2 < eidx))
        rank = rank + jnp.where(beats, 1, 0)
    sel = rank < MOE_TOPK
    sel01 = jnp.where(sel, 1.0, 0.0)
    denom = jnp.sum(jnp.where(sel, s, 0.0), axis=0, keepdims=True)
    wn = s / denom * ROUTED_SCALE
    before = _dot(sel01.astype(BF16), upper_ref[...]) + carry_ref[:, 0:1]
    ids, wts, rnk = [], [], []
    for k in range(MOE_TOPK):
        hit = rank == k
        ids.append(jnp.sum(jnp.where(hit, eidx, 0), axis=0, keepdims=True))
        wts.append(jnp.sum(jnp.where(hit, wn, 0.0), axis=0, keepdims=True))
        rnk.append(jnp.sum(jnp.where(hit, before, 0.0), axis=0, keepdims=True))
    ids_ref[0] = jnp.concatenate(ids, axis=0)
    wts_ref[0] = jnp.concatenate(wts, axis=0)
    rnk_ref[0] = jnp.concatenate(rnk, axis=0).astype(I32)
    carry_ref[...] = carry_ref[...] + jnp.sum(sel01, axis=1, keepdims=True)
    sizes_ref[...] = carry_ref[...]


def _route(lgT, bias):
    bsz, ne, s = lgT.shape
    tn = _pick(s, (1024, 512, 256, 128))
    kern = functools.partial(_route_kernel, ne=ne)
    slot = pl.BlockSpec((1, MOE_TOPK, tn), lambda b, j: (b, 0, j))
    return pl.pallas_call(
        kern,
        grid=(bsz, s // tn),
        in_specs=[pl.BlockSpec((1, ne, tn), lambda b, j: (b, 0, j)),
                  pl.BlockSpec((ne, 1), lambda b, j: (0, 0))],
        out_specs=[slot, slot, slot, pl.BlockSpec((ne, LANES), lambda b, j: (0, 0))],
        out_shape=[jax.ShapeDtypeStruct((bsz, MOE_TOPK, s), I32),
                   jax.ShapeDtypeStruct((bsz, MOE_TOPK, s), F32),
                   jax.ShapeDtypeStruct((bsz, MOE_TOPK, s), I32),
                   jax.ShapeDtypeStruct((ne, LANES), F32)],
        scratch_shapes=[pltpu.VMEM((tn, tn), BF16), pltpu.VMEM((ne, LANES), F32)],
        compiler_params=_cparams("arbitrary", "arbitrary"),
        name="route",
    )(lgT, bias.reshape(ne, 1))


def _dest_kernel(pstart_ref, ids_ref, rnk_ref, o_ref, *, ne):
    ids = ids_ref[0]
    base = jnp.zeros(ids.shape, I32)
    for e in range(ne):
        base = jnp.where(ids == e, pstart_ref[e], base)
    o_ref[0] = base + rnk_ref[0]


def _dest(pstart, ids, rnk, ne):
    bsz, k, s = ids.shape
    tn = _pick(s, (2048, 1024, 512, 256, 128))
    blk = lambda: pl.BlockSpec((1, k, tn), lambda b, j, ps: (b, 0, j))
    return pl.pallas_call(
        functools.partial(_dest_kernel, ne=ne),
        grid_spec=pltpu.PrefetchScalarGridSpec(num_scalar_prefetch=1, grid=(bsz, s // tn),
                                               in_specs=[blk(), blk()], out_specs=blk()),
        out_shape=jax.ShapeDtypeStruct((bsz, k, s), I32),
        compiler_params=_cparams("arbitrary", "arbitrary"),
        name="dest",
    )(pstart, ids, rnk)


def _dispatch_kernel(dst_ref, h_ref, xs_hbm, sem, *, tm, topk):
    def start(t, c):
        for k in range(topk):
            pltpu.make_async_copy(h_ref.at[pl.ds(t, 1)], xs_hbm.at[pl.ds(dst_ref[0, k, t], 1)], sem.at[0]).start()
        return c

    lax.fori_loop(0, tm, start, 0)

    def wait(t, c):
        for k in range(topk):
            pltpu.make_async_copy(h_ref.at[pl.ds(t, 1)], xs_hbm.at[pl.ds(0, 1)], sem.at[0]).wait()
        return c

    lax.fori_loop(0, tm, wait, 0)


def _dispatch(dest, h2, n_rows):
    n, d = h2.shape
    nt, topk, tm = dest.shape
    return pl.pallas_call(
        functools.partial(_dispatch_kernel, tm=tm, topk=topk),
        grid=(nt,),
        in_specs=[pl.BlockSpec((1, topk, tm), lambda i: (i, 0, 0), memory_space=pltpu.SMEM),
                  pl.BlockSpec((tm, d), lambda i: (i, 0))],
        out_specs=pl.BlockSpec(memory_space=pl.ANY),
        out_shape=jax.ShapeDtypeStruct((n_rows, d), h2.dtype),
        scratch_shapes=[pltpu.SemaphoreType.DMA((1,))],
        compiler_params=_cparams("arbitrary"),
        name="dispatch",
    )(dest, h2)


def _experts_kernel(be_ref, nv_ref, x_ref, wg_ref, wu_ref, wd_ref, y_ref, wgb, wub, wdb, *, blk):
    i = pl.program_id(0)
    prev_e = be_ref[jnp.maximum(i - 1, 0)]

    @pl.when(jnp.logical_or(i == 0, be_ref[i] != prev_e))
    def _():
        wgb[...] = wg_ref[0].astype(BF16)
        wub[...] = wu_ref[0].astype(BF16)
        wdb[...] = wd_ref[0].astype(BF16)

    @pl.when(nv_ref[i] > 0)
    def _():
        rows = lax.broadcasted_iota(I32, (blk, 1), 0)
        xu = jnp.where(rows < nv_ref[i], x_ref[...], jnp.uint32(0))
        x = jnp.concatenate(_unpack_bf16_pairs(xu), axis=1).astype(BF16)
        act = (_silu(_dot(x, wgb[...])) * _dot(x, wub[...])).astype(BF16)
        y_ref[...] = _pack_bf16_pairs(_dot(act, wdb[...]))


def _experts(xs, be, nvalid, wg, wu, wd):
    n_rows, dh = xs.shape
    ne, d, f = wg.shape
    blk = EXPERT_BLOCK
    nblocks = n_rows // blk
    grid_spec = pltpu.PrefetchScalarGridSpec(
        num_scalar_prefetch=2,
        grid=(nblocks,),
        in_specs=[pl.BlockSpec((blk, dh), lambda i, be, nv: (i, 0)),
                  pl.BlockSpec((1, d, f), lambda i, be, nv: (be[i], 0, 0)),
                  pl.BlockSpec((1, d, f), lambda i, be, nv: (be[i], 0, 0)),
                  pl.BlockSpec((1, f, d), lambda i, be, nv: (be[i], 0, 0))],
        out_specs=pl.BlockSpec((blk, dh), lambda i, be, nv: (i, 0)),
        scratch_shapes=[pltpu.VMEM((d, f), BF16), pltpu.VMEM((d, f), BF16), pltpu.VMEM((f, d), BF16)],
    )
    return pl.pallas_call(
        functools.partial(_experts_kernel, blk=blk),
        grid_spec=grid_spec,
        out_shape=jax.ShapeDtypeStruct((n_rows, dh), U32),
        compiler_params=_cparams("arbitrary"),
        name="experts",
    )(be, nvalid, xs, wg, wu, wd)


def _final_kernel(dst_ref, dstn_ref, wts_ref, h2_ref, x1_ref, g2_ref, wg_ref, wu_ref, wd_ref, lng_ref, lnb_ref,
                  ys_hbm, o_ref, ybuf, sem, *, alpha, topk, tm, nt):
    i = pl.program_id(0)
    slot = lax.rem(i, 2)

    def gather_start(ids_ref, sl):
        def body(t, c):
            for k in range(topk):
                pltpu.make_async_copy(ys_hbm.at[pl.ds(ids_ref[0, k, t], 1)], ybuf.at[sl, k, pl.ds(t, 1)],
                                      sem.at[sl]).start()
            return c
        lax.fori_loop(0, tm, body, 0)

    @pl.when(i == 0)
    def _():
        gather_start(dst_ref, 0)

    for t in range(tm):
        for k in range(topk):
            pltpu.make_async_copy(ys_hbm.at[pl.ds(dstn_ref[0, k, t], 1)], ybuf.at[1 - slot, k, pl.ds(t, 1)],
                                  sem.at[1 - slot]).start()

    h = jnp.concatenate(_unpack_bf16_pairs(h2_ref[...]), axis=1).astype(BF16)
    y = _dot((_silu(_dot(h, wg_ref[...])) * _dot(h, wu_ref[...])).astype(BF16), wd_ref[...])

    def wait_slot(sl):
        def body(t, c):
            for k in range(topk):
                pltpu.make_async_copy(ys_hbm.at[pl.ds(0, 1)], ybuf.at[sl, k, pl.ds(t, 1)], sem.at[sl]).wait()
            return c
        lax.fori_loop(0, tm, body, 0)

    wait_slot(slot)
    w = wts_ref[...]
    ylo = jnp.zeros((tm, y.shape[1] // 2), F32)
    yhi = jnp.zeros((tm, y.shape[1] // 2), F32)
    for k in range(topk):
        lo, hi = _unpack_bf16_pairs(ybuf[slot, k])
        ylo = ylo + w[:, k:k + 1] * lo
        yhi = yhi + w[:, k:k + 1] * hi
    y = y + jnp.concatenate([ylo, yhi], axis=1)
    o_ref[...] = _ln(alpha * x1_ref[...] + g2_ref[0] * y) * lng_ref[...] + lnb_ref[...]

    @pl.when(i == nt - 1)
    def _():
        wait_slot(1 - slot)


def _final(ys, dest, wtsT, h2, x1, gate2, wg, wu, wd, lng, lnb, alpha, tiles_per_batch):
    n, d = x1.shape
    nt, topk, tm = dest.shape
    kern = functools.partial(_final_kernel, alpha=alpha, topk=topk, tm=tm, nt=nt)
    row = pl.BlockSpec((tm, d), lambda i: (i, 0))
    full = lambda a: pl.BlockSpec(a.shape, lambda i: (0,) * a.ndim, pipeline_mode=pl.Buffered(1))
    return pl.pallas_call(
        kern,
        grid=(nt,),
        in_specs=[pl.BlockSpec((1, topk, tm), lambda i: (i, 0, 0), memory_space=pltpu.SMEM),
                  pl.BlockSpec((1, topk, tm), lambda i: (jnp.minimum(i + 1, nt - 1), 0, 0), memory_space=pltpu.SMEM),
                  pl.BlockSpec((tm, topk), lambda i: (i, 0)),
                  pl.BlockSpec((tm, d // 2), lambda i: (i, 0)), row,
                  pl.BlockSpec((1, 1, d), lambda i: (i // tiles_per_batch, 0, 0)),
                  full(wg), full(wu), full(wd),
                  pl.BlockSpec((1, d), lambda i: (0, 0)), pl.BlockSpec((1, d), lambda i: (0, 0)),
                  pl.BlockSpec(memory_space=pl.ANY)],
        out_specs=row,
        out_shape=jax.ShapeDtypeStruct((n, d), F32),
        scratch_shapes=[pltpu.VMEM((2, topk, tm, d // 2), U32), pltpu.SemaphoreType.DMA((2,))],
        compiler_params=_cparams("arbitrary"),
        name="final",
    )(dest, dest, wtsT, h2, x1, gate2, wg, wu, wd, lng.reshape(1, d), lnb.reshape(1, d), ys)


def _proj_layout(d, qr, kvr, idim, ih, hk, hv):
    src = np.cumsum([0, qr, kvr, idim, ih, hk, hk, hv, hv, d, d])
    wa = -(-(qr + kvr + idim + ih) // LANES) * LANES
    wa = max(wa, 1 << (wa - 1).bit_length())
    pieces = [("a", wa, (int(src[0]), int(src[4]))),
              ("hq", hk, (int(src[4]), int(src[5]))), ("hf", hk, (int(src[5]), int(src[6]))),
              ("hi", hv, (int(src[6]), int(src[7]))), ("hg", hv, (int(src[7]), int(src[8]))),
              ("ga", d, (int(src[8]), int(src[9]))), ("gb", d, (int(src[9]), int(src[10])))]
    pieces.sort(key=lambda p: -p[1])
    off = 0
    layout = {}
    for name, wdt, rng in pieces:
        assert off % wdt == 0
        layout[name] = (off, wdt, rng)
        off += wdt
    return layout, off


def kernel(x, c, rpb_table, hgrn_lb_logits, ada_w, ada_b, w_in, q_norm_g, kv_norm_g, w_uq, w_uk, w_uv, w_qidx,
           idx_k_norm_g, idx_k_norm_b, hgrn_out_norm_g, w_branch_a, w_branch_b, w_o, ln1_g, ln1_b, w_router,
           router_bias, w_exp_gate, w_exp_up, w_exp_down, w_sh_gate, w_sh_up, w_sh_down, ln2_g, ln2_b):
    bsz, s, d = x.shape
    depth = ada_w.shape[0]
    qr = w_uq.shape[1]
    ah, kvr, qk = w_uk.shape[1], w_uk.shape[2], w_uk.shape[3]
    idim = idx_k_norm_g.shape[1]
    ih = w_qidx.shape[2] // idim
    dv = hgrn_out_norm_g.shape[1]
    nh = w_branch_b.shape[1] // dv
    dk = hgrn_lb_logits.shape[1] // nh
    ne = w_router.shape[2]
    topk = min(IDX_TOPK, s // 4)
    alpha = float((2 * depth) ** 0.25)
    n_tok = bsz * s

    lower_bounds = jnp.cumsum(jax.nn.softmax(hgrn_lb_logits.astype(F32), axis=0), axis=0)
    layout, wtot = _proj_layout(d, qr, kvr, idim, ih, nh * dk, nh * dv)

    for l in range(depth):
        mod = _ada(c, ada_w[l], ada_b[l])[:, None, :]
        shift1, scale1, gate1, shift2, scale2, gate2 = jnp.split(mod, 6, axis=-1)

        cols = []
        for off, wdt, (lo, hi) in sorted(layout.values()):
            cols.append(w_in[l][:, lo:hi].astype(BF16))
            if wdt > hi - lo:
                cols.append(jnp.zeros((d, wdt - (hi - lo)), BF16))
        proj = _inproj(x, scale1, shift1, jnp.concatenate(cols, axis=1))
        blk = lambda name: layout[name][0] // layout[name][1]

        qb_sz = _pick(s, (256, 128))
        q_latT, q_idxT, w_idxT, k_idx, kv_lat = _prep(
            proj, blk("a"), layout["a"][1], (qr, kvr, idim, ih, ah, qk), q_norm_g[l], kv_norm_g[l],
            idx_k_norm_g[l], idx_k_norm_b[l], jnp.swapaxes(w_uq[l], 0, 1).astype(BF16),
            w_uk[l].astype(BF16), jnp.swapaxes(w_qidx[l], 0, 1).astype(BF16), qb_sz)
        kv_latT = jnp.concatenate([jnp.swapaxes(kv_lat, 1, 2), jnp.ones((bsz, 1, s), BF16),
                                   jnp.zeros((bsz, 7, s), BF16)], axis=1)
        o_a = _dsa(q_idxT, w_idxT, q_latT, k_idx, kv_lat, kv_latT,
                   rpb_table, jnp.swapaxes(w_uv[l], 1, 2).astype(BF16), topk, qb_sz)

        o_b = _hgrn(proj, (blk("hq"), blk("hf"), blk("hi"), blk("hg")), lower_bounds[l],
                    hgrn_out_norm_g[l], nh, dk, dv)

        x1, h2, lgT = _postmix(o_a, o_b, proj, blk("ga"), blk("gb"), x, gate1, scale2, shift2,
                               w_branch_a[l].astype(BF16), w_branch_b[l].astype(BF16), w_o[l].astype(BF16),
                               jnp.swapaxes(w_router[l], 0, 1).astype(BF16), ln1_g[l], ln1_b[l], alpha)

        ids, wts, rnk, sizes_f = _route(lgT, router_bias[l])
        eb = EXPERT_BLOCK
        sizes = sizes_f[:, 0].astype(I32)
        padded = (sizes + eb - 1) // eb * eb
        pend = jnp.cumsum(padded)
        pstart = pend - padded
        n_rows = -(-(n_tok * MOE_TOPK + ne * (eb - 1)) // eb) * eb
        blk_start = jnp.arange(n_rows // eb, dtype=I32) * eb
        blk_expert = jnp.minimum(jnp.sum((pend[None, :] <= blk_start[:, None]).astype(I32), axis=1), ne - 1)
        onehot = (blk_expert[:, None] == jnp.arange(ne, dtype=I32)[None, :]).astype(I32)
        blk_size = jnp.sum(onehot * sizes[None, :], axis=1)
        blk_pstart = jnp.sum(onehot * pstart[None, :], axis=1)
        nvalid = jnp.clip(blk_size - (blk_start - blk_pstart), 0, eb).astype(I32)
        dest = _dest(pstart.astype(I32), ids, rnk, ne)
        tm = _pick(s, (128,))
        dest_t = dest.reshape(bsz, MOE_TOPK, s // tm, tm).transpose(0, 2, 1, 3).reshape(n_tok // tm, MOE_TOPK, tm)
        wtsT = jnp.swapaxes(wts, 1, 2).reshape(n_tok, MOE_TOPK)

        h2f = h2.reshape(n_tok, d // 2)
        xs = _dispatch(dest_t, h2f, n_rows)
        ys = _experts(xs, blk_expert, nvalid, w_exp_gate[l], w_exp_up[l], w_exp_down[l])
        x = _final(ys, dest_t, wtsT, h2f, x1.reshape(n_tok, d), gate2, w_sh_gate[l].astype(BF16),
                   w_sh_up[l].astype(BF16), w_sh_down[l].astype(BF16), ln2_g[l], ln2_b[l], alpha,
                   s // tm).reshape(bsz, s, d)
    return x
```

```python
import functools
import math

import numpy as np
import jax
import jax.numpy as jnp
from jax import lax
from jax.experimental import pallas as pl
from jax.experimental.pallas import tpu as pltpu

F32 = jnp.float32
BF16 = jnp.bfloat16
I32 = jnp.int32
U32 = jnp.uint32

EPS = 1e-6
IDX_TOPK = 256
RPB_MAX_DIST = 128
MOE_TOPK = 8
N_GROUPS = 8
TOPK_GROUPS = 4
ROUTED_SCALE = 2.5
EXPERT_BLOCK = 512

V7X_VMEM_LIMIT_BYTES = 56 * 1024 * 1024
LANES = 128
INT_MIN = -2 ** 31
NEG_BIG = -1e30
DSA_FAR_GROUP = 3
LOG2E = math.log2(math.e)


def _cparams(*sem):
    return pltpu.CompilerParams(dimension_semantics=tuple(sem), vmem_limit_bytes=V7X_VMEM_LIMIT_BYTES)


def _pick(n, prefs):
    for p in prefs:
        if n % p == 0:
            return p
    return n


def _sigmoid(v):
    return 1.0 / (1.0 + jnp.exp(-v))


def _silu(v):
    return v * _sigmoid(v)


def _ln(v):
    mu = jnp.mean(v, axis=-1, keepdims=True)
    d = v - mu
    var = jnp.mean(d * d, axis=-1, keepdims=True)
    return d * lax.rsqrt(var + EPS)


def _rms(v):
    return v * lax.rsqrt(jnp.mean(v * v, axis=-1, keepdims=True) + EPS)


def _dot(a, b):
    return jnp.dot(a, b, preferred_element_type=F32)


def _pack_bf16_pairs(v):
    n = v.shape[1] // 2
    lo = pltpu.bitcast(v[:, :n].astype(BF16).astype(F32), U32) >> 16
    hi = pltpu.bitcast(v[:, n:].astype(BF16).astype(F32), U32) & jnp.uint32(0xFFFF0000)
    return lo | hi


def _unpack_bf16_pairs(u):
    return pltpu.bitcast(u << 16, F32), pltpu.bitcast(u & jnp.uint32(0xFFFF0000), F32)


def _dot_nt(a, b):
    return lax.dot_general(a, b, (((1,), (1,)), ((), ())), preferred_element_type=F32)


def _dot_tn(a, b):
    return lax.dot_general(a, b, (((0,), (0,)), ((), ())), preferred_element_type=F32)


def _ada_kernel(c_ref, w_ref, b_ref, o_ref):
    c = c_ref[...]
    o_ref[...] = _dot(_silu(c).astype(BF16), w_ref[...].astype(BF16)) + b_ref[...]


def _ada(c, w, b):
    bsz, d = c.shape
    n = w.shape[1]
    rows = 8
    cp = jnp.zeros((rows, d), F32).at[:bsz].set(c)
    tn = _pick(n, (1024, 512, 256, 128))
    out = pl.pallas_call(
        _ada_kernel,
        grid=(n // tn,),
        in_specs=[pl.BlockSpec((rows, d), lambda j: (0, 0)),
                  pl.BlockSpec((d, tn), lambda j: (0, j)),
                  pl.BlockSpec((1, tn), lambda j: (0, j))],
        out_specs=pl.BlockSpec((rows, tn), lambda j: (0, j)),
        out_shape=jax.ShapeDtypeStruct((rows, n), F32),
        compiler_params=_cparams("arbitrary"),
        name="ada",
    )(cp, w, b.reshape(1, n))
    return out[:bsz]


def _inproj_kernel(x_ref, sc_ref, sh_ref, w_ref, o_ref, h_ref):
    @pl.when(pl.program_id(2) == 0)
    def _():
        h = _ln(x_ref[0]) * (1.0 + sc_ref[0]) + sh_ref[0]
        h_ref[...] = h.astype(BF16)

    o_ref[0] = _dot(h_ref[...], w_ref[...]).astype(o_ref.dtype)


def _inproj(x, scale, shift, w):
    bsz, s, d = x.shape
    n = w.shape[1]
    tm = _pick(s, (1024, 512, 256, 128))
    tn = _pick(n, (1024, 512, 256, 128))
    return pl.pallas_call(
        _inproj_kernel,
        grid=(bsz, s // tm, n // tn),
        in_specs=[pl.BlockSpec((1, tm, d), lambda b, i, j: (b, i, 0)),
                  pl.BlockSpec((1, 1, d), lambda b, i, j: (b, 0, 0)),
                  pl.BlockSpec((1, 1, d), lambda b, i, j: (b, 0, 0)),
                  pl.BlockSpec((d, tn), lambda b, i, j: (0, j))],
        out_specs=pl.BlockSpec((1, tm, tn), lambda b, i, j: (b, i, j)),
        out_shape=jax.ShapeDtypeStruct((bsz, s, n), BF16),
        scratch_shapes=[pltpu.VMEM((tm, d), BF16)],
        compiler_params=_cparams("arbitrary", "arbitrary", "arbitrary"),
        name="inproj",
    )(x, scale, shift, w)


def _prep_kernel(a_ref, qg_ref, kvg_ref, ikg_ref, ikb_ref, wuqT_ref, wuk_ref, wqiT_ref,
                 qlatT_ref, qidxT_ref, widxT_ref, kidx_ref, kv_ref, *, qr, kvr, idim, ih, ah, qk):
    a = a_ref[0]
    af = a.astype(F32)
    cq = (_rms(af[:, :qr]) * qg_ref[...]).astype(BF16)
    ckv = af[:, qr:qr + kvr]
    ki = af[:, qr + kvr:qr + kvr + idim]
    kv_ref[0] = (_rms(ckv) * kvg_ref[...]).astype(BF16)
    kidx_ref[0] = (_ln(ki) * ikg_ref[...] + ikb_ref[...]).astype(BF16)
    tail = a[:, qr + kvr:qr + kvr + LANES]
    eye = (lax.broadcasted_iota(I32, (LANES, LANES), 0) == lax.broadcasted_iota(I32, (LANES, LANES), 1))
    tailT = _dot_nt(jnp.where(eye, 1.0, 0.0).astype(BF16), tail)
    widxT_ref[0] = tailT[idim:idim + ih] * float((ih * idim) ** -0.5)
    tm = a.shape[0]
    qT = _dot_nt(wuqT_ref[...], cq)
    for h in range(ah):
        qh = qT[h * qk:(h + 1) * qk].astype(BF16)
        qlatT_ref[0, 0, :, h * tm:(h + 1) * tm] = (_dot(wuk_ref[h], qh) * float(qk ** -0.5 * LOG2E)).astype(BF16)
    qiT = _dot_nt(wqiT_ref[...], cq)
    for h in range(ih):
        qidxT_ref[0, 0, :, h * tm:(h + 1) * tm] = qiT[h * idim:(h + 1) * idim].astype(BF16)


def _prep(proj, a_blk, wa, dims, q_norm_g, kv_norm_g, ikg, ikb, w_uqT, w_uk, w_qidxT, tm):
    bsz, s, _ = proj.shape
    qr, kvr, idim, ih, ah, qk = dims
    assert idim + ih <= LANES and qr + kvr + LANES <= wa
    kern = functools.partial(_prep_kernel, qr=qr, kvr=kvr, idim=idim, ih=ih, ah=ah, qk=qk)
    full = lambda shape: pl.BlockSpec(shape, lambda b, i: (0,) * len(shape))
    return pl.pallas_call(
        kern,
        grid=(bsz, s // tm),
        in_specs=[pl.BlockSpec((1, tm, wa), lambda b, i: (b, i, a_blk)),
                  full((1, qr)), full((1, kvr)), full((1, idim)), full((1, idim)),
                  full(w_uqT.shape), full(w_uk.shape), full(w_qidxT.shape)],
        out_specs=[pl.BlockSpec((1, 1, kvr, ah * tm), lambda b, i: (b, i, 0, 0)),
                   pl.BlockSpec((1, 1, idim, ih * tm), lambda b, i: (b, i, 0, 0)),
                   pl.BlockSpec((1, ih, tm), lambda b, i: (b, 0, i)),
                   pl.BlockSpec((1, tm, idim), lambda b, i: (b, i, 0)),
                   pl.BlockSpec((1, tm, kvr), lambda b, i: (b, i, 0))],
        out_shape=[jax.ShapeDtypeStruct((bsz, s // tm, kvr, ah * tm), BF16),
                   jax.ShapeDtypeStruct((bsz, s // tm, idim, ih * tm), BF16),
                   jax.ShapeDtypeStruct((bsz, ih, s), F32),
                   jax.ShapeDtypeStruct((bsz, s, idim), BF16),
                   jax.ShapeDtypeStruct((bsz, s, kvr), BF16)],
        compiler_params=_cparams("arbitrary", "arbitrary"),
        name="prep",
    )(proj, q_norm_g.reshape(1, qr), kv_norm_g.reshape(1, kvr), ikg.reshape(1, idim), ikb.reshape(1, idim),
      w_uqT, w_uk, w_qidxT)


def _dsa_kernel(qidxT_ref, widxT_ref, qlatT_ref, kidx_ref, kv_ref, kvT_ref, bkt0_ref, bkt1_ref, rel_ref, wuvT_ref,
                o_ref, keys_ref, gmax_ref, s_ref, p_ref, m_ref, a_ref, acc_ref, t0_ref, t1_ref, lstrict_ref, taken_ref, sx_ref, px_ref, ax_ref,
                *, qb_sz, topk, ih, ah, c, nb):
    QB = qb_sz
    qb = pl.program_id(1)
    nchunks = qb + 1
    qpos = qb * QB + lax.broadcasted_iota(I32, (1, QB), 1)
    kpos = lax.broadcasted_iota(I32, (QB, 1), 0)
    wT = widxT_ref[0]
    hcols = lambda h: slice(h * QB, (h + 1) * QB)

    @pl.when(jnp.logical_and(pl.program_id(0) == 0, qb == 0))
    def _():
        for bkt_ref, t_ref in ((bkt0_ref, t0_ref), (bkt1_ref, t1_ref)):
            bkt = bkt_ref[...]
            for h in range(ah):
                tile = jnp.zeros((QB, QB), F32)
                for b in range(nb - 1):
                    tile = jnp.where(bkt == b, rel_ref[b, h], tile)
                t_ref[:, hcols(h)] = tile
        lstrict_ref[...] = jnp.where(lax.broadcasted_iota(I32, (QB, QB), 1) < lax.broadcasted_iota(I32, (QB, QB), 0),
                                     1.0, 0.0).astype(BF16)

    def score_chunk(off, z_ref):
        z_ref[:, :ih * QB] = _dot(kidx_ref[0, pl.ds(off, QB), :], qidxT_ref[0, 0])
        acc = jnp.zeros((QB, QB), F32)
        for h in range(ih):
            acc = acc + wT[h:h + 1] * jnp.maximum(z_ref[:, hcols(h)], 0.0)
        bits = pltpu.bitcast(acc, I32)
        skey = bits ^ ((bits >> 31) & 0x7FFFFFFF)
        causal = (off + kpos) <= qpos
        skey = jnp.where(causal, skey, INT_MIN)
        keys_ref[pl.ds(off, QB), :] = skey
        gmax_ref[...] = jnp.maximum(gmax_ref[...], skey)

    def score_pair(kp, carry):
        off = pl.multiple_of(kp * (2 * QB), 2 * QB)
        score_chunk(off, s_ref)
        score_chunk(off + QB, sx_ref.at[0])
        return carry

    gmax_ref[...] = jnp.full(gmax_ref.shape, INT_MIN, I32)
    lax.fori_loop(0, nchunks // 2, score_pair, 0)

    @pl.when(lax.rem(nchunks, 2) == 1)
    def _():
        score_chunk(pl.multiple_of((nchunks - 1) * QB, QB), s_ref)

    gmax = gmax_ref[...]
    lo0 = jnp.min(gmax, axis=0, keepdims=True) if QB >= topk else jnp.full((1, QB), INT_MIN, I32)
    hi0 = jnp.max(gmax, axis=0, keepdims=True) + 1

    def count_ge(cand):
        def body(kc, cnt):
            off = pl.multiple_of(kc * QB, QB)
            hit = jnp.where(keys_ref[pl.ds(off, QB), :] >= cand, 1, 0)
            return cnt + jnp.sum(hit.reshape(QB // 8, 8, QB), axis=0)

        cnt = lax.fori_loop(0, nchunks, body, jnp.zeros((8, QB), I32))
        return jnp.sum(cnt.astype(F32), axis=0, keepdims=True)

    def bis_cond(st):
        it, _, _, done = st
        return jnp.logical_and(it < 34, jnp.min(done) < 0.5)

    def bis_body(st):
        it, lo, hi, done = st
        cand = (lo >> 1) + (hi >> 1) + (lo & hi & 1)
        cnt = count_ge(cand)
        ge = cnt >= float(topk)
        conv = cand == lo
        fin = jnp.logical_or(conv, cnt == float(topk))
        thr_new = jnp.where(conv, lo, cand)
        lo = jnp.where(fin, thr_new, jnp.where(ge, cand, lo))
        hi = jnp.where(fin, thr_new + 1, jnp.where(ge, hi, cand))
        return it + 1, lo, hi, jnp.where(fin, 1.0, done)

    _, thr, _, _ = lax.while_loop(bis_cond, bis_body, (jnp.int32(0), lo0, hi0, jnp.zeros((1, QB), F32)))
    need = float(topk) - count_ge(thr + 1)

    m_ref[...] = jnp.full(m_ref.shape, NEG_BIG, F32)
    acc_ref[...] = jnp.zeros(acc_ref.shape, F32)

    def attn_chunk(off, bias_ref, diag, taken, s_ref=s_ref, p_ref=p_ref, a_ref=a_ref):
        kt = keys_ref[pl.ds(off, QB), :]
        eq = kt == thr
        eq01 = jnp.where(eq, 1.0, 0.0)
        before = _dot(lstrict_ref[...], eq01.astype(BF16)) + taken
        sel = jnp.logical_or(kt > thr, jnp.logical_and(eq, before < need))
        taken = taken + jnp.sum(eq01, axis=0, keepdims=True)
        if diag:
            sel = jnp.logical_and(sel, (off + kpos) <= qpos)
        madd = jnp.where(sel, 0.0, NEG_BIG)
        s_ref[:, :ah * QB] = _dot(kv_ref[0, pl.ds(off, QB), :], qlatT_ref[0, 0])
        for g in range(ah * QB // LANES):
            cols = slice(g * LANES, (g + 1) * LANES)
            qcols = slice(g * LANES % QB, g * LANES % QB + LANES)
            s = s_ref[:, cols] + madd[:, qcols]
            if bias_ref is not None:
                s = s + bias_ref[:, cols]
            m_prev = m_ref[:, cols]
            m_new = jnp.maximum(m_prev, jnp.max(s, axis=0, keepdims=True))
            p_ref[:, cols] = jnp.exp2(s - m_new).astype(BF16)
            a_ref[:, cols] = jnp.exp2(m_prev - m_new)
            m_ref[:, cols] = m_new
        acc_ref[...] = a_ref[...] * acc_ref[...] + _dot(kvT_ref[0, :, pl.ds(off, QB)], p_ref[...])
        return taken

    G = DSA_FAR_GROUP

    def far_group(kg, taken):
        off = pl.multiple_of(kg * (G * QB), G * QB)
        taken = attn_chunk(off, None, False, taken)
        for j in range(1, G):
            taken = attn_chunk(off + j * QB, None, False, taken, sx_ref.at[j - 1], px_ref.at[j - 1], ax_ref.at[j - 1])
        return taken

    nfar = jnp.maximum(qb - 1, 0)
    ngroups = nfar // G
    taken = lax.fori_loop(0, ngroups, far_group, jnp.zeros((1, QB), F32))

    def far_single(r, taken):
        return attn_chunk(pl.multiple_of((ngroups * G + r) * QB, QB), None, False, taken)

    taken_ref[...] = lax.fori_loop(0, nfar - ngroups * G, far_single, taken)

    @pl.when(qb >= 1)
    def _():
        taken_ref[...] = attn_chunk(pl.multiple_of((qb - 1) * QB, QB), t1_ref, False, taken_ref[...])

    attn_chunk(pl.multiple_of(qb * QB, QB), t0_ref, True, taken_ref[...])

    outs = []
    for h in range(ah):
        o = (acc_ref[:c, hcols(h)] / acc_ref[c:c + 1, hcols(h)]).astype(BF16)
        outs.append(_dot(wuvT_ref[h], o))
    o_ref[0] = jnp.concatenate(outs, axis=0).T.astype(o_ref.dtype)


def _t5_bucket_np(d, nbuckets):
    max_exact = nbuckets // 2
    dd = np.maximum(d, 1).astype(np.float32)
    large = max_exact + (np.log(dd / np.float32(max_exact)) / np.float32(math.log(RPB_MAX_DIST / max_exact))
                         * np.float32(nbuckets - max_exact)).astype(np.int32)
    large = np.minimum(large, nbuckets - 1)
    return np.where(d < max_exact, d, large).astype(np.int32)


def _dsa(q_idxT, w_idxT, q_latT, k_idx, kv_lat, kv_latT, rpb_table, w_uvT, topk, QB):
    bsz, nqb, idim, ihq = q_idxT.shape
    c, ahq = q_latT.shape[2], q_latT.shape[3]
    ca = kv_latT.shape[1]
    ih, ah = ihq // QB, ahq // QB
    s = nqb * QB
    vd = w_uvT.shape[1]
    nb = rpb_table.shape[0]
    assert QB >= RPB_MAX_DIST and ih <= ah and DSA_FAR_GROUP >= 2
    j = np.arange(QB)[:, None]
    i = np.arange(QB)[None, :]
    bkt0 = jnp.asarray(_t5_bucket_np(np.maximum(i - j, 0), nb))
    bkt1 = jnp.asarray(_t5_bucket_np(QB + i - j, nb))
    assert int(_t5_bucket_np(np.array([RPB_MAX_DIST]), nb)[0]) == nb - 1
    rel = (rpb_table.astype(F32) - rpb_table[nb - 1].astype(F32)[None, :]) * LOG2E
    kern = functools.partial(_dsa_kernel, qb_sz=QB, topk=topk, ih=ih, ah=ah, c=c, nb=nb)
    const = lambda shape: pl.BlockSpec(shape, lambda b, i: (0,) * len(shape), pipeline_mode=pl.Buffered(1))
    hw = max(ih, ah) * QB
    return pl.pallas_call(
        kern,
        grid=(bsz, nqb),
        in_specs=[pl.BlockSpec((1, 1, idim, ih * QB), lambda b, i: (b, i, 0, 0)),
                  pl.BlockSpec((1, ih, QB), lambda b, i: (b, 0, i)),
                  pl.BlockSpec((1, 1, c, ah * QB), lambda b, i: (b, i, 0, 0)),
                  pl.BlockSpec((1, s, idim), lambda b, i: (b, 0, 0), pipeline_mode=pl.Buffered(1)),
                  pl.BlockSpec((1, s, c), lambda b, i: (b, 0, 0), pipeline_mode=pl.Buffered(1)),
                  pl.BlockSpec((1, ca, s), lambda b, i: (b, 0, 0), pipeline_mode=pl.Buffered(1)),
                  const((QB, QB)), const((QB, QB)),
                  pl.BlockSpec(memory_space=pltpu.SMEM), const(w_uvT.shape)],
        out_specs=pl.BlockSpec((1, QB, ah * vd), lambda b, i: (b, i, 0)),
        out_shape=jax.ShapeDtypeStruct((bsz, s, ah * vd), BF16),
        scratch_shapes=[pltpu.VMEM((s, QB), I32),
                        pltpu.VMEM((QB, QB), I32),
                        pltpu.VMEM((QB, hw), F32),
                        pltpu.VMEM((QB, ah * QB), BF16),
                        pltpu.VMEM((1, ah * QB), F32),
                        pltpu.VMEM((1, ah * QB), F32),
                        pltpu.VMEM((ca, ah * QB), F32),
                        pltpu.VMEM((QB, ah * QB), F32),
                        pltpu.VMEM((QB, ah * QB), F32),
                        pltpu.VMEM((QB, QB), BF16),
                        pltpu.VMEM((1, QB), F32),
                        pltpu.VMEM((DSA_FAR_GROUP - 1, QB, ah * QB), F32),
                        pltpu.VMEM((DSA_FAR_GROUP - 1, QB, ah * QB), BF16),
                        pltpu.VMEM((DSA_FAR_GROUP - 1, 1, ah * QB), F32)],
        compiler_params=_cparams("arbitrary", "arbitrary"),
        name="dsa",
    )(q_idxT, w_idxT, q_latT, k_idx, kv_lat, kv_latT, bkt0, bkt1, rel, w_uvT)


HGRN_CHUNK = 64
HGRN_SUB = 16
HGRN_EXP_CLAMP = 80.0


def _dot_exact(a, b, dims):
    return lax.dot_general(a, b, (dims, ((), ())), precision=lax.Precision.HIGHEST, preferred_element_type=F32)


def _hgrn_kernel(hq_ref, hf_ref, hi_ref, hg_ref, lb_ref, g_ref, o_ref, st_ref, st0_ref, *, nh, dk, dv, tt):
    C, SUB = HGRN_CHUNK, HGRN_SUB

    @pl.when(pl.program_id(1) == 0)
    def _():
        st_ref[...] = jnp.zeros(st_ref.shape, F32)

    r = lax.broadcasted_iota(I32, (C, C), 0)
    cc = lax.broadcasted_iota(I32, (C, C), 1)
    tri_mask = r >= cc
    tri = jnp.where(tri_mask, 1.0, 0.0).astype(BF16)
    g = g_ref[...]
    st0_ref[...] = st_ref[...]
    decay = jnp.zeros((1, nh * dk), F32)
    lb = lb_ref[...]

    for c in range(tt // C):
        rows = pl.ds(c * C, C)
        f = lb + (1.0 - lb) * _sigmoid(hf_ref[0, rows, :].astype(F32))
        lf = jnp.log(f)
        t1 = lf.astype(BF16)
        r1 = lf - t1.astype(F32)
        t2 = r1.astype(BF16)
        t3 = (r1 - t2.astype(F32)).astype(BF16)
        b = _dot(tri, t1) + _dot(tri, t2) + _dot(tri, t3)
        kk = 1.0 - f
        hq = hq_ref[0, rows, :].astype(F32)
        q = _silu(hq) * float(dk ** -0.5)
        qe = (q * jnp.exp(b)).astype(BF16)
        b_last = b[C - 1:C]
        k_dec = (kk * jnp.exp(b_last - b)).astype(BF16)
        dec_last = jnp.exp(b_last)
        qs, ks = [], []
        for i in range(C // SUB):
            lo, n = i * SUB, (i + 1) * SUB
            bi = b[lo - 1:lo] if i > 0 else jnp.zeros((1, nh * dk), F32)
            decay = jnp.maximum(decay, bi - b[n - 1:n])
            qs.append((q[lo:n] * jnp.exp(b[lo:n] - bi)).astype(BF16))
            ks.append((kk * jnp.exp(jnp.minimum(bi - b, HGRN_EXP_CLAMP))).astype(BF16))
        v_all = hi_ref[0, rows, :]
        outs = []
        for h in range(nh):
            kc = slice(h * dk, (h + 1) * dk)
            v = v_all[:, h * dv:(h + 1) * dv]
            stT = st_ref[h]
            att = jnp.concatenate([_dot_nt(qs[i][:, kc], ks[i][:, kc]) for i in range(C // SUB)], axis=0)
            att = jnp.where(tri_mask, att, 0.0)
            o = _dot_nt(qe[:, kc], stT.astype(BF16)) + _dot(att.astype(BF16), v)
            st_ref[h] = stT * dec_last[:, kc] + _dot_tn(v, k_dec[:, kc])
            outs.append(_rms(o) * g)
        o_all = jnp.concatenate(outs, axis=1) * _silu(hg_ref[0, rows, :].astype(F32))
        o_ref[0, rows, :] = o_all.astype(o_ref.dtype)

    @pl.when(jnp.max(decay) > HGRN_EXP_CLAMP)
    def _():
        row = lax.broadcasted_iota(I32, (SUB, 1), 0)

        def head(h, carry):
            ko = pl.multiple_of(h * dk, dk)
            vo = pl.multiple_of(h * dv, dv)
            lb = lb_ref[:, pl.ds(ko, dk)]

            def slab(j, S):
                rows = pl.ds(pl.multiple_of(j * SUB, SUB), SUB)
                f = lb + (1.0 - lb) * _sigmoid(hf_ref[0, rows, pl.ds(ko, dk)].astype(F32))
                kk = 1.0 - f
                hq = hq_ref[0, rows, pl.ds(ko, dk)].astype(F32)
                q = _silu(hq) * float(dk ** -0.5)
                v = hi_ref[0, rows, pl.ds(vo, dv)].astype(F32)
                o = jnp.zeros((SUB, dv), F32)
                for r in range(SUB):
                    S = S * f[r:r + 1] + _dot_exact(jnp.where(row == r, v, 0.0), kk, ((0,), (0,)))
                    o = jnp.where(row == r, _dot_exact(q, S, ((1,), (1,))), o)
                hg = hg_ref[0, rows, pl.ds(vo, dv)].astype(F32)
                o_ref[0, rows, pl.ds(vo, dv)] = (_rms(o) * g * _silu(hg)).astype(o_ref.dtype)
                return S

            st_ref[h] = lax.fori_loop(0, tt // SUB, slab, st0_ref[h])
            return carry

        lax.fori_loop(0, nh, head, 0)


def _hgrn(proj, blks, lb, g, nh, dk, dv):
    bsz, s, _ = proj.shape
    tt = _pick(s, (256, 128, 64))
    kern = functools.partial(_hgrn_kernel, nh=nh, dk=dk, dv=dv, tt=tt)
    col = lambda blk, wdt: pl.BlockSpec((1, tt, wdt), lambda b, i: (b, i, blk))
    return pl.pallas_call(
        kern,
        grid=(bsz, s // tt),
        in_specs=[col(blks[0], nh * dk), col(blks[1], nh * dk), col(blks[2], nh * dv), col(blks[3], nh * dv),
                  pl.BlockSpec((1, nh * dk), lambda b, i: (0, 0)),
                  pl.BlockSpec((1, dv), lambda b, i: (0, 0))],
        out_specs=pl.BlockSpec((1, tt, nh * dv), lambda b, i: (b, i, 0)),
        out_shape=jax.ShapeDtypeStruct((bsz, s, nh * dv), BF16),
        scratch_shapes=[pltpu.VMEM((nh, dv, dk), F32), pltpu.VMEM((nh, dv, dk), F32)],
        compiler_params=_cparams("arbitrary", "arbitrary"),
        name="hgrn",
    )(proj, proj, proj, proj, lb.reshape(1, nh * dk), g.reshape(1, dv))


def _postmix_kernel(oa_ref, ob_ref, ga_ref, gb_ref, x_ref, g1_ref, sc2_ref, sh2_ref,
                    wa_ref, wb_ref, wo_ref, wrT_ref, lng_ref, lnb_ref,
                    x1_ref, h2_ref, lgT_ref, *, alpha):
    ya = _dot(oa_ref[0], wa_ref[...])
    yb = _dot(ob_ref[0], wb_ref[...])
    mix = _sigmoid(ga_ref[0].astype(F32)) * ya + _sigmoid(gb_ref[0].astype(F32)) * yb
    mixed = _dot(mix.astype(BF16), wo_ref[...])
    x1 = _ln(alpha * x_ref[0] + g1_ref[0] * mixed) * lng_ref[...] + lnb_ref[...]
    x1_ref[0] = x1
    h2 = _ln(x1) * (1.0 + sc2_ref[0]) + sh2_ref[0]
    h2_ref[0] = _pack_bf16_pairs(h2)
    lgT_ref[0] = _dot_nt(wrT_ref[...], h2.astype(BF16))


def _postmix(o_a, o_b, proj, ga_blk, gb_blk, x, gate1, scale2, shift2, wa, wb, wo, wrT, lng, lnb, alpha):
    bsz, s, d = x.shape
    ne = wrT.shape[0]
    tm = _pick(s, (256, 128))
    kern = functools.partial(_postmix_kernel, alpha=alpha)
    row = lambda wdt: pl.BlockSpec((1, tm, wdt), lambda b, i: (b, i, 0))
    vec = pl.BlockSpec((1, 1, d), lambda b, i: (b, 0, 0))
    full = lambda a: pl.BlockSpec(a.shape, lambda b, i: (0,) * a.ndim, pipeline_mode=pl.Buffered(1))
    return pl.pallas_call(
        kern,
        grid=(bsz, s // tm),
        in_specs=[row(o_a.shape[2]), row(o_b.shape[2]),
                  pl.BlockSpec((1, tm, d), lambda b, i: (b, i, ga_blk)),
                  pl.BlockSpec((1, tm, d), lambda b, i: (b, i, gb_blk)),
                  row(d), vec, vec, vec,
                  full(wa), full(wb), full(wo), full(wrT),
                  pl.BlockSpec((1, d), lambda b, i: (0, 0)), pl.BlockSpec((1, d), lambda b, i: (0, 0))],
        out_specs=[row(d), row(d // 2), pl.BlockSpec((1, ne, tm), lambda b, i: (b, 0, i))],
        out_shape=[jax.ShapeDtypeStruct((bsz, s, d), F32),
                   jax.ShapeDtypeStruct((bsz, s, d // 2), U32),
                   jax.ShapeDtypeStruct((bsz, ne, s), F32)],
        compiler_params=_cparams("arbitrary", "arbitrary"),
        name="postmix",
    )(o_a, o_b, proj, proj, x, gate1, scale2, shift2, wa, wb, wo, wrT, lng.reshape(1, d), lnb.reshape(1, d))


def _route_kernel(lg_ref, bias_ref, ids_ref, wts_ref, rnk_ref, sizes_ref, upper_ref, carry_ref, *, ne):
    first = jnp.logical_and(pl.program_id(0) == 0, pl.program_id(1) == 0)
    tn = lg_ref.shape[2]

    @pl.when(first)
    def _():
        carry_ref[...] = jnp.zeros(carry_ref.shape, F32)
        r_ = lax.broadcasted_iota(I32, (tn, tn), 0)
        c_ = lax.broadcasted_iota(I32, (tn, tn), 1)
        upper_ref[...] = jnp.where(r_ < c_, 1.0, 0.0).astype(BF16)

    per = ne // N_GROUPS
    s = _sigmoid(lg_ref[0])
    bz = s + bias_ref[...]
    ridx = lax.broadcasted_iota(I32, (per, tn), 0)
    neg_inf = jnp.float32(-jnp.inf)
    gs = []
    for g in range(N_GROUPS):
        blk = bz[g * per:(g + 1) * per]
        m1 = jnp.max(blk, axis=0, keepdims=True)
        first_hit = jnp.min(jnp.where(blk == m1, ridx, per), axis=0, keepdims=True)
        m2 = jnp.max(jnp.where(ridx == first_hit, neg_inf, blk), axis=0, keepdims=True)
        gs.append(m1 + m2)
    emask_rows = []
    for g in range(N_GROUPS):
        rank = jnp.zeros((1, tn), I32)
        for g2 in range(N_GROUPS):
            if g2 == g:
                continue
            beats = (gs[g2] > gs[g]) if g2 > g else (gs[g2] >= gs[g])
            rank = rank + jnp.where(beats, 1, 0)
        emask_rows.append(jnp.broadcast_to(rank < TOPK_GROUPS, (per, tn)))
    emask = jnp.concatenate(emask_rows, axis=0)
    masked = jnp.where(emask, bz, neg_inf)
    eidx = lax.broadcasted_iota(I32, (ne, tn), 0)
    rank = jnp.zeros((ne, tn), I32)
    for e2 in range(ne):
        row = masked[e2:e2 + 1]
        beats = jnp.logical_or(row > masked, jnp.logical_and(row == masked, e2 < eidx))
        rank = rank + jnp.where(beats, 1, 0)
    sel = rank < MOE_TOPK
    sel01 = jnp.where(sel, 1.0, 0.0)
    denom = jnp.sum(jnp.where(sel, s, 0.0), axis=0, keepdims=True)
    wn = s / denom * ROUTED_SCALE
    before = _dot(sel01.astype(BF16), upper_ref[...]) + carry_ref[:, 0:1]
    ids, wts, rnk = [], [], []
    for k in range(MOE_TOPK):
        hit = rank == k
        ids.append(jnp.sum(jnp.where(hit, eidx, 0), axis=0, keepdims=True))
        wts.append(jnp.sum(jnp.where(hit, wn, 0.0), axis=0, keepdims=True))
        rnk.append(jnp.sum(jnp.where(hit, before, 0.0), axis=0, keepdims=True))
    ids_ref[0] = jnp.concatenate(ids, axis=0)
    wts_ref[0] = jnp.concatenate(wts, axis=0)
    rnk_ref[0] = jnp.concatenate(rnk, axis=0).astype(I32)
    carry_ref[...] = carry_ref[...] + jnp.sum(sel01, axis=1, keepdims=True)
    sizes_ref[...] = carry_ref[...]


def _route(lgT, bias):
    bsz, ne, s = lgT.shape
    tn = _pick(s, (1024, 512, 256, 128))
    kern = functools.partial(_route_kernel, ne=ne)
    slot = pl.BlockSpec((1, MOE_TOPK, tn), lambda b, j: (b, 0, j))
    return pl.pallas_call(
        kern,
        grid=(bsz, s // tn),
        in_specs=[pl.BlockSpec((1, ne, tn), lambda b, j: (b, 0, j)),
                  pl.BlockSpec((ne, 1), lambda b, j: (0, 0))],
        out_specs=[slot, slot, slot, pl.BlockSpec((ne, LANES), lambda b, j: (0, 0))],
        out_shape=[jax.ShapeDtypeStruct((bsz, MOE_TOPK, s), I32),
                   jax.ShapeDtypeStruct((bsz, MOE_TOPK, s), F32),
                   jax.ShapeDtypeStruct((bsz, MOE_TOPK, s), I32),
                   jax.ShapeDtypeStruct((ne, LANES), F32)],
        scratch_shapes=[pltpu.VMEM((tn, tn), BF16), pltpu.VMEM((ne, LANES), F32)],
        compiler_params=_cparams("arbitrary", "arbitrary"),
        name="route",
    )(lgT, bias.reshape(ne, 1))


def _dest_kernel(pstart_ref, ids_ref, rnk_ref, o_ref, *, ne):
    ids = ids_ref[0]
    base = jnp.zeros(ids.shape, I32)
    for e in range(ne):
        base = jnp.where(ids == e, pstart_ref[e], base)
    o_ref[0] = base + rnk_ref[0]


def _dest(pstart, ids, rnk, ne):
    bsz, k, s = ids.shape
    tn = _pick(s, (2048, 1024, 512, 256, 128))
    blk = lambda: pl.BlockSpec((1, k, tn), lambda b, j, ps: (b, 0, j))
    return pl.pallas_call(
        functools.partial(_dest_kernel, ne=ne),
        grid_spec=pltpu.PrefetchScalarGridSpec(num_scalar_prefetch=1, grid=(bsz, s // tn),
                                               in_specs=[blk(), blk()], out_specs=blk()),
        out_shape=jax.ShapeDtypeStruct((bsz, k, s), I32),
        compiler_params=_cparams("arbitrary", "arbitrary"),
        name="dest",
    )(pstart, ids, rnk)


def _dispatch_kernel(dst_ref, h_ref, xs_hbm, sem, *, tm, topk):
    def start(t, c):
        for k in range(topk):
            pltpu.make_async_copy(h_ref.at[pl.ds(t, 1)], xs_hbm.at[pl.ds(dst_ref[0, k, t], 1)], sem.at[0]).start()
        return c

    lax.fori_loop(0, tm, start, 0)

    def wait(t, c):
        for k in range(topk):
            pltpu.make_async_copy(h_ref.at[pl.ds(t, 1)], xs_hbm.at[pl.ds(0, 1)], sem.at[0]).wait()
        return c

    lax.fori_loop(0, tm, wait, 0)


def _dispatch(dest, h2, n_rows):
    n, d = h2.shape
    nt, topk, tm = dest.shape
    return pl.pallas_call(
        functools.partial(_dispatch_kernel, tm=tm, topk=topk),
        grid=(nt,),
        in_specs=[pl.BlockSpec((1, topk, tm), lambda i: (i, 0, 0), memory_space=pltpu.SMEM),
                  pl.BlockSpec((tm, d), lambda i: (i, 0))],
        out_specs=pl.BlockSpec(memory_space=pl.ANY),
        out_shape=jax.ShapeDtypeStruct((n_rows, d), h2.dtype),
        scratch_shapes=[pltpu.SemaphoreType.DMA((1,))],
        compiler_params=_cparams("arbitrary"),
        name="dispatch",
    )(dest, h2)


def _experts_kernel(be_ref, nv_ref, x_ref, wg_ref, wu_ref, wd_ref, y_ref, wgb, wub, wdb, *, blk):
    i = pl.program_id(0)
    prev_e = be_ref[jnp.maximum(i - 1, 0)]

    @pl.when(jnp.logical_or(i == 0, be_ref[i] != prev_e))
    def _():
        wgb[...] = wg_ref[0].astype(BF16)
        wub[...] = wu_ref[0].astype(BF16)
        wdb[...] = wd_ref[0].astype(BF16)

    @pl.when(nv_ref[i] > 0)
    def _():
        rows = lax.broadcasted_iota(I32, (blk, 1), 0)
        xu = jnp.where(rows < nv_ref[i], x_ref[...], jnp.uint32(0))
        x = jnp.concatenate(_unpack_bf16_pairs(xu), axis=1).astype(BF16)
        act = (_silu(_dot(x, wgb[...])) * _dot(x, wub[...])).astype(BF16)
        y_ref[...] = _pack_bf16_pairs(_dot(act, wdb[...]))


def _experts(xs, be, nvalid, wg, wu, wd):
    n_rows, dh = xs.shape
    ne, d, f = wg.shape
    blk = EXPERT_BLOCK
    nblocks = n_rows // blk
    grid_spec = pltpu.PrefetchScalarGridSpec(
        num_scalar_prefetch=2,
        grid=(nblocks,),
        in_specs=[pl.BlockSpec((blk, dh), lambda i, be, nv: (i, 0)),
                  pl.BlockSpec((1, d, f), lambda i, be, nv: (be[i], 0, 0)),
                  pl.BlockSpec((1, d, f), lambda i, be, nv: (be[i], 0, 0)),
                  pl.BlockSpec((1, f, d), lambda i, be, nv: (be[i], 0, 0))],
        out_specs=pl.BlockSpec((blk, dh), lambda i, be, nv: (i, 0)),
        scratch_shapes=[pltpu.VMEM((d, f), BF16), pltpu.VMEM((d, f), BF16), pltpu.VMEM((f, d), BF16)],
    )
    return pl.pallas_call(
        functools.partial(_experts_kernel, blk=blk),
        grid_spec=grid_spec,
        out_shape=jax.ShapeDtypeStruct((n_rows, dh), U32),
        compiler_params=_cparams("arbitrary"),
        name="experts",
    )(be, nvalid, xs, wg, wu, wd)


def _final_kernel(dst_ref, dstn_ref, wts_ref, h2_ref, x1_ref, g2_ref, wg_ref, wu_ref, wd_ref, lng_ref, lnb_ref,
                  ys_hbm, o_ref, ybuf, sem, *, alpha, topk, tm, nt):
    i = pl.program_id(0)
    slot = lax.rem(i, 2)

    def gather_start(ids_ref, sl):
        def body(t, c):
            for k in range(topk):
                pltpu.make_async_copy(ys_hbm.at[pl.ds(ids_ref[0, k, t], 1)], ybuf.at[sl, k, pl.ds(t, 1)],
                                      sem.at[sl]).start()
            return c
        lax.fori_loop(0, tm, body, 0)

    @pl.when(i == 0)
    def _():
        gather_start(dst_ref, 0)

    for t in range(tm):
        for k in range(topk):
            pltpu.make_async_copy(ys_hbm.at[pl.ds(dstn_ref[0, k, t], 1)], ybuf.at[1 - slot, k, pl.ds(t, 1)],
                                  sem.at[1 - slot]).start()

    h = jnp.concatenate(_unpack_bf16_pairs(h2_ref[...]), axis=1).astype(BF16)
    y = _dot((_silu(_dot(h, wg_ref[...])) * _dot(h, wu_ref[...])).astype(BF16), wd_ref[...])

    def wait_slot(sl):
        def body(t, c):
            for k in range(topk):
                pltpu.make_async_copy(ys_hbm.at[pl.ds(0, 1)], ybuf.at[sl, k, pl.ds(t, 1)], sem.at[sl]).wait()
            return c
        lax.fori_loop(0, tm, body, 0)

    wait_slot(slot)
    w = wts_ref[...]
    ylo = jnp.zeros((tm, y.shape[1] // 2), F32)
    yhi = jnp.zeros((tm, y.shape[1] // 2), F32)
    for k in range(topk):
        lo, hi = _unpack_bf16_pairs(ybuf[slot, k])
        ylo = ylo + w[:, k:k + 1] * lo
        yhi = yhi + w[:, k:k + 1] * hi
    y = y + jnp.concatenate([ylo, yhi], axis=1)
    o_ref[...] = _ln(alpha * x1_ref[...] + g2_ref[0] * y) * lng_ref[...] + lnb_ref[...]

    @pl.when(i == nt - 1)
    def _():
        wait_slot(1 - slot)


def _final(ys, dest, wtsT, h2, x1, gate2, wg, wu, wd, lng, lnb, alpha, tiles_per_batch):
    n, d = x1.shape
    nt, topk, tm = dest.shape
    kern = functools.partial(_final_kernel, alpha=alpha, topk=topk, tm=tm, nt=nt)
    row = pl.BlockSpec((tm, d), lambda i: (i, 0))
    full = lambda a: pl.BlockSpec(a.shape, lambda i: (0,) * a.ndim, pipeline_mode=pl.Buffered(1))
    return pl.pallas_call(
        kern,
        grid=(nt,),
        in_specs=[pl.BlockSpec((1, topk, tm), lambda i: (i, 0, 0), memory_space=pltpu.SMEM),
                  pl.BlockSpec((1, topk, tm), lambda i: (jnp.minimum(i + 1, nt - 1), 0, 0), memory_space=pltpu.SMEM),
                  pl.BlockSpec((tm, topk), lambda i: (i, 0)),
                  pl.BlockSpec((tm, d // 2), lambda i: (i, 0)), row,
                  pl.BlockSpec((1, 1, d), lambda i: (i // tiles_per_batch, 0, 0)),
                  full(wg), full(wu), full(wd),
                  pl.BlockSpec((1, d), lambda i: (0, 0)), pl.BlockSpec((1, d), lambda i: (0, 0)),
                  pl.BlockSpec(memory_space=pl.ANY)],
        out_specs=row,
        out_shape=jax.ShapeDtypeStruct((n, d), F32),
        scratch_shapes=[pltpu.VMEM((2, topk, tm, d // 2), U32), pltpu.SemaphoreType.DMA((2,))],
        compiler_params=_cparams("arbitrary"),
        name="final",
    )(dest, dest, wtsT, h2, x1, gate2, wg, wu, wd, lng.reshape(1, d), lnb.reshape(1, d), ys)


def _proj_layout(d, qr, kvr, idim, ih, hk, hv):
    src = np.cumsum([0, qr, kvr, idim, ih, hk, hk, hv, hv, d, d])
    wa = -(-(qr + kvr + idim + ih) // LANES) * LANES
    wa = max(wa, 1 << (wa - 1).bit_length())
    pieces = [("a", wa, (int(src[0]), int(src[4]))),
              ("hq", hk, (int(src[4]), int(src[5]))), ("hf", hk, (int(src[5]), int(src[6]))),
              ("hi", hv, (int(src[6]), int(src[7]))), ("hg", hv, (int(src[7]), int(src[8]))),
              ("ga", d, (int(src[8]), int(src[9]))), ("gb", d, (int(src[9]), int(src[10])))]
    pieces.sort(key=lambda p: -p[1])
    off = 0
    layout = {}
    for name, wdt, rng in pieces:
        assert off % wdt == 0
        layout[name] = (off, wdt, rng)
        off += wdt
    return layout, off


def kernel(x, c, rpb_table, hgrn_lb_logits, ada_w, ada_b, w_in, q_norm_g, kv_norm_g, w_uq, w_uk, w_uv, w_qidx,
           idx_k_norm_g, idx_k_norm_b, hgrn_out_norm_g, w_branch_a, w_branch_b, w_o, ln1_g, ln1_b, w_router,
           router_bias, w_exp_gate, w_exp_up, w_exp_down, w_sh_gate, w_sh_up, w_sh_down, ln2_g, ln2_b):
    bsz, s, d = x.shape
    depth = ada_w.shape[0]
    qr = w_uq.shape[1]
    ah, kvr, qk = w_uk.shape[1], w_uk.shape[2], w_uk.shape[3]
    idim = idx_k_norm_g.shape[1]
    ih = w_qidx.shape[2] // idim
    dv = hgrn_out_norm_g.shape[1]
    nh = w_branch_b.shape[1] // dv
    dk = hgrn_lb_logits.shape[1] // nh
    ne = w_router.shape[2]
    topk = min(IDX_TOPK, s // 4)
    alpha = float((2 * depth) ** 0.25)
    n_tok = bsz * s

    lower_bounds = jnp.cumsum(jax.nn.softmax(hgrn_lb_logits.astype(F32), axis=0), axis=0)
    layout, wtot = _proj_layout(d, qr, kvr, idim, ih, nh * dk, nh * dv)

    for l in range(depth):
        mod = _ada(c, ada_w[l], ada_b[l])[:, None, :]
        shift1, scale1, gate1, shift2, scale2, gate2 = jnp.split(mod, 6, axis=-1)

        cols = []
        for off, wdt, (lo, hi) in sorted(layout.values()):
            cols.append(w_in[l][:, lo:hi].astype(BF16))
            if wdt > hi - lo:
                cols.append(jnp.zeros((d, wdt - (hi - lo)), BF16))
        proj = _inproj(x, scale1, shift1, jnp.concatenate(cols, axis=1))
        blk = lambda name: layout[name][0] // layout[name][1]

        qb_sz = _pick(s, (256, 128))
        q_latT, q_idxT, w_idxT, k_idx, kv_lat = _prep(
            proj, blk("a"), layout["a"][1], (qr, kvr, idim, ih, ah, qk), q_norm_g[l], kv_norm_g[l],
            idx_k_norm_g[l], idx_k_norm_b[l], jnp.swapaxes(w_uq[l], 0, 1).astype(BF16),
            w_uk[l].astype(BF16), jnp.swapaxes(w_qidx[l], 0, 1).astype(BF16), qb_sz)
        kv_latT = jnp.concatenate([jnp.swapaxes(kv_lat, 1, 2), jnp.ones((bsz, 1, s), BF16),
                                   jnp.zeros((bsz, 7, s), BF16)], axis=1)
        o_a = _dsa(q_idxT, w_idxT, q_latT, k_idx, kv_lat, kv_latT,
                   rpb_table, jnp.swapaxes(w_uv[l], 1, 2).astype(BF16), topk, qb_sz)

        o_b = _hgrn(proj, (blk("hq"), blk("hf"), blk("hi"), blk("hg")), lower_bounds[l],
                    hgrn_out_norm_g[l], nh, dk, dv)

        x1, h2, lgT = _postmix(o_a, o_b, proj, blk("ga"), blk("gb"), x, gate1, scale2, shift2,
                               w_branch_a[l].astype(BF16), w_branch_b[l].astype(BF16), w_o[l].astype(BF16),
                               jnp.swapaxes(w_router[l], 0, 1).astype(BF16), ln1_g[l], ln1_b[l], alpha)

        ids, wts, rnk, sizes_f = _route(lgT, router_bias[l])
        eb = EXPERT_BLOCK
        sizes = sizes_f[:, 0].astype(I32)
        padded = (sizes + eb - 1) // eb * eb
        pend = jnp.cumsum(padded)
        pstart = pend - padded
        n_rows = -(-(n_tok * MOE_TOPK + ne * (eb - 1)) // eb) * eb
        blk_start = jnp.arange(n_rows // eb, dtype=I32) * eb
        blk_expert = jnp.minimum(jnp.sum((pend[None, :] <= blk_start[:, None]).astype(I32), axis=1), ne - 1)
        onehot = (blk_expert[:, None] == jnp.arange(ne, dtype=I32)[None, :]).astype(I32)
        blk_size = jnp.sum(onehot * sizes[None, :], axis=1)
        blk_pstart = jnp.sum(onehot * pstart[None, :], axis=1)
        nvalid = jnp.clip(blk_size - (blk_start - blk_pstart), 0, eb).astype(I32)
        dest = _dest(pstart.astype(I32), ids, rnk, ne)
        tm = _pick(s, (128,))
        dest_t = dest.reshape(bsz, MOE_TOPK, s // tm, tm).transpose(0, 2, 1, 3).reshape(n_tok // tm, MOE_TOPK, tm)
        wtsT = jnp.swapaxes(wts, 1, 2).reshape(n_tok, MOE_TOPK)

        h2f = h2.reshape(n_tok, d // 2)
        xs = _dispatch(dest_t, h2f, n_rows)
        ys = _experts(xs, blk_expert, nvalid, w_exp_gate[l], w_exp_up[l], w_exp_down[l])
        x = _final(ys, dest_t, wtsT, h2f, x1.reshape(n_tok, d), gate2, w_sh_gate[l].astype(BF16),
                   w_sh_up[l].astype(BF16), w_sh_down[l].astype(BF16), ln2_g[l], ln2_b[l], alpha,
                   s // tm).reshape(bsz, s, d)
    return x
```

```python
import functools
import math

import numpy as np
import jax
import jax.numpy as jnp
from jax import lax
from jax.experimental import pallas as pl
from jax.experimental.pallas import tpu as pltpu

F32 = jnp.float32
BF16 = jnp.bfloat16
I32 = jnp.int32
U32 = jnp.uint32

EPS = 1e-6
IDX_TOPK = 256
RPB_MAX_DIST = 128
MOE_TOPK = 8
N_GROUPS = 8
TOPK_GROUPS = 4
ROUTED_SCALE = 2.5
EXPERT_BLOCK = 512

V7X_VMEM_LIMIT_BYTES = 56 * 1024 * 1024
LANES = 128
INT_MIN = -2 ** 31
NEG_BIG = -1e30
DSA_FAR_GROUP = 4
LOG2E = math.log2(math.e)


def _cparams(*sem):
    return pltpu.CompilerParams(dimension_semantics=tuple(sem), vmem_limit_bytes=V7X_VMEM_LIMIT_BYTES)


def _pick(n, prefs):
    for p in prefs:
        if n % p == 0:
            return p
    return n


def _sigmoid(v):
    return 1.0 / (1.0 + jnp.exp(-v))


def _silu(v):
    return v * _sigmoid(v)


def _ln(v):
    mu = jnp.mean(v, axis=-1, keepdims=True)
    d = v - mu
    var = jnp.mean(d * d, axis=-1, keepdims=True)
    return d * lax.rsqrt(var + EPS)


def _rms(v):
    return v * lax.rsqrt(jnp.mean(v * v, axis=-1, keepdims=True) + EPS)


def _dot(a, b):
    return jnp.dot(a, b, preferred_element_type=F32)


def _pack_bf16_pairs(v):
    n = v.shape[1] // 2
    lo = pltpu.bitcast(v[:, :n].astype(BF16).astype(F32), U32) >> 16
    hi = pltpu.bitcast(v[:, n:].astype(BF16).astype(F32), U32) & jnp.uint32(0xFFFF0000)
    return lo | hi


def _unpack_bf16_pairs(u):
    return pltpu.bitcast(u << 16, F32), pltpu.bitcast(u & jnp.uint32(0xFFFF0000), F32)


def _dot_nt(a, b):
    return lax.dot_general(a, b, (((1,), (1,)), ((), ())), preferred_element_type=F32)


def _dot_tn(a, b):
    return lax.dot_general(a, b, (((0,), (0,)), ((), ())), preferred_element_type=F32)


def _ada_kernel(c_ref, w_ref, b_ref, o_ref):
    c = c_ref[...]
    o_ref[...] = _dot(_silu(c).astype(BF16), w_ref[...].astype(BF16)) + b_ref[...]


def _ada(c, w, b):
    bsz, d = c.shape
    n = w.shape[1]
    rows = 8
    cp = jnp.zeros((rows, d), F32).at[:bsz].set(c)
    tn = _pick(n, (1024, 512, 256, 128))
    out = pl.pallas_call(
        _ada_kernel,
        grid=(n // tn,),
        in_specs=[pl.BlockSpec((rows, d), lambda j: (0, 0)),
                  pl.BlockSpec((d, tn), lambda j: (0, j)),
                  pl.BlockSpec((1, tn), lambda j: (0, j))],
        out_specs=pl.BlockSpec((rows, tn), lambda j: (0, j)),
        out_shape=jax.ShapeDtypeStruct((rows, n), F32),
        compiler_params=_cparams("arbitrary"),
        name="ada",
    )(cp, w, b.reshape(1, n))
    return out[:bsz]


def _inproj_kernel(x_ref, sc_ref, sh_ref, w_ref, o_ref, h_ref):
    @pl.when(pl.program_id(2) == 0)
    def _():
        h = _ln(x_ref[0]) * (1.0 + sc_ref[0]) + sh_ref[0]
        h_ref[...] = h.astype(BF16)

    o_ref[0] = _dot(h_ref[...], w_ref[...]).astype(o_ref.dtype)


def _inproj(x, scale, shift, w):
    bsz, s, d = x.shape
    n = w.shape[1]
    tm = _pick(s, (1024, 512, 256, 128))
    tn = _pick(n, (1024, 512, 256, 128))
    return pl.pallas_call(
        _inproj_kernel,
        grid=(bsz, s // tm, n // tn),
        in_specs=[pl.BlockSpec((1, tm, d), lambda b, i, j: (b, i, 0)),
                  pl.BlockSpec((1, 1, d), lambda b, i, j: (b, 0, 0)),
                  pl.BlockSpec((1, 1, d), lambda b, i, j: (b, 0, 0)),
                  pl.BlockSpec((d, tn), lambda b, i, j: (0, j))],
        out_specs=pl.BlockSpec((1, tm, tn), lambda b, i, j: (b, i, j)),
        out_shape=jax.ShapeDtypeStruct((bsz, s, n), BF16),
        scratch_shapes=[pltpu.VMEM((tm, d), BF16)],
        compiler_params=_cparams("arbitrary", "arbitrary", "arbitrary"),
        name="inproj",
    )(x, scale, shift, w)


def _prep_kernel(a_ref, qg_ref, kvg_ref, ikg_ref, ikb_ref, wuqT_ref, wuk_ref, wqiT_ref,
                 qlatT_ref, qidxT_ref, widxT_ref, kidx_ref, kv_ref, *, qr, kvr, idim, ih, ah, qk):
    a = a_ref[0]
    af = a.astype(F32)
    cq = (_rms(af[:, :qr]) * qg_ref[...]).astype(BF16)
    ckv = af[:, qr:qr + kvr]
    ki = af[:, qr + kvr:qr + kvr + idim]
    kv_ref[0] = (_rms(ckv) * kvg_ref[...]).astype(BF16)
    kidx_ref[0] = (_ln(ki) * ikg_ref[...] + ikb_ref[...]).astype(BF16)
    tail = a[:, qr + kvr:qr + kvr + LANES]
    eye = (lax.broadcasted_iota(I32, (LANES, LANES), 0) == lax.broadcasted_iota(I32, (LANES, LANES), 1))
    tailT = _dot_nt(jnp.where(eye, 1.0, 0.0).astype(BF16), tail)
    widxT_ref[0] = tailT[idim:idim + ih] * float((ih * idim) ** -0.5)
    tm = a.shape[0]
    qT = _dot_nt(wuqT_ref[...], cq)
    for h in range(ah):
        qh = qT[h * qk:(h + 1) * qk].astype(BF16)
        qlatT_ref[0, 0, :, h * tm:(h + 1) * tm] = (_dot(wuk_ref[h], qh) * float(qk ** -0.5 * LOG2E)).astype(BF16)
    qiT = _dot_nt(wqiT_ref[...], cq)
    for h in range(ih):
        qidxT_ref[0, 0, :, h * tm:(h + 1) * tm] = qiT[h * idim:(h + 1) * idim].astype(BF16)


def _prep(proj, a_blk, wa, dims, q_norm_g, kv_norm_g, ikg, ikb, w_uqT, w_uk, w_qidxT, tm):
    bsz, s, _ = proj.shape
    qr, kvr, idim, ih, ah, qk = dims
    assert idim + ih <= LANES and qr + kvr + LANES <= wa
    kern = functools.partial(_prep_kernel, qr=qr, kvr=kvr, idim=idim, ih=ih, ah=ah, qk=qk)
    full = lambda shape: pl.BlockSpec(shape, lambda b, i: (0,) * len(shape))
    return pl.pallas_call(
        kern,
        grid=(bsz, s // tm),
        in_specs=[pl.BlockSpec((1, tm, wa), lambda b, i: (b, i, a_blk)),
                  full((1, qr)), full((1, kvr)), full((1, idim)), full((1, idim)),
                  full(w_uqT.shape), full(w_uk.shape), full(w_qidxT.shape)],
        out_specs=[pl.BlockSpec((1, 1, kvr, ah * tm), lambda b, i: (b, i, 0, 0)),
                   pl.BlockSpec((1, 1, idim, ih * tm), lambda b, i: (b, i, 0, 0)),
                   pl.BlockSpec((1, ih, tm), lambda b, i: (b, 0, i)),
                   pl.BlockSpec((1, tm, idim), lambda b, i: (b, i, 0)),
                   pl.BlockSpec((1, tm, kvr), lambda b, i: (b, i, 0))],
        out_shape=[jax.ShapeDtypeStruct((bsz, s // tm, kvr, ah * tm), BF16),
                   jax.ShapeDtypeStruct((bsz, s // tm, idim, ih * tm), BF16),
                   jax.ShapeDtypeStruct((bsz, ih, s), F32),
                   jax.ShapeDtypeStruct((bsz, s, idim), BF16),
                   jax.ShapeDtypeStruct((bsz, s, kvr), BF16)],
        compiler_params=_cparams("arbitrary", "arbitrary"),
        name="prep",
    )(proj, q_norm_g.reshape(1, qr), kv_norm_g.reshape(1, kvr), ikg.reshape(1, idim), ikb.reshape(1, idim),
      w_uqT, w_uk, w_qidxT)


def _dsa_kernel(qidxT_ref, widxT_ref, qlatT_ref, kidx_ref, kv_ref, kvT_ref, bkt0_ref, bkt1_ref, rel_ref, wuvT_ref,
                o_ref, keys_ref, gmax_ref, s_ref, p_ref, m_ref, a_ref, acc_ref, t0_ref, t1_ref, lstrict_ref, taken_ref, sx_ref, px_ref, ax_ref,
                *, qb_sz, topk, ih, ah, c, nb):
    QB = qb_sz
    qb = pl.program_id(1)
    nchunks = qb + 1
    qpos = qb * QB + lax.broadcasted_iota(I32, (1, QB), 1)
    kpos = lax.broadcasted_iota(I32, (QB, 1), 0)
    wT = widxT_ref[0]
    hcols = lambda h: slice(h * QB, (h + 1) * QB)

    @pl.when(jnp.logical_and(pl.program_id(0) == 0, qb == 0))
    def _():
        for bkt_ref, t_ref in ((bkt0_ref, t0_ref), (bkt1_ref, t1_ref)):
            bkt = bkt_ref[...]
            for h in range(ah):
                tile = jnp.zeros((QB, QB), F32)
                for b in range(nb - 1):
                    tile = jnp.where(bkt == b, rel_ref[b, h], tile)
                t_ref[:, hcols(h)] = tile
        lstrict_ref[...] = jnp.where(lax.broadcasted_iota(I32, (QB, QB), 1) < lax.broadcasted_iota(I32, (QB, QB), 0),
                                     1.0, 0.0).astype(BF16)

    def score_chunk(kc, carry):
        off = pl.multiple_of(kc * QB, QB)
        s_ref[:, :ih * QB] = _dot(kidx_ref[0, pl.ds(off, QB), :], qidxT_ref[0, 0])
        acc = jnp.zeros((QB, QB), F32)
        for h in range(ih):
            acc = acc + wT[h:h + 1] * jnp.maximum(s_ref[:, hcols(h)], 0.0)
        bits = pltpu.bitcast(acc, I32)
        skey = bits ^ ((bits >> 31) & 0x7FFFFFFF)
        causal = (off + kpos) <= qpos
        skey = jnp.where(causal, skey, INT_MIN)
        keys_ref[pl.ds(off, QB), :] = skey
        gmax_ref[...] = jnp.maximum(gmax_ref[...], skey)
        return carry

    gmax_ref[...] = jnp.full(gmax_ref.shape, INT_MIN, I32)
    lax.fori_loop(0, nchunks, score_chunk, 0)

    gmax = gmax_ref[...]
    lo0 = jnp.min(gmax, axis=0, keepdims=True) if QB >= topk else jnp.full((1, QB), INT_MIN, I32)
    hi0 = jnp.max(gmax, axis=0, keepdims=True) + 1

    def count_ge(cand):
        def body(kc, cnt):
            off = pl.multiple_of(kc * QB, QB)
            hit = jnp.where(keys_ref[pl.ds(off, QB), :] >= cand, 1, 0)
            return cnt + jnp.sum(hit.reshape(QB // 8, 8, QB), axis=0)

        cnt = lax.fori_loop(0, nchunks, body, jnp.zeros((8, QB), I32))
        return jnp.sum(cnt.astype(F32), axis=0, keepdims=True)

    def bis_cond(st):
        it, _, _, done, _, _ = st
        return jnp.logical_and(it < 34, jnp.min(done) < 0.5)

    def bis_body(st):
        it, lo, hi, done, c_hi, need = st
        cand = (lo >> 1) + (hi >> 1) + (lo & hi & 1)
        cnt = count_ge(cand)
        ge = cnt >= float(topk)
        conv = cand == lo
        fin = jnp.logical_or(conv, cnt == float(topk))
        need = jnp.where(jnp.logical_and(fin, done < 0.5), jnp.where(conv, float(topk) - c_hi, float(topk)), need)
        c_hi = jnp.where(jnp.logical_or(ge, fin), c_hi, cnt)
        thr_new = jnp.where(conv, lo, cand)
        lo = jnp.where(fin, thr_new, jnp.where(ge, cand, lo))
        hi = jnp.where(fin, thr_new + 1, jnp.where(ge, hi, cand))
        return it + 1, lo, hi, jnp.where(fin, 1.0, done), c_hi, need

    zeros_q = jnp.zeros((1, QB), F32)
    _, thr, _, _, _, need = lax.while_loop(bis_cond, bis_body, (jnp.int32(0), lo0, hi0, zeros_q, zeros_q, zeros_q))

    m_ref[...] = jnp.full(m_ref.shape, NEG_BIG, F32)
    acc_ref[...] = jnp.zeros(acc_ref.shape, F32)

    def attn_chunk(off, bias_ref, diag, taken, s_ref=s_ref, p_ref=p_ref, a_ref=a_ref):
        kt = keys_ref[pl.ds(off, QB), :]
        eq = kt == thr
        eq01 = jnp.where(eq, 1.0, 0.0)
        before = _dot(lstrict_ref[...], eq01.astype(BF16)) + taken
        sel = jnp.logical_or(kt > thr, jnp.logical_and(eq, before < need))
        taken = taken + jnp.sum(eq01, axis=0, keepdims=True)
        if diag:
            sel = jnp.logical_and(sel, (off + kpos) <= qpos)
        madd = jnp.where(sel, 0.0, NEG_BIG)
        s_ref[:, :ah * QB] = _dot(kv_ref[0, pl.ds(off, QB), :], qlatT_ref[0, 0])
        for g in range(ah * QB // LANES):
            cols = slice(g * LANES, (g + 1) * LANES)
            qcols = slice(g * LANES % QB, g * LANES % QB + LANES)
            s = s_ref[:, cols] + madd[:, qcols]
            if bias_ref is not None:
                s = s + bias_ref[:, cols]
            m_prev = m_ref[:, cols]
            m_new = jnp.maximum(m_prev, jnp.max(s, axis=0, keepdims=True))
            p_ref[:, cols] = jnp.exp2(s - m_new).astype(BF16)
            a_ref[:, cols] = jnp.exp2(m_prev - m_new)
            m_ref[:, cols] = m_new
        acc_ref[...] = a_ref[...] * acc_ref[...] + _dot(kvT_ref[0, :, pl.ds(off, QB)], p_ref[...])
        return taken

    G = DSA_FAR_GROUP

    def far_group(kg, taken):
        off = pl.multiple_of(kg * (G * QB), G * QB)
        taken = attn_chunk(off, None, False, taken)
        for j in range(1, G):
            taken = attn_chunk(off + j * QB, None, False, taken, sx_ref.at[j - 1], px_ref.at[j - 1], ax_ref.at[j - 1])
        return taken

    nfar = jnp.maximum(qb - 1, 0)
    ngroups = nfar // G
    taken = lax.fori_loop(0, ngroups, far_group, jnp.zeros((1, QB), F32))

    def far_single(r, taken):
        return attn_chunk(pl.multiple_of((ngroups * G + r) * QB, QB), None, False, taken)

    taken_ref[...] = lax.fori_loop(0, nfar - ngroups * G, far_single, taken)

    @pl.when(qb >= 1)
    def _():
        taken_ref[...] = attn_chunk(pl.multiple_of((qb - 1) * QB, QB), t1_ref, False, taken_ref[...])

    attn_chunk(pl.multiple_of(qb * QB, QB), t0_ref, True, taken_ref[...])

    outs = []
    for h in range(ah):
        o = (acc_ref[:c, hcols(h)] / acc_ref[c:c + 1, hcols(h)]).astype(BF16)
        outs.append(_dot(wuvT_ref[h], o))
    o_ref[0] = jnp.concatenate(outs, axis=0).T.astype(o_ref.dtype)


def _t5_bucket_np(d, nbuckets):
    max_exact = nbuckets // 2
    dd = np.maximum(d, 1).astype(np.float32)
    large = max_exact + (np.log(dd / np.float32(max_exact)) / np.float32(math.log(RPB_MAX_DIST / max_exact))
                         * np.float32(nbuckets - max_exact)).astype(np.int32)
    large = np.minimum(large, nbuckets - 1)
    return np.where(d < max_exact, d, large).astype(np.int32)


def _dsa(q_idxT, w_idxT, q_latT, k_idx, kv_lat, kv_latT, rpb_table, w_uvT, topk, QB):
    bsz, nqb, idim, ihq = q_idxT.shape
    c, ahq = q_latT.shape[2], q_latT.shape[3]
    ca = kv_latT.shape[1]
    ih, ah = ihq // QB, ahq // QB
    s = nqb * QB
    vd = w_uvT.shape[1]
    nb = rpb_table.shape[0]
    assert QB >= RPB_MAX_DIST
    j = np.arange(QB)[:, None]
    i = np.arange(QB)[None, :]
    bkt0 = jnp.asarray(_t5_bucket_np(np.maximum(i - j, 0), nb))
    bkt1 = jnp.asarray(_t5_bucket_np(QB + i - j, nb))
    assert int(_t5_bucket_np(np.array([RPB_MAX_DIST]), nb)[0]) == nb - 1
    rel = (rpb_table.astype(F32) - rpb_table[nb - 1].astype(F32)[None, :]) * LOG2E
    kern = functools.partial(_dsa_kernel, qb_sz=QB, topk=topk, ih=ih, ah=ah, c=c, nb=nb)
    const = lambda shape: pl.BlockSpec(shape, lambda b, i: (0,) * len(shape), pipeline_mode=pl.Buffered(1))
    hw = max(ih, ah) * QB
    return pl.pallas_call(
        kern,
        grid=(bsz, nqb),
        in_specs=[pl.BlockSpec((1, 1, idim, ih * QB), lambda b, i: (b, i, 0, 0)),
                  pl.BlockSpec((1, ih, QB), lambda b, i: (b, 0, i)),
                  pl.BlockSpec((1, 1, c, ah * QB), lambda b, i: (b, i, 0, 0)),
                  pl.BlockSpec((1, s, idim), lambda b, i: (b, 0, 0), pipeline_mode=pl.Buffered(1)),
                  pl.BlockSpec((1, s, c), lambda b, i: (b, 0, 0), pipeline_mode=pl.Buffered(1)),
                  pl.BlockSpec((1, ca, s), lambda b, i: (b, 0, 0), pipeline_mode=pl.Buffered(1)),
                  const((QB, QB)), const((QB, QB)),
                  pl.BlockSpec(memory_space=pltpu.SMEM), const(w_uvT.shape)],
        out_specs=pl.BlockSpec((1, QB, ah * vd), lambda b, i: (b, i, 0)),
        out_shape=jax.ShapeDtypeStruct((bsz, s, ah * vd), BF16),
        scratch_shapes=[pltpu.VMEM((s, QB), I32),
                        pltpu.VMEM((QB, QB), I32),
                        pltpu.VMEM((QB, hw), F32),
                        pltpu.VMEM((QB, ah * QB), BF16),
                        pltpu.VMEM((1, ah * QB), F32),
                        pltpu.VMEM((1, ah * QB), F32),
                        pltpu.VMEM((ca, ah * QB), F32),
                        pltpu.VMEM((QB, ah * QB), F32),
                        pltpu.VMEM((QB, ah * QB), F32),
                        pltpu.VMEM((QB, QB), BF16),
                        pltpu.VMEM((1, QB), F32),
                        pltpu.VMEM((DSA_FAR_GROUP - 1, QB, ah * QB), F32),
                        pltpu.VMEM((DSA_FAR_GROUP - 1, QB, ah * QB), BF16),
                        pltpu.VMEM((DSA_FAR_GROUP - 1, 1, ah * QB), F32)],
        compiler_params=_cparams("arbitrary", "arbitrary"),
        name="dsa",
    )(q_idxT, w_idxT, q_latT, k_idx, kv_lat, kv_latT, bkt0, bkt1, rel, w_uvT)


HGRN_CHUNK = 64
HGRN_SUB = 16
HGRN_EXP_CLAMP = 80.0


def _dot_exact(a, b, dims):
    return lax.dot_general(a, b, (dims, ((), ())), precision=lax.Precision.HIGHEST, preferred_element_type=F32)


def _hgrn_kernel(hq_ref, hf_ref, hi_ref, hg_ref, lb_ref, g_ref, o_ref, st_ref, st0_ref, *, nh, dk, dv, tt):
    C, SUB = HGRN_CHUNK, HGRN_SUB

    @pl.when(pl.program_id(1) == 0)
    def _():
        st_ref[...] = jnp.zeros(st_ref.shape, F32)

    r = lax.broadcasted_iota(I32, (C, C), 0)
    cc = lax.broadcasted_iota(I32, (C, C), 1)
    tri_mask = r >= cc
    tri = jnp.where(tri_mask, 1.0, 0.0).astype(BF16)
    g = g_ref[...]
    st0_ref[...] = st_ref[...]
    decay = jnp.zeros((1, nh * dk), F32)
    lb = lb_ref[...]

    for c in range(tt // C):
        rows = pl.ds(c * C, C)
        f = lb + (1.0 - lb) * _sigmoid(hf_ref[0, rows, :].astype(F32))
        lf = jnp.log(f)
        t1 = lf.astype(BF16)
        r1 = lf - t1.astype(F32)
        t2 = r1.astype(BF16)
        t3 = (r1 - t2.astype(F32)).astype(BF16)
        b = _dot(tri, t1) + _dot(tri, t2) + _dot(tri, t3)
        kk = 1.0 - f
        hq = hq_ref[0, rows, :].astype(F32)
        q = _silu(hq) * float(dk ** -0.5)
        qe = (q * jnp.exp(b)).astype(BF16)
        b_last = b[C - 1:C]
        k_dec = (kk * jnp.exp(b_last - b)).astype(BF16)
        dec_last = jnp.exp(b_last)
        qs, ks = [], []
        for i in range(C // SUB):
            lo, n = i * SUB, (i + 1) * SUB
            bi = b[lo - 1:lo] if i > 0 else jnp.zeros((1, nh * dk), F32)
            decay = jnp.maximum(decay, bi - b[n - 1:n])
            qs.append((q[lo:n] * jnp.exp(b[lo:n] - bi)).astype(BF16))
            ks.append((kk * jnp.exp(jnp.minimum(bi - b, HGRN_EXP_CLAMP))).astype(BF16))
        v_all = hi_ref[0, rows, :]
        outs = []
        for h in range(nh):
            kc = slice(h * dk, (h + 1) * dk)
            v = v_all[:, h * dv:(h + 1) * dv]
            stT = st_ref[h]
            att = jnp.concatenate([_dot_nt(qs[i][:, kc], ks[i][:, kc]) for i in range(C // SUB)], axis=0)
            att = jnp.where(tri_mask, att, 0.0)
            o = _dot_nt(qe[:, kc], stT.astype(BF16)) + _dot(att.astype(BF16), v)
            st_ref[h] = stT * dec_last[:, kc] + _dot_tn(v, k_dec[:, kc])
            outs.append(_rms(o) * g)
        o_all = jnp.concatenate(outs, axis=1) * _silu(hg_ref[0, rows, :].astype(F32))
        o_ref[0, rows, :] = o_all.astype(o_ref.dtype)

    @pl.when(jnp.max(decay) > HGRN_EXP_CLAMP)
    def _():
        row = lax.broadcasted_iota(I32, (SUB, 1), 0)

        def head(h, carry):
            ko = pl.multiple_of(h * dk, dk)
            vo = pl.multiple_of(h * dv, dv)
            lb = lb_ref[:, pl.ds(ko, dk)]

            def slab(j, S):
                rows = pl.ds(pl.multiple_of(j * SUB, SUB), SUB)
                f = lb + (1.0 - lb) * _sigmoid(hf_ref[0, rows, pl.ds(ko, dk)].astype(F32))
                kk = 1.0 - f
                hq = hq_ref[0, rows, pl.ds(ko, dk)].astype(F32)
                q = _silu(hq) * float(dk ** -0.5)
                v = hi_ref[0, rows, pl.ds(vo, dv)].astype(F32)
                o = jnp.zeros((SUB, dv), F32)
                for r in range(SUB):
                    S = S * f[r:r + 1] + _dot_exact(jnp.where(row == r, v, 0.0), kk, ((0,), (0,)))
                    o = jnp.where(row == r, _dot_exact(q, S, ((1,), (1,))), o)
                hg = hg_ref[0, rows, pl.ds(vo, dv)].astype(F32)
                o_ref[0, rows, pl.ds(vo, dv)] = (_rms(o) * g * _silu(hg)).astype(o_ref.dtype)
                return S

            st_ref[h] = lax.fori_loop(0, tt // SUB, slab, st0_ref[h])
            return carry

        lax.fori_loop(0, nh, head, 0)


def _hgrn(proj, blks, lb, g, nh, dk, dv):
    bsz, s, _ = proj.shape
    tt = _pick(s, (256, 128, 64))
    kern = functools.partial(_hgrn_kernel, nh=nh, dk=dk, dv=dv, tt=tt)
    col = lambda blk, wdt: pl.BlockSpec((1, tt, wdt), lambda b, i: (b, i, blk))
    return pl.pallas_call(
        kern,
        grid=(bsz, s // tt),
        in_specs=[col(blks[0], nh * dk), col(blks[1], nh * dk), col(blks[2], nh * dv), col(blks[3], nh * dv),
                  pl.BlockSpec((1, nh * dk), lambda b, i: (0, 0)),
                  pl.BlockSpec((1, dv), lambda b, i: (0, 0))],
        out_specs=pl.BlockSpec((1, tt, nh * dv), lambda b, i: (b, i, 0)),
        out_shape=jax.ShapeDtypeStruct((bsz, s, nh * dv), BF16),
        scratch_shapes=[pltpu.VMEM((nh, dv, dk), F32), pltpu.VMEM((nh, dv, dk), F32)],
        compiler_params=_cparams("arbitrary", "arbitrary"),
        name="hgrn",
    )(proj, proj, proj, proj, lb.reshape(1, nh * dk), g.reshape(1, dv))


def _postmix_kernel(oa_ref, ob_ref, ga_ref, gb_ref, x_ref, g1_ref, sc2_ref, sh2_ref,
                    wa_ref, wb_ref, wo_ref, wrT_ref, lng_ref, lnb_ref,
                    x1_ref, h2_ref, lgT_ref, *, alpha):
    ya = _dot(oa_ref[0], wa_ref[...])
    yb = _dot(ob_ref[0], wb_ref[...])
    mix = _sigmoid(ga_ref[0].astype(F32)) * ya + _sigmoid(gb_ref[0].astype(F32)) * yb
    mixed = _dot(mix.astype(BF16), wo_ref[...])
    x1 = _ln(alpha * x_ref[0] + g1_ref[0] * mixed) * lng_ref[...] + lnb_ref[...]
    x1_ref[0] = x1
    h2 = _ln(x1) * (1.0 + sc2_ref[0]) + sh2_ref[0]
    h2_ref[0] = _pack_bf16_pairs(h2)
    lgT_ref[0] = _dot_nt(wrT_ref[...], h2.astype(BF16))


def _postmix(o_a, o_b, proj, ga_blk, gb_blk, x, gate1, scale2, shift2, wa, wb, wo, wrT, lng, lnb, alpha):
    bsz, s, d = x.shape
    ne = wrT.shape[0]
    tm = _pick(s, (256, 128))
    kern = functools.partial(_postmix_kernel, alpha=alpha)
    row = lambda wdt: pl.BlockSpec((1, tm, wdt), lambda b, i: (b, i, 0))
    vec = pl.BlockSpec((1, 1, d), lambda b, i: (b, 0, 0))
    full = lambda a: pl.BlockSpec(a.shape, lambda b, i: (0,) * a.ndim, pipeline_mode=pl.Buffered(1))
    return pl.pallas_call(
        kern,
        grid=(bsz, s // tm),
        in_specs=[row(o_a.shape[2]), row(o_b.shape[2]),
                  pl.BlockSpec((1, tm, d), lambda b, i: (b, i, ga_blk)),
                  pl.BlockSpec((1, tm, d), lambda b, i: (b, i, gb_blk)),
                  row(d), vec, vec, vec,
                  full(wa), full(wb), full(wo), full(wrT),
                  pl.BlockSpec((1, d), lambda b, i: (0, 0)), pl.BlockSpec((1, d), lambda b, i: (0, 0))],
        out_specs=[row(d), row(d // 2), pl.BlockSpec((1, ne, tm), lambda b, i: (b, 0, i))],
        out_shape=[jax.ShapeDtypeStruct((bsz, s, d), F32),
                   jax.ShapeDtypeStruct((bsz, s, d // 2), U32),
                   jax.ShapeDtypeStruct((bsz, ne, s), F32)],
        compiler_params=_cparams("arbitrary", "arbitrary"),
        name="postmix",
    )(o_a, o_b, proj, proj, x, gate1, scale2, shift2, wa, wb, wo, wrT, lng.reshape(1, d), lnb.reshape(1, d))


def _route_kernel(lg_ref, bias_ref, ids_ref, wts_ref, rnk_ref, sizes_ref, upper_ref, carry_ref, *, ne):
    first = jnp.logical_and(pl.program_id(0) == 0, pl.program_id(1) == 0)
    tn = lg_ref.shape[2]

    @pl.when(first)
    def _():
        carry_ref[...] = jnp.zeros(carry_ref.shape, F32)
        r_ = lax.broadcasted_iota(I32, (tn, tn), 0)
        c_ = lax.broadcasted_iota(I32, (tn, tn), 1)
        upper_ref[...] = jnp.where(r_ < c_, 1.0, 0.0).astype(BF16)

    per = ne // N_GROUPS
    s = _sigmoid(lg_ref[0])
    bz = s + bias_ref[...]
    ridx = lax.broadcasted_iota(I32, (per, tn), 0)
    neg_inf = jnp.float32(-jnp.inf)
    gs = []
    for g in range(N_GROUPS):
        blk = bz[g * per:(g + 1) * per]
        m1 = jnp.max(blk, axis=0, keepdims=True)
        first_hit = jnp.min(jnp.where(blk == m1, ridx, per), axis=0, keepdims=True)
        m2 = jnp.max(jnp.where(ridx == first_hit, neg_inf, blk), axis=0, keepdims=True)
        gs.append(m1 + m2)
    emask_rows = []
    for g in range(N_GROUPS):
        rank = jnp.zeros((1, tn), I32)
        for g2 in range(N_GROUPS):
            if g2 == g:
                continue
            beats = (gs[g2] > gs[g]) if g2 > g else (gs[g2] >= gs[g])
            rank = rank + jnp.where(beats, 1, 0)
        emask_rows.append(jnp.broadcast_to(rank < TOPK_GROUPS, (per, tn)))
    emask = jnp.concatenate(emask_rows, axis=0)
    masked = jnp.where(emask, bz, neg_inf)
    eidx = lax.broadcasted_iota(I32, (ne, tn), 0)
    rank = jnp.zeros((ne, tn), I32)
    for e2 in range(ne):
        row = masked[e2:e2 + 1]
        beats = jnp.logical_or(row > masked, jnp.logical_and(row == masked, e2 < eidx))
        rank = rank + jnp.where(beats, 1, 0)
    sel = rank < MOE_TOPK
    sel01 = jnp.where(sel, 1.0, 0.0)
    denom = jnp.sum(jnp.where(sel, s, 0.0), axis=0, keepdims=True)
    wn = s / denom * ROUTED_SCALE
    before = _dot(sel01.astype(BF16), upper_ref[...]) + carry_ref[:, 0:1]
    ids, wts, rnk = [], [], []
    for k in range(MOE_TOPK):
        hit = rank == k
        ids.append(jnp.sum(jnp.where(hit, eidx, 0), axis=0, keepdims=True))
        wts.append(jnp.sum(jnp.where(hit, wn, 0.0), axis=0, keepdims=True))
        rnk.append(jnp.sum(jnp.where(hit, before, 0.0), axis=0, keepdims=True))
    ids_ref[0] = jnp.concatenate(ids, axis=0)
    wts_ref[0] = jnp.concatenate(wts, axis=0)
    rnk_ref[0] = jnp.concatenate(rnk, axis=0).astype(I32)
    carry_ref[...] = carry_ref[...] + jnp.sum(sel01, axis=1, keepdims=True)
    sizes_ref[...] = carry_ref[...]


def _route(lgT, bias):
    bsz, ne, s = lgT.shape
    tn = _pick(s, (1024, 512, 256, 128))
    kern = functools.partial(_route_kernel, ne=ne)
    slot = pl.BlockSpec((1, MOE_TOPK, tn), lambda b, j: (b, 0, j))
    return pl.pallas_call(
        kern,
        grid=(bsz, s // tn),
        in_specs=[pl.BlockSpec((1, ne, tn), lambda b, j: (b, 0, j)),
                  pl.BlockSpec((ne, 1), lambda b, j: (0, 0))],
        out_specs=[slot, slot, slot, pl.BlockSpec((ne, LANES), lambda b, j: (0, 0))],
        out_shape=[jax.ShapeDtypeStruct((bsz, MOE_TOPK, s), I32),
                   jax.ShapeDtypeStruct((bsz, MOE_TOPK, s), F32),
                   jax.ShapeDtypeStruct((bsz, MOE_TOPK, s), I32),
                   jax.ShapeDtypeStruct((ne, LANES), F32)],
        scratch_shapes=[pltpu.VMEM((tn, tn), BF16), pltpu.VMEM((ne, LANES), F32)],
        compiler_params=_cparams("arbitrary", "arbitrary"),
        name="route",
    )(lgT, bias.reshape(ne, 1))


def _dest_kernel(pstart_ref, ids_ref, rnk_ref, o_ref, *, ne):
    ids = ids_ref[0]
    base = jnp.zeros(ids.shape, I32)
    for e in range(ne):
        base = jnp.where(ids == e, pstart_ref[e], base)
    o_ref[0] = base + rnk_ref[0]


def _dest(pstart, ids, rnk, ne):
    bsz, k, s = ids.shape
    tn = _pick(s, (2048, 1024, 512, 256, 128))
    blk = lambda: pl.BlockSpec((1, k, tn), lambda b, j, ps: (b, 0, j))
    return pl.pallas_call(
        functools.partial(_dest_kernel, ne=ne),
        grid_spec=pltpu.PrefetchScalarGridSpec(num_scalar_prefetch=1, grid=(bsz, s // tn),
                                               in_specs=[blk(), blk()], out_specs=blk()),
        out_shape=jax.ShapeDtypeStruct((bsz, k, s), I32),
        compiler_params=_cparams("arbitrary", "arbitrary"),
        name="dest",
    )(pstart, ids, rnk)


def _dispatch_kernel(dst_ref, h_ref, xs_hbm, sem, *, tm, topk):
    def start(t, c):
        for k in range(topk):
            pltpu.make_async_copy(h_ref.at[pl.ds(t, 1)], xs_hbm.at[pl.ds(dst_ref[0, k, t], 1)], sem.at[0]).start()
        return c

    lax.fori_loop(0, tm, start, 0)

    def wait(t, c):
        for k in range(topk):
            pltpu.make_async_copy(h_ref.at[pl.ds(t, 1)], xs_hbm.at[pl.ds(0, 1)], sem.at[0]).wait()
        return c

    lax.fori_loop(0, tm, wait, 0)


def _dispatch(dest, h2, n_rows):
    n, d = h2.shape
    nt, topk, tm = dest.shape
    return pl.pallas_call(
        functools.partial(_dispatch_kernel, tm=tm, topk=topk),
        grid=(nt,),
        in_specs=[pl.BlockSpec((1, topk, tm), lambda i: (i, 0, 0), memory_space=pltpu.SMEM),
                  pl.BlockSpec((tm, d), lambda i: (i, 0))],
        out_specs=pl.BlockSpec(memory_space=pl.ANY),
        out_shape=jax.ShapeDtypeStruct((n_rows, d), h2.dtype),
        scratch_shapes=[pltpu.SemaphoreType.DMA((1,))],
        compiler_params=_cparams("arbitrary"),
        name="dispatch",
    )(dest, h2)


def _experts_kernel(be_ref, nv_ref, x_ref, wg_ref, wu_ref, wd_ref, y_ref, wgb, wub, wdb, *, blk):
    i = pl.program_id(0)
    prev_e = be_ref[jnp.maximum(i - 1, 0)]

    @pl.when(jnp.logical_or(i == 0, be_ref[i] != prev_e))
    def _():
        wgb[...] = wg_ref[0].astype(BF16)
        wub[...] = wu_ref[0].astype(BF16)
        wdb[...] = wd_ref[0].astype(BF16)

    @pl.when(nv_ref[i] > 0)
    def _():
        rows = lax.broadcasted_iota(I32, (blk, 1), 0)
        xu = jnp.where(rows < nv_ref[i], x_ref[...], jnp.uint32(0))
        x = jnp.concatenate(_unpack_bf16_pairs(xu), axis=1).astype(BF16)
        act = (_silu(_dot(x, wgb[...])) * _dot(x, wub[...])).astype(BF16)
        y_ref[...] = _pack_bf16_pairs(_dot(act, wdb[...]))


def _experts(xs, be, nvalid, wg, wu, wd):
    n_rows, dh = xs.shape
    ne, d, f = wg.shape
    blk = EXPERT_BLOCK
    nblocks = n_rows // blk
    grid_spec = pltpu.PrefetchScalarGridSpec(
        num_scalar_prefetch=2,
        grid=(nblocks,),
        in_specs=[pl.BlockSpec((blk, dh), lambda i, be, nv: (i, 0)),
                  pl.BlockSpec((1, d, f), lambda i, be, nv: (be[i], 0, 0)),
                  pl.BlockSpec((1, d, f), lambda i, be, nv: (be[i], 0, 0)),
                  pl.BlockSpec((1, f, d), lambda i, be, nv: (be[i], 0, 0))],
        out_specs=pl.BlockSpec((blk, dh), lambda i, be, nv: (i, 0)),
        scratch_shapes=[pltpu.VMEM((d, f), BF16), pltpu.VMEM((d, f), BF16), pltpu.VMEM((f, d), BF16)],
    )
    return pl.pallas_call(
        functools.partial(_experts_kernel, blk=blk),
        grid_spec=grid_spec,
        out_shape=jax.ShapeDtypeStruct((n_rows, dh), U32),
        compiler_params=_cparams("arbitrary"),
        name="experts",
    )(be, nvalid, xs, wg, wu, wd)


def _final_kernel(dst_ref, dstn_ref, wts_ref, h2_ref, x1_ref, g2_ref, wg_ref, wu_ref, wd_ref, lng_ref, lnb_ref,
                  ys_hbm, o_ref, ybuf, sem, *, alpha, topk, tm, nt):
    i = pl.program_id(0)
    slot = lax.rem(i, 2)

    def gather_start(ids_ref, sl):
        def body(t, c):
            for k in range(topk):
                pltpu.make_async_copy(ys_hbm.at[pl.ds(ids_ref[0, k, t], 1)], ybuf.at[sl, k, pl.ds(t, 1)],
                                      sem.at[sl]).start()
            return c
        lax.fori_loop(0, tm, body, 0)

    @pl.when(i == 0)
    def _():
        gather_start(dst_ref, 0)

    for t in range(tm):
        for k in range(topk):
            pltpu.make_async_copy(ys_hbm.at[pl.ds(dstn_ref[0, k, t], 1)], ybuf.at[1 - slot, k, pl.ds(t, 1)],
                                  sem.at[1 - slot]).start()

    h = jnp.concatenate(_unpack_bf16_pairs(h2_ref[...]), axis=1).astype(BF16)
    y = _dot((_silu(_dot(h, wg_ref[...])) * _dot(h, wu_ref[...])).astype(BF16), wd_ref[...])

    def wait_slot(sl):
        def body(t, c):
            for k in range(topk):
                pltpu.make_async_copy(ys_hbm.at[pl.ds(0, 1)], ybuf.at[sl, k, pl.ds(t, 1)], sem.at[sl]).wait()
            return c
        lax.fori_loop(0, tm, body, 0)

    wait_slot(slot)
    w = wts_ref[...]
    ylo = jnp.zeros((tm, y.shape[1] // 2), F32)
    yhi = jnp.zeros((tm, y.shape[1] // 2), F32)
    for k in range(topk):
        lo, hi = _unpack_bf16_pairs(ybuf[slot, k])
        ylo = ylo + w[:, k:k + 1] * lo
        yhi = yhi + w[:, k:k + 1] * hi
    y = y + jnp.concatenate([ylo, yhi], axis=1)
    o_ref[...] = _ln(alpha * x1_ref[...] + g2_ref[0] * y) * lng_ref[...] + lnb_ref[...]

    @pl.when(i == nt - 1)
    def _():
        wait_slot(1 - slot)


def _final(ys, dest, wtsT, h2, x1, gate2, wg, wu, wd, lng, lnb, alpha, tiles_per_batch):
    n, d = x1.shape
    nt, topk, tm = dest.shape
    kern = functools.partial(_final_kernel, alpha=alpha, topk=topk, tm=tm, nt=nt)
    row = pl.BlockSpec((tm, d), lambda i: (i, 0))
    full = lambda a: pl.BlockSpec(a.shape, lambda i: (0,) * a.ndim, pipeline_mode=pl.Buffered(1))
    return pl.pallas_call(
        kern,
        grid=(nt,),
        in_specs=[pl.BlockSpec((1, topk, tm), lambda i: (i, 0, 0), memory_space=pltpu.SMEM),
                  pl.BlockSpec((1, topk, tm), lambda i: (jnp.minimum(i + 1, nt - 1), 0, 0), memory_space=pltpu.SMEM),
                  pl.BlockSpec((tm, topk), lambda i: (i, 0)),
                  pl.BlockSpec((tm, d // 2), lambda i: (i, 0)), row,
                  pl.BlockSpec((1, 1, d), lambda i: (i // tiles_per_batch, 0, 0)),
                  full(wg), full(wu), full(wd),
                  pl.BlockSpec((1, d), lambda i: (0, 0)), pl.BlockSpec((1, d), lambda i: (0, 0)),
                  pl.BlockSpec(memory_space=pl.ANY)],
        out_specs=row,
        out_shape=jax.ShapeDtypeStruct((n, d), F32),
        scratch_shapes=[pltpu.VMEM((2, topk, tm, d // 2), U32), pltpu.SemaphoreType.DMA((2,))],
        compiler_params=_cparams("arbitrary"),
        name="final",
    )(dest, dest, wtsT, h2, x1, gate2, wg, wu, wd, lng.reshape(1, d), lnb.reshape(1, d), ys)


def _proj_layout(d, qr, kvr, idim, ih, hk, hv):
    src = np.cumsum([0, qr, kvr, idim, ih, hk, hk, hv, hv, d, d])
    wa = -(-(qr + kvr + idim + ih) // LANES) * LANES
    wa = max(wa, 1 << (wa - 1).bit_length())
    pieces = [("a", wa, (int(src[0]), int(src[4]))),
              ("hq", hk, (int(src[4]), int(src[5]))), ("hf", hk, (int(src[5]), int(src[6]))),
              ("hi", hv, (int(src[6]), int(src[7]))), ("hg", hv, (int(src[7]), int(src[8]))),
              ("ga", d, (int(src[8]), int(src[9]))), ("gb", d, (int(src[9]), int(src[10])))]
    pieces.sort(key=lambda p: -p[1])
    off = 0
    layout = {}
    for name, wdt, rng in pieces:
        assert off % wdt == 0
        layout[name] = (off, wdt, rng)
        off += wdt
    return layout, off


def kernel(x, c, rpb_table, hgrn_lb_logits, ada_w, ada_b, w_in, q_norm_g, kv_norm_g, w_uq, w_uk, w_uv, w_qidx,
           idx_k_norm_g, idx_k_norm_b, hgrn_out_norm_g, w_branch_a, w_branch_b, w_o, ln1_g, ln1_b, w_router,
           router_bias, w_exp_gate, w_exp_up, w_exp_down, w_sh_gate, w_sh_up, w_sh_down, ln2_g, ln2_b):
    bsz, s, d = x.shape
    depth = ada_w.shape[0]
    qr = w_uq.shape[1]
    ah, kvr, qk = w_uk.shape[1], w_uk.shape[2], w_uk.shape[3]
    idim = idx_k_norm_g.shape[1]
    ih = w_qidx.shape[2] // idim
    dv = hgrn_out_norm_g.shape[1]
    nh = w_branch_b.shape[1] // dv
    dk = hgrn_lb_logits.shape[1] // nh
    ne = w_router.shape[2]
    topk = min(IDX_TOPK, s // 4)
    alpha = float((2 * depth) ** 0.25)
    n_tok = bsz * s

    lower_bounds = jnp.cumsum(jax.nn.softmax(hgrn_lb_logits.astype(F32), axis=0), axis=0)
    layout, wtot = _proj_layout(d, qr, kvr, idim, ih, nh * dk, nh * dv)

    for l in range(depth):
        mod = _ada(c, ada_w[l], ada_b[l])[:, None, :]
        shift1, scale1, gate1, shift2, scale2, gate2 = jnp.split(mod, 6, axis=-1)

        cols = []
        for off, wdt, (lo, hi) in sorted(layout.values()):
            cols.append(w_in[l][:, lo:hi].astype(BF16))
            if wdt > hi - lo:
                cols.append(jnp.zeros((d, wdt - (hi - lo)), BF16))
        proj = _inproj(x, scale1, shift1, jnp.concatenate(cols, axis=1))
        blk = lambda name: layout[name][0] // layout[name][1]

        qb_sz = _pick(s, (256, 128))
        q_latT, q_idxT, w_idxT, k_idx, kv_lat = _prep(
            proj, blk("a"), layout["a"][1], (qr, kvr, idim, ih, ah, qk), q_norm_g[l], kv_norm_g[l],
            idx_k_norm_g[l], idx_k_norm_b[l], jnp.swapaxes(w_uq[l], 0, 1).astype(BF16),
            w_uk[l].astype(BF16), jnp.swapaxes(w_qidx[l], 0, 1).astype(BF16), qb_sz)
        kv_latT = jnp.concatenate([jnp.swapaxes(kv_lat, 1, 2), jnp.ones((bsz, 1, s), BF16),
                                   jnp.zeros((bsz, 7, s), BF16)], axis=1)
        o_a = _dsa(q_idxT, w_idxT, q_latT, k_idx, kv_lat, kv_latT,
                   rpb_table, jnp.swapaxes(w_uv[l], 1, 2).astype(BF16), topk, qb_sz)

        o_b = _hgrn(proj, (blk("hq"), blk("hf"), blk("hi"), blk("hg")), lower_bounds[l],
                    hgrn_out_norm_g[l], nh, dk, dv)

        x1, h2, lgT = _postmix(o_a, o_b, proj, blk("ga"), blk("gb"), x, gate1, scale2, shift2,
                               w_branch_a[l].astype(BF16), w_branch_b[l].astype(BF16), w_o[l].astype(BF16),
                               jnp.swapaxes(w_router[l], 0, 1).astype(BF16), ln1_g[l], ln1_b[l], alpha)

        ids, wts, rnk, sizes_f = _route(lgT, router_bias[l])
        eb = EXPERT_BLOCK
        sizes = sizes_f[:, 0].astype(I32)
        padded = (sizes + eb - 1) // eb * eb
        pend = jnp.cumsum(padded)
        pstart = pend - padded
        n_rows = -(-(n_tok * MOE_TOPK + ne * (eb - 1)) // eb) * eb
        blk_start = jnp.arange(n_rows // eb, dtype=I32) * eb
        blk_expert = jnp.minimum(jnp.sum((pend[None, :] <= blk_start[:, None]).astype(I32), axis=1), ne - 1)
        onehot = (blk_expert[:, None] == jnp.arange(ne, dtype=I32)[None, :]).astype(I32)
        blk_size = jnp.sum(onehot * sizes[None, :], axis=1)
        blk_pstart = jnp.sum(onehot * pstart[None, :], axis=1)
        nvalid = jnp.clip(blk_size - (blk_start - blk_pstart), 0, eb).astype(I32)
        dest = _dest(pstart.astype(I32), ids, rnk, ne)
        tm = _pick(s, (128,))
        dest_t = dest.reshape(bsz, MOE_TOPK, s // tm, tm).transpose(0, 2, 1, 3).reshape(n_tok // tm, MOE_TOPK, tm)
        wtsT = jnp.swapaxes(wts, 1, 2).reshape(n_tok, MOE_TOPK)

        h2f = h2.reshape(n_tok, d // 2)
        xs = _dispatch(dest_t, h2f, n_rows)
        ys = _experts(xs, blk_expert, nvalid, w_exp_gate[l], w_exp_up[l], w_exp_down[l])
        x = _final(ys, dest_t, wtsT, h2f, x1.reshape(n_tok, d), gate2, w_sh_gate[l].astype(BF16),
                   w_sh_up[l].astype(BF16), w_sh_down[l].astype(BF16), ln2_g[l], ln2_b[l], alpha,
                   s // tm).reshape(bsz, s, d)
    return x
```

```python
import functools
import math

import numpy as np
import jax
import jax.numpy as jnp
from jax import lax
from jax.experimental import pallas as pl
from jax.experimental.pallas import tpu as pltpu

F32 = jnp.float32
BF16 = jnp.bfloat16
I32 = jnp.int32
U32 = jnp.uint32

EPS = 1e-6
IDX_TOPK = 256
RPB_MAX_DIST = 128
MOE_TOPK = 8
N_GROUPS = 8
TOPK_GROUPS = 4
ROUTED_SCALE = 2.5
EXPERT_BLOCK = 512

V7X_VMEM_LIMIT_BYTES = 56 * 1024 * 1024
LANES = 128
INT_MIN = -2 ** 31
NEG_BIG = -1e30
DSA_FAR_GROUP = 4
LOG2E = math.log2(math.e)


def _cparams(*sem):
    return pltpu.CompilerParams(dimension_semantics=tuple(sem), vmem_limit_bytes=V7X_VMEM_LIMIT_BYTES)


def _pick(n, prefs):
    for p in prefs:
        if n % p == 0:
            return p
    return n


def _sigmoid(v):
    return 1.0 / (1.0 + jnp.exp(-v))


def _silu(v):
    return v * _sigmoid(v)


def _ln(v):
    mu = jnp.mean(v, axis=-1, keepdims=True)
    d = v - mu
    var = jnp.mean(d * d, axis=-1, keepdims=True)
    return d * lax.rsqrt(var + EPS)


def _rms(v):
    return v * lax.rsqrt(jnp.mean(v * v, axis=-1, keepdims=True) + EPS)


def _dot(a, b):
    return jnp.dot(a, b, preferred_element_type=F32)


def _pack_bf16_pairs(v):
    n = v.shape[1] // 2
    lo = pltpu.bitcast(v[:, :n].astype(BF16).astype(F32), U32) >> 16
    hi = pltpu.bitcast(v[:, n:].astype(BF16).astype(F32), U32) & jnp.uint32(0xFFFF0000)
    return lo | hi


def _unpack_bf16_pairs(u):
    return pltpu.bitcast(u << 16, F32), pltpu.bitcast(u & jnp.uint32(0xFFFF0000), F32)


def _dot_nt(a, b):
    return lax.dot_general(a, b, (((1,), (1,)), ((), ())), preferred_element_type=F32)


def _dot_tn(a, b):
    return lax.dot_general(a, b, (((0,), (0,)), ((), ())), preferred_element_type=F32)


def _ada_kernel(c_ref, w_ref, b_ref, o_ref):
    c = c_ref[...]
    o_ref[...] = _dot(_silu(c).astype(BF16), w_ref[...].astype(BF16)) + b_ref[...]


def _ada(c, w, b):
    bsz, d = c.shape
    n = w.shape[1]
    rows = 8
    cp = jnp.zeros((rows, d), F32).at[:bsz].set(c)
    tn = _pick(n, (1024, 512, 256, 128))
    out = pl.pallas_call(
        _ada_kernel,
        grid=(n // tn,),
        in_specs=[pl.BlockSpec((rows, d), lambda j: (0, 0)),
                  pl.BlockSpec((d, tn), lambda j: (0, j)),
                  pl.BlockSpec((1, tn), lambda j: (0, j))],
        out_specs=pl.BlockSpec((rows, tn), lambda j: (0, j)),
        out_shape=jax.ShapeDtypeStruct((rows, n), F32),
        compiler_params=_cparams("arbitrary"),
        name="ada",
    )(cp, w, b.reshape(1, n))
    return out[:bsz]


def _inproj_kernel(x_ref, sc_ref, sh_ref, w_ref, o_ref, h_ref):
    @pl.when(pl.program_id(2) == 0)
    def _():
        h = _ln(x_ref[0]) * (1.0 + sc_ref[0]) + sh_ref[0]
        h_ref[...] = h.astype(BF16)

    o_ref[0] = _dot(h_ref[...], w_ref[...]).astype(o_ref.dtype)


def _inproj(x, scale, shift, w):
    bsz, s, d = x.shape
    n = w.shape[1]
    tm = _pick(s, (1024, 512, 256, 128))
    tn = _pick(n, (1024, 512, 256, 128))
    return pl.pallas_call(
        _inproj_kernel,
        grid=(bsz, s // tm, n // tn),
        in_specs=[pl.BlockSpec((1, tm, d), lambda b, i, j: (b, i, 0)),
                  pl.BlockSpec((1, 1, d), lambda b, i, j: (b, 0, 0)),
                  pl.BlockSpec((1, 1, d), lambda b, i, j: (b, 0, 0)),
                  pl.BlockSpec((d, tn), lambda b, i, j: (0, j))],
        out_specs=pl.BlockSpec((1, tm, tn), lambda b, i, j: (b, i, j)),
        out_shape=jax.ShapeDtypeStruct((bsz, s, n), BF16),
        scratch_shapes=[pltpu.VMEM((tm, d), BF16)],
        compiler_params=_cparams("arbitrary", "arbitrary", "arbitrary"),
        name="inproj",
    )(x, scale, shift, w)


def _prep_kernel(a_ref, qg_ref, kvg_ref, ikg_ref, ikb_ref, wuqT_ref, wuk_ref, wqiT_ref,
                 qlatT_ref, qidxT_ref, widxT_ref, kidx_ref, kv_ref, *, qr, kvr, idim, ih, ah, qk):
    a = a_ref[0]
    af = a.astype(F32)
    cq = (_rms(af[:, :qr]) * qg_ref[...]).astype(BF16)
    ckv = af[:, qr:qr + kvr]
    ki = af[:, qr + kvr:qr + kvr + idim]
    kv_ref[0] = (_rms(ckv) * kvg_ref[...]).astype(BF16)
    kidx_ref[0] = (_ln(ki) * ikg_ref[...] + ikb_ref[...]).astype(BF16)
    tail = a[:, qr + kvr:qr + kvr + LANES]
    eye = (lax.broadcasted_iota(I32, (LANES, LANES), 0) == lax.broadcasted_iota(I32, (LANES, LANES), 1))
    tailT = _dot_nt(jnp.where(eye, 1.0, 0.0).astype(BF16), tail)
    widxT_ref[0] = tailT[idim:idim + ih] * float((ih * idim) ** -0.5)
    tm = a.shape[0]
    qT = _dot_nt(wuqT_ref[...], cq)
    for h in range(ah):
        qh = qT[h * qk:(h + 1) * qk].astype(BF16)
        qlatT_ref[0, 0, :, h * tm:(h + 1) * tm] = (_dot(wuk_ref[h], qh) * float(qk ** -0.5 * LOG2E)).astype(BF16)
    qiT = _dot_nt(wqiT_ref[...], cq)
    for h in range(ih):
        qidxT_ref[0, 0, :, h * tm:(h + 1) * tm] = qiT[h * idim:(h + 1) * idim].astype(BF16)


def _prep(proj, a_blk, wa, dims, q_norm_g, kv_norm_g, ikg, ikb, w_uqT, w_uk, w_qidxT, tm):
    bsz, s, _ = proj.shape
    qr, kvr, idim, ih, ah, qk = dims
    assert idim + ih <= LANES and qr + kvr + LANES <= wa
    kern = functools.partial(_prep_kernel, qr=qr, kvr=kvr, idim=idim, ih=ih, ah=ah, qk=qk)
    full = lambda shape: pl.BlockSpec(shape, lambda b, i: (0,) * len(shape))
    return pl.pallas_call(
        kern,
        grid=(bsz, s // tm),
        in_specs=[pl.BlockSpec((1, tm, wa), lambda b, i: (b, i, a_blk)),
                  full((1, qr)), full((1, kvr)), full((1, idim)), full((1, idim)),
                  full(w_uqT.shape), full(w_uk.shape), full(w_qidxT.shape)],
        out_specs=[pl.BlockSpec((1, 1, kvr, ah * tm), lambda b, i: (b, i, 0, 0)),
                   pl.BlockSpec((1, 1, idim, ih * tm), lambda b, i: (b, i, 0, 0)),
                   pl.BlockSpec((1, ih, tm), lambda b, i: (b, 0, i)),
                   pl.BlockSpec((1, tm, idim), lambda b, i: (b, i, 0)),
                   pl.BlockSpec((1, tm, kvr), lambda b, i: (b, i, 0))],
        out_shape=[jax.ShapeDtypeStruct((bsz, s // tm, kvr, ah * tm), BF16),
                   jax.ShapeDtypeStruct((bsz, s // tm, idim, ih * tm), BF16),
                   jax.ShapeDtypeStruct((bsz, ih, s), F32),
                   jax.ShapeDtypeStruct((bsz, s, idim), BF16),
                   jax.ShapeDtypeStruct((bsz, s, kvr), BF16)],
        compiler_params=_cparams("arbitrary", "arbitrary"),
        name="prep",
    )(proj, q_norm_g.reshape(1, qr), kv_norm_g.reshape(1, kvr), ikg.reshape(1, idim), ikb.reshape(1, idim),
      w_uqT, w_uk, w_qidxT)


def _dsa_kernel(qidxT_ref, widxT_ref, qlatT_ref, kidx_ref, kv_ref, kvT_ref, bkt0_ref, bkt1_ref, rel_ref, wuvT_ref,
                o_ref, keys_ref, gmax_ref, s_ref, p_ref, m_ref, a_ref, acc_ref, t0_ref, t1_ref, lstrict_ref, taken_ref, sx_ref, px_ref, ax_ref,
                *, qb_sz, topk, ih, ah, c, nb):
    QB = qb_sz
    qb = pl.program_id(1)
    nchunks = qb + 1
    qpos = qb * QB + lax.broadcasted_iota(I32, (1, QB), 1)
    kpos = lax.broadcasted_iota(I32, (QB, 1), 0)
    wT = widxT_ref[0]
    hcols = lambda h: slice(h * QB, (h + 1) * QB)

    @pl.when(jnp.logical_and(pl.program_id(0) == 0, qb == 0))
    def _():
        for bkt_ref, t_ref in ((bkt0_ref, t0_ref), (bkt1_ref, t1_ref)):
            bkt = bkt_ref[...]
            for h in range(ah):
                tile = jnp.zeros((QB, QB), F32)
                for b in range(nb - 1):
                    tile = jnp.where(bkt == b, rel_ref[b, h], tile)
                t_ref[:, hcols(h)] = tile
        lstrict_ref[...] = jnp.where(lax.broadcasted_iota(I32, (QB, QB), 1) < lax.broadcasted_iota(I32, (QB, QB), 0),
                                     1.0, 0.0).astype(BF16)

    def score_chunk(kc, carry):
        off = pl.multiple_of(kc * QB, QB)
        s_ref[:, :ih * QB] = _dot(kidx_ref[0, pl.ds(off, QB), :], qidxT_ref[0, 0])
        acc = jnp.zeros((QB, QB), F32)
        for h in range(ih):
            acc = acc + wT[h:h + 1] * jnp.maximum(s_ref[:, hcols(h)], 0.0)
        bits = pltpu.bitcast(acc, I32)
        skey = bits ^ ((bits >> 31) & 0x7FFFFFFF)
        causal = (off + kpos) <= qpos
        skey = jnp.where(causal, skey, INT_MIN)
        keys_ref[pl.ds(off, QB), :] = skey
        gmax_ref[...] = jnp.maximum(gmax_ref[...], skey)
        return carry

    gmax_ref[...] = jnp.full(gmax_ref.shape, INT_MIN, I32)
    lax.fori_loop(0, nchunks, score_chunk, 0)

    gmax = gmax_ref[...]
    lo0 = jnp.min(gmax, axis=0, keepdims=True) if QB >= topk else jnp.full((1, QB), INT_MIN, I32)
    hi0 = jnp.max(gmax, axis=0, keepdims=True) + 1

    def count_ge(cand):
        def body(kc, cnt):
            off = pl.multiple_of(kc * QB, QB)
            hit = jnp.where(keys_ref[pl.ds(off, QB), :] >= cand, 1, 0)
            return cnt + jnp.sum(hit.reshape(QB // 8, 8, QB), axis=0)

        cnt = lax.fori_loop(0, nchunks, body, jnp.zeros((8, QB), I32))
        return jnp.sum(cnt.astype(F32), axis=0, keepdims=True)

    def bis_cond(st):
        it, _, _, done, _, _ = st
        return jnp.logical_and(it < 34, jnp.min(done) < 0.5)

    def bis_body(st):
        it, lo, hi, done, c_hi, need = st
        cand = (lo >> 1) + (hi >> 1) + (lo & hi & 1)
        cnt = count_ge(cand)
        ge = cnt >= float(topk)
        conv = cand == lo
        fin = jnp.logical_or(conv, cnt == float(topk))
        need = jnp.where(jnp.logical_and(fin, done < 0.5), jnp.where(conv, float(topk) - c_hi, float(topk)), need)
        c_hi = jnp.where(jnp.logical_or(ge, fin), c_hi, cnt)
        thr_new = jnp.where(conv, lo, cand)
        lo = jnp.where(fin, thr_new, jnp.where(ge, cand, lo))
        hi = jnp.where(fin, thr_new + 1, jnp.where(ge, hi, cand))
        return it + 1, lo, hi, jnp.where(fin, 1.0, done), c_hi, need

    zeros_q = jnp.zeros((1, QB), F32)
    _, thr, _, _, _, need = lax.while_loop(bis_cond, bis_body, (jnp.int32(0), lo0, hi0, zeros_q, zeros_q, zeros_q))

    m_ref[...] = jnp.full(m_ref.shape, NEG_BIG, F32)
    acc_ref[...] = jnp.zeros(acc_ref.shape, F32)

    def attn_chunk(off, bias_ref, diag, taken, s_ref=s_ref, p_ref=p_ref, a_ref=a_ref):
        kt = keys_ref[pl.ds(off, QB), :]
        eq = kt == thr
        eq01 = jnp.where(eq, 1.0, 0.0)
        before = _dot(lstrict_ref[...], eq01.astype(BF16)) + taken
        sel = jnp.logical_or(kt > thr, jnp.logical_and(eq, before < need))
        taken = taken + jnp.sum(eq01, axis=0, keepdims=True)
        if diag:
            sel = jnp.logical_and(sel, (off + kpos) <= qpos)
        madd = jnp.where(sel, 0.0, NEG_BIG)
        s_ref[:, :ah * QB] = _dot(kv_ref[0, pl.ds(off, QB), :], qlatT_ref[0, 0])
        for g in range(ah * QB // LANES):
            cols = slice(g * LANES, (g + 1) * LANES)
            qcols = slice(g * LANES % QB, g * LANES % QB + LANES)
            s = s_ref[:, cols] + madd[:, qcols]
            if bias_ref is not None:
                s = s + bias_ref[:, cols]
            m_prev = m_ref[:, cols]
            m_new = jnp.maximum(m_prev, jnp.max(s, axis=0, keepdims=True))
            p_ref[:, cols] = jnp.exp2(s - m_new).astype(BF16)
            a_ref[:, cols] = jnp.exp2(m_prev - m_new)
            m_ref[:, cols] = m_new
        acc_ref[...] = a_ref[...] * acc_ref[...] + _dot(kvT_ref[0, :, pl.ds(off, QB)], p_ref[...])
        return taken

    G = DSA_FAR_GROUP

    def far_group(kg, taken):
        off = pl.multiple_of(kg * (G * QB), G * QB)
        taken = attn_chunk(off, None, False, taken)
        for j in range(1, G):
            taken = attn_chunk(off + j * QB, None, False, taken, sx_ref.at[j - 1], px_ref.at[j - 1], ax_ref.at[j - 1])
        return taken

    nfar = jnp.maximum(qb - 1, 0)
    ngroups = nfar // G
    taken = lax.fori_loop(0, ngroups, far_group, jnp.zeros((1, QB), F32))

    def far_single(r, taken):
        return attn_chunk(pl.multiple_of((ngroups * G + r) * QB, QB), None, False, taken)

    taken_ref[...] = lax.fori_loop(0, nfar - ngroups * G, far_single, taken)

    @pl.when(qb >= 1)
    def _():
        taken_ref[...] = attn_chunk(pl.multiple_of((qb - 1) * QB, QB), t1_ref, False, taken_ref[...])

    attn_chunk(pl.multiple_of(qb * QB, QB), t0_ref, True, taken_ref[...])

    outs = []
    for h in range(ah):
        o = (acc_ref[:c, hcols(h)] / acc_ref[c:c + 1, hcols(h)]).astype(BF16)
        outs.append(_dot(wuvT_ref[h], o))
    o_ref[0] = jnp.concatenate(outs, axis=0).T.astype(o_ref.dtype)


def _t5_bucket_np(d, nbuckets):
    max_exact = nbuckets // 2
    dd = np.maximum(d, 1).astype(np.float32)
    large = max_exact + (np.log(dd / np.float32(max_exact)) / np.float32(math.log(RPB_MAX_DIST / max_exact))
                         * np.float32(nbuckets - max_exact)).astype(np.int32)
    large = np.minimum(large, nbuckets - 1)
    return np.where(d < max_exact, d, large).astype(np.int32)


def _dsa(q_idxT, w_idxT, q_latT, k_idx, kv_lat, kv_latT, rpb_table, w_uvT, topk, QB):
    bsz, nqb, idim, ihq = q_idxT.shape
    c, ahq = q_latT.shape[2], q_latT.shape[3]
    ca = kv_latT.shape[1]
    ih, ah = ihq // QB, ahq // QB
    s = nqb * QB
    vd = w_uvT.shape[1]
    nb = rpb_table.shape[0]
    assert QB >= RPB_MAX_DIST
    j = np.arange(QB)[:, None]
    i = np.arange(QB)[None, :]
    bkt0 = jnp.asarray(_t5_bucket_np(np.maximum(i - j, 0), nb))
    bkt1 = jnp.asarray(_t5_bucket_np(QB + i - j, nb))
    assert int(_t5_bucket_np(np.array([RPB_MAX_DIST]), nb)[0]) == nb - 1
    rel = (rpb_table.astype(F32) - rpb_table[nb - 1].astype(F32)[None, :]) * LOG2E
    kern = functools.partial(_dsa_kernel, qb_sz=QB, topk=topk, ih=ih, ah=ah, c=c, nb=nb)
    const = lambda shape: pl.BlockSpec(shape, lambda b, i: (0,) * len(shape), pipeline_mode=pl.Buffered(1))
    hw = max(ih, ah) * QB
    return pl.pallas_call(
        kern,
        grid=(bsz, nqb),
        in_specs=[pl.BlockSpec((1, 1, idim, ih * QB), lambda b, i: (b, i, 0, 0)),
                  pl.BlockSpec((1, ih, QB), lambda b, i: (b, 0, i)),
                  pl.BlockSpec((1, 1, c, ah * QB), lambda b, i: (b, i, 0, 0)),
                  pl.BlockSpec((1, s, idim), lambda b, i: (b, 0, 0), pipeline_mode=pl.Buffered(1)),
                  pl.BlockSpec((1, s, c), lambda b, i: (b, 0, 0), pipeline_mode=pl.Buffered(1)),
                  pl.BlockSpec((1, ca, s), lambda b, i: (b, 0, 0), pipeline_mode=pl.Buffered(1)),
                  const((QB, QB)), const((QB, QB)),
                  pl.BlockSpec(memory_space=pltpu.SMEM), const(w_uvT.shape)],
        out_specs=pl.BlockSpec((1, QB, ah * vd), lambda b, i: (b, i, 0)),
        out_shape=jax.ShapeDtypeStruct((bsz, s, ah * vd), BF16),
        scratch_shapes=[pltpu.VMEM((s, QB), I32),
                        pltpu.VMEM((QB, QB), I32),
                        pltpu.VMEM((QB, hw), F32),
                        pltpu.VMEM((QB, ah * QB), BF16),
                        pltpu.VMEM((1, ah * QB), F32),
                        pltpu.VMEM((1, ah * QB), F32),
                        pltpu.VMEM((ca, ah * QB), F32),
                        pltpu.VMEM((QB, ah * QB), F32),
                        pltpu.VMEM((QB, ah * QB), F32),
                        pltpu.VMEM((QB, QB), BF16),
                        pltpu.VMEM((1, QB), F32),
                        pltpu.VMEM((DSA_FAR_GROUP - 1, QB, ah * QB), F32),
                        pltpu.VMEM((DSA_FAR_GROUP - 1, QB, ah * QB), BF16),
                        pltpu.VMEM((DSA_FAR_GROUP - 1, 1, ah * QB), F32)],
        compiler_params=_cparams("arbitrary", "arbitrary"),
        name="dsa",
    )(q_idxT, w_idxT, q_latT, k_idx, kv_lat, kv_latT, bkt0, bkt1, rel, w_uvT)


HGRN_CHUNK = 64
HGRN_SUB = 16
HGRN_EXP_CLAMP = 80.0


def _dot_exact(a, b, dims):
    return lax.dot_general(a, b, (dims, ((), ())), precision=lax.Precision.HIGHEST, preferred_element_type=F32)


def _hgrn_kernel(hq_ref, hf_ref, hi_ref, hg_ref, lb_ref, g_ref, o_ref, st_ref, st0_ref, *, nh, dk, dv, tt):
    C, SUB = HGRN_CHUNK, HGRN_SUB

    @pl.when(pl.program_id(1) == 0)
    def _():
        st_ref[...] = jnp.zeros(st_ref.shape, F32)

    r = lax.broadcasted_iota(I32, (C, C), 0)
    cc = lax.broadcasted_iota(I32, (C, C), 1)
    tri_mask = r >= cc
    tri = jnp.where(tri_mask, 1.0, 0.0).astype(BF16)
    g = g_ref[...]
    st0_ref[...] = st_ref[...]
    decay = jnp.zeros((1, nh * dk), F32)
    lb = lb_ref[...]

    for c in range(tt // C):
        rows = pl.ds(c * C, C)
        f = lb + (1.0 - lb) * _sigmoid(hf_ref[0, rows, :].astype(F32))
        lf = jnp.log(f)
        t1 = lf.astype(BF16)
        r1 = lf - t1.astype(F32)
        t2 = r1.astype(BF16)
        t3 = (r1 - t2.astype(F32)).astype(BF16)
        b = _dot(tri, t1) + _dot(tri, t2) + _dot(tri, t3)
        kk = 1.0 - f
        hq = hq_ref[0, rows, :].astype(F32)
        q = _silu(hq) * float(dk ** -0.5)
        qe = (q * jnp.exp(b)).astype(BF16)
        b_last = b[C - 1:C]
        k_dec = (kk * jnp.exp(b_last - b)).astype(BF16)
        dec_last = jnp.exp(b_last)
        qs, ks = [], []
        for i in range(C // SUB):
            lo, n = i * SUB, (i + 1) * SUB
            bi = b[lo - 1:lo] if i > 0 else jnp.zeros((1, nh * dk), F32)
            decay = jnp.maximum(decay, bi - b[n - 1:n])
            qs.append((q[lo:n] * jnp.exp(b[lo:n] - bi)).astype(BF16))
            ks.append((kk * jnp.exp(jnp.minimum(bi - b, HGRN_EXP_CLAMP))).astype(BF16))
        v_all = hi_ref[0, rows, :]
        outs = []
        for h in range(nh):
            kc = slice(h * dk, (h + 1) * dk)
            v = v_all[:, h * dv:(h + 1) * dv]
            stT = st_ref[h]
            att = jnp.concatenate([_dot_nt(qs[i][:, kc], ks[i][:, kc]) for i in range(C // SUB)], axis=0)
            att = jnp.where(tri_mask, att, 0.0)
            o = _dot_nt(qe[:, kc], stT.astype(BF16)) + _dot(att.astype(BF16), v)
            st_ref[h] = stT * dec_last[:, kc] + _dot_tn(v, k_dec[:, kc])
            outs.append(_rms(o) * g)
        o_all = jnp.concatenate(outs, axis=1) * _silu(hg_ref[0, rows, :].astype(F32))
        o_ref[0, rows, :] = o_all.astype(o_ref.dtype)

    @pl.when(jnp.max(decay) > HGRN_EXP_CLAMP)
    def _():
        row = lax.broadcasted_iota(I32, (SUB, 1), 0)

        def head(h, carry):
            ko = pl.multiple_of(h * dk, dk)
            vo = pl.multiple_of(h * dv, dv)
            lb = lb_ref[:, pl.ds(ko, dk)]

            def slab(j, S):
                rows = pl.ds(pl.multiple_of(j * SUB, SUB), SUB)
                f = lb + (1.0 - lb) * _sigmoid(hf_ref[0, rows, pl.ds(ko, dk)].astype(F32))
                kk = 1.0 - f
                hq = hq_ref[0, rows, pl.ds(ko, dk)].astype(F32)
                q = _silu(hq) * float(dk ** -0.5)
                v = hi_ref[0, rows, pl.ds(vo, dv)].astype(F32)
                o = jnp.zeros((SUB, dv), F32)
                for r in range(SUB):
                    S = S * f[r:r + 1] + _dot_exact(jnp.where(row == r, v, 0.0), kk, ((0,), (0,)))
                    o = jnp.where(row == r, _dot_exact(q, S, ((1,), (1,))), o)
                hg = hg_ref[0, rows, pl.ds(vo, dv)].astype(F32)
                o_ref[0, rows, pl.ds(vo, dv)] = (_rms(o) * g * _silu(hg)).astype(o_ref.dtype)
                return S

            st_ref[h] = lax.fori_loop(0, tt // SUB, slab, st0_ref[h])
            return carry

        lax.fori_loop(0, nh, head, 0)


def _hgrn(proj, blks, lb, g, nh, dk, dv):
    bsz, s, _ = proj.shape
    tt = _pick(s, (256, 128, 64))
    kern = functools.partial(_hgrn_kernel, nh=nh, dk=dk, dv=dv, tt=tt)
    col = lambda blk, wdt: pl.BlockSpec((1, tt, wdt), lambda b, i: (b, i, blk))
    return pl.pallas_call(
        kern,
        grid=(bsz, s // tt),
        in_specs=[col(blks[0], nh * dk), col(blks[1], nh * dk), col(blks[2], nh * dv), col(blks[3], nh * dv),
                  pl.BlockSpec((1, nh * dk), lambda b, i: (0, 0)),
                  pl.BlockSpec((1, dv), lambda b, i: (0, 0))],
        out_specs=pl.BlockSpec((1, tt, nh * dv), lambda b, i: (b, i, 0)),
        out_shape=jax.ShapeDtypeStruct((bsz, s, nh * dv), BF16),
        scratch_shapes=[pltpu.VMEM((nh, dv, dk), F32), pltpu.VMEM((nh, dv, dk), F32)],
        compiler_params=_cparams("arbitrary", "arbitrary"),
        name="hgrn",
    )(proj, proj, proj, proj, lb.reshape(1, nh * dk), g.reshape(1, dv))


def _postmix_kernel(oa_ref, ob_ref, ga_ref, gb_ref, x_ref, g1_ref, sc2_ref, sh2_ref,
                    wa_ref, wb_ref, wo_ref, wrT_ref, lng_ref, lnb_ref,
                    x1_ref, h2_ref, lgT_ref, *, alpha):
    ya = _dot(oa_ref[0], wa_ref[...])
    yb = _dot(ob_ref[0], wb_ref[...])
    mix = _sigmoid(ga_ref[0].astype(F32)) * ya + _sigmoid(gb_ref[0].astype(F32)) * yb
    mixed = _dot(mix.astype(BF16), wo_ref[...])
    x1 = _ln(alpha * x_ref[0] + g1_ref[0] * mixed) * lng_ref[...] + lnb_ref[...]
    x1_ref[0] = x1
    h2 = _ln(x1) * (1.0 + sc2_ref[0]) + sh2_ref[0]
    h2_ref[0] = _pack_bf16_pairs(h2)
    lgT_ref[0] = _dot_nt(wrT_ref[...], h2.astype(BF16))


def _postmix(o_a, o_b, proj, ga_blk, gb_blk, x, gate1, scale2, shift2, wa, wb, wo, wrT, lng, lnb, alpha):
    bsz, s, d = x.shape
    ne = wrT.shape[0]
    tm = _pick(s, (256, 128))
    kern = functools.partial(_postmix_kernel, alpha=alpha)
    row = lambda wdt: pl.BlockSpec((1, tm, wdt), lambda b, i: (b, i, 0))
    vec = pl.BlockSpec((1, 1, d), lambda b, i: (b, 0, 0))
    full = lambda a: pl.BlockSpec(a.shape, lambda b, i: (0,) * a.ndim, pipeline_mode=pl.Buffered(1))
    return pl.pallas_call(
        kern,
        grid=(bsz, s // tm),
        in_specs=[row(o_a.shape[2]), row(o_b.shape[2]),
                  pl.BlockSpec((1, tm, d), lambda b, i: (b, i, ga_blk)),
                  pl.BlockSpec((1, tm, d), lambda b, i: (b, i, gb_blk)),
                  row(d), vec, vec, vec,
                  full(wa), full(wb), full(wo), full(wrT),
                  pl.BlockSpec((1, d), lambda b, i: (0, 0)), pl.BlockSpec((1, d), lambda b, i: (0, 0))],
        out_specs=[row(d), row(d // 2), pl.BlockSpec((1, ne, tm), lambda b, i: (b, 0, i))],
        out_shape=[jax.ShapeDtypeStruct((bsz, s, d), F32),
                   jax.ShapeDtypeStruct((bsz, s, d // 2), U32),
                   jax.ShapeDtypeStruct((bsz, ne, s), F32)],
        compiler_params=_cparams("arbitrary", "arbitrary"),
        name="postmix",
    )(o_a, o_b, proj, proj, x, gate1, scale2, shift2, wa, wb, wo, wrT, lng.reshape(1, d), lnb.reshape(1, d))


def _route_kernel(lg_ref, bias_ref, ids_ref, wts_ref, rnk_ref, sizes_ref, upper_ref, carry_ref, *, ne):
    first = jnp.logical_and(pl.program_id(0) == 0, pl.program_id(1) == 0)
    tn = lg_ref.shape[2]

    @pl.when(first)
    def _():
        carry_ref[...] = jnp.zeros(carry_ref.shape, F32)
        r_ = lax.broadcasted_iota(I32, (tn, tn), 0)
        c_ = lax.broadcasted_iota(I32, (tn, tn), 1)
        upper_ref[...] = jnp.where(r_ < c_, 1.0, 0.0).astype(BF16)

    per = ne // N_GROUPS
    s = _sigmoid(lg_ref[0])
    bz = s + bias_ref[...]
    ridx = lax.broadcasted_iota(I32, (per, tn), 0)
    neg_inf = jnp.float32(-jnp.inf)
    gs = []
    for g in range(N_GROUPS):
        blk = bz[g * per:(g + 1) * per]
        m1 = jnp.max(blk, axis=0, keepdims=True)
        first_hit = jnp.min(jnp.where(blk == m1, ridx, per), axis=0, keepdims=True)
        m2 = jnp.max(jnp.where(ridx == first_hit, neg_inf, blk), axis=0, keepdims=True)
        gs.append(m1 + m2)
    emask_rows = []
    for g in range(N_GROUPS):
        rank = jnp.zeros((1, tn), I32)
        for g2 in range(N_GROUPS):
            if g2 == g:
                continue
            beats = (gs[g2] > gs[g]) if g2 > g else (gs[g2] >= gs[g])
            rank = rank + jnp.where(beats, 1, 0)
        emask_rows.append(jnp.broadcast_to(rank < TOPK_GROUPS, (per, tn)))
    emask = jnp.concatenate(emask_rows, axis=0)
    masked = jnp.where(emask, bz, neg_inf)
    eidx = lax.broadcasted_iota(I32, (ne, tn), 0)
    rank = jnp.zeros((ne, tn), I32)
    for e2 in range(ne):
        row = masked[e2:e2 + 1]
        beats = jnp.logical_or(row > masked, jnp.logical_and(row == masked, e2 < eidx))
        rank = rank + jnp.where(beats, 1, 0)
    sel = rank < MOE_TOPK
    sel01 = jnp.where(sel, 1.0, 0.0)
    denom = jnp.sum(jnp.where(sel, s, 0.0), axis=0, keepdims=True)
    wn = s / denom * ROUTED_SCALE
    before = _dot(sel01.astype(BF16), upper_ref[...]) + carry_ref[:, 0:1]
    ids, wts, rnk = [], [], []
    for k in range(MOE_TOPK):
        hit = rank == k
        ids.append(jnp.sum(jnp.where(hit, eidx, 0), axis=0, keepdims=True))
        wts.append(jnp.sum(jnp.where(hit, wn, 0.0), axis=0, keepdims=True))
        rnk.append(jnp.sum(jnp.where(hit, before, 0.0), axis=0, keepdims=True))
    ids_ref[0] = jnp.concatenate(ids, axis=0)
    wts_ref[0] = jnp.concatenate(wts, axis=0)
    rnk_ref[0] = jnp.concatenate(rnk, axis=0).astype(I32)
    carry_ref[...] = carry_ref[...] + jnp.sum(sel01, axis=1, keepdims=True)
    sizes_ref[...] = carry_ref[...]


def _route(lgT, bias):
    bsz, ne, s = lgT.shape
    tn = _pick(s, (1024, 512, 256, 128))
    kern = functools.partial(_route_kernel, ne=ne)
    slot = pl.BlockSpec((1, MOE_TOPK, tn), lambda b, j: (b, 0, j))
    return pl.pallas_call(
        kern,
        grid=(bsz, s // tn),
        in_specs=[pl.BlockSpec((1, ne, tn), lambda b, j: (b, 0, j)),
                  pl.BlockSpec((ne, 1), lambda b, j: (0, 0))],
        out_specs=[slot, slot, slot, pl.BlockSpec((ne, LANES), lambda b, j: (0, 0))],
        out_shape=[jax.ShapeDtypeStruct((bsz, MOE_TOPK, s), I32),
                   jax.ShapeDtypeStruct((bsz, MOE_TOPK, s), F32),
                   jax.ShapeDtypeStruct((bsz, MOE_TOPK, s), I32),
                   jax.ShapeDtypeStruct((ne, LANES), F32)],
        scratch_shapes=[pltpu.VMEM((tn, tn), BF16), pltpu.VMEM((ne, LANES), F32)],
        compiler_params=_cparams("arbitrary", "arbitrary"),
        name="route",
    )(lgT, bias.reshape(ne, 1))


def _dest_kernel(pstart_ref, ids_ref, rnk_ref, o_ref, *, ne):
    ids = ids_ref[0]
    base = jnp.zeros(ids.shape, I32)
    for e in range(ne):
        base = jnp.where(ids == e, pstart_ref[e], base)
    o_ref[0] = base + rnk_ref[0]


def _dest(pstart, ids, rnk, ne):
    bsz, k, s = ids.shape
    tn = _pick(s, (2048, 1024, 512, 256, 128))
    blk = lambda: pl.BlockSpec((1, k, tn), lambda b, j, ps: (b, 0, j))
    return pl.pallas_call(
        functools.partial(_dest_kernel, ne=ne),
        grid_spec=pltpu.PrefetchScalarGridSpec(num_scalar_prefetch=1, grid=(bsz, s // tn),
                                               in_specs=[blk(), blk()], out_specs=blk()),
        out_shape=jax.ShapeDtypeStruct((bsz, k, s), I32),
        compiler_params=_cparams("arbitrary", "arbitrary"),
        name="dest",
    )(pstart, ids, rnk)


def _dispatch_kernel(dst_ref, h_ref, xs_hbm, sem, *, tm, topk):
    def start(t, c):
        for k in range(topk):
            pltpu.make_async_copy(h_ref.at[pl.ds(t, 1)], xs_hbm.at[pl.ds(dst_ref[0, k, t], 1)],
                                  sem.at[0]).start(priority=k % 2)
        return c

    lax.fori_loop(0, tm, start, 0)

    def wait(t, c):
        for k in range(topk):
            pltpu.make_async_copy(h_ref.at[pl.ds(t, 1)], xs_hbm.at[pl.ds(0, 1)], sem.at[0]).wait()
        return c

    lax.fori_loop(0, tm, wait, 0)


def _dispatch(dest, h2, n_rows):
    n, d = h2.shape
    nt, topk, tm = dest.shape
    return pl.pallas_call(
        functools.partial(_dispatch_kernel, tm=tm, topk=topk),
        grid=(nt,),
        in_specs=[pl.BlockSpec((1, topk, tm), lambda i: (i, 0, 0), memory_space=pltpu.SMEM),
                  pl.BlockSpec((tm, d), lambda i: (i, 0))],
        out_specs=pl.BlockSpec(memory_space=pl.ANY),
        out_shape=jax.ShapeDtypeStruct((n_rows, d), h2.dtype),
        scratch_shapes=[pltpu.SemaphoreType.DMA((1,))],
        compiler_params=_cparams("arbitrary"),
        name="dispatch",
    )(dest, h2)


def _experts_kernel(be_ref, nv_ref, x_ref, wg_ref, wu_ref, wd_ref, y_ref, wgb, wub, wdb, *, blk):
    i = pl.program_id(0)
    prev_e = be_ref[jnp.maximum(i - 1, 0)]

    @pl.when(jnp.logical_or(i == 0, be_ref[i] != prev_e))
    def _():
        wgb[...] = wg_ref[0].astype(BF16)
        wub[...] = wu_ref[0].astype(BF16)
        wdb[...] = wd_ref[0].astype(BF16)

    @pl.when(nv_ref[i] > 0)
    def _():
        rows = lax.broadcasted_iota(I32, (blk, 1), 0)
        xu = jnp.where(rows < nv_ref[i], x_ref[...], jnp.uint32(0))
        x = jnp.concatenate(_unpack_bf16_pairs(xu), axis=1).astype(BF16)
        act = (_silu(_dot(x, wgb[...])) * _dot(x, wub[...])).astype(BF16)
        y_ref[...] = _pack_bf16_pairs(_dot(act, wdb[...]))


def _experts(xs, be, nvalid, wg, wu, wd):
    n_rows, dh = xs.shape
    ne, d, f = wg.shape
    blk = EXPERT_BLOCK
    nblocks = n_rows // blk
    grid_spec = pltpu.PrefetchScalarGridSpec(
        num_scalar_prefetch=2,
        grid=(nblocks,),
        in_specs=[pl.BlockSpec((blk, dh), lambda i, be, nv: (i, 0)),
                  pl.BlockSpec((1, d, f), lambda i, be, nv: (be[i], 0, 0)),
                  pl.BlockSpec((1, d, f), lambda i, be, nv: (be[i], 0, 0)),
                  pl.BlockSpec((1, f, d), lambda i, be, nv: (be[i], 0, 0))],
        out_specs=pl.BlockSpec((blk, dh), lambda i, be, nv: (i, 0)),
        scratch_shapes=[pltpu.VMEM((d, f), BF16), pltpu.VMEM((d, f), BF16), pltpu.VMEM((f, d), BF16)],
    )
    return pl.pallas_call(
        functools.partial(_experts_kernel, blk=blk),
        grid_spec=grid_spec,
        out_shape=jax.ShapeDtypeStruct((n_rows, dh), U32),
        compiler_params=_cparams("arbitrary"),
        name="experts",
    )(be, nvalid, xs, wg, wu, wd)


def _final_kernel(dst_ref, dstn_ref, wts_ref, h2_ref, x1_ref, g2_ref, wg_ref, wu_ref, wd_ref, lng_ref, lnb_ref,
                  ys_hbm, o_ref, ybuf, sem, *, alpha, topk, tm, nt):
    i = pl.program_id(0)
    slot = lax.rem(i, 2)

    def gather_start(ids_ref, sl):
        def body(t, c):
            for k in range(topk):
                pltpu.make_async_copy(ys_hbm.at[pl.ds(ids_ref[0, k, t], 1)], ybuf.at[sl, k, pl.ds(t, 1)],
                                      sem.at[sl]).start(priority=k % 2)
            return c
        lax.fori_loop(0, tm, body, 0)

    @pl.when(i == 0)
    def _():
        gather_start(dst_ref, 0)

    for t in range(tm):
        for k in range(topk):
            pltpu.make_async_copy(ys_hbm.at[pl.ds(dstn_ref[0, k, t], 1)], ybuf.at[1 - slot, k, pl.ds(t, 1)],
                                  sem.at[1 - slot]).start(priority=k % 2)

    h = jnp.concatenate(_unpack_bf16_pairs(h2_ref[...]), axis=1).astype(BF16)
    y = _dot((_silu(_dot(h, wg_ref[...])) * _dot(h, wu_ref[...])).astype(BF16), wd_ref[...])

    def wait_slot(sl):
        def body(t, c):
            for k in range(topk):
                pltpu.make_async_copy(ys_hbm.at[pl.ds(0, 1)], ybuf.at[sl, k, pl.ds(t, 1)], sem.at[sl]).wait()
            return c
        lax.fori_loop(0, tm, body, 0)

    wait_slot(slot)
    w = wts_ref[...]
    ylo = jnp.zeros((tm, y.shape[1] // 2), F32)
    yhi = jnp.zeros((tm, y.shape[1] // 2), F32)
    for k in range(topk):
        lo, hi = _unpack_bf16_pairs(ybuf[slot, k])
        ylo = ylo + w[:, k:k + 1] * lo
        yhi = yhi + w[:, k:k + 1] * hi
    y = y + jnp.concatenate([ylo, yhi], axis=1)
    o_ref[...] = _ln(alpha * x1_ref[...] + g2_ref[0] * y) * lng_ref[...] + lnb_ref[...]

    @pl.when(i == nt - 1)
    def _():
        wait_slot(1 - slot)


def _final(ys, dest, wtsT, h2, x1, gate2, wg, wu, wd, lng, lnb, alpha, tiles_per_batch):
    n, d = x1.shape
    nt, topk, tm = dest.shape
    kern = functools.partial(_final_kernel, alpha=alpha, topk=topk, tm=tm, nt=nt)
    row = pl.BlockSpec((tm, d), lambda i: (i, 0))
    full = lambda a: pl.BlockSpec(a.shape, lambda i: (0,) * a.ndim, pipeline_mode=pl.Buffered(1))
    return pl.pallas_call(
        kern,
        grid=(nt,),
        in_specs=[pl.BlockSpec((1, topk, tm), lambda i: (i, 0, 0), memory_space=pltpu.SMEM),
                  pl.BlockSpec((1, topk, tm), lambda i: (jnp.minimum(i + 1, nt - 1), 0, 0), memory_space=pltpu.SMEM),
                  pl.BlockSpec((tm, topk), lambda i: (i, 0)),
                  pl.BlockSpec((tm, d // 2), lambda i: (i, 0)), row,
                  pl.BlockSpec((1, 1, d), lambda i: (i // tiles_per_batch, 0, 0)),
                  full(wg), full(wu), full(wd),
                  pl.BlockSpec((1, d), lambda i: (0, 0)), pl.BlockSpec((1, d), lambda i: (0, 0)),
                  pl.BlockSpec(memory_space=pl.ANY)],
        out_specs=row,
        out_shape=jax.ShapeDtypeStruct((n, d), F32),
        scratch_shapes=[pltpu.VMEM((2, topk, tm, d // 2), U32), pltpu.SemaphoreType.DMA((2,))],
        compiler_params=_cparams("arbitrary"),
        name="final",
    )(dest, dest, wtsT, h2, x1, gate2, wg, wu, wd, lng.reshape(1, d), lnb.reshape(1, d), ys)


def _proj_layout(d, qr, kvr, idim, ih, hk, hv):
    src = np.cumsum([0, qr, kvr, idim, ih, hk, hk, hv, hv, d, d])
    wa = -(-(qr + kvr + idim + ih) // LANES) * LANES
    wa = max(wa, 1 << (wa - 1).bit_length())
    pieces = [("a", wa, (int(src[0]), int(src[4]))),
              ("hq", hk, (int(src[4]), int(src[5]))), ("hf", hk, (int(src[5]), int(src[6]))),
              ("hi", hv, (int(src[6]), int(src[7]))), ("hg", hv, (int(src[7]), int(src[8]))),
              ("ga", d, (int(src[8]), int(src[9]))), ("gb", d, (int(src[9]), int(src[10])))]
    pieces.sort(key=lambda p: -p[1])
    off = 0
    layout = {}
    for name, wdt, rng in pieces:
        assert off % wdt == 0
        layout[name] = (off, wdt, rng)
        off += wdt
    return layout, off


def kernel(x, c, rpb_table, hgrn_lb_logits, ada_w, ada_b, w_in, q_norm_g, kv_norm_g, w_uq, w_uk, w_uv, w_qidx,
           idx_k_norm_g, idx_k_norm_b, hgrn_out_norm_g, w_branch_a, w_branch_b, w_o, ln1_g, ln1_b, w_router,
           router_bias, w_exp_gate, w_exp_up, w_exp_down, w_sh_gate, w_sh_up, w_sh_down, ln2_g, ln2_b):
    bsz, s, d = x.shape
    depth = ada_w.shape[0]
    qr = w_uq.shape[1]
    ah, kvr, qk = w_uk.shape[1], w_uk.shape[2], w_uk.shape[3]
    idim = idx_k_norm_g.shape[1]
    ih = w_qidx.shape[2] // idim
    dv = hgrn_out_norm_g.shape[1]
    nh = w_branch_b.shape[1] // dv
    dk = hgrn_lb_logits.shape[1] // nh
    ne = w_router.shape[2]
    topk = min(IDX_TOPK, s // 4)
    alpha = float((2 * depth) ** 0.25)
    n_tok = bsz * s

    lower_bounds = jnp.cumsum(jax.nn.softmax(hgrn_lb_logits.astype(F32), axis=0), axis=0)
    layout, wtot = _proj_layout(d, qr, kvr, idim, ih, nh * dk, nh * dv)

    for l in range(depth):
        mod = _ada(c, ada_w[l], ada_b[l])[:, None, :]
        shift1, scale1, gate1, shift2, scale2, gate2 = jnp.split(mod, 6, axis=-1)

        cols = []
        for off, wdt, (lo, hi) in sorted(layout.values()):
            cols.append(w_in[l][:, lo:hi].astype(BF16))
            if wdt > hi - lo:
                cols.append(jnp.zeros((d, wdt - (hi - lo)), BF16))
        proj = _inproj(x, scale1, shift1, jnp.concatenate(cols, axis=1))
        blk = lambda name: layout[name][0] // layout[name][1]

        qb_sz = _pick(s, (256, 128))
        q_latT, q_idxT, w_idxT, k_idx, kv_lat = _prep(
            proj, blk("a"), layout["a"][1], (qr, kvr, idim, ih, ah, qk), q_norm_g[l], kv_norm_g[l],
            idx_k_norm_g[l], idx_k_norm_b[l], jnp.swapaxes(w_uq[l], 0, 1).astype(BF16),
            w_uk[l].astype(BF16), jnp.swapaxes(w_qidx[l], 0, 1).astype(BF16), qb_sz)
        kv_latT = jnp.concatenate([jnp.swapaxes(kv_lat, 1, 2), jnp.ones((bsz, 1, s), BF16),
                                   jnp.zeros((bsz, 7, s), BF16)], axis=1)
        o_a = _dsa(q_idxT, w_idxT, q_latT, k_idx, kv_lat, kv_latT,
                   rpb_table, jnp.swapaxes(w_uv[l], 1, 2).astype(BF16), topk, qb_sz)

        o_b = _hgrn(proj, (blk("hq"), blk("hf"), blk("hi"), blk("hg")), lower_bounds[l],
                    hgrn_out_norm_g[l], nh, dk, dv)

        x1, h2, lgT = _postmix(o_a, o_b, proj, blk("ga"), blk("gb"), x, gate1, scale2, shift2,
                               w_branch_a[l].astype(BF16), w_branch_b[l].astype(BF16), w_o[l].astype(BF16),
                               jnp.swapaxes(w_router[l], 0, 1).astype(BF16), ln1_g[l], ln1_b[l], alpha)

        ids, wts, rnk, sizes_f = _route(lgT, router_bias[l])
        eb = EXPERT_BLOCK
        sizes = sizes_f[:, 0].astype(I32)
        padded = (sizes + eb - 1) // eb * eb
        pend = jnp.cumsum(padded)
        pstart = pend - padded
        n_rows = -(-(n_tok * MOE_TOPK + ne * (eb - 1)) // eb) * eb
        blk_start = jnp.arange(n_rows // eb, dtype=I32) * eb
        blk_expert = jnp.minimum(jnp.sum((pend[None, :] <= blk_start[:, None]).astype(I32), axis=1), ne - 1)
        onehot = (blk_expert[:, None] == jnp.arange(ne, dtype=I32)[None, :]).astype(I32)
        blk_size = jnp.sum(onehot * sizes[None, :], axis=1)
        blk_pstart = jnp.sum(onehot * pstart[None, :], axis=1)
        nvalid = jnp.clip(blk_size - (blk_start - blk_pstart), 0, eb).astype(I32)
        dest = _dest(pstart.astype(I32), ids, rnk, ne)
        tm = _pick(s, (128,))
        dest_t = dest.reshape(bsz, MOE_TOPK, s // tm, tm).transpose(0, 2, 1, 3).reshape(n_tok // tm, MOE_TOPK, tm)
        wtsT = jnp.swapaxes(wts, 1, 2).reshape(n_tok, MOE_TOPK)

        h2f = h2.reshape(n_tok, d // 2)
        xs = _dispatch(dest_t, h2f, n_rows)
        ys = _experts(xs, blk_expert, nvalid, w_exp_gate[l], w_exp_up[l], w_exp_down[l])
        x = _final(ys, dest_t, wtsT, h2f, x1.reshape(n_tok, d), gate2, w_sh_gate[l].astype(BF16),
                   w_sh_up[l].astype(BF16), w_sh_down[l].astype(BF16), ln2_g[l], ln2_b[l], alpha,
                   s // tm).reshape(bsz, s, d)
    return x
```
